```python
import jax
import jax.numpy as jnp
from jax import lax
import numpy as np

D_MODEL = 1024
BATCH = 32
SEQ = 2048
DEPTH = 4

CTX_LEN = 256
GRID_W = 64
N_MIXERS = 4
MLP_HIDDEN = 4 * D_MODEL
N_MOD = 6
NORM_EPS = 1e-6
NEG_INF = -1e30

ATTN_HEADS = 16
ATTN_KV_HEADS = 4
ATTN_GROUP = ATTN_HEADS // ATTN_KV_HEADS
HEAD_DIM = D_MODEL // ATTN_HEADS
WINDOW = 128
ATTN_BLOCK = 128
ROPE_BASE = 10000.0
ROPE_AXIS_DIM = HEAD_DIM // 2
ROPE_FREQS = ROPE_AXIS_DIM // 2

GLA_HEADS = 4
GLA_KEY_DIM = D_MODEL // 2
GLA_VAL_DIM = D_MODEL
GLA_DK = GLA_KEY_DIM // GLA_HEADS
GLA_DV = GLA_VAL_DIM // GLA_HEADS
GLA_GATE_RANK = 16
GLA_TAU = 16.0
SCAN_CHUNK = 64

RWKV_HEAD_SIZE = 64
RWKV_HEADS = D_MODEL // RWKV_HEAD_SIZE
RWKV_DECAY_RANK = 64
RWKV_AAA_RANK = 64
RWKV_GATE_RANK = 128
RWKV_LN_EPS = 64e-5
L2_EPS = 1e-12

HGRN_EXPAND = 128
HGRN_HEADS = D_MODEL // HGRN_EXPAND
HGRN_FORGET_DIM = HGRN_HEADS * HGRN_EXPAND
HGRN_IN_DIM = D_MODEL // HGRN_HEADS

N_ATTN_LAYERS = (DEPTH + 3) // N_MIXERS
N_GLA_LAYERS = (DEPTH + 2) // N_MIXERS
N_RWKV_LAYERS = (DEPTH + 1) // N_MIXERS
N_HGRN_LAYERS = DEPTH // N_MIXERS

kernel_name = "hybrid_interleaved_flow_backbone"


def rms_norm(x, gain):
    xf = x.astype(jnp.float32)
    y = xf * lax.rsqrt(jnp.mean(xf * xf, axis=-1, keepdims=True) + NORM_EPS)
    return (y * gain.astype(jnp.float32)).astype(x.dtype)


def modulate(x, gain, shift, scale):
    return rms_norm(x, gain) * (1 + scale) + shift


def channel_mlp(h, w_in, w_out):
    return jnp.square(jax.nn.relu(h @ w_in)) @ w_out


def axial_rope_tables(rows):
    inv_freq = ROPE_BASE ** (-jnp.arange(ROPE_FREQS, dtype=jnp.float32) * 2.0 / ROPE_AXIS_DIM)
    pos = jnp.arange(rows * GRID_W)
    row = (pos // GRID_W).astype(jnp.float32)
    col = (pos % GRID_W).astype(jnp.float32)
    ang = jnp.stack([row[:, None] * inv_freq, col[:, None] * inv_freq], axis=1)
    return jnp.cos(ang), jnp.sin(ang)


def apply_axial_rope(x, cos, sin):
    lead = x.shape[:-1]
    xr = x.reshape(lead + (2, 2, ROPE_FREQS))
    x1, x2 = xr[..., 0, :], xr[..., 1, :]
    bshape = (1, x.shape[1]) + (1,) * (x.ndim - 3) + cos.shape[1:]
    cb = cos.reshape(bshape).astype(x.dtype)
    sb = sin.reshape(bshape).astype(x.dtype)
    out = jnp.stack([x1 * cb - x2 * sb, x2 * cb + x1 * sb], axis=-2)
    return out.reshape(x.shape)


def softmax_with_sink(s, sink):
    sk = sink.astype(jnp.float32)[None, :, :, None, None]
    m = jnp.maximum(jnp.max(s, axis=-1, keepdims=True), sk)
    e = jnp.exp(s - m)
    return e / (jnp.sum(e, axis=-1, keepdims=True) + jnp.exp(sk - m))


def windowed_gqa_sink(h_lat, h_ctx, w_qkv, w_o, sink, cos, sin, need_ctx):
    B, L, _ = h_lat.shape
    nb = L // ATTN_BLOCK
    q_cols = ATTN_HEADS * HEAD_DIM
    kv_cols = ATTN_KV_HEADS * HEAD_DIM

    def project(h):
        T = h.shape[1]
        qkv = h @ w_qkv
        q = qkv[..., :q_cols].reshape(B, T, ATTN_KV_HEADS, ATTN_GROUP, HEAD_DIM) * (HEAD_DIM ** -0.5)
        k = qkv[..., q_cols:q_cols + kv_cols].reshape(B, T, ATTN_KV_HEADS, HEAD_DIM)
        v = qkv[..., q_cols + kv_cols:].reshape(B, T, ATTN_KV_HEADS, HEAD_DIM)
        return q, k, v

    q_lat, k_lat, v_lat = project(h_lat)
    q_ctx, k_ctx, v_ctx = project(h_ctx)
    q_lat = apply_axial_rope(q_lat, cos, sin)
    k_lat = apply_axial_rope(k_lat, cos, sin)
    sink_g = sink.reshape(ATTN_KV_HEADS, ATTN_GROUP)

    def band(t):
        tp = jnp.pad(t, ((0, 0), (ATTN_BLOCK, ATTN_BLOCK), (0, 0), (0, 0)))
        tp = tp.reshape(B, nb + 2, ATTN_BLOCK, ATTN_KV_HEADS, HEAD_DIM)
        return jnp.moveaxis(jnp.concatenate([tp[:, :-2], tp[:, 1:-1], tp[:, 2:]], axis=2), 1, 0)

    k_band, v_band = band(k_lat), band(v_lat)
    q_blocks = jnp.moveaxis(q_lat.reshape(B, nb, ATTN_BLOCK, ATTN_KV_HEADS, ATTN_GROUP, HEAD_DIM), 1, 0)
    n_loc = 3 * ATTN_BLOCK
    rel = jnp.arange(n_loc)[None, :] - ATTN_BLOCK - jnp.arange(ATTN_BLOCK)[:, None]
    in_window = jnp.abs(rel) <= WINDOW

    def attend_block(args):
        q, k, v, blk = args
        key_pos = blk * ATTN_BLOCK - ATTN_BLOCK + jnp.arange(n_loc)
        ok = in_window & ((key_pos >= 0) & (key_pos < L))[None, :]
        s_lat = jnp.einsum('bqhgd,bkhd->bhgqk', q, k).astype(jnp.float32)
        s_lat = jnp.where(ok, s_lat, NEG_INF)
        s_ctx = jnp.einsum('bqhgd,bkhd->bhgqk', q, k_ctx).astype(jnp.float32)
        p = softmax_with_sink(jnp.concatenate([s_lat, s_ctx], axis=-1), sink_g).astype(v.dtype)
        return (jnp.einsum('bhgqk,bkhd->bqhgd', p[..., :n_loc], v)
                + jnp.einsum('bhgqk,bkhd->bqhgd', p[..., n_loc:], v_ctx))

    o_lat = lax.map(attend_block, (q_blocks, k_band, v_band, jnp.arange(nb)))
    o_lat = jnp.moveaxis(o_lat, 0, 1).reshape(B, L, q_cols)
    y_lat = o_lat @ w_o
    if not need_ctx:
        return y_lat, None
    s_c = jnp.einsum('bqhgd,bkhd->bhgqk', q_ctx, k_ctx).astype(jnp.float32)
    p_c = softmax_with_sink(s_c, sink_g).astype(v_ctx.dtype)
    o_ctx = jnp.einsum('bhgqk,bkhd->bqhgd', p_c, v_ctx).reshape(B, h_ctx.shape[1], q_cols)
    return y_lat, o_ctx @ w_o


def reverse_segments(t, n_ctx):
    return jnp.concatenate([t[:, :n_ctx][:, ::-1], t[:, n_ctx:][:, ::-1]], axis=1)


def chunk_gated_scan(q, k, v, log_g):
    B, T, H, dk = q.shape
    dv = v.shape[-1]
    n = T // SCAN_CHUNK

    def chunks(t):
        return jnp.moveaxis(t.astype(jnp.float32).reshape(B, n, SCAN_CHUNK, H, t.shape[-1]), 1, 0)

    lower = jnp.tril(jnp.ones((SCAN_CHUNK, SCAN_CHUNK), dtype=bool))

    def step(S, xs):
        qc, kc, vc, gc = xs
        b = jnp.cumsum(gc, axis=1)
        q_dec = qc * jnp.exp(b)
        k_inv = kc * jnp.exp(-b)
        A = jnp.where(lower, jnp.einsum('bqhd,bkhd->bhqk', q_dec, k_inv), 0.0)
        o = jnp.einsum('bhqk,bkhv->bqhv', A, vc) + jnp.einsum('bqhd,bhdv->bqhv', q_dec, S)
        b_last = b[:, -1]
        k_end = kc * jnp.exp(b_last[:, None] - b)
        S = jnp.exp(b_last)[..., None] * S + jnp.einsum('bkhd,bkhv->bhdv', k_end, vc)
        return S, o

    S0 = jnp.zeros((B, H, dk, dv), jnp.float32)
    _, o = lax.scan(step, S0, (chunks(q), chunks(k), chunks(v), chunks(log_g)))
    return jnp.moveaxis(o, 0, 1).reshape(B, T, H, dv).astype(v.dtype)


def bidir_gated_scan(q, v, k_fwd, g_fwd, k_bwd, g_bwd, n_ctx):
    rev = lambda t: reverse_segments(t, n_ctx)
    o_f = chunk_gated_scan(q, k_fwd, v, g_fwd)
    o_b = chunk_gated_scan(rev(q), rev(k_bwd), rev(v), rev(g_bwd))
    return o_f + rev(o_b)


def gla_mixer(h_lat, h_ctx, w_in, w_gate_down, w_gate_up, gate_bias, g_norm, w_o, need_ctx):
    n_ctx = h_ctx.shape[1]
    h = jnp.concatenate([h_ctx, h_lat], axis=1)
    B, T, _ = h.shape
    z = h @ w_in
    q = z[..., :GLA_KEY_DIM].reshape(B, T, GLA_HEADS, GLA_DK) * (GLA_DK ** -0.5)
    k = z[..., GLA_KEY_DIM:2 * GLA_KEY_DIM].reshape(B, T, GLA_HEADS, GLA_DK)
    v = z[..., 2 * GLA_KEY_DIM:2 * GLA_KEY_DIM + GLA_VAL_DIM].reshape(B, T, GLA_HEADS, GLA_DV)
    out_gate = z[..., 2 * GLA_KEY_DIM + GLA_VAL_DIM:]

    def log_decay(d):
        zg = (h @ w_gate_down[d]) @ w_gate_up[d] + gate_bias[d]
        return (jax.nn.log_sigmoid(zg.astype(jnp.float32)) / GLA_TAU).reshape(B, T, GLA_HEADS, GLA_DK)

    o = bidir_gated_scan(q, v, k, log_decay(0), k, log_decay(1), n_ctx)
    o = rms_norm(o, g_norm).reshape(B, T, GLA_VAL_DIM) * jax.nn.silu(out_gate)
    y_lat = o[:, n_ctx:] @ w_o
    y_ctx = o[:, :n_ctx] @ w_o if need_ctx else None
    return y_lat, y_ctx


def centred_shift(h):
    hp = jnp.pad(h, ((0, 0), (1, 1), (0, 0)))
    return 0.5 * (hp[:, :-2] + hp[:, 2:]) - h


def rwkv7_scan(r, decay, kk, a, k, v):
    B, T, H, N = r.shape

    def step(S, xs):
        r_t, w_t, kk_t, a_t, k_t, v_t = xs
        sa = jnp.einsum('bhvk,bhk->bhv', S, -kk_t)
        S = (S * w_t[:, :, None, :] + sa[..., None] * (kk_t * a_t)[:, :, None, :]
             + v_t[..., None] * k_t[:, :, None, :])
        return S, jnp.einsum('bhvk,bhk->bhv', S, r_t)

    tm = lambda t: jnp.moveaxis(t.astype(jnp.float32), 1, 0)
    S0 = jnp.zeros((B, H, N, N), jnp.float32)
    _, y = lax.scan(step, S0, (tm(r), tm(decay), tm(kk), tm(a), tm(k), tm(v)))
    return jnp.moveaxis(y, 0, 1)


def rwkv7_mixer(h_lat, h_ctx, mix, w_rkv, w0, w_down, w_up, a0, a_down, a_up, g_down, g_up,
                k_k, k_a, r_k, ln_w, ln_b, w_o, need_ctx):
    n_ctx = h_ctx.shape[1]
    h = jnp.concatenate([h_ctx, h_lat], axis=1)
    dx = jnp.concatenate([centred_shift(h_ctx), centred_shift(h_lat)], axis=1)
    B, T, D = h.shape
    hm = h[None] + dx[None] * mix[:, None, None, :]
    r, k, v = jnp.einsum('nbtd,nde->nbte', hm[:3], w_rkv)
    hw, ha, hg = hm[3], hm[4], hm[5]
    heads = lambda t: t.reshape(B, T, RWKV_HEADS, RWKV_HEAD_SIZE)
    g = jax.nn.sigmoid(hg @ g_down) @ g_up
    kk = heads((k * k_k).astype(jnp.float32))
    kk = kk / jnp.maximum(jnp.sqrt(jnp.sum(kk * kk, axis=-1, keepdims=True)), L2_EPS)

    def direction(d):
        w_log = -jax.nn.softplus(-(w0[d] + jnp.tanh(hw @ w_down[d]) @ w_up[d]).astype(jnp.float32)) - 0.5
        a = jax.nn.sigmoid((a0[d] + (ha @ a_down[d]) @ a_up[d]).astype(jnp.float32))
        return heads(jnp.exp(-jnp.exp(w_log))), heads(a), heads(k * (1 + (a - 1) * k_a))

    dec_f, a_f, k_f = direction(0)
    dec_b, a_b, k_b = direction(1)
    r_h, v_h = heads(r), heads(v)
    rev = lambda t: reverse_segments(t, n_ctx)
    y = (rwkv7_scan(r_h, dec_f, kk, a_f, k_f, v_h)
         + rev(rwkv7_scan(rev(r_h), rev(dec_b), rev(kk), rev(a_b), rev(k_b), rev(v_h))))
    mu = jnp.mean(y, axis=-1, keepdims=True)
    var = jnp.mean(jnp.square(y - mu), axis=-1, keepdims=True)
    yn = ((y - mu) * lax.rsqrt(var + RWKV_LN_EPS) * ln_w.reshape(RWKV_HEADS, RWKV_HEAD_SIZE)
          + ln_b.reshape(RWKV_HEADS, RWKV_HEAD_SIZE))
    bonus = jnp.sum(r_h * (0.5 * (k_f + k_b)) * r_k, axis=-1, keepdims=True) * v_h
    out = (yn + bonus).reshape(B, T, D).astype(h.dtype) * g
    y_lat = out[:, n_ctx:] @ w_o
    y_ctx = out[:, :n_ctx] @ w_o if need_ctx else None
    return y_lat, y_ctx


def hgrn_lower_bound(lb_param, layer):
    p = jax.nn.softmax(lb_param.astype(jnp.float32), axis=0)
    return (jnp.cumsum(p, axis=0) - p[0])[layer]


def hgrn2_mixer(h_lat, h_ctx, w_in, w_f, lower_bound, g_norm, w_o, need_ctx):
    n_ctx = h_ctx.shape[1]
    h = jnp.concatenate([h_ctx, h_lat], axis=1)
    B, T, _ = h.shape
    z = h @ w_in
    q = jax.nn.silu(z[..., :HGRN_FORGET_DIM]).reshape(B, T, HGRN_HEADS, HGRN_EXPAND)
    i_in = z[..., HGRN_FORGET_DIM:HGRN_FORGET_DIM + D_MODEL].reshape(B, T, HGRN_HEADS, HGRN_IN_DIM)
    out_gate = z[..., HGRN_FORGET_DIM + D_MODEL:]

    def gates(d):
        f = lower_bound + (1 - lower_bound) * jax.nn.sigmoid((h @ w_f[d]).astype(jnp.float32))
        f = f.reshape(B, T, HGRN_HEADS, HGRN_EXPAND)
        return 1 - f, jnp.log(f)

    k_fwd, g_fwd = gates(0)
    k_bwd, g_bwd = gates(1)
    o = bidir_gated_scan(q, i_in, k_fwd, g_fwd, k_bwd, g_bwd, n_ctx)
    o = rms_norm(o, g_norm).reshape(B, T, D_MODEL) * jax.nn.silu(out_gate)
    y_lat = o[:, n_ctx:] @ w_o
    y_ctx = o[:, :n_ctx] @ w_o if need_ctx else None
    return y_lat, y_ctx


def setup_inputs(seed: int = 0) -> dict:
    key = jax.random.key(seed)
    keys = iter(jax.random.split(key, 48))
    D = D_MODEL

    def normal(shape, scale):
        return jax.random.normal(next(keys), shape, jnp.float32) * scale

    def gain(shape):
        return 1.0 + normal(shape, 0.02)

    nA, nB, nC, nH = N_ATTN_LAYERS, N_GLA_LAYERS, N_RWKV_LAYERS, N_HGRN_LAYERS
    return {
        "x": normal((BATCH, SEQ, D), 1.0),
        "c": normal((BATCH, D), 1.0),
        "ctx": normal((BATCH, CTX_LEN, D), 1.0),
        "c_ctx": normal((D,), 1.0),
        "w_mod": normal((DEPTH, D, N_MOD * D), 0.5 * D ** -0.5),
        "b_mod": normal((DEPTH, N_MOD * D), 0.01),
        "g_pre_mix": gain((DEPTH, D)),
        "g_post_mix": gain((DEPTH, D)),
        "g_pre_mlp": gain((DEPTH, D)),
        "g_post_mlp": gain((DEPTH, D)),
        "w_mlp_in": normal((DEPTH, D, MLP_HIDDEN), D ** -0.5),
        "w_mlp_out": normal((DEPTH, MLP_HIDDEN, D), MLP_HIDDEN ** -0.5),
        "attn_w_qkv": normal((nA, D, (ATTN_HEADS + 2 * ATTN_KV_HEADS) * HEAD_DIM), D ** -0.5),
        "attn_w_o": normal((nA, ATTN_HEADS * HEAD_DIM, D), (ATTN_HEADS * HEAD_DIM) ** -0.5),
        "attn_sink": normal((nA, ATTN_HEADS), 0.5),
        "gla_w_in": normal((nB, D, 2 * GLA_KEY_DIM + 2 * GLA_VAL_DIM), D ** -0.5),
        "gla_w_gate_down": normal((nB, 2, D, GLA_GATE_RANK), D ** -0.5),
        "gla_w_gate_up": normal((nB, 2, GLA_GATE_RANK, GLA_KEY_DIM), GLA_GATE_RANK ** -0.5),
        "gla_gate_bias": normal((nB, 2, GLA_KEY_DIM), 0.1),
        "gla_g_norm": gain((nB, GLA_DV)),
        "gla_w_o": normal((nB, GLA_VAL_DIM, D), GLA_VAL_DIM ** -0.5),
        "rwkv_mix": jax.random.uniform(next(keys), (nC, 6, D), jnp.float32),
        "rwkv_w_rkv": normal((nC, 3, D, D), D ** -0.5),
        "rwkv_w0": -1.5 + normal((nC, 2, D), 0.5),
        "rwkv_w_down": normal((nC, 2, D, RWKV_DECAY_RANK), D ** -0.5),
        "rwkv_w_up": normal((nC, 2, RWKV_DECAY_RANK, D), 0.1 * RWKV_DECAY_RANK ** -0.5),
        "rwkv_a0": normal((nC, 2, D), 0.1),
        "rwkv_a_down": normal((nC, 2, D, RWKV_AAA_RANK), D ** -0.5),
        "rwkv_a_up": normal((nC, 2, RWKV_AAA_RANK, D), RWKV_AAA_RANK ** -0.5),
        "rwkv_g_down": normal((nC, D, RWKV_GATE_RANK), D ** -0.5),
        "rwkv_g_up": normal((nC, RWKV_GATE_RANK, D), RWKV_GATE_RANK ** -0.5),
        "rwkv_k_k": 0.85 + normal((nC, D), 0.02),
        "rwkv_k_a": gain((nC, D)),
        "rwkv_r_k": normal((nC, RWKV_HEADS, RWKV_HEAD_SIZE), 0.1),
        "rwkv_ln_w": gain((nC, D)),
        "rwkv_ln_b": normal((nC, D), 0.02),
        "rwkv_w_o": normal((nC, D, D), D ** -0.5),
        "hgrn_w_in": normal((nH, D, HGRN_FORGET_DIM + 2 * D), D ** -0.5),
        "hgrn_w_f": normal((nH, 2, D, HGRN_FORGET_DIM), D ** -0.5),
        "hgrn_lb": normal((DEPTH, HGRN_FORGET_DIM), 0.1),
        "hgrn_g_norm": gain((nH, HGRN_IN_DIM)),
        "hgrn_w_o": normal((nH, D, D), D ** -0.5),
    }


def reference(x, c, ctx, c_ctx, w_mod, b_mod, g_pre_mix, g_post_mix, g_pre_mlp, g_post_mlp,
              w_mlp_in, w_mlp_out, attn_w_qkv, attn_w_o, attn_sink,
              gla_w_in, gla_w_gate_down, gla_w_gate_up, gla_gate_bias, gla_g_norm, gla_w_o,
              rwkv_mix, rwkv_w_rkv, rwkv_w0, rwkv_w_down, rwkv_w_up, rwkv_a0, rwkv_a_down,
              rwkv_a_up, rwkv_g_down, rwkv_g_up, rwkv_k_k, rwkv_k_a, rwkv_r_k, rwkv_ln_w,
              rwkv_ln_b, rwkv_w_o, hgrn_w_in, hgrn_w_f, hgrn_lb, hgrn_g_norm, hgrn_w_o):
    n_latent = x.shape[1]
    rows = n_latent // GRID_W
    cos, sin = axial_rope_tables(rows)
    c_lat = jax.nn.silu(c)
    c_con = jax.nn.silu(c_ctx)
    x_lat, x_ctx = x, ctx
    for i in range(DEPTH):
        kind, j = i % N_MIXERS, i // N_MIXERS
        need_ctx = i < DEPTH - 1
        m_lat = jnp.split((c_lat @ w_mod[i] + b_mod[i])[:, None, :], N_MOD, axis=-1)
        m_ctx = jnp.split(c_con @ w_mod[i] + b_mod[i], N_MOD, axis=-1)
        h_lat = modulate(x_lat, g_pre_mix[i], m_lat[0], m_lat[1])
        h_ctx = modulate(x_ctx, g_pre_mix[i], m_ctx[0], m_ctx[1])
        if kind == 0:
            y_lat, y_ctx = windowed_gqa_sink(h_lat, h_ctx, attn_w_qkv[j], attn_w_o[j], attn_sink[j],
                                             cos, sin, need_ctx)
        elif kind == 1:
            y_lat, y_ctx = gla_mixer(h_lat, h_ctx, gla_w_in[j], gla_w_gate_down[j], gla_w_gate_up[j],
                                     gla_gate_bias[j], gla_g_norm[j], gla_w_o[j], need_ctx)
        elif kind == 2:
            y_lat, y_ctx = rwkv7_mixer(h_lat, h_ctx, rwkv_mix[j], rwkv_w_rkv[j], rwkv_w0[j],
                                       rwkv_w_down[j], rwkv_w_up[j], rwkv_a0[j], rwkv_a_down[j],
                                       rwkv_a_up[j], rwkv_g_down[j], rwkv_g_up[j], rwkv_k_k[j],
                                       rwkv_k_a[j], rwkv_r_k[j], rwkv_ln_w[j], rwkv_ln_b[j],
                                       rwkv_w_o[j], need_ctx)
        else:
            y_lat, y_ctx = hgrn2_mixer(h_lat, h_ctx, hgrn_w_in[j], hgrn_w_f[j],
                                       hgrn_lower_bound(hgrn_lb, i), hgrn_g_norm[j], hgrn_w_o[j],
                                       need_ctx)
        x_lat = x_lat + m_lat[2] * rms_norm(y_lat, g_post_mix[i])
        f_lat = channel_mlp(modulate(x_lat, g_pre_mlp[i], m_lat[3], m_lat[4]), w_mlp_in[i], w_mlp_out[i])
        x_lat = x_lat + m_lat[5] * rms_norm(f_lat, g_post_mlp[i])
        if need_ctx:
            x_ctx = x_ctx + m_ctx[2] * rms_norm(y_ctx, g_post_mix[i])
            f_ctx = channel_mlp(modulate(x_ctx, g_pre_mlp[i], m_ctx[3], m_ctx[4]), w_mlp_in[i], w_mlp_out[i])
            x_ctx = x_ctx + m_ctx[5] * rms_norm(f_ctx, g_post_mlp[i])
    return x_lat
```

```python
import functools

import jax
import jax.numpy as jnp
from jax import lax
from jax.experimental import pallas as pl
from jax.experimental.pallas import tpu as pltpu

f32 = jnp.float32
bf16 = jnp.bfloat16

D_MODEL = 1024
N_MOD = 6
MLP_HIDDEN = 4 * D_MODEL
NORM_EPS = 1e-6
NEG_INF = -1e30
GRID_W = 64

ATTN_HEADS = 16
ATTN_KV_HEADS = 4
ATTN_GROUP = ATTN_HEADS // ATTN_KV_HEADS
HEAD_DIM = 64
WINDOW = 128
ATTN_BLOCK = 128
ROPE_BASE = 10000.0
ROPE_AXIS_DIM = HEAD_DIM // 2
ROPE_FREQS = ROPE_AXIS_DIM // 2

GLA_HEADS = 4
GLA_KEY_DIM = D_MODEL // 2
GLA_DK = GLA_KEY_DIM // GLA_HEADS
GLA_DV = D_MODEL // GLA_HEADS
GLA_GATE_RANK = 16
GLA_TAU = 16.0
SCAN_CHUNK = 64

RWKV_HEAD_SIZE = 64
RWKV_LN_EPS = 64e-5
L2_EPS = 1e-12
RWKV_PAIR = 2 * RWKV_HEAD_SIZE

HGRN_EXPAND = 128
HGRN_HEADS = D_MODEL // HGRN_EXPAND

LANES = 128
MOD_ROWS = 8
VMEM_LIMIT = 56 * 1024 * 1024

NT = (((1,), (1,)), ((), ()))
TN = (((0,), (0,)), ((), ()))


def _params(n_grid):
    return pltpu.CompilerParams(dimension_semantics=("arbitrary",) * n_grid, vmem_limit_bytes=VMEM_LIMIT)


def _const_spec(shape):
    nd = len(shape)
    return pl.BlockSpec(shape, lambda *_: (0,) * nd, pipeline_mode=pl.Buffered(1))


def _dot(a, b):
    return jnp.dot(a, b, preferred_element_type=f32)


def _sigmoid(x):
    return 1.0 / (1.0 + jnp.exp(-x))


def _silu(x):
    return x * _sigmoid(x)


def _softplus(x):
    return jnp.maximum(x, 0.0) + jnp.log1p(jnp.exp(-jnp.abs(x)))


def _rms(x, gain):
    return x * lax.rsqrt(jnp.mean(x * x, axis=-1, keepdims=True) + NORM_EPS) * gain


def _modulate(x, gain, shift, scale):
    return _rms(x, gain) * (1.0 + scale) + shift


def _seg_sum(x, seg):
    r = lax.broadcasted_iota(jnp.int32, (LANES, LANES), 0) // seg
    c = lax.broadcasted_iota(jnp.int32, (LANES, LANES), 1) // seg
    ones_bd = (r == c).astype(bf16)
    hi = x.astype(bf16)
    lo = (x - hi.astype(f32)).astype(bf16)
    outs = []
    for j in range(x.shape[1] // LANES):
        sl = slice(j * LANES, (j + 1) * LANES)
        outs.append(_dot(hi[:, sl], ones_bd) + _dot(lo[:, sl], ones_bd))
    return jnp.concatenate(outs, axis=1)


def _cumsum_rows(g, reverse):
    n = g.shape[0]
    row = lax.broadcasted_iota(jnp.int32, g.shape, 0)
    x = g
    s = 1
    while s < n:
        if reverse:
            x = x + jnp.where(row < n - s, pltpu.roll(x, n - s, axis=0), 0.0)
        else:
            x = x + jnp.where(row >= s, pltpu.roll(x, s, axis=0), 0.0)
        s *= 2
    return x


def _mod_kernel(c_ref, w_ref, b_ref, o_ref):
    o_ref[0] = _dot(_silu(c_ref[...]).astype(bf16), w_ref[0]) + b_ref[0]


def _mod_vectors(c, c_ctx, w_mod, b_mod):
    depth, d, _ = w_mod.shape
    batch = c.shape[0]
    rows = -(-(batch + 1) // 8) * 8
    cc = jnp.zeros((rows, d), f32).at[:batch].set(c).at[batch].set(c_ctx)
    out = pl.pallas_call(
        _mod_kernel,
        grid=(depth, N_MOD),
        in_specs=[
            pl.BlockSpec((rows, d), lambda i, j: (0, 0)),
            pl.BlockSpec((1, d, d), lambda i, j: (i, 0, j)),
            pl.BlockSpec((1, 1, d), lambda i, j: (i, 0, j)),
        ],
        out_specs=pl.BlockSpec((1, rows, d), lambda i, j: (i, 0, j)),
        out_shape=jax.ShapeDtypeStruct((depth, rows, N_MOD * d), f32),
        compiler_params=_params(2),
        name="mod_vectors",
    )(cc, w_mod.astype(bf16), b_mod.reshape(depth, 1, N_MOD * d))
    lat = out[:, :batch].reshape(depth, batch, 1, N_MOD, d)
    con = jnp.broadcast_to(out[:, batch].reshape(depth, 1, 1, N_MOD, d), lat.shape)
    mods = jnp.concatenate([con, lat], axis=2)
    return jnp.pad(mods, ((0, 0), (0, 0), (0, 0), (0, MOD_ROWS - N_MOD), (0, 0)))


class _Geom:
    def __init__(self, batch, n_ctx, n_lat):
        self.batch, self.n_ctx, self.n_lat = batch, n_ctx, n_lat
        self.t = n_ctx + n_lat
        self.tm = min(256, n_ctx)
        assert n_ctx % self.tm == 0 and n_lat % self.tm == 0
        assert n_ctx % ATTN_BLOCK == 0 and n_lat % ATTN_BLOCK == 0 and n_lat % GRID_W == 0
        self.nt = self.t // self.tm
        self.ctx_tiles = n_ctx // self.tm
        self.nc = self.t // SCAN_CHUNK
        self.ctx_chunks = n_ctx // SCAN_CHUNK

    def x_spec(self, t0=0):
        return pl.BlockSpec((1, self.tm, D_MODEL), lambda b, t: (b, t + t0, 0))

    def mod_spec(self, t0=0):
        ct = self.ctx_tiles
        return pl.BlockSpec((1, 1, MOD_ROWS, D_MODEL), lambda b, t: (b, ((t + t0) >= ct).astype(jnp.int32), 0, 0))

    def bwd_chunk(self, i):
        cc = self.ctx_chunks
        return jnp.where(i < cc, cc - 1 - i, self.nc - 1 + cc - i)


def _lin_kernel(x_ref, m_ref, g_ref, w_ref, o_ref):
    h = _modulate(x_ref[0], g_ref[0:1], m_ref[0, 0, 0:1], m_ref[0, 0, 1:2])
    o_ref[0] = _dot(h.astype(bf16), w_ref[...])


def _mod_linear(geo, x, mods, gains, w, name):
    n = w.shape[1]
    return pl.pallas_call(
        _lin_kernel,
        grid=(geo.batch, geo.nt),
        in_specs=[geo.x_spec(), geo.mod_spec(), _const_spec((8, D_MODEL)), _const_spec((D_MODEL, n))],
        out_specs=pl.BlockSpec((1, geo.tm, n), lambda b, t: (b, t, 0)),
        out_shape=jax.ShapeDtypeStruct((geo.batch, geo.t, n), f32),
        compiler_params=_params(2),
        name=name,
    )(x, mods, gains, w)


def _head_rms(o, gain, width):
    outs = []
    for h in range(o.shape[1] // width):
        oh = o[:, h * width:(h + 1) * width]
        outs.append(oh * lax.rsqrt(jnp.mean(oh * oh, axis=-1, keepdims=True) + NORM_EPS) * gain)
    return jnp.concatenate(outs, axis=1)


def _pre_attn(refs):
    (o_ref,) = refs
    return o_ref[0]


def _pre_scan(width, refs):
    of_ref, ob_ref, gate_ref, gn_ref = refs
    o = _head_rms(of_ref[0] + ob_ref[0], gn_ref[...], width)
    return (o * _silu(gate_ref[0])).astype(bf16)


def _pre_rwkv(refs):
    yf_ref, yb_ref, r_ref, k0_ref, k1_ref, v_ref, g_ref, vec_ref = refs
    y = yf_ref[0] + yb_ref[0]
    inv_n = 1.0 / RWKV_HEAD_SIZE
    mu = _seg_sum(y, RWKV_HEAD_SIZE) * inv_n
    dlt = y - mu
    var = _seg_sum(dlt * dlt, RWKV_HEAD_SIZE) * inv_n
    yn = dlt * lax.rsqrt(var + RWKV_LN_EPS) * vec_ref[0:1] + vec_ref[1:2]
    bonus = _seg_sum(r_ref[0] * (0.5 * (k0_ref[0] + k1_ref[0])) * vec_ref[2:3], RWKV_HEAD_SIZE) * v_ref[0]
    return ((yn + bonus) * g_ref[0]).astype(bf16)


def _post_kernel(pre, n_mix, x_ref, m_ref, g_ref, *rest):
    mix_refs = rest[:n_mix]
    wo_ref, win_ref, wout_ref, out_ref = rest[n_mix:]
    o = pre(mix_refs)
    y = _dot(o, wo_ref[...])
    x1 = x_ref[0] + m_ref[0, 0, 2:3] * _rms(y, g_ref[1:2])
    h2 = _modulate(x1, g_ref[2:3], m_ref[0, 0, 3:4], m_ref[0, 0, 4:5]).astype(bf16)
    acc = jnp.zeros_like(x1)
    for c in range(MLP_HIDDEN // D_MODEL):
        u = jnp.square(jnp.maximum(_dot(h2, win_ref[:, c * D_MODEL:(c + 1) * D_MODEL]), 0.0))
        acc = acc + _dot(u.astype(bf16), wout_ref[c * D_MODEL:(c + 1) * D_MODEL, :])
    out_ref[0] = x1 + m_ref[0, 0, 5:6] * _rms(acc, g_ref[3:4])


def _post_mlp(geo, x, mods, gains, pre, mix_args, mix_specs, w_o, w_in, w_out, skip_ctx, name):
    t0 = geo.ctx_tiles if skip_ctx else 0
    return pl.pallas_call(
        functools.partial(_post_kernel, pre, len(mix_args)),
        grid=(geo.batch, geo.nt - t0),
        in_specs=[geo.x_spec(t0), geo.mod_spec(t0), _const_spec((8, D_MODEL))] + mix_specs + [
            _const_spec((D_MODEL, D_MODEL)), _const_spec((D_MODEL, MLP_HIDDEN)), _const_spec((MLP_HIDDEN, D_MODEL))],
        out_specs=pl.BlockSpec((1, geo.tm, D_MODEL), lambda b, t: (b, t, 0)),
        out_shape=jax.ShapeDtypeStruct((geo.batch, geo.t - t0 * geo.tm, D_MODEL), f32),
        compiler_params=_params(2),
        name=name,
    )(x, mods, gains, *mix_args, w_o, w_in, w_out)


def _tile_spec(geo, width, col, t0):
    return pl.BlockSpec((1, geo.tm, width), lambda b, t: (b, t + t0, col))


def _rope(x, cos, sin):
    w = x.shape[1]
    reps = w // LANES
    cw = jnp.concatenate([cos] * reps, axis=1)
    sw = jnp.concatenate([sin] * reps, axis=1)
    lane = lax.broadcasted_iota(jnp.int32, x.shape, 1)
    first = (lane % ROPE_AXIS_DIM) < ROPE_FREQS
    partner = jnp.where(first, pltpu.roll(x, w - ROPE_FREQS, axis=1), pltpu.roll(x, ROPE_FREQS, axis=1))
    return x * cw + partner * sw


def _attn_proj_kernel(x_ref, m_ref, g_ref, cos_ref, sin_ref, wq_ref, wk_ref, wv_ref, q_ref, k_ref, v_ref):
    h = _modulate(x_ref[0], g_ref[0:1], m_ref[0, 0, 0:1], m_ref[0, 0, 1:2]).astype(bf16)
    cos, sin = cos_ref[...], sin_ref[...]
    q = _rope(_dot(h, wq_ref[...]) * (HEAD_DIM ** -0.5), cos, sin)
    k = _rope(_dot(h, wk_ref[...]), cos, sin)
    v = _dot(h, wv_ref[...])
    for hd in range(ATTN_HEADS):
        q_ref[0, hd] = q[:, hd * HEAD_DIM:(hd + 1) * HEAD_DIM].astype(bf16)
    for hd in range(ATTN_KV_HEADS):
        k_ref[0, hd] = k[:, hd * HEAD_DIM:(hd + 1) * HEAD_DIM].astype(bf16)
        v_ref[0, hd] = v[:, hd * HEAD_DIM:(hd + 1) * HEAD_DIM].astype(bf16)


def _attn_core_kernel(geo, sink_ref, q_ref, kp_ref, kc_ref, kn_ref, kx_ref, vp_ref, vc_ref, vn_ref, vx_ref, o_ref):
    j = pl.program_id(1)
    qb = pl.program_id(2)
    blk = ATTN_BLOCK
    n_loc = 3 * blk
    n_keys = n_loc + geo.n_ctx
    rows = ATTN_GROUP * blk
    first_lat = geo.n_ctx // blk
    n_blocks = geo.t // blk
    q4 = q_ref[0].reshape(rows, HEAD_DIM)
    keys = jnp.concatenate([kp_ref[0, 0], kc_ref[0, 0], kn_ref[0, 0], kx_ref[0, 0]], axis=0)
    vals = jnp.concatenate([vp_ref[0, 0], vc_ref[0, 0], vn_ref[0, 0], vx_ref[0, 0]], axis=0)
    s = lax.dot_general(q4, keys, NT, preferred_element_type=f32)
    row = lax.broadcasted_iota(jnp.int32, (rows, n_keys), 0)
    col = lax.broadcasted_iota(jnp.int32, (rows, n_keys), 1)
    rel = col - blk - (row % blk)
    kblk = qb - 1 + col // blk
    ok_local = (jnp.abs(rel) <= WINDOW) & (qb >= first_lat) & (kblk >= first_lat) & (kblk < n_blocks)
    s = jnp.where((col >= n_loc) | ok_local, s, NEG_INF)
    grp = lax.broadcasted_iota(jnp.int32, (rows, 1), 0) // blk
    sk = jnp.zeros((rows, 1), f32)
    for g in range(ATTN_GROUP):
        sk = jnp.where(grp == g, sink_ref[ATTN_GROUP * j + g], sk)
    m = jnp.maximum(jnp.max(s, axis=-1, keepdims=True), sk)
    e = jnp.exp(s - m)
    p = e / (jnp.sum(e, axis=-1, keepdims=True) + jnp.exp(sk - m))
    o4 = _dot(p.astype(bf16), vals)
    o_ref[0] = jnp.concatenate([o4[g * blk:(g + 1) * blk] for g in range(ATTN_GROUP)], axis=1).astype(bf16)


def _rope_tables(geo):
    inv_freq = ROPE_BASE ** (-jnp.arange(ROPE_FREQS, dtype=f32) * 2.0 / ROPE_AXIS_DIM)
    pos = jnp.arange(geo.n_lat)
    row = (pos // GRID_W).astype(f32)
    col = (pos % GRID_W).astype(f32)
    ang = jnp.stack([row[:, None] * inv_freq, col[:, None] * inv_freq], axis=1)
    cos = jnp.cos(ang)
    sin = jnp.sin(ang)
    cos_h = jnp.concatenate([cos, cos], axis=2).reshape(geo.n_lat, HEAD_DIM)
    sin_h = jnp.concatenate([-sin, sin], axis=2).reshape(geo.n_lat, HEAD_DIM)
    cos_t = jnp.concatenate([jnp.ones((geo.n_ctx, HEAD_DIM), f32), cos_h], axis=0)
    sin_t = jnp.concatenate([jnp.zeros((geo.n_ctx, HEAD_DIM), f32), sin_h], axis=0)
    return jnp.tile(cos_t, (1, 2)), jnp.tile(sin_t, (1, 2))


def _attn_layer(geo, x, mods, gains, w_qkv, w_o, sink, w_in, w_out, skip_ctx):
    b, t, tm = geo.batch, geo.t, geo.tm
    q_cols = ATTN_HEADS * HEAD_DIM
    kv_cols = ATTN_KV_HEADS * HEAD_DIM
    wb = w_qkv.astype(bf16)
    cos_t, sin_t = _rope_tables(geo)
    tab_spec = pl.BlockSpec((tm, LANES), lambda bb, tt: (tt, 0))
    q, k, v = pl.pallas_call(
        _attn_proj_kernel,
        grid=(b, geo.nt),
        in_specs=[geo.x_spec(), geo.mod_spec(), _const_spec((8, D_MODEL)), tab_spec, tab_spec,
                  _const_spec((D_MODEL, q_cols)), _const_spec((D_MODEL, kv_cols)), _const_spec((D_MODEL, kv_cols))],
        out_specs=[pl.BlockSpec((1, ATTN_HEADS, tm, HEAD_DIM), lambda bb, tt: (bb, 0, tt, 0)),
                   pl.BlockSpec((1, ATTN_KV_HEADS, tm, HEAD_DIM), lambda bb, tt: (bb, 0, tt, 0)),
                   pl.BlockSpec((1, ATTN_KV_HEADS, tm, HEAD_DIM), lambda bb, tt: (bb, 0, tt, 0))],
        out_shape=[jax.ShapeDtypeStruct((b, ATTN_HEADS, t, HEAD_DIM), bf16),
                   jax.ShapeDtypeStruct((b, ATTN_KV_HEADS, t, HEAD_DIM), bf16),
                   jax.ShapeDtypeStruct((b, ATTN_KV_HEADS, t, HEAD_DIM), bf16)],
        compiler_params=_params(2),
        name="attn_proj",
    )(x, mods, gains, cos_t, sin_t, wb[:, :q_cols], wb[:, q_cols:q_cols + kv_cols], wb[:, q_cols + kv_cols:])

    blk = ATTN_BLOCK
    n_blocks = t // blk

    def kv_spec(off):
        return pl.BlockSpec((1, 1, blk, HEAD_DIM),
                            lambda bb, j, qb: (bb, j, jnp.clip(qb + off, 0, n_blocks - 1), 0))

    ctx_spec = pl.BlockSpec((1, 1, geo.n_ctx, HEAD_DIM), lambda bb, j, qb: (bb, j, 0, 0))
    o = pl.pallas_call(
        functools.partial(_attn_core_kernel, geo),
        grid=(b, ATTN_KV_HEADS, n_blocks),
        in_specs=[pl.BlockSpec(memory_space=pltpu.SMEM),
                  pl.BlockSpec((1, ATTN_GROUP, blk, HEAD_DIM), lambda bb, j, qb: (bb, j, qb, 0)),
                  kv_spec(-1), kv_spec(0), kv_spec(1), ctx_spec,
                  kv_spec(-1), kv_spec(0), kv_spec(1), ctx_spec],
        out_specs=pl.BlockSpec((1, blk, ATTN_GROUP * HEAD_DIM), lambda bb, j, qb: (bb, qb, j)),
        out_shape=jax.ShapeDtypeStruct((b, t, q_cols), bf16),
        compiler_params=_params(3),
        name="attn_core",
    )(sink, q, k, k, k, k, v, v, v, v)

    t0 = geo.ctx_tiles if skip_ctx else 0
    return _post_mlp(geo, x, mods, gains, _pre_attn, [o], [_tile_spec(geo, D_MODEL, 0, t0)],
                     w_o.astype(bf16), w_in, w_out, skip_ctx, "attn_post")


def _gated_chunk(q, k, v, g, st_ref, reverse, heads, dk, dv):
    n = q.shape[0]
    b = _cumsum_rows(g, reverse)
    btot = b[0:1] if reverse else b[n - 1:n]
    q_dec = (q * jnp.exp(b)).astype(bf16)
    k_inv = (k * jnp.exp(-b)).astype(bf16)
    k_end = (k * jnp.exp(btot - b)).astype(bf16)
    dec = jnp.exp(btot)
    vb = v.astype(bf16)
    ri = lax.broadcasted_iota(jnp.int32, (n, n), 0)
    ci = lax.broadcasted_iota(jnp.int32, (n, n), 1)
    tri = (ci >= ri) if reverse else (ci <= ri)
    outs = []
    for h in range(heads):
        ks = slice(h * dk, (h + 1) * dk)
        vs = slice(h * dv, (h + 1) * dv)
        a = lax.dot_general(q_dec[:, ks], k_inv[:, ks], NT, preferred_element_type=f32)
        a = jnp.where(tri, a, 0.0).astype(bf16)
        st = st_ref[h]
        outs.append(_dot(a, vb[:, vs])
                    + lax.dot_general(q_dec[:, ks], st.astype(bf16), NT, preferred_element_type=f32))
        st_ref[h] = st * dec[:, ks] + lax.dot_general(vb[:, vs], k_end[:, ks], TN, preferred_element_type=f32)
    return jnp.concatenate(outs, axis=1)


def _gla_gate(zd_ref, wu_ref, bias_ref, d):
    zg = _dot(zd_ref[0].astype(bf16), wu_ref[d]) + bias_ref[d:d + 1]
    return (jnp.minimum(zg, 0.0) - jnp.log1p(jnp.exp(-jnp.abs(zg)))) * (1.0 / GLA_TAU)


def _gla_scan_kernel(qkf_ref, vf_ref, zdf_ref, qkb_ref, vb_ref, zdb_ref, wu_ref, bias_ref,
                     of_ref, ob_ref, sf_ref, sb_ref):
    @pl.when(pl.program_id(1) == 0)
    def _():
        sf_ref[...] = jnp.zeros_like(sf_ref)
        sb_ref[...] = jnp.zeros_like(sb_ref)

    for d, (qk_ref, v_ref, zd_ref, o_ref, s_ref) in enumerate(
            ((qkf_ref, vf_ref, zdf_ref, of_ref, sf_ref), (qkb_ref, vb_ref, zdb_ref, ob_ref, sb_ref))):
        qk = qk_ref[0]
        q = qk[:, :GLA_KEY_DIM] * (GLA_DK ** -0.5)
        k = qk[:, GLA_KEY_DIM:]
        g = _gla_gate(zd_ref, wu_ref, bias_ref, d)
        o_ref[0] = _gated_chunk(q, k, v_ref[0], g, s_ref, d == 1, GLA_HEADS, GLA_DK, GLA_DV)


def _hgrn_scan_kernel(qf_ref, if_ref, zff_ref, qb_ref, ib_ref, zfb_ref, lb_ref, of_ref, ob_ref, sf_ref, sb_ref):
    @pl.when(pl.program_id(1) == 0)
    def _():
        sf_ref[...] = jnp.zeros_like(sf_ref)
        sb_ref[...] = jnp.zeros_like(sb_ref)

    lb = lb_ref[...]
    for d, (q_ref, i_ref, zf_ref, o_ref, s_ref) in enumerate(
            ((qf_ref, if_ref, zff_ref, of_ref, sf_ref), (qb_ref, ib_ref, zfb_ref, ob_ref, sb_ref))):
        f = lb + (1.0 - lb) * _sigmoid(zf_ref[0])
        o_ref[0] = _gated_chunk(_silu(q_ref[0]), 1.0 - f, i_ref[0], jnp.log(f), s_ref, d == 1,
                                HGRN_HEADS, HGRN_EXPAND, HGRN_EXPAND)


def _chunk_spec(geo, width, col, reverse):
    if reverse:
        return pl.BlockSpec((1, SCAN_CHUNK, width), lambda b, i: (b, geo.bwd_chunk(i), col))
    return pl.BlockSpec((1, SCAN_CHUNK, width), lambda b, i: (b, i, col))


def _scan_call(geo, body, args, specs, heads, dk, dv, name):
    out_sds = jax.ShapeDtypeStruct((geo.batch, geo.t, heads * dv), f32)
    return pl.pallas_call(
        body,
        grid=(geo.batch, geo.nc),
        in_specs=specs,
        out_specs=[_chunk_spec(geo, heads * dv, 0, False), _chunk_spec(geo, heads * dv, 0, True)],
        out_shape=[out_sds, out_sds],
        scratch_shapes=[pltpu.VMEM((heads, dv, dk), f32), pltpu.VMEM((heads, dv, dk), f32)],
        compiler_params=_params(2),
        name=name,
    )(*args)


def _gla_layer(geo, x, mods, gains, w_in_p, w_gd, w_gu, g_bias, g_norm, w_o, w_in, w_out, skip_ctx):
    r = GLA_GATE_RANK
    n_z = 2 * GLA_KEY_DIM + 2 * D_MODEL
    w_all = jnp.concatenate([w_in_p, w_gd[0], w_gd[1], jnp.zeros((D_MODEL, LANES - 2 * r), f32)], axis=1).astype(bf16)
    z = _mod_linear(geo, x, mods, gains, w_all, "gla_proj")
    wu = jnp.zeros((2, LANES, GLA_KEY_DIM), f32).at[0, :r].set(w_gu[0]).at[1, r:2 * r].set(w_gu[1]).astype(bf16)
    zd_col = n_z // LANES
    specs = []
    for rev in (False, True):
        specs += [_chunk_spec(geo, 2 * GLA_KEY_DIM, 0, rev), _chunk_spec(geo, D_MODEL, 1, rev),
                  _chunk_spec(geo, LANES, zd_col, rev)]
    specs += [_const_spec((2, LANES, GLA_KEY_DIM)), _const_spec((2, GLA_KEY_DIM))]
    o_f, o_b = _scan_call(geo, _gla_scan_kernel, [z, z, z, z, z, z, wu, g_bias], specs,
                          GLA_HEADS, GLA_DK, GLA_DV, "gla_scan")
    t0 = geo.ctx_tiles if skip_ctx else 0
    mix_specs = [_tile_spec(geo, D_MODEL, 0, t0), _tile_spec(geo, D_MODEL, 0, t0), _tile_spec(geo, D_MODEL, 2, t0),
                 _const_spec((1, GLA_DV))]
    return _post_mlp(geo, x, mods, gains, functools.partial(_pre_scan, GLA_DV),
                     [o_f, o_b, z, g_norm.reshape(1, GLA_DV)], mix_specs,
                     w_o.astype(bf16), w_in, w_out, skip_ctx, "gla_post")


def _hgrn_layer(geo, x, mods, gains, w_in_p, w_f, lower_bound, g_norm, w_o, w_in, w_out, skip_ctx):
    w_all = jnp.concatenate([w_in_p, w_f[0], w_f[1]], axis=1).astype(bf16)
    z = _mod_linear(geo, x, mods, gains, w_all, "hgrn_proj")
    specs = []
    for rev in (False, True):
        specs += [_chunk_spec(geo, D_MODEL, 0, rev), _chunk_spec(geo, D_MODEL, 1, rev),
                  _chunk_spec(geo, D_MODEL, 4 if rev else 3, rev)]
    specs += [_const_spec((1, D_MODEL))]
    o_f, o_b = _scan_call(geo, _hgrn_scan_kernel, [z, z, z, z, z, z, lower_bound.reshape(1, D_MODEL)], specs,
                          HGRN_HEADS, HGRN_EXPAND, HGRN_EXPAND, "hgrn_scan")
    t0 = geo.ctx_tiles if skip_ctx else 0
    mix_specs = [_tile_spec(geo, D_MODEL, 0, t0), _tile_spec(geo, D_MODEL, 0, t0), _tile_spec(geo, D_MODEL, 2, t0),
                 _const_spec((1, HGRN_EXPAND))]
    return _post_mlp(geo, x, mods, gains, functools.partial(_pre_scan, HGRN_EXPAND),
                     [o_f, o_b, z, g_norm.reshape(1, HGRN_EXPAND)], mix_specs,
                     w_o.astype(bf16), w_in, w_out, skip_ctx, "hgrn_post")


def _rwkv_proj_kernel(geo, x_ref, xp_ref, xn_ref, m_ref, g_ref, mix_ref, vec_ref, wrkv_ref, wdn_ref, wup_ref,
                      aup_ref, gup_ref, r_ref, v_ref, gate_ref, kk_ref, lw0_ref, lw1_ref, a0_ref, a1_ref,
                      k0_ref, k1_ref):
    t = pl.program_id(1)
    tm = geo.tm
    gain, shift, scale = g_ref[0:1], m_ref[0, 0, 0:1], m_ref[0, 0, 1:2]
    h = _modulate(x_ref[0], gain, shift, scale)
    seg_first = (t == 0) | (t == geo.ctx_tiles)
    seg_last = (t == geo.ctx_tiles - 1) | (t == geo.nt - 1)
    h_prev = jnp.where(seg_first, 0.0, _modulate(xp_ref[0], gain, shift, scale)[7:8])
    h_next = jnp.where(seg_last, 0.0, _modulate(xn_ref[0], gain, shift, scale)[0:1])
    row = lax.broadcasted_iota(jnp.int32, h.shape, 0)
    up = jnp.where(row == 0, h_prev, pltpu.roll(h, 1, axis=0))
    dn = jnp.where(row == tm - 1, h_next, pltpu.roll(h, tm - 1, axis=0))
    dx = 0.5 * (up + dn) - h

    def mixed(n):
        return (h + dx * mix_ref[n:n + 1]).astype(bf16)

    r = _dot(mixed(0), wrkv_ref[0])
    k = _dot(mixed(1), wrkv_ref[1])
    v = _dot(mixed(2), wrkv_ref[2])
    dw = jnp.tanh(_dot(mixed(3), wdn_ref[:, 0:LANES])).astype(bf16)
    da = _dot(mixed(4), wdn_ref[:, LANES:2 * LANES]).astype(bf16)
    dg = _sigmoid(_dot(mixed(5), wdn_ref[:, 2 * LANES:3 * LANES])).astype(bf16)
    r_ref[0] = r
    v_ref[0] = v
    gate_ref[0] = _dot(dg, gup_ref[...])
    kk = k * vec_ref[4:5]
    kk_ref[0] = kk / jnp.maximum(jnp.sqrt(_seg_sum(kk * kk, RWKV_HEAD_SIZE)), L2_EPS)
    for d, (lw_ref, a_ref, kd_ref) in enumerate(((lw0_ref, a0_ref, k0_ref), (lw1_ref, a1_ref, k1_ref))):
        w_log = -_softplus(-(vec_ref[d:d + 1] + _dot(dw, wup_ref[d]))) - 0.5
        lw_ref[0] = -jnp.exp(w_log)
        a = _sigmoid(vec_ref[2 + d:3 + d] + _dot(da, aup_ref[d]))
        a_ref[0] = a
        kd_ref[0] = k * (1.0 + (a - 1.0) * vec_ref[5:6])


def _rwkv_pair_chunk(r, kd, v, kk, a, lw, ht, reverse):
    n = r.shape[0]
    hs = RWKV_HEAD_SIZE
    c = _cumsum_rows(lw, reverse)
    tot = c[0:1] if reverse else c[n - 1:n]
    e_neg = jnp.exp(-c)
    e_end = jnp.exp(tot - c)
    beta = kk * a
    a_bar = (-kk * jnp.exp(c - lw)).astype(bf16)
    r_bar = (r * jnp.exp(c)).astype(bf16)
    lane = lax.broadcasted_iota(jnp.int32, (n, RWKV_PAIR), 1)
    head0 = lane < hs

    def block_diag(x):
        return jnp.concatenate([jnp.where(head0, x, 0.0), jnp.where(head0, 0.0, x)], axis=0)

    rhs = jnp.concatenate([block_diag(kd * e_neg), block_diag(beta * e_neg)], axis=0).astype(bf16)
    gmat = lax.dot_general(jnp.concatenate([a_bar, r_bar], axis=0), rhs, NT, preferred_element_type=f32)
    t_i = lax.broadcasted_iota(jnp.int32, (n, RWKV_PAIR), 0)
    s_i = lane % hs
    strict = (s_i > t_i) if reverse else (s_i < t_i)
    incl = (s_i >= t_i) if reverse else (s_i <= t_i)
    a_ak = jnp.where(strict, gmat[0:n, 0:RWKV_PAIR], 0.0).astype(bf16)
    a_ab = jnp.where(strict, gmat[0:n, RWKV_PAIR:], 0.0)
    q_k = jnp.where(incl, gmat[n:, 0:RWKV_PAIR], 0.0).astype(bf16)
    q_b = jnp.where(incl, gmat[n:, RWKV_PAIR:], 0.0).astype(bf16)
    v_bd = block_diag(v).astype(bf16)
    ht_b = ht.astype(bf16)
    x = lax.dot_general(a_bar, ht_b, NT, preferred_element_type=f32) + _dot(a_ak, v_bd)
    x_bd = block_diag(x)
    n_bd = block_diag(a_ab)
    steps = 6
    for j in range(steps):
        nb = n_bd.astype(bf16)
        x_bd = x_bd + _dot(nb, x_bd.astype(bf16))
        if j + 1 < steps:
            n_bd = _dot(nb, nb)
    u_bd = x_bd.astype(bf16)
    y = lax.dot_general(r_bar, ht_b, NT, preferred_element_type=f32) + _dot(q_k, v_bd) + _dot(q_b, u_bd)
    u = x_bd[0:n] + x_bd[n:]
    upd = lax.dot_general(jnp.concatenate([v, u], axis=0).astype(bf16),
                          jnp.concatenate([kd * e_end, beta * e_end], axis=0).astype(bf16),
                          TN, preferred_element_type=f32)
    same_head = (lax.broadcasted_iota(jnp.int32, (RWKV_PAIR, RWKV_PAIR), 0) // hs
                 == lax.broadcasted_iota(jnp.int32, (RWKV_PAIR, RWKV_PAIR), 1) // hs)
    return y, ht * jnp.exp(tot) + jnp.where(same_head, upd, 0.0)


def _rwkv_scan_kernel(pairs, *refs):
    fwd, bwd = refs[0:6], refs[6:12]
    yf_ref, yb_ref, hf_ref, hb_ref = refs[12:]

    @pl.when(pl.program_id(2) == 0)
    def _():
        hf_ref[...] = jnp.zeros_like(hf_ref)
        hb_ref[...] = jnp.zeros_like(hb_ref)

    for ins, y_ref, h_ref, reverse in ((fwd, yf_ref, hf_ref, False), (bwd, yb_ref, hb_ref, True)):
        r_ref, v_ref, kk_ref, lw_ref, a_ref, kd_ref = ins
        for p in range(pairs):
            sl = slice(p * RWKV_PAIR, (p + 1) * RWKV_PAIR)
            y, h_new = _rwkv_pair_chunk(r_ref[0, :, sl], kd_ref[0, :, sl], v_ref[0, :, sl], kk_ref[0, :, sl],
                                        a_ref[0, :, sl], lw_ref[0, :, sl], h_ref[p], reverse)
            y_ref[0, :, sl] = y
            h_ref[p] = h_new


RWKV_PAIRS_PER_STEP = 4


def _rwkv_layer(geo, x, mods, gains, mix, w_rkv, w0, w_down, w_up, a0, a_down, a_up, g_down, g_up, k_k, k_a,
                r_k, ln_w, ln_b, w_o, w_in, w_out, skip_ctx):
    b, t, tm = geo.batch, geo.t, geo.tm
    d = D_MODEL
    rank = w_down.shape[-1]
    assert 2 * rank == LANES and a_down.shape[-1] == rank and g_down.shape[-1] == LANES
    mix8 = jnp.pad(mix, ((0, 2), (0, 0)))
    vec = jnp.stack([w0[0], w0[1], a0[0], a0[1], k_k, k_a, jnp.zeros_like(k_k), jnp.zeros_like(k_k)])
    w_dn = jnp.concatenate([w_down[0], w_down[1], a_down[0], a_down[1], g_down], axis=1).astype(bf16)

    def padded_up(w):
        return jnp.zeros((2, LANES, d), f32).at[0, :rank].set(w[0]).at[1, rank:].set(w[1]).astype(bf16)

    halo = 8
    n_halo = t // halo
    per = tm // halo
    x_prev = pl.BlockSpec((1, halo, d), lambda bb, tt: (bb, jnp.maximum(tt * per - 1, 0), 0))
    x_next = pl.BlockSpec((1, halo, d), lambda bb, tt: (bb, jnp.minimum((tt + 1) * per, n_halo - 1), 0))
    out_spec = pl.BlockSpec((1, tm, d), lambda bb, tt: (bb, tt, 0))
    sds = jax.ShapeDtypeStruct((b, t, d), f32)
    r, v, gate, kk, lw0, lw1, a_0, a_1, k0, k1 = pl.pallas_call(
        functools.partial(_rwkv_proj_kernel, geo),
        grid=(b, geo.nt),
        in_specs=[geo.x_spec(), x_prev, x_next, geo.mod_spec(), _const_spec((8, d)), _const_spec((8, d)),
                  _const_spec((8, d)), _const_spec((3, d, d)), _const_spec((d, 3 * LANES)),
                  _const_spec((2, LANES, d)), _const_spec((2, LANES, d)), _const_spec((LANES, d))],
        out_specs=[out_spec] * 10,
        out_shape=[sds] * 10,
        compiler_params=_params(2),
        name="rwkv_proj",
    )(x, x, x, mods, gains, mix8, vec, w_rkv.astype(bf16), w_dn, padded_up(w_up), padded_up(a_up),
      g_up.astype(bf16))

    pairs = RWKV_PAIRS_PER_STEP
    width = pairs * RWKV_PAIR
    groups = d // width

    def cspec(reverse):
        if reverse:
            return pl.BlockSpec((1, SCAN_CHUNK, width), lambda bb, gg, i: (bb, geo.bwd_chunk(i), gg))
        return pl.BlockSpec((1, SCAN_CHUNK, width), lambda bb, gg, i: (bb, i, gg))

    y_f, y_b = pl.pallas_call(
        functools.partial(_rwkv_scan_kernel, pairs),
        grid=(b, groups, geo.nc),
        in_specs=[cspec(False)] * 6 + [cspec(True)] * 6,
        out_specs=[cspec(False), cspec(True)],
        out_shape=[sds, sds],
        scratch_shapes=[pltpu.VMEM((pairs, RWKV_PAIR, RWKV_PAIR), f32), pltpu.VMEM((pairs, RWKV_PAIR, RWKV_PAIR), f32)],
        compiler_params=_params(3),
        name="rwkv_scan",
    )(r, v, kk, lw0, a_0, k0, r, v, kk, lw1, a_1, k1)

    t0 = geo.ctx_tiles if skip_ctx else 0
    post_vec = jnp.stack([ln_w, ln_b, r_k.reshape(d)] + [jnp.zeros_like(ln_w)] * 5)
    mix_specs = [_tile_spec(geo, d, 0, t0)] * 7 + [_const_spec((8, d))]
    return _post_mlp(geo, x, mods, gains, _pre_rwkv, [y_f, y_b, r, k0, k1, v, gate, post_vec], mix_specs,
                     w_o.astype(bf16), w_in, w_out, skip_ctx, "rwkv_post")


def _hgrn_lower_bound(lb_param, layer):
    p = jax.nn.softmax(lb_param.astype(f32), axis=0)
    return (jnp.cumsum(p, axis=0) - p[0])[layer]


def kernel(x, c, ctx, c_ctx, w_mod, b_mod, g_pre_mix, g_post_mix, g_pre_mlp, g_post_mlp, w_mlp_in, w_mlp_out, attn_w_qkv, attn_w_o, attn_sink, gla_w_in, gla_w_gate_down, gla_w_gate_up, gla_gate_bias, gla_g_norm, gla_w_o, rwkv_mix, rwkv_w_rkv, rwkv_w0, rwkv_w_down, rwkv_w_up, rwkv_a0, rwkv_a_down, rwkv_a_up, rwkv_g_down, rwkv_g_up, rwkv_k_k, rwkv_k_a, rwkv_r_k, rwkv_ln_w, rwkv_ln_b, rwkv_w_o, hgrn_w_in, hgrn_w_f, hgrn_lb, hgrn_g_norm, hgrn_w_o):
    depth = w_mod.shape[0]
    geo = _Geom(x.shape[0], ctx.shape[1], x.shape[1])
    mods_all = _mod_vectors(c, c_ctx, w_mod, b_mod)
    xs = jnp.concatenate([ctx, x], axis=1)
    for i in range(depth):
        kind, j = i % 4, i // 4
        skip_ctx = i == depth - 1
        mods = mods_all[i]
        gains = jnp.stack([g_pre_mix[i], g_post_mix[i], g_pre_mlp[i], g_post_mlp[i]] + [jnp.zeros_like(g_pre_mix[i])] * 4)
        w_in, w_out = w_mlp_in[i].astype(bf16), w_mlp_out[i].astype(bf16)
        if kind == 0:
            xs = _attn_layer(geo, xs, mods, gains, attn_w_qkv[j], attn_w_o[j], attn_sink[j], w_in, w_out, skip_ctx)
        elif kind == 1:
            xs = _gla_layer(geo, xs, mods, gains, gla_w_in[j], gla_w_gate_down[j], gla_w_gate_up[j],
                            gla_gate_bias[j], gla_g_norm[j], gla_w_o[j], w_in, w_out, skip_ctx)
        elif kind == 2:
            xs = _rwkv_layer(geo, xs, mods, gains, rwkv_mix[j], rwkv_w_rkv[j], rwkv_w0[j], rwkv_w_down[j],
                             rwkv_w_up[j], rwkv_a0[j], rwkv_a_down[j], rwkv_a_up[j], rwkv_g_down[j], rwkv_g_up[j],
                             rwkv_k_k[j], rwkv_k_a[j], rwkv_r_k[j], rwkv_ln_w[j], rwkv_ln_b[j], rwkv_w_o[j],
                             w_in, w_out, skip_ctx)
        else:
            xs = _hgrn_layer(geo, xs, mods, gains, hgrn_w_in[j], hgrn_w_f[j], _hgrn_lower_bound(hgrn_lb, i),
                             hgrn_g_norm[j], hgrn_w_o[j], w_in, w_out, skip_ctx)
        if skip_ctx:
            return xs
    return xs[:, geo.n_ctx:]
```

```python
import functools

import jax
import jax.numpy as jnp
from jax import lax
from jax.experimental import pallas as pl
from jax.experimental.pallas import tpu as pltpu

f32 = jnp.float32
bf16 = jnp.bfloat16

D_MODEL = 1024
N_MOD = 6
MLP_HIDDEN = 4 * D_MODEL
NORM_EPS = 1e-6
NEG_INF = -1e30
GRID_W = 64

ATTN_HEADS = 16
ATTN_KV_HEADS = 4
ATTN_GROUP = ATTN_HEADS // ATTN_KV_HEADS
HEAD_DIM = 64
WINDOW = 128
ATTN_BLOCK = 128
ROPE_BASE = 10000.0
ROPE_AXIS_DIM = HEAD_DIM // 2
ROPE_FREQS = ROPE_AXIS_DIM // 2

GLA_HEADS = 4
GLA_KEY_DIM = D_MODEL // 2
GLA_DK = GLA_KEY_DIM // GLA_HEADS
GLA_DV = D_MODEL // GLA_HEADS
GLA_GATE_RANK = 16
GLA_TAU = 16.0
SCAN_CHUNK = 64

RWKV_HEAD_SIZE = 64
RWKV_LN_EPS = 64e-5
L2_EPS = 1e-12
RWKV_PAIR = 2 * RWKV_HEAD_SIZE

HGRN_EXPAND = 128
HGRN_HEADS = D_MODEL // HGRN_EXPAND

LANES = 128
MOD_ROWS = 8
VMEM_LIMIT = 56 * 1024 * 1024

NT = (((1,), (1,)), ((), ()))
TN = (((0,), (0,)), ((), ()))


def _params(n_grid):
    return pltpu.CompilerParams(dimension_semantics=("arbitrary",) * n_grid, vmem_limit_bytes=VMEM_LIMIT)


def _const_spec(shape):
    nd = len(shape)
    return pl.BlockSpec(shape, lambda *_: (0,) * nd, pipeline_mode=pl.Buffered(1))


def _dot(a, b):
    return jnp.dot(a, b, preferred_element_type=f32)


def _sigmoid(x):
    return 1.0 / (1.0 + jnp.exp(-x))


def _silu(x):
    return x * _sigmoid(x)


def _softplus(x):
    return jnp.maximum(x, 0.0) + jnp.log1p(jnp.exp(-jnp.abs(x)))


def _rms(x, gain):
    return x * lax.rsqrt(jnp.mean(x * x, axis=-1, keepdims=True) + NORM_EPS) * gain


def _modulate(x, gain, shift, scale):
    return _rms(x, gain) * (1.0 + scale) + shift


def _seg_sum(x, seg):
    r = lax.broadcasted_iota(jnp.int32, (LANES, LANES), 0) // seg
    c = lax.broadcasted_iota(jnp.int32, (LANES, LANES), 1) // seg
    ones_bd = (r == c).astype(bf16)
    hi = x.astype(bf16)
    lo = (x - hi.astype(f32)).astype(bf16)
    outs = []
    for j in range(x.shape[1] // LANES):
        sl = slice(j * LANES, (j + 1) * LANES)
        outs.append(_dot(hi[:, sl], ones_bd) + _dot(lo[:, sl], ones_bd))
    return jnp.concatenate(outs, axis=1)


def _cumsum_rows(g, reverse):
    n = g.shape[0]
    row = lax.broadcasted_iota(jnp.int32, g.shape, 0)
    x = g
    s = 1
    while s < n:
        if reverse:
            x = x + jnp.where(row < n - s, pltpu.roll(x, n - s, axis=0), 0.0)
        else:
            x = x + jnp.where(row >= s, pltpu.roll(x, s, axis=0), 0.0)
        s *= 2
    return x


def _mod_kernel(c_ref, w_ref, b_ref, o_ref):
    o_ref[0] = _dot(_silu(c_ref[...]).astype(bf16), w_ref[0]) + b_ref[0]


def _mod_vectors(c, c_ctx, w_mod, b_mod):
    depth, d, _ = w_mod.shape
    batch = c.shape[0]
    rows = -(-(batch + 1) // 8) * 8
    cc = jnp.zeros((rows, d), f32).at[:batch].set(c).at[batch].set(c_ctx)
    out = pl.pallas_call(
        _mod_kernel,
        grid=(depth, N_MOD),
        in_specs=[
            pl.BlockSpec((rows, d), lambda i, j: (0, 0)),
            pl.BlockSpec((1, d, d), lambda i, j: (i, 0, j)),
            pl.BlockSpec((1, 1, d), lambda i, j: (i, 0, j)),
        ],
        out_specs=pl.BlockSpec((1, rows, d), lambda i, j: (i, 0, j)),
        out_shape=jax.ShapeDtypeStruct((depth, rows, N_MOD * d), f32),
        compiler_params=_params(2),
        name="mod_vectors",
    )(cc, w_mod.astype(bf16), b_mod.reshape(depth, 1, N_MOD * d))
    lat = out[:, :batch].reshape(depth, batch, 1, N_MOD, d)
    con = jnp.broadcast_to(out[:, batch].reshape(depth, 1, 1, N_MOD, d), lat.shape)
    mods = jnp.concatenate([con, lat], axis=2)
    return jnp.pad(mods, ((0, 0), (0, 0), (0, 0), (0, MOD_ROWS - N_MOD), (0, 0)))


class _Geom:
    def __init__(self, batch, n_ctx, n_lat):
        self.batch, self.n_ctx, self.n_lat = batch, n_ctx, n_lat
        self.t = n_ctx + n_lat
        self.tm = min(256, n_ctx)
        assert n_ctx % self.tm == 0 and n_lat % self.tm == 0
        assert n_ctx % ATTN_BLOCK == 0 and n_lat % ATTN_BLOCK == 0 and n_lat % GRID_W == 0
        self.nt = self.t // self.tm
        self.ctx_tiles = n_ctx // self.tm
        self.nc = self.t // SCAN_CHUNK
        self.ctx_chunks = n_ctx // SCAN_CHUNK

    def x_spec(self, t0=0):
        return pl.BlockSpec((1, self.tm, D_MODEL), lambda b, t: (b, t + t0, 0))

    def mod_spec(self, t0=0):
        ct = self.ctx_tiles
        return pl.BlockSpec((1, 1, MOD_ROWS, D_MODEL), lambda b, t: (b, ((t + t0) >= ct).astype(jnp.int32), 0, 0))

    def bwd_chunk(self, i):
        cc = self.ctx_chunks
        return jnp.where(i < cc, cc - 1 - i, self.nc - 1 + cc - i)


def _lin_kernel(x_ref, m_ref, g_ref, w_ref, o_ref):
    h = _modulate(x_ref[0], g_ref[0:1], m_ref[0, 0, 0:1], m_ref[0, 0, 1:2])
    o_ref[0] = _dot(h.astype(bf16), w_ref[...])


def _mod_linear(geo, x, mods, gains, w, name):
    n = w.shape[1]
    return pl.pallas_call(
        _lin_kernel,
        grid=(geo.batch, geo.nt),
        in_specs=[geo.x_spec(), geo.mod_spec(), _const_spec((8, D_MODEL)), _const_spec((D_MODEL, n))],
        out_specs=pl.BlockSpec((1, geo.tm, n), lambda b, t: (b, t, 0)),
        out_shape=jax.ShapeDtypeStruct((geo.batch, geo.t, n), f32),
        compiler_params=_params(2),
        name=name,
    )(x, mods, gains, w)


def _head_rms(o, gain, width):
    outs = []
    for h in range(o.shape[1] // width):
        oh = o[:, h * width:(h + 1) * width]
        outs.append(oh * lax.rsqrt(jnp.mean(oh * oh, axis=-1, keepdims=True) + NORM_EPS) * gain)
    return jnp.concatenate(outs, axis=1)


def _pre_attn(refs):
    (o_ref,) = refs
    return o_ref[0]


def _pre_scan(width, refs):
    of_ref, ob_ref, gate_ref, gn_ref = refs
    o = _head_rms(of_ref[0] + ob_ref[0], gn_ref[...], width)
    return (o * _silu(gate_ref[0])).astype(bf16)


def _pre_rwkv(refs):
    yf_ref, yb_ref, r_ref, k0_ref, k1_ref, v_ref, g_ref, vec_ref = refs
    y = yf_ref[0] + yb_ref[0]
    inv_n = 1.0 / RWKV_HEAD_SIZE
    mu = _seg_sum(y, RWKV_HEAD_SIZE) * inv_n
    dlt = y - mu
    var = _seg_sum(dlt * dlt, RWKV_HEAD_SIZE) * inv_n
    yn = dlt * lax.rsqrt(var + RWKV_LN_EPS) * vec_ref[0:1] + vec_ref[1:2]
    bonus = _seg_sum(r_ref[0] * (0.5 * (k0_ref[0] + k1_ref[0])) * vec_ref[2:3], RWKV_HEAD_SIZE) * v_ref[0]
    return ((yn + bonus) * g_ref[0]).astype(bf16)


def _post_kernel(pre, n_mix, x_ref, m_ref, g_ref, *rest):
    mix_refs = rest[:n_mix]
    wo_ref, win_ref, wout_ref, out_ref = rest[n_mix:]
    o = pre(mix_refs)
    y = _dot(o, wo_ref[...])
    x1 = x_ref[0] + m_ref[0, 0, 2:3] * _rms(y, g_ref[1:2])
    h2 = _modulate(x1, g_ref[2:3], m_ref[0, 0, 3:4], m_ref[0, 0, 4:5]).astype(bf16)
    acc = jnp.zeros_like(x1)
    for c in range(MLP_HIDDEN // D_MODEL):
        u = jnp.square(jnp.maximum(_dot(h2, win_ref[:, c * D_MODEL:(c + 1) * D_MODEL]), 0.0))
        acc = acc + _dot(u.astype(bf16), wout_ref[c * D_MODEL:(c + 1) * D_MODEL, :])
    out_ref[0] = x1 + m_ref[0, 0, 5:6] * _rms(acc, g_ref[3:4])


def _post_mlp(geo, x, mods, gains, pre, mix_args, mix_specs, w_o, w_in, w_out, skip_ctx, name):
    t0 = geo.ctx_tiles if skip_ctx else 0
    return pl.pallas_call(
        functools.partial(_post_kernel, pre, len(mix_args)),
        grid=(geo.batch, geo.nt - t0),
        in_specs=[geo.x_spec(t0), geo.mod_spec(t0), _const_spec((8, D_MODEL))] + mix_specs + [
            _const_spec((D_MODEL, D_MODEL)), _const_spec((D_MODEL, MLP_HIDDEN)), _const_spec((MLP_HIDDEN, D_MODEL))],
        out_specs=pl.BlockSpec((1, geo.tm, D_MODEL), lambda b, t: (b, t, 0)),
        out_shape=jax.ShapeDtypeStruct((geo.batch, geo.t - t0 * geo.tm, D_MODEL), f32),
        compiler_params=_params(2),
        name=name,
    )(x, mods, gains, *mix_args, w_o, w_in, w_out)


def _tile_spec(geo, width, col, t0):
    return pl.BlockSpec((1, geo.tm, width), lambda b, t: (b, t + t0, col))


def _rope(x, cos, sin):
    w = x.shape[1]
    reps = w // LANES
    cw = jnp.concatenate([cos] * reps, axis=1)
    sw = jnp.concatenate([sin] * reps, axis=1)
    lane = lax.broadcasted_iota(jnp.int32, x.shape, 1)
    first = (lane % ROPE_AXIS_DIM) < ROPE_FREQS
    partner = jnp.where(first, pltpu.roll(x, w - ROPE_FREQS, axis=1), pltpu.roll(x, ROPE_FREQS, axis=1))
    return x * cw + partner * sw


def _attn_proj_kernel(x_ref, m_ref, g_ref, cos_ref, sin_ref, wq_ref, wk_ref, wv_ref, q_ref, k_ref, v_ref):
    h = _modulate(x_ref[0], g_ref[0:1], m_ref[0, 0, 0:1], m_ref[0, 0, 1:2]).astype(bf16)
    cos, sin = cos_ref[...], sin_ref[...]
    q = _rope(_dot(h, wq_ref[...]) * (HEAD_DIM ** -0.5), cos, sin)
    k = _rope(_dot(h, wk_ref[...]), cos, sin)
    v = _dot(h, wv_ref[...])
    for hd in range(ATTN_HEADS):
        q_ref[0, hd] = q[:, hd * HEAD_DIM:(hd + 1) * HEAD_DIM].astype(bf16)
    for hd in range(ATTN_KV_HEADS):
        k_ref[0, hd] = k[:, hd * HEAD_DIM:(hd + 1) * HEAD_DIM].astype(bf16)
        v_ref[0, hd] = v[:, hd * HEAD_DIM:(hd + 1) * HEAD_DIM].astype(bf16)


def _attn_core_kernel(geo, sink_ref, q_ref, kp_ref, kc_ref, kn_ref, kx_ref, vp_ref, vc_ref, vn_ref, vx_ref, o_ref):
    j = pl.program_id(1)
    qb = pl.program_id(2)
    blk = ATTN_BLOCK
    n_loc = 3 * blk
    n_keys = n_loc + geo.n_ctx
    rows = ATTN_GROUP * blk
    first_lat = geo.n_ctx // blk
    n_blocks = geo.t // blk
    q4 = q_ref[0].reshape(rows, HEAD_DIM)
    keys = jnp.concatenate([kp_ref[0, 0], kc_ref[0, 0], kn_ref[0, 0], kx_ref[0, 0]], axis=0)
    vals = jnp.concatenate([vp_ref[0, 0], vc_ref[0, 0], vn_ref[0, 0], vx_ref[0, 0]], axis=0)
    s = lax.dot_general(q4, keys, NT, preferred_element_type=f32)
    row = lax.broadcasted_iota(jnp.int32, (rows, n_keys), 0)
    col = lax.broadcasted_iota(jnp.int32, (rows, n_keys), 1)
    rel = col - blk - (row % blk)
    kblk = qb - 1 + col // blk
    ok_local = (jnp.abs(rel) <= WINDOW) & (qb >= first_lat) & (kblk >= first_lat) & (kblk < n_blocks)
    s = jnp.where((col >= n_loc) | ok_local, s, NEG_INF)
    grp = lax.broadcasted_iota(jnp.int32, (rows, 1), 0) // blk
    sk = jnp.zeros((rows, 1), f32)
    for g in range(ATTN_GROUP):
        sk = jnp.where(grp == g, sink_ref[ATTN_GROUP * j + g], sk)
    m = jnp.maximum(jnp.max(s, axis=-1, keepdims=True), sk)
    e = jnp.exp(s - m)
    p = e / (jnp.sum(e, axis=-1, keepdims=True) + jnp.exp(sk - m))
    o4 = _dot(p.astype(bf16), vals)
    o_ref[0] = jnp.concatenate([o4[g * blk:(g + 1) * blk] for g in range(ATTN_GROUP)], axis=1).astype(bf16)


def _rope_tables(geo):
    inv_freq = ROPE_BASE ** (-jnp.arange(ROPE_FREQS, dtype=f32) * 2.0 / ROPE_AXIS_DIM)
    pos = jnp.arange(geo.n_lat)
    row = (pos // GRID_W).astype(f32)
    col = (pos % GRID_W).astype(f32)
    ang = jnp.stack([row[:, None] * inv_freq, col[:, None] * inv_freq], axis=1)
    cos = jnp.cos(ang)
    sin = jnp.sin(ang)
    cos_h = jnp.concatenate([cos, cos], axis=2).reshape(geo.n_lat, HEAD_DIM)
    sin_h = jnp.concatenate([-sin, sin], axis=2).reshape(geo.n_lat, HEAD_DIM)
    cos_t = jnp.concatenate([jnp.ones((geo.n_ctx, HEAD_DIM), f32), cos_h], axis=0)
    sin_t = jnp.concatenate([jnp.zeros((geo.n_ctx, HEAD_DIM), f32), sin_h], axis=0)
    return jnp.tile(cos_t, (1, 2)), jnp.tile(sin_t, (1, 2))


def _attn_layer(geo, x, mods, gains, w_qkv, w_o, sink, w_in, w_out, skip_ctx):
    b, t, tm = geo.batch, geo.t, geo.tm
    q_cols = ATTN_HEADS * HEAD_DIM
    kv_cols = ATTN_KV_HEADS * HEAD_DIM
    wb = w_qkv.astype(bf16)
    cos_t, sin_t = _rope_tables(geo)
    tab_spec = pl.BlockSpec((tm, LANES), lambda bb, tt: (tt, 0))
    q, k, v = pl.pallas_call(
        _attn_proj_kernel,
        grid=(b, geo.nt),
        in_specs=[geo.x_spec(), geo.mod_spec(), _const_spec((8, D_MODEL)), tab_spec, tab_spec,
                  _const_spec((D_MODEL, q_cols)), _const_spec((D_MODEL, kv_cols)), _const_spec((D_MODEL, kv_cols))],
        out_specs=[pl.BlockSpec((1, ATTN_HEADS, tm, HEAD_DIM), lambda bb, tt: (bb, 0, tt, 0)),
                   pl.BlockSpec((1, ATTN_KV_HEADS, tm, HEAD_DIM), lambda bb, tt: (bb, 0, tt, 0)),
                   pl.BlockSpec((1, ATTN_KV_HEADS, tm, HEAD_DIM), lambda bb, tt: (bb, 0, tt, 0))],
        out_shape=[jax.ShapeDtypeStruct((b, ATTN_HEADS, t, HEAD_DIM), bf16),
                   jax.ShapeDtypeStruct((b, ATTN_KV_HEADS, t, HEAD_DIM), bf16),
                   jax.ShapeDtypeStruct((b, ATTN_KV_HEADS, t, HEAD_DIM), bf16)],
        compiler_params=_params(2),
        name="attn_proj",
    )(x, mods, gains, cos_t, sin_t, wb[:, :q_cols], wb[:, q_cols:q_cols + kv_cols], wb[:, q_cols + kv_cols:])

    blk = ATTN_BLOCK
    n_blocks = t // blk

    def kv_spec(off):
        return pl.BlockSpec((1, 1, blk, HEAD_DIM),
                            lambda bb, j, qb: (bb, j, jnp.clip(qb + off, 0, n_blocks - 1), 0))

    ctx_spec = pl.BlockSpec((1, 1, geo.n_ctx, HEAD_DIM), lambda bb, j, qb: (bb, j, 0, 0))
    o = pl.pallas_call(
        functools.partial(_attn_core_kernel, geo),
        grid=(b, ATTN_KV_HEADS, n_blocks),
        in_specs=[pl.BlockSpec(memory_space=pltpu.SMEM),
                  pl.BlockSpec((1, ATTN_GROUP, blk, HEAD_DIM), lambda bb, j, qb: (bb, j, qb, 0)),
                  kv_spec(-1), kv_spec(0), kv_spec(1), ctx_spec,
                  kv_spec(-1), kv_spec(0), kv_spec(1), ctx_spec],
        out_specs=pl.BlockSpec((1, blk, ATTN_GROUP * HEAD_DIM), lambda bb, j, qb: (bb, qb, j)),
        out_shape=jax.ShapeDtypeStruct((b, t, q_cols), bf16),
        compiler_params=_params(3),
        name="attn_core",
    )(sink, q, k, k, k, k, v, v, v, v)

    t0 = geo.ctx_tiles if skip_ctx else 0
    return _post_mlp(geo, x, mods, gains, _pre_attn, [o], [_tile_spec(geo, D_MODEL, 0, t0)],
                     w_o.astype(bf16), w_in, w_out, skip_ctx, "attn_post")


def _gated_chunk(q, k, v, g, st_ref, reverse, heads, dk, dv):
    n = q.shape[0]
    b = _cumsum_rows(g, reverse)
    btot = b[0:1] if reverse else b[n - 1:n]
    q_dec = (q * jnp.exp(b)).astype(bf16)
    k_inv = (k * jnp.exp(-b)).astype(bf16)
    k_end = (k * jnp.exp(btot - b)).astype(bf16)
    dec = jnp.exp(btot)
    vb = v.astype(bf16)
    ri = lax.broadcasted_iota(jnp.int32, (n, n), 0)
    ci = lax.broadcasted_iota(jnp.int32, (n, n), 1)
    tri = (ci >= ri) if reverse else (ci <= ri)
    outs = []
    for h in range(heads):
        ks = slice(h * dk, (h + 1) * dk)
        vs = slice(h * dv, (h + 1) * dv)
        a = lax.dot_general(q_dec[:, ks], k_inv[:, ks], NT, preferred_element_type=f32)
        a = jnp.where(tri, a, 0.0).astype(bf16)
        st = st_ref[h]
        outs.append(_dot(a, vb[:, vs])
                    + lax.dot_general(q_dec[:, ks], st.astype(bf16), NT, preferred_element_type=f32))
        st_ref[h] = st * dec[:, ks] + lax.dot_general(vb[:, vs], k_end[:, ks], TN, preferred_element_type=f32)
    return jnp.concatenate(outs, axis=1)


def _gla_gate(zd_ref, wu_ref, bias_ref, d):
    zg = _dot(zd_ref[0].astype(bf16), wu_ref[d]) + bias_ref[d:d + 1]
    return (jnp.minimum(zg, 0.0) - jnp.log1p(jnp.exp(-jnp.abs(zg)))) * (1.0 / GLA_TAU)


def _gla_scan_kernel(qkf_ref, vf_ref, zdf_ref, qkb_ref, vb_ref, zdb_ref, wu_ref, bias_ref,
                     of_ref, ob_ref, sf_ref, sb_ref):
    @pl.when(pl.program_id(1) == 0)
    def _():
        sf_ref[...] = jnp.zeros_like(sf_ref)
        sb_ref[...] = jnp.zeros_like(sb_ref)

    for d, (qk_ref, v_ref, zd_ref, o_ref, s_ref) in enumerate(
            ((qkf_ref, vf_ref, zdf_ref, of_ref, sf_ref), (qkb_ref, vb_ref, zdb_ref, ob_ref, sb_ref))):
        qk = qk_ref[0]
        q = qk[:, :GLA_KEY_DIM] * (GLA_DK ** -0.5)
        k = qk[:, GLA_KEY_DIM:]
        g = _gla_gate(zd_ref, wu_ref, bias_ref, d)
        o_ref[0] = _gated_chunk(q, k, v_ref[0], g, s_ref, d == 1, GLA_HEADS, GLA_DK, GLA_DV)


def _hgrn_scan_kernel(qf_ref, if_ref, zff_ref, qb_ref, ib_ref, zfb_ref, lb_ref, of_ref, ob_ref, sf_ref, sb_ref):
    @pl.when(pl.program_id(1) == 0)
    def _():
        sf_ref[...] = jnp.zeros_like(sf_ref)
        sb_ref[...] = jnp.zeros_like(sb_ref)

    lb = lb_ref[...]
    for d, (q_ref, i_ref, zf_ref, o_ref, s_ref) in enumerate(
            ((qf_ref, if_ref, zff_ref, of_ref, sf_ref), (qb_ref, ib_ref, zfb_ref, ob_ref, sb_ref))):
        f = lb + (1.0 - lb) * _sigmoid(zf_ref[0])
        o_ref[0] = _gated_chunk(_silu(q_ref[0]), 1.0 - f, i_ref[0], jnp.log(f), s_ref, d == 1,
                                HGRN_HEADS, HGRN_EXPAND, HGRN_EXPAND)


def _chunk_spec(geo, width, col, reverse):
    if reverse:
        return pl.BlockSpec((1, SCAN_CHUNK, width), lambda b, i: (b, geo.bwd_chunk(i), col))
    return pl.BlockSpec((1, SCAN_CHUNK, width), lambda b, i: (b, i, col))


def _scan_call(geo, body, args, specs, heads, dk, dv, name):
    out_sds = jax.ShapeDtypeStruct((geo.batch, geo.t, heads * dv), f32)
    return pl.pallas_call(
        body,
        grid=(geo.batch, geo.nc),
        in_specs=specs,
        out_specs=[_chunk_spec(geo, heads * dv, 0, False), _chunk_spec(geo, heads * dv, 0, True)],
        out_shape=[out_sds, out_sds],
        scratch_shapes=[pltpu.VMEM((heads, dv, dk), f32), pltpu.VMEM((heads, dv, dk), f32)],
        compiler_params=_params(2),
        name=name,
    )(*args)


def _gla_layer(geo, x, mods, gains, w_in_p, w_gd, w_gu, g_bias, g_norm, w_o, w_in, w_out, skip_ctx):
    r = GLA_GATE_RANK
    n_z = 2 * GLA_KEY_DIM + 2 * D_MODEL
    w_all = jnp.concatenate([w_in_p, w_gd[0], w_gd[1], jnp.zeros((D_MODEL, LANES - 2 * r), f32)], axis=1).astype(bf16)
    z = _mod_linear(geo, x, mods, gains, w_all, "gla_proj")
    wu = jnp.zeros((2, LANES, GLA_KEY_DIM), f32).at[0, :r].set(w_gu[0]).at[1, r:2 * r].set(w_gu[1]).astype(bf16)
    zd_col = n_z // LANES
    specs = []
    for rev in (False, True):
        specs += [_chunk_spec(geo, 2 * GLA_KEY_DIM, 0, rev), _chunk_spec(geo, D_MODEL, 1, rev),
                  _chunk_spec(geo, LANES, zd_col, rev)]
    specs += [_const_spec((2, LANES, GLA_KEY_DIM)), _const_spec((2, GLA_KEY_DIM))]
    o_f, o_b = _scan_call(geo, _gla_scan_kernel, [z, z, z, z, z, z, wu, g_bias], specs,
                          GLA_HEADS, GLA_DK, GLA_DV, "gla_scan")
    t0 = geo.ctx_tiles if skip_ctx else 0
    mix_specs = [_tile_spec(geo, D_MODEL, 0, t0), _tile_spec(geo, D_MODEL, 0, t0), _tile_spec(geo, D_MODEL, 2, t0),
                 _const_spec((1, GLA_DV))]
    return _post_mlp(geo, x, mods, gains, functools.partial(_pre_scan, GLA_DV),
                     [o_f, o_b, z, g_norm.reshape(1, GLA_DV)], mix_specs,
                     w_o.astype(bf16), w_in, w_out, skip_ctx, "gla_post")


def _hgrn_layer(geo, x, mods, gains, w_in_p, w_f, lower_bound, g_norm, w_o, w_in, w_out, skip_ctx):
    w_all = jnp.concatenate([w_in_p, w_f[0], w_f[1]], axis=1).astype(bf16)
    z = _mod_linear(geo, x, mods, gains, w_all, "hgrn_proj")
    specs = []
    for rev in (False, True):
        specs += [_chunk_spec(geo, D_MODEL, 0, rev), _chunk_spec(geo, D_MODEL, 1, rev),
                  _chunk_spec(geo, D_MODEL, 4 if rev else 3, rev)]
    specs += [_const_spec((1, D_MODEL))]
    o_f, o_b = _scan_call(geo, _hgrn_scan_kernel, [z, z, z, z, z, z, lower_bound.reshape(1, D_MODEL)], specs,
                          HGRN_HEADS, HGRN_EXPAND, HGRN_EXPAND, "hgrn_scan")
    t0 = geo.ctx_tiles if skip_ctx else 0
    mix_specs = [_tile_spec(geo, D_MODEL, 0, t0), _tile_spec(geo, D_MODEL, 0, t0), _tile_spec(geo, D_MODEL, 2, t0),
                 _const_spec((1, HGRN_EXPAND))]
    return _post_mlp(geo, x, mods, gains, functools.partial(_pre_scan, HGRN_EXPAND),
                     [o_f, o_b, z, g_norm.reshape(1, HGRN_EXPAND)], mix_specs,
                     w_o.astype(bf16), w_in, w_out, skip_ctx, "hgrn_post")


def _rwkv_proj_kernel(geo, x_ref, xp_ref, xn_ref, m_ref, g_ref, mix_ref, vec_ref, wrkv_ref, wdn_ref, wup_ref,
                      aup_ref, gup_ref, r_ref, v_ref, gate_ref, kk_ref, lw0_ref, lw1_ref, a0_ref, a1_ref,
                      k0_ref, k1_ref):
    t = pl.program_id(1)
    tm = geo.tm
    gain, shift, scale = g_ref[0:1], m_ref[0, 0, 0:1], m_ref[0, 0, 1:2]
    h = _modulate(x_ref[0], gain, shift, scale)
    seg_first = (t == 0) | (t == geo.ctx_tiles)
    seg_last = (t == geo.ctx_tiles - 1) | (t == geo.nt - 1)
    h_prev = jnp.where(seg_first, 0.0, _modulate(xp_ref[0], gain, shift, scale)[7:8])
    h_next = jnp.where(seg_last, 0.0, _modulate(xn_ref[0], gain, shift, scale)[0:1])
    row = lax.broadcasted_iota(jnp.int32, h.shape, 0)
    up = jnp.where(row == 0, h_prev, pltpu.roll(h, 1, axis=0))
    dn = jnp.where(row == tm - 1, h_next, pltpu.roll(h, tm - 1, axis=0))
    dx = 0.5 * (up + dn) - h

    def mixed(n):
        return (h + dx * mix_ref[n:n + 1]).astype(bf16)

    r = _dot(mixed(0), wrkv_ref[0])
    k = _dot(mixed(1), wrkv_ref[1])
    v = _dot(mixed(2), wrkv_ref[2])
    dw = jnp.tanh(_dot(mixed(3), wdn_ref[:, 0:LANES])).astype(bf16)
    da = _dot(mixed(4), wdn_ref[:, LANES:2 * LANES]).astype(bf16)
    dg = _sigmoid(_dot(mixed(5), wdn_ref[:, 2 * LANES:3 * LANES])).astype(bf16)
    r_ref[0] = r
    v_ref[0] = v
    gate_ref[0] = _dot(dg, gup_ref[...])
    kk = k * vec_ref[4:5]
    kk_ref[0] = kk / jnp.maximum(jnp.sqrt(_seg_sum(kk * kk, RWKV_HEAD_SIZE)), L2_EPS)
    for d, (lw_ref, a_ref, kd_ref) in enumerate(((lw0_ref, a0_ref, k0_ref), (lw1_ref, a1_ref, k1_ref))):
        w_log = -_softplus(-(vec_ref[d:d + 1] + _dot(dw, wup_ref[d]))) - 0.5
        lw_ref[0] = -jnp.exp(w_log)
        a = _sigmoid(vec_ref[2 + d:3 + d] + _dot(da, aup_ref[d]))
        a_ref[0] = a
        kd_ref[0] = k * (1.0 + (a - 1.0) * vec_ref[5:6])


def _hi_lo(x):
    hi = x.astype(bf16)
    return hi, (x - hi.astype(f32)).astype(bf16)


def _rwkv_chunk_units(units):
    n = SCAN_CHUNK
    hs = RWKV_HEAD_SIZE
    lane = lax.broadcasted_iota(jnp.int32, (n, RWKV_PAIR), 1)
    t_i = lax.broadcasted_iota(jnp.int32, (n, RWKV_PAIR), 0)
    s_i = lane % hs
    head0 = lane < hs
    same_head = (lax.broadcasted_iota(jnp.int32, (RWKV_PAIR, RWKV_PAIR), 0) // hs
                 == lax.broadcasted_iota(jnp.int32, (RWKV_PAIR, RWKV_PAIR), 1) // hs)

    def block_diag(x):
        return jnp.concatenate([jnp.where(head0, x, 0.0), jnp.where(head0, 0.0, x)], axis=0)

    st = []
    for r, kd, v, kk, a, lw, ht, reverse in units:
        c = _cumsum_rows(lw, reverse)
        tot = c[0:1] if reverse else c[n - 1:n]
        e_neg = jnp.exp(-c)
        e_end = jnp.exp(tot - c)
        beta = kk * a
        lhs_hi, lhs_lo = _hi_lo(jnp.concatenate([-kk * jnp.exp(c - lw), r * jnp.exp(c)], axis=0))
        rhs_hi, rhs_lo = _hi_lo(jnp.concatenate([block_diag(kd * e_neg), block_diag(beta * e_neg)], axis=0))
        v_hi, v_lo = _hi_lo(block_diag(v))
        h_hi, h_lo = _hi_lo(ht)
        st.append(dict(
            strict=(s_i > t_i) if reverse else (s_i < t_i), incl=(s_i >= t_i) if reverse else (s_i <= t_i),
            lhs3=jnp.concatenate([lhs_hi, lhs_hi, lhs_lo], axis=1), rhs3=jnp.concatenate([rhs_hi, rhs_lo, rhs_hi], axis=1),
            a_bar=lhs_hi[0:n], r3=jnp.concatenate([lhs_hi[n:], lhs_hi[n:], lhs_lo[n:]], axis=1),
            v_hi=v_hi, v3=jnp.concatenate([v_hi, v_lo, v_hi], axis=0),
            h_hi=h_hi, h3=jnp.concatenate([h_hi, h_lo, h_hi], axis=1),
            ends=jnp.concatenate([kd * e_end, beta * e_end], axis=0).astype(bf16),
            v=v, ht=ht, dec=jnp.exp(tot)))
    for s in st:
        g = lax.dot_general(s["lhs3"], s["rhs3"], NT, preferred_element_type=f32)
        s["a_ak"] = jnp.where(s["strict"], g[0:n, 0:RWKV_PAIR], 0.0).astype(bf16)
        s["n_bd"] = block_diag(jnp.where(s["strict"], g[0:n, RWKV_PAIR:], 0.0))
        qk_hi, qk_lo = _hi_lo(jnp.where(s["incl"], g[n:, 0:RWKV_PAIR], 0.0))
        s["q3"] = jnp.concatenate([qk_hi, qk_hi, qk_lo], axis=1)
        s["q_b"] = jnp.where(s["incl"], g[n:, RWKV_PAIR:], 0.0).astype(bf16)
    ri = lax.broadcasted_iota(jnp.int32, (RWKV_PAIR, RWKV_PAIR), 0)
    ci = lax.broadcasted_iota(jnp.int32, (RWKV_PAIR, RWKV_PAIR), 1)

    def cross_blocks(b):
        return (ri // (2 * b) == ci // (2 * b)) & (ri // b != ci // b)

    for s in st:
        s["inv"] = jnp.where(ri == ci, 1.0, jnp.where(cross_blocks(1), s["n_bd"], 0.0))
    b = 2
    while 2 * b < n:
        for s in st:
            cb = jnp.where(cross_blocks(b), s["n_bd"], 0.0).astype(bf16)
            s["pc"] = _dot(s["inv"].astype(bf16), cb)
        for s in st:
            s["inv"] = s["inv"] + _dot(s["pc"].astype(bf16), s["inv"].astype(bf16))
        b *= 2
    for s in st:
        x = lax.dot_general(s["a_bar"], s["h_hi"], NT, preferred_element_type=f32) + _dot(s["a_ak"], s["v_hi"])
        s["inv"] = s["inv"].astype(bf16)
        s["w"] = _dot(s["inv"], block_diag(x).astype(bf16))
    for s in st:
        s["cw"] = _dot(jnp.where(cross_blocks(n // 2), s["n_bd"], 0.0).astype(bf16), s["w"].astype(bf16))
    for s in st:
        s["x_bd"] = s["w"] + _dot(s["inv"], s["cw"].astype(bf16))
    outs = []
    for s in st:
        x_bd = s["x_bd"]
        y = (lax.dot_general(s["r3"], s["h3"], NT, preferred_element_type=f32)
             + _dot(jnp.concatenate([s["q3"], s["q_b"]], axis=1),
                    jnp.concatenate([s["v3"], x_bd.astype(bf16)], axis=0)))
        u = x_bd[0:n] + x_bd[n:]
        upd = lax.dot_general(jnp.concatenate([s["v"], u], axis=0).astype(bf16), s["ends"], TN,
                              preferred_element_type=f32)
        outs.append((y, s["ht"] * s["dec"] + jnp.where(same_head, upd, 0.0)))
    return outs


def _rwkv_scan_kernel(pairs, *refs):
    fwd, bwd = refs[0:6], refs[6:12]
    yf_ref, yb_ref, hf_ref, hb_ref = refs[12:]

    @pl.when(pl.program_id(2) == 0)
    def _():
        hf_ref[...] = jnp.zeros_like(hf_ref)
        hb_ref[...] = jnp.zeros_like(hb_ref)

    units, dests = [], []
    for ins, y_ref, h_ref, reverse in ((fwd, yf_ref, hf_ref, False), (bwd, yb_ref, hb_ref, True)):
        r_ref, v_ref, kk_ref, lw_ref, a_ref, kd_ref = ins
        for p in range(pairs):
            sl = slice(p * RWKV_PAIR, (p + 1) * RWKV_PAIR)
            units.append((r_ref[0, :, sl], kd_ref[0, :, sl], v_ref[0, :, sl], kk_ref[0, :, sl],
                          a_ref[0, :, sl], lw_ref[0, :, sl], h_ref[p], reverse))
            dests.append((y_ref, h_ref, p, sl))
    for (y, h_new), (y_ref, h_ref, p, sl) in zip(_rwkv_chunk_units(units), dests):
        y_ref[0, :, sl] = y
        h_ref[p] = h_new


RWKV_PAIRS_PER_STEP = 8


def _rwkv_layer(geo, x, mods, gains, mix, w_rkv, w0, w_down, w_up, a0, a_down, a_up, g_down, g_up, k_k, k_a,
                r_k, ln_w, ln_b, w_o, w_in, w_out, skip_ctx):
    b, t, tm = geo.batch, geo.t, geo.tm
    d = D_MODEL
    rank = w_down.shape[-1]
    assert 2 * rank == LANES and a_down.shape[-1] == rank and g_down.shape[-1] == LANES
    mix8 = jnp.pad(mix, ((0, 2), (0, 0)))
    vec = jnp.stack([w0[0], w0[1], a0[0], a0[1], k_k, k_a, jnp.zeros_like(k_k), jnp.zeros_like(k_k)])
    w_dn = jnp.concatenate([w_down[0], w_down[1], a_down[0], a_down[1], g_down], axis=1).astype(bf16)

    def padded_up(w):
        return jnp.zeros((2, LANES, d), f32).at[0, :rank].set(w[0]).at[1, rank:].set(w[1]).astype(bf16)

    halo = 8
    n_halo = t // halo
    per = tm // halo
    x_prev = pl.BlockSpec((1, halo, d), lambda bb, tt: (bb, jnp.maximum(tt * per - 1, 0), 0))
    x_next = pl.BlockSpec((1, halo, d), lambda bb, tt: (bb, jnp.minimum((tt + 1) * per, n_halo - 1), 0))
    out_spec = pl.BlockSpec((1, tm, d), lambda bb, tt: (bb, tt, 0))
    sds = jax.ShapeDtypeStruct((b, t, d), f32)
    r, v, gate, kk, lw0, lw1, a_0, a_1, k0, k1 = pl.pallas_call(
        functools.partial(_rwkv_proj_kernel, geo),
        grid=(b, geo.nt),
        in_specs=[geo.x_spec(), x_prev, x_next, geo.mod_spec(), _const_spec((8, d)), _const_spec((8, d)),
                  _const_spec((8, d)), _const_spec((3, d, d)), _const_spec((d, 3 * LANES)),
                  _const_spec((2, LANES, d)), _const_spec((2, LANES, d)), _const_spec((LANES, d))],
        out_specs=[out_spec] * 10,
        out_shape=[sds] * 10,
        compiler_params=_params(2),
        name="rwkv_proj",
    )(x, x, x, mods, gains, mix8, vec, w_rkv.astype(bf16), w_dn, padded_up(w_up), padded_up(a_up),
      g_up.astype(bf16))

    pairs = RWKV_PAIRS_PER_STEP
    width = pairs * RWKV_PAIR
    groups = d // width

    def cspec(reverse):
        if reverse:
            return pl.BlockSpec((1, SCAN_CHUNK, width), lambda bb, gg, i: (bb, geo.bwd_chunk(i), gg))
        return pl.BlockSpec((1, SCAN_CHUNK, width), lambda bb, gg, i: (bb, i, gg))

    y_f, y_b = pl.pallas_call(
        functools.partial(_rwkv_scan_kernel, pairs),
        grid=(b, groups, geo.nc),
        in_specs=[cspec(False)] * 6 + [cspec(True)] * 6,
        out_specs=[cspec(False), cspec(True)],
        out_shape=[sds, sds],
        scratch_shapes=[pltpu.VMEM((pairs, RWKV_PAIR, RWKV_PAIR), f32), pltpu.VMEM((pairs, RWKV_PAIR, RWKV_PAIR), f32)],
        compiler_params=_params(3),
        name="rwkv_scan",
    )(r, v, kk, lw0, a_0, k0, r, v, kk, lw1, a_1, k1)

    t0 = geo.ctx_tiles if skip_ctx else 0
    post_vec = jnp.stack([ln_w, ln_b, r_k.reshape(d)] + [jnp.zeros_like(ln_w)] * 5)
    mix_specs = [_tile_spec(geo, d, 0, t0)] * 7 + [_const_spec((8, d))]
    return _post_mlp(geo, x, mods, gains, _pre_rwkv, [y_f, y_b, r, k0, k1, v, gate, post_vec], mix_specs,
                     w_o.astype(bf16), w_in, w_out, skip_ctx, "rwkv_post")


def _hgrn_lower_bound(lb_param, layer):
    p = jax.nn.softmax(lb_param.astype(f32), axis=0)
    return (jnp.cumsum(p, axis=0) - p[0])[layer]


def kernel(x, c, ctx, c_ctx, w_mod, b_mod, g_pre_mix, g_post_mix, g_pre_mlp, g_post_mlp, w_mlp_in, w_mlp_out, attn_w_qkv, attn_w_o, attn_sink, gla_w_in, gla_w_gate_down, gla_w_gate_up, gla_gate_bias, gla_g_norm, gla_w_o, rwkv_mix, rwkv_w_rkv, rwkv_w0, rwkv_w_down, rwkv_w_up, rwkv_a0, rwkv_a_down, rwkv_a_up, rwkv_g_down, rwkv_g_up, rwkv_k_k, rwkv_k_a, rwkv_r_k, rwkv_ln_w, rwkv_ln_b, rwkv_w_o, hgrn_w_in, hgrn_w_f, hgrn_lb, hgrn_g_norm, hgrn_w_o):
    depth = w_mod.shape[0]
    geo = _Geom(x.shape[0], ctx.shape[1], x.shape[1])
    mods_all = _mod_vectors(c, c_ctx, w_mod, b_mod)
    xs = jnp.concatenate([ctx, x], axis=1)
    for i in range(depth):
        kind, j = i % 4, i // 4
        skip_ctx = i == depth - 1
        mods = mods_all[i]
        gains = jnp.stack([g_pre_mix[i], g_post_mix[i], g_pre_mlp[i], g_post_mlp[i]] + [jnp.zeros_like(g_pre_mix[i])] * 4)
        w_in, w_out = w_mlp_in[i].astype(bf16), w_mlp_out[i].astype(bf16)
        if kind == 0:
            xs = _attn_layer(geo, xs, mods, gains, attn_w_qkv[j], attn_w_o[j], attn_sink[j], w_in, w_out, skip_ctx)
        elif kind == 1:
            xs = _gla_layer(geo, xs, mods, gains, gla_w_in[j], gla_w_gate_down[j], gla_w_gate_up[j],
                            gla_gate_bias[j], gla_g_norm[j], gla_w_o[j], w_in, w_out, skip_ctx)
        elif kind == 2:
            xs = _rwkv_layer(geo, xs, mods, gains, rwkv_mix[j], rwkv_w_rkv[j], rwkv_w0[j], rwkv_w_down[j],
                             rwkv_w_up[j], rwkv_a0[j], rwkv_a_down[j], rwkv_a_up[j], rwkv_g_down[j], rwkv_g_up[j],
                             rwkv_k_k[j], rwkv_k_a[j], rwkv_r_k[j], rwkv_ln_w[j], rwkv_ln_b[j], rwkv_w_o[j],
                             w_in, w_out, skip_ctx)
        else:
            xs = _hgrn_layer(geo, xs, mods, gains, hgrn_w_in[j], hgrn_w_f[j], _hgrn_lower_bound(hgrn_lb, i),
                             hgrn_g_norm[j], hgrn_w_o[j], w_in, w_out, skip_ctx)
        if skip_ctx:
            return xs
    return xs[:, geo.n_ctx:]
```

```python
import functools

import jax
import jax.numpy as jnp
from jax import lax
from jax.experimental import pallas as pl
from jax.experimental.pallas import tpu as pltpu

f32 = jnp.float32
bf16 = jnp.bfloat16

D_MODEL = 1024
N_MOD = 6
MLP_HIDDEN = 4 * D_MODEL
NORM_EPS = 1e-6
NEG_INF = -1e30
GRID_W = 64

ATTN_HEADS = 16
ATTN_KV_HEADS = 4
ATTN_GROUP = ATTN_HEADS // ATTN_KV_HEADS
HEAD_DIM = 64
WINDOW = 128
ATTN_BLOCK = 128
ROPE_BASE = 10000.0
ROPE_AXIS_DIM = HEAD_DIM // 2
ROPE_FREQS = ROPE_AXIS_DIM // 2

GLA_HEADS = 4
GLA_KEY_DIM = D_MODEL // 2
GLA_DK = GLA_KEY_DIM // GLA_HEADS
GLA_DV = D_MODEL // GLA_HEADS
GLA_GATE_RANK = 16
GLA_TAU = 16.0
SCAN_CHUNK = 64

RWKV_HEAD_SIZE = 64
RWKV_LN_EPS = 64e-5
L2_EPS = 1e-12
RWKV_PAIR = 2 * RWKV_HEAD_SIZE

HGRN_EXPAND = 128
HGRN_HEADS = D_MODEL // HGRN_EXPAND

LANES = 128
MOD_ROWS = 8
VMEM_LIMIT = 56 * 1024 * 1024

NT = (((1,), (1,)), ((), ()))
TN = (((0,), (0,)), ((), ()))


def _params(n_grid):
    return pltpu.CompilerParams(dimension_semantics=("arbitrary",) * n_grid, vmem_limit_bytes=VMEM_LIMIT)


def _const_spec(shape):
    nd = len(shape)
    return pl.BlockSpec(shape, lambda *_: (0,) * nd, pipeline_mode=pl.Buffered(1))


def _dot(a, b):
    return jnp.dot(a, b, preferred_element_type=f32)


def _sigmoid(x):
    return 1.0 / (1.0 + jnp.exp(-x))


def _silu(x):
    return x * _sigmoid(x)


def _softplus(x):
    return jnp.maximum(x, 0.0) + jnp.log1p(jnp.exp(-jnp.abs(x)))


def _rms(x, gain):
    return x * lax.rsqrt(jnp.mean(x * x, axis=-1, keepdims=True) + NORM_EPS) * gain


def _modulate(x, gain, shift, scale):
    return _rms(x, gain) * (1.0 + scale) + shift


def _seg_sum(x, seg):
    r = lax.broadcasted_iota(jnp.int32, (LANES, LANES), 0) // seg
    c = lax.broadcasted_iota(jnp.int32, (LANES, LANES), 1) // seg
    ones_bd = (r == c).astype(bf16)
    hi = x.astype(bf16)
    lo = (x - hi.astype(f32)).astype(bf16)
    outs = []
    for j in range(x.shape[1] // LANES):
        sl = slice(j * LANES, (j + 1) * LANES)
        outs.append(_dot(hi[:, sl], ones_bd) + _dot(lo[:, sl], ones_bd))
    return jnp.concatenate(outs, axis=1)


def _cumsum_rows(g, reverse):
    n = g.shape[0]
    row = lax.broadcasted_iota(jnp.int32, g.shape, 0)
    x = g
    s = 1
    while s < n:
        if reverse:
            x = x + jnp.where(row < n - s, pltpu.roll(x, n - s, axis=0), 0.0)
        else:
            x = x + jnp.where(row >= s, pltpu.roll(x, s, axis=0), 0.0)
        s *= 2
    return x


def _mod_kernel(c_ref, w_ref, b_ref, o_ref):
    o_ref[0] = _dot(_silu(c_ref[...]).astype(bf16), w_ref[0]) + b_ref[0]


def _mod_vectors(c, c_ctx, w_mod, b_mod):
    depth, d, _ = w_mod.shape
    batch = c.shape[0]
    rows = -(-(batch + 1) // 8) * 8
    cc = jnp.zeros((rows, d), f32).at[:batch].set(c).at[batch].set(c_ctx)
    out = pl.pallas_call(
        _mod_kernel,
        grid=(depth, N_MOD),
        in_specs=[
            pl.BlockSpec((rows, d), lambda i, j: (0, 0)),
            pl.BlockSpec((1, d, d), lambda i, j: (i, 0, j)),
            pl.BlockSpec((1, 1, d), lambda i, j: (i, 0, j)),
        ],
        out_specs=pl.BlockSpec((1, rows, d), lambda i, j: (i, 0, j)),
        out_shape=jax.ShapeDtypeStruct((depth, rows, N_MOD * d), f32),
        compiler_params=_params(2),
        name="mod_vectors",
    )(cc, w_mod.astype(bf16), b_mod.reshape(depth, 1, N_MOD * d))
    lat = out[:, :batch].reshape(depth, batch, 1, N_MOD, d)
    con = jnp.broadcast_to(out[:, batch].reshape(depth, 1, 1, N_MOD, d), lat.shape)
    mods = jnp.concatenate([con, lat], axis=2)
    return jnp.pad(mods, ((0, 0), (0, 0), (0, 0), (0, MOD_ROWS - N_MOD), (0, 0)))


class _Geom:
    def __init__(self, batch, n_ctx, n_lat):
        self.batch, self.n_ctx, self.n_lat = batch, n_ctx, n_lat
        self.t = n_ctx + n_lat
        self.tm = min(256, n_ctx)
        assert n_ctx % self.tm == 0 and n_lat % self.tm == 0
        assert n_ctx % ATTN_BLOCK == 0 and n_lat % ATTN_BLOCK == 0 and n_lat % GRID_W == 0
        self.nt = self.t // self.tm
        self.ctx_tiles = n_ctx // self.tm
        self.nc = self.t // SCAN_CHUNK
        self.ctx_chunks = n_ctx // SCAN_CHUNK

    def x_spec(self, t0=0):
        return pl.BlockSpec((1, self.tm, D_MODEL), lambda b, t: (b, t + t0, 0))

    def mod_spec(self, t0=0):
        ct = self.ctx_tiles
        return pl.BlockSpec((1, 1, MOD_ROWS, D_MODEL), lambda b, t: (b, ((t + t0) >= ct).astype(jnp.int32), 0, 0))

    def bwd_chunk(self, i):
        cc = self.ctx_chunks
        return jnp.where(i < cc, cc - 1 - i, self.nc - 1 + cc - i)


def _lin_kernel(x_ref, m_ref, g_ref, w_ref, o_ref):
    h = _modulate(x_ref[0], g_ref[0:1], m_ref[0, 0, 0:1], m_ref[0, 0, 1:2])
    o_ref[0] = _dot(h.astype(bf16), w_ref[...])


def _mod_linear(geo, x, mods, gains, w, name):
    n = w.shape[1]
    return pl.pallas_call(
        _lin_kernel,
        grid=(geo.batch, geo.nt),
        in_specs=[geo.x_spec(), geo.mod_spec(), _const_spec((8, D_MODEL)), _const_spec((D_MODEL, n))],
        out_specs=pl.BlockSpec((1, geo.tm, n), lambda b, t: (b, t, 0)),
        out_shape=jax.ShapeDtypeStruct((geo.batch, geo.t, n), f32),
        compiler_params=_params(2),
        name=name,
    )(x, mods, gains, w)


def _head_rms(o, gain, width):
    outs = []
    for h in range(o.shape[1] // width):
        oh = o[:, h * width:(h + 1) * width]
        outs.append(oh * lax.rsqrt(jnp.mean(oh * oh, axis=-1, keepdims=True) + NORM_EPS) * gain)
    return jnp.concatenate(outs, axis=1)


def _pre_attn(refs):
    (o_ref,) = refs
    return o_ref[0]


def _pre_scan(width, refs):
    of_ref, ob_ref, gate_ref, gn_ref = refs
    o = _head_rms(of_ref[0] + ob_ref[0], gn_ref[...], width)
    return (o * _silu(gate_ref[0])).astype(bf16)


def _pre_rwkv(refs):
    yf_ref, yb_ref, r_ref, k0_ref, k1_ref, v_ref, g_ref, vec_ref = refs
    y = yf_ref[0] + yb_ref[0]
    inv_n = 1.0 / RWKV_HEAD_SIZE
    mu = _seg_sum(y, RWKV_HEAD_SIZE) * inv_n
    dlt = y - mu
    var = _seg_sum(dlt * dlt, RWKV_HEAD_SIZE) * inv_n
    yn = dlt * lax.rsqrt(var + RWKV_LN_EPS) * vec_ref[0:1] + vec_ref[1:2]
    bonus = _seg_sum(r_ref[0] * (0.5 * (k0_ref[0] + k1_ref[0])) * vec_ref[2:3], RWKV_HEAD_SIZE) * v_ref[0]
    return ((yn + bonus) * g_ref[0]).astype(bf16)


def _post_kernel(pre, n_mix, x_ref, m_ref, g_ref, *rest):
    mix_refs = rest[:n_mix]
    wo_ref, win_ref, wout_ref, out_ref = rest[n_mix:]
    o = pre(mix_refs)
    y = _dot(o, wo_ref[...])
    x1 = x_ref[0] + m_ref[0, 0, 2:3] * _rms(y, g_ref[1:2])
    h2 = _modulate(x1, g_ref[2:3], m_ref[0, 0, 3:4], m_ref[0, 0, 4:5]).astype(bf16)
    acc = jnp.zeros_like(x1)
    for c in range(MLP_HIDDEN // D_MODEL):
        u = jnp.square(jnp.maximum(_dot(h2, win_ref[:, c * D_MODEL:(c + 1) * D_MODEL]), 0.0))
        acc = acc + _dot(u.astype(bf16), wout_ref[c * D_MODEL:(c + 1) * D_MODEL, :])
    out_ref[0] = x1 + m_ref[0, 0, 5:6] * _rms(acc, g_ref[3:4])


def _post_mlp(geo, x, mods, gains, pre, mix_args, mix_specs, w_o, w_in, w_out, skip_ctx, name):
    t0 = geo.ctx_tiles if skip_ctx else 0
    return pl.pallas_call(
        functools.partial(_post_kernel, pre, len(mix_args)),
        grid=(geo.batch, geo.nt - t0),
        in_specs=[geo.x_spec(t0), geo.mod_spec(t0), _const_spec((8, D_MODEL))] + mix_specs + [
            _const_spec((D_MODEL, D_MODEL)), _const_spec((D_MODEL, MLP_HIDDEN)), _const_spec((MLP_HIDDEN, D_MODEL))],
        out_specs=pl.BlockSpec((1, geo.tm, D_MODEL), lambda b, t: (b, t, 0)),
        out_shape=jax.ShapeDtypeStruct((geo.batch, geo.t - t0 * geo.tm, D_MODEL), f32),
        compiler_params=_params(2),
        name=name,
    )(x, mods, gains, *mix_args, w_o, w_in, w_out)


def _tile_spec(geo, width, col, t0):
    return pl.BlockSpec((1, geo.tm, width), lambda b, t: (b, t + t0, col))


def _rope(x, cos, sin):
    w = x.shape[1]
    reps = w // LANES
    cw = jnp.concatenate([cos] * reps, axis=1)
    sw = jnp.concatenate([sin] * reps, axis=1)
    lane = lax.broadcasted_iota(jnp.int32, x.shape, 1)
    first = (lane % ROPE_AXIS_DIM) < ROPE_FREQS
    partner = jnp.where(first, pltpu.roll(x, w - ROPE_FREQS, axis=1), pltpu.roll(x, ROPE_FREQS, axis=1))
    return x * cw + partner * sw


def _attn_proj_kernel(x_ref, m_ref, g_ref, cos_ref, sin_ref, wq_ref, wk_ref, wv_ref, q_ref, k_ref, v_ref):
    h = _modulate(x_ref[0], g_ref[0:1], m_ref[0, 0, 0:1], m_ref[0, 0, 1:2]).astype(bf16)
    cos, sin = cos_ref[...], sin_ref[...]
    q = _rope(_dot(h, wq_ref[...]) * (HEAD_DIM ** -0.5), cos, sin)
    k = _rope(_dot(h, wk_ref[...]), cos, sin)
    v = _dot(h, wv_ref[...])
    for hd in range(ATTN_HEADS):
        q_ref[0, hd] = q[:, hd * HEAD_DIM:(hd + 1) * HEAD_DIM].astype(bf16)
    for hd in range(ATTN_KV_HEADS):
        k_ref[0, hd] = k[:, hd * HEAD_DIM:(hd + 1) * HEAD_DIM].astype(bf16)
        v_ref[0, hd] = v[:, hd * HEAD_DIM:(hd + 1) * HEAD_DIM].astype(bf16)


def _attn_core_kernel(geo, sink_ref, q_ref, kp_ref, kc_ref, kn_ref, kx_ref, vp_ref, vc_ref, vn_ref, vx_ref, o_ref):
    j = pl.program_id(1)
    qb = pl.program_id(2)
    blk = ATTN_BLOCK
    n_loc = 3 * blk
    n_keys = n_loc + geo.n_ctx
    rows = ATTN_GROUP * blk
    first_lat = geo.n_ctx // blk
    n_blocks = geo.t // blk
    q4 = q_ref[0].reshape(rows, HEAD_DIM)
    keys = jnp.concatenate([kp_ref[0, 0], kc_ref[0, 0], kn_ref[0, 0], kx_ref[0, 0]], axis=0)
    vals = jnp.concatenate([vp_ref[0, 0], vc_ref[0, 0], vn_ref[0, 0], vx_ref[0, 0]], axis=0)
    s = lax.dot_general(q4, keys, NT, preferred_element_type=f32)
    row = lax.broadcasted_iota(jnp.int32, (blk, n_keys), 0)
    col = lax.broadcasted_iota(jnp.int32, (blk, n_keys), 1)
    kblk = qb - 1 + col // blk
    ok_local = ((jnp.abs(col - blk - row) <= WINDOW) & (qb >= first_lat) & (kblk >= first_lat) & (kblk < n_blocks))
    ok = (col >= n_loc) | ok_local
    ps = []
    for g in range(ATTN_GROUP):
        sg = jnp.where(ok, s[g * blk:(g + 1) * blk], NEG_INF)
        sk = sink_ref[ATTN_GROUP * j + g]
        m = jnp.maximum(jnp.max(sg, axis=-1, keepdims=True), sk)
        e = jnp.exp(sg - m)
        inv = 1.0 / (jnp.sum(e, axis=-1, keepdims=True) + jnp.exp(sk - m))
        ps.append((e * inv).astype(bf16))
    o4 = _dot(jnp.concatenate(ps, axis=0), vals)
    o_ref[0] = jnp.concatenate([o4[g * blk:(g + 1) * blk] for g in range(ATTN_GROUP)], axis=1).astype(bf16)


def _rope_tables(geo):
    inv_freq = ROPE_BASE ** (-jnp.arange(ROPE_FREQS, dtype=f32) * 2.0 / ROPE_AXIS_DIM)
    pos = jnp.arange(geo.n_lat)
    row = (pos // GRID_W).astype(f32)
    col = (pos % GRID_W).astype(f32)
    ang = jnp.stack([row[:, None] * inv_freq, col[:, None] * inv_freq], axis=1)
    cos = jnp.cos(ang)
    sin = jnp.sin(ang)
    cos_h = jnp.concatenate([cos, cos], axis=2).reshape(geo.n_lat, HEAD_DIM)
    sin_h = jnp.concatenate([-sin, sin], axis=2).reshape(geo.n_lat, HEAD_DIM)
    cos_t = jnp.concatenate([jnp.ones((geo.n_ctx, HEAD_DIM), f32), cos_h], axis=0)
    sin_t = jnp.concatenate([jnp.zeros((geo.n_ctx, HEAD_DIM), f32), sin_h], axis=0)
    return jnp.tile(cos_t, (1, 2)), jnp.tile(sin_t, (1, 2))


def _attn_layer(geo, x, mods, gains, w_qkv, w_o, sink, w_in, w_out, skip_ctx):
    b, t, tm = geo.batch, geo.t, geo.tm
    q_cols = ATTN_HEADS * HEAD_DIM
    kv_cols = ATTN_KV_HEADS * HEAD_DIM
    wb = w_qkv.astype(bf16)
    cos_t, sin_t = _rope_tables(geo)
    tab_spec = pl.BlockSpec((tm, LANES), lambda bb, tt: (tt, 0))
    q, k, v = pl.pallas_call(
        _attn_proj_kernel,
        grid=(b, geo.nt),
        in_specs=[geo.x_spec(), geo.mod_spec(), _const_spec((8, D_MODEL)), tab_spec, tab_spec,
                  _const_spec((D_MODEL, q_cols)), _const_spec((D_MODEL, kv_cols)), _const_spec((D_MODEL, kv_cols))],
        out_specs=[pl.BlockSpec((1, ATTN_HEADS, tm, HEAD_DIM), lambda bb, tt: (bb, 0, tt, 0)),
                   pl.BlockSpec((1, ATTN_KV_HEADS, tm, HEAD_DIM), lambda bb, tt: (bb, 0, tt, 0)),
                   pl.BlockSpec((1, ATTN_KV_HEADS, tm, HEAD_DIM), lambda bb, tt: (bb, 0, tt, 0))],
        out_shape=[jax.ShapeDtypeStruct((b, ATTN_HEADS, t, HEAD_DIM), bf16),
                   jax.ShapeDtypeStruct((b, ATTN_KV_HEADS, t, HEAD_DIM), bf16),
                   jax.ShapeDtypeStruct((b, ATTN_KV_HEADS, t, HEAD_DIM), bf16)],
        compiler_params=_params(2),
        name="attn_proj",
    )(x, mods, gains, cos_t, sin_t, wb[:, :q_cols], wb[:, q_cols:q_cols + kv_cols], wb[:, q_cols + kv_cols:])

    blk = ATTN_BLOCK
    n_blocks = t // blk

    def kv_spec(off):
        return pl.BlockSpec((1, 1, blk, HEAD_DIM),
                            lambda bb, j, qb: (bb, j, jnp.clip(qb + off, 0, n_blocks - 1), 0))

    ctx_spec = pl.BlockSpec((1, 1, geo.n_ctx, HEAD_DIM), lambda bb, j, qb: (bb, j, 0, 0))
    o = pl.pallas_call(
        functools.partial(_attn_core_kernel, geo),
        grid=(b, ATTN_KV_HEADS, n_blocks),
        in_specs=[pl.BlockSpec(memory_space=pltpu.SMEM),
                  pl.BlockSpec((1, ATTN_GROUP, blk, HEAD_DIM), lambda bb, j, qb: (bb, j, qb, 0)),
                  kv_spec(-1), kv_spec(0), kv_spec(1), ctx_spec,
                  kv_spec(-1), kv_spec(0), kv_spec(1), ctx_spec],
        out_specs=pl.BlockSpec((1, blk, ATTN_GROUP * HEAD_DIM), lambda bb, j, qb: (bb, qb, j)),
        out_shape=jax.ShapeDtypeStruct((b, t, q_cols), bf16),
        compiler_params=_params(3),
        name="attn_core",
    )(sink, q, k, k, k, k, v, v, v, v)

    t0 = geo.ctx_tiles if skip_ctx else 0
    return _post_mlp(geo, x, mods, gains, _pre_attn, [o], [_tile_spec(geo, D_MODEL, 0, t0)],
                     w_o.astype(bf16), w_in, w_out, skip_ctx, "attn_post")


def _gated_chunk(dirs, heads, dk, dv):
    units = []
    for q, k, v, g, st_ref, reverse in dirs:
        n = q.shape[0]
        b = _cumsum_rows(g, reverse)
        btot = b[0:1] if reverse else b[n - 1:n]
        q_dec = (q * jnp.exp(b)).astype(bf16)
        k_inv = (k * jnp.exp(-b)).astype(bf16)
        k_end = (k * jnp.exp(btot - b)).astype(bf16)
        dec = jnp.exp(btot)
        vb = v.astype(bf16)
        ri = lax.broadcasted_iota(jnp.int32, (n, n), 0)
        ci = lax.broadcasted_iota(jnp.int32, (n, n), 1)
        tri = (ci >= ri) if reverse else (ci <= ri)
        for h in range(heads):
            ks = slice(h * dk, (h + 1) * dk)
            vs = slice(h * dv, (h + 1) * dv)
            units.append(dict(q=q_dec[:, ks], ki=k_inv[:, ks], ke=k_end[:, ks], v=vb[:, vs], dec=dec[:, ks],
                              tri=tri, st=st_ref[h], ref=st_ref, h=h))
    for u in units:
        a = lax.dot_general(u["q"], u["ki"], NT, preferred_element_type=f32)
        u["a"] = jnp.where(u["tri"], a, 0.0).astype(bf16)
        u["qs"] = lax.dot_general(u["q"], u["st"].astype(bf16), NT, preferred_element_type=f32)
    for u in units:
        u["o"] = _dot(u["a"], u["v"]) + u["qs"]
        u["new"] = u["st"] * u["dec"] + lax.dot_general(u["v"], u["ke"], TN, preferred_element_type=f32)
    for u in units:
        u["ref"][u["h"]] = u["new"]
    return [jnp.concatenate([u["o"] for u in units[d * heads:(d + 1) * heads]], axis=1) for d in range(len(dirs))]


def _gla_gate(zd_ref, wu_ref, bias_ref, d):
    zg = _dot(zd_ref[0].astype(bf16), wu_ref[d]) + bias_ref[d:d + 1]
    return (jnp.minimum(zg, 0.0) - jnp.log1p(jnp.exp(-jnp.abs(zg)))) * (1.0 / GLA_TAU)


def _gla_scan_kernel(qkf_ref, vf_ref, zdf_ref, qkb_ref, vb_ref, zdb_ref, wu_ref, bias_ref,
                     of_ref, ob_ref, sf_ref, sb_ref):
    @pl.when(pl.program_id(1) == 0)
    def _():
        sf_ref[...] = jnp.zeros_like(sf_ref)
        sb_ref[...] = jnp.zeros_like(sb_ref)

    dirs = []
    for d, (qk_ref, v_ref, zd_ref, s_ref) in enumerate(
            ((qkf_ref, vf_ref, zdf_ref, sf_ref), (qkb_ref, vb_ref, zdb_ref, sb_ref))):
        qk = qk_ref[0]
        q = qk[:, :GLA_KEY_DIM] * (GLA_DK ** -0.5)
        k = qk[:, GLA_KEY_DIM:]
        dirs.append((q, k, v_ref[0], _gla_gate(zd_ref, wu_ref, bias_ref, d), s_ref, d == 1))
    of_ref[0], ob_ref[0] = _gated_chunk(dirs, GLA_HEADS, GLA_DK, GLA_DV)


def _hgrn_scan_kernel(qf_ref, if_ref, zff_ref, qb_ref, ib_ref, zfb_ref, lb_ref, of_ref, ob_ref, sf_ref, sb_ref):
    @pl.when(pl.program_id(1) == 0)
    def _():
        sf_ref[...] = jnp.zeros_like(sf_ref)
        sb_ref[...] = jnp.zeros_like(sb_ref)

    lb = lb_ref[...]
    dirs = []
    for d, (q_ref, i_ref, zf_ref, s_ref) in enumerate(
            ((qf_ref, if_ref, zff_ref, sf_ref), (qb_ref, ib_ref, zfb_ref, sb_ref))):
        f = lb + (1.0 - lb) * _sigmoid(zf_ref[0])
        dirs.append((_silu(q_ref[0]), 1.0 - f, i_ref[0], jnp.log(f), s_ref, d == 1))
    of_ref[0], ob_ref[0] = _gated_chunk(dirs, HGRN_HEADS, HGRN_EXPAND, HGRN_EXPAND)


def _chunk_spec(geo, width, col, reverse):
    if reverse:
        return pl.BlockSpec((1, SCAN_CHUNK, width), lambda b, i: (b, geo.bwd_chunk(i), col))
    return pl.BlockSpec((1, SCAN_CHUNK, width), lambda b, i: (b, i, col))


def _scan_call(geo, body, args, specs, heads, dk, dv, name):
    out_sds = jax.ShapeDtypeStruct((geo.batch, geo.t, heads * dv), f32)
    return pl.pallas_call(
        body,
        grid=(geo.batch, geo.nc),
        in_specs=specs,
        out_specs=[_chunk_spec(geo, heads * dv, 0, False), _chunk_spec(geo, heads * dv, 0, True)],
        out_shape=[out_sds, out_sds],
        scratch_shapes=[pltpu.VMEM((heads, dv, dk), f32), pltpu.VMEM((heads, dv, dk), f32)],
        compiler_params=_params(2),
        name=name,
    )(*args)


def _gla_layer(geo, x, mods, gains, w_in_p, w_gd, w_gu, g_bias, g_norm, w_o, w_in, w_out, skip_ctx):
    r = GLA_GATE_RANK
    n_z = 2 * GLA_KEY_DIM + 2 * D_MODEL
    w_all = jnp.concatenate([w_in_p, w_gd[0], w_gd[1], jnp.zeros((D_MODEL, LANES - 2 * r), f32)], axis=1).astype(bf16)
    z = _mod_linear(geo, x, mods, gains, w_all, "gla_proj")
    wu = jnp.zeros((2, LANES, GLA_KEY_DIM), f32).at[0, :r].set(w_gu[0]).at[1, r:2 * r].set(w_gu[1]).astype(bf16)
    zd_col = n_z // LANES
    specs = []
    for rev in (False, True):
        specs += [_chunk_spec(geo, 2 * GLA_KEY_DIM, 0, rev), _chunk_spec(geo, D_MODEL, 1, rev),
                  _chunk_spec(geo, LANES, zd_col, rev)]
    specs += [_const_spec((2, LANES, GLA_KEY_DIM)), _const_spec((2, GLA_KEY_DIM))]
    o_f, o_b = _scan_call(geo, _gla_scan_kernel, [z, z, z, z, z, z, wu, g_bias], specs,
                          GLA_HEADS, GLA_DK, GLA_DV, "gla_scan")
    t0 = geo.ctx_tiles if skip_ctx else 0
    mix_specs = [_tile_spec(geo, D_MODEL, 0, t0), _tile_spec(geo, D_MODEL, 0, t0), _tile_spec(geo, D_MODEL, 2, t0),
                 _const_spec((1, GLA_DV))]
    return _post_mlp(geo, x, mods, gains, functools.partial(_pre_scan, GLA_DV),
                     [o_f, o_b, z, g_norm.reshape(1, GLA_DV)], mix_specs,
                     w_o.astype(bf16), w_in, w_out, skip_ctx, "gla_post")


def _hgrn_layer(geo, x, mods, gains, w_in_p, w_f, lower_bound, g_norm, w_o, w_in, w_out, skip_ctx):
    w_all = jnp.concatenate([w_in_p, w_f[0], w_f[1]], axis=1).astype(bf16)
    z = _mod_linear(geo, x, mods, gains, w_all, "hgrn_proj")
    specs = []
    for rev in (False, True):
        specs += [_chunk_spec(geo, D_MODEL, 0, rev), _chunk_spec(geo, D_MODEL, 1, rev),
                  _chunk_spec(geo, D_MODEL, 4 if rev else 3, rev)]
    specs += [_const_spec((1, D_MODEL))]
    o_f, o_b = _scan_call(geo, _hgrn_scan_kernel, [z, z, z, z, z, z, lower_bound.reshape(1, D_MODEL)], specs,
                          HGRN_HEADS, HGRN_EXPAND, HGRN_EXPAND, "hgrn_scan")
    t0 = geo.ctx_tiles if skip_ctx else 0
    mix_specs = [_tile_spec(geo, D_MODEL, 0, t0), _tile_spec(geo, D_MODEL, 0, t0), _tile_spec(geo, D_MODEL, 2, t0),
                 _const_spec((1, HGRN_EXPAND))]
    return _post_mlp(geo, x, mods, gains, functools.partial(_pre_scan, HGRN_EXPAND),
                     [o_f, o_b, z, g_norm.reshape(1, HGRN_EXPAND)], mix_specs,
                     w_o.astype(bf16), w_in, w_out, skip_ctx, "hgrn_post")


def _rwkv_proj_kernel(geo, x_ref, xp_ref, xn_ref, m_ref, g_ref, mix_ref, vec_ref, wrkv_ref, wdn_ref, wup_ref,
                      aup_ref, gup_ref, r_ref, v_ref, gate_ref, kk_ref, lw0_ref, lw1_ref, a0_ref, a1_ref,
                      k0_ref, k1_ref):
    t = pl.program_id(1)
    tm = geo.tm
    gain, shift, scale = g_ref[0:1], m_ref[0, 0, 0:1], m_ref[0, 0, 1:2]
    h = _modulate(x_ref[0], gain, shift, scale)
    seg_first = (t == 0) | (t == geo.ctx_tiles)
    seg_last = (t == geo.ctx_tiles - 1) | (t == geo.nt - 1)
    h_prev = jnp.where(seg_first, 0.0, _modulate(xp_ref[0], gain, shift, scale)[7:8])
    h_next = jnp.where(seg_last, 0.0, _modulate(xn_ref[0], gain, shift, scale)[0:1])
    row = lax.broadcasted_iota(jnp.int32, h.shape, 0)
    up = jnp.where(row == 0, h_prev, pltpu.roll(h, 1, axis=0))
    dn = jnp.where(row == tm - 1, h_next, pltpu.roll(h, tm - 1, axis=0))
    dx = 0.5 * (up + dn) - h

    def mixed(n):
        return (h + dx * mix_ref[n:n + 1]).astype(bf16)

    r = _dot(mixed(0), wrkv_ref[0])
    k = _dot(mixed(1), wrkv_ref[1])
    v = _dot(mixed(2), wrkv_ref[2])
    dw = jnp.tanh(_dot(mixed(3), wdn_ref[:, 0:LANES])).astype(bf16)
    da = _dot(mixed(4), wdn_ref[:, LANES:2 * LANES]).astype(bf16)
    dg = _sigmoid(_dot(mixed(5), wdn_ref[:, 2 * LANES:3 * LANES])).astype(bf16)
    r_ref[0] = r
    v_ref[0] = v
    gate_ref[0] = _dot(dg, gup_ref[...])
    kk = k * vec_ref[4:5]
    kk_ref[0] = kk / jnp.maximum(jnp.sqrt(_seg_sum(kk * kk, RWKV_HEAD_SIZE)), L2_EPS)
    for d, (lw_ref, a_ref, kd_ref) in enumerate(((lw0_ref, a0_ref, k0_ref), (lw1_ref, a1_ref, k1_ref))):
        w_log = -_softplus(-(vec_ref[d:d + 1] + _dot(dw, wup_ref[d]))) - 0.5
        lw_ref[0] = -jnp.exp(w_log)
        a = _sigmoid(vec_ref[2 + d:3 + d] + _dot(da, aup_ref[d]))
        a_ref[0] = a
        kd_ref[0] = k * (1.0 + (a - 1.0) * vec_ref[5:6])


def _rwkv_chunk_units(units):
    n = SCAN_CHUNK
    hs = RWKV_HEAD_SIZE
    lane = lax.broadcasted_iota(jnp.int32, (n, RWKV_PAIR), 1)
    t_i = lax.broadcasted_iota(jnp.int32, (n, RWKV_PAIR), 0)
    s_i = lane % hs
    head0 = lane < hs
    ri = lax.broadcasted_iota(jnp.int32, (RWKV_PAIR, RWKV_PAIR), 0)
    ci = lax.broadcasted_iota(jnp.int32, (RWKV_PAIR, RWKV_PAIR), 1)
    same_head = ri // hs == ci // hs
    top_rows = ri < hs

    def block_diag(x):
        return jnp.concatenate([jnp.where(head0, x, 0.0), jnp.where(head0, 0.0, x)], axis=0)

    def cross_blocks(b):
        return (t_i // (2 * b) == s_i // (2 * b)) & (t_i // b != s_i // b)

    st = []
    for r, kd, v, kk, a, lw, ht, reverse in units:
        c = _cumsum_rows(lw, reverse)
        tot = c[0:1] if reverse else c[n - 1:n]
        e_neg = jnp.exp(-c)
        e_end = jnp.exp(tot - c)
        beta = kk * a
        kb_t = jnp.concatenate([kd * e_neg, beta * e_neg], axis=0).T
        kb_sw = pltpu.roll(kb_t, hs, axis=1)
        rhs = jnp.concatenate([jnp.where(same_head, jnp.where(top_rows, kb_t, kb_sw), 0.0),
                               jnp.where(same_head, jnp.where(top_rows, kb_sw, kb_t), 0.0)], axis=1)
        st.append(dict(
            strict=(s_i > t_i) if reverse else (s_i < t_i), incl=(s_i >= t_i) if reverse else (s_i <= t_i),
            a_bar=(-kk * jnp.exp(c - lw)).astype(bf16), r_bar=(r * jnp.exp(c)).astype(bf16), rhs=rhs.astype(bf16),
            v_bd=block_diag(v).astype(bf16), h_t=ht.T.astype(bf16),
            ends=jnp.concatenate([kd * e_end, beta * e_end], axis=0).astype(bf16),
            v=v, ht=ht, dec=jnp.exp(tot)))
    for s in st:
        g = _dot(jnp.concatenate([s["a_bar"], s["r_bar"]], axis=0), s["rhs"])
        s["a_ak"] = jnp.where(s["strict"], g[0:n, 0:RWKV_PAIR], 0.0).astype(bf16)
        s["nmat"] = jnp.where(s["strict"], g[0:n, RWKV_PAIR:], 0.0)
        s["q_k"] = jnp.where(s["incl"], g[n:, 0:RWKV_PAIR], 0.0).astype(bf16)
        s["q_b"] = jnp.where(s["incl"], g[n:, RWKV_PAIR:], 0.0).astype(bf16)
    for s in st:
        s["inv"] = jnp.where(s_i == t_i, 1.0, jnp.where(cross_blocks(1), s["nmat"], 0.0))
    b = 2
    while 2 * b < n:
        for s in st:
            s["pc"] = _dot(s["inv"].astype(bf16), block_diag(jnp.where(cross_blocks(b), s["nmat"], 0.0)).astype(bf16))
        for s in st:
            s["inv"] = s["inv"] + _dot(s["pc"].astype(bf16), block_diag(s["inv"]).astype(bf16))
        b *= 2
    for s in st:
        x = _dot(jnp.concatenate([s["a_bar"], s["a_ak"]], axis=1), jnp.concatenate([s["h_t"], s["v_bd"]], axis=0))
        s["inv"] = s["inv"].astype(bf16)
        s["w"] = _dot(s["inv"], block_diag(x).astype(bf16))
    for s in st:
        s["cw"] = _dot(jnp.where(cross_blocks(n // 2), s["nmat"], 0.0).astype(bf16), block_diag(s["w"]).astype(bf16))
    outs = []
    for s in st:
        u = s["w"] + _dot(s["inv"], block_diag(s["cw"]).astype(bf16))
        y = _dot(jnp.concatenate([s["r_bar"], s["q_k"], s["q_b"]], axis=1),
                 jnp.concatenate([s["h_t"], s["v_bd"], block_diag(u).astype(bf16)], axis=0))
        upd = lax.dot_general(jnp.concatenate([s["v"], u], axis=0).astype(bf16), s["ends"], TN,
                              preferred_element_type=f32)
        outs.append((y, s["ht"] * s["dec"] + jnp.where(same_head, upd, 0.0)))
    return outs


def _rwkv_scan_kernel(rows, pairs, *refs):
    fwd, bwd = refs[0:6], refs[6:12]
    yf_ref, yb_ref, hf_ref, hb_ref = refs[12:]

    @pl.when(pl.program_id(2) == 0)
    def _():
        hf_ref[...] = jnp.zeros_like(hf_ref)
        hb_ref[...] = jnp.zeros_like(hb_ref)

    units, dests = [], []
    for ins, y_ref, h_ref, reverse in ((fwd, yf_ref, hf_ref, False), (bwd, yb_ref, hb_ref, True)):
        r_ref, v_ref, kk_ref, lw_ref, a_ref, kd_ref = ins
        for row in range(rows):
            for p in range(pairs):
                sl = slice(p * RWKV_PAIR, (p + 1) * RWKV_PAIR)
                slot = row * pairs + p
                units.append((r_ref[row, :, sl], kd_ref[row, :, sl], v_ref[row, :, sl], kk_ref[row, :, sl],
                              a_ref[row, :, sl], lw_ref[row, :, sl], h_ref[slot], reverse))
                dests.append((y_ref, h_ref, row, slot, sl))
    for (y, h_new), (y_ref, h_ref, row, slot, sl) in zip(_rwkv_chunk_units(units), dests):
        y_ref[row, :, sl] = y
        h_ref[slot] = h_new


RWKV_PAIRS_PER_STEP = 8
RWKV_ROWS_PER_STEP = 2


def _rwkv_layer(geo, x, mods, gains, mix, w_rkv, w0, w_down, w_up, a0, a_down, a_up, g_down, g_up, k_k, k_a,
                r_k, ln_w, ln_b, w_o, w_in, w_out, skip_ctx):
    b, t, tm = geo.batch, geo.t, geo.tm
    d = D_MODEL
    rank = w_down.shape[-1]
    assert 2 * rank == LANES and a_down.shape[-1] == rank and g_down.shape[-1] == LANES
    mix8 = jnp.pad(mix, ((0, 2), (0, 0)))
    vec = jnp.stack([w0[0], w0[1], a0[0], a0[1], k_k, k_a, jnp.zeros_like(k_k), jnp.zeros_like(k_k)])
    w_dn = jnp.concatenate([w_down[0], w_down[1], a_down[0], a_down[1], g_down], axis=1).astype(bf16)

    def padded_up(w):
        return jnp.zeros((2, LANES, d), f32).at[0, :rank].set(w[0]).at[1, rank:].set(w[1]).astype(bf16)

    halo = 8
    n_halo = t // halo
    per = tm // halo
    x_prev = pl.BlockSpec((1, halo, d), lambda bb, tt: (bb, jnp.maximum(tt * per - 1, 0), 0))
    x_next = pl.BlockSpec((1, halo, d), lambda bb, tt: (bb, jnp.minimum((tt + 1) * per, n_halo - 1), 0))
    out_spec = pl.BlockSpec((1, tm, d), lambda bb, tt: (bb, tt, 0))
    sds = jax.ShapeDtypeStruct((b, t, d), f32)
    r, v, gate, kk, lw0, lw1, a_0, a_1, k0, k1 = pl.pallas_call(
        functools.partial(_rwkv_proj_kernel, geo),
        grid=(b, geo.nt),
        in_specs=[geo.x_spec(), x_prev, x_next, geo.mod_spec(), _const_spec((8, d)), _const_spec((8, d)),
                  _const_spec((8, d)), _const_spec((3, d, d)), _const_spec((d, 3 * LANES)),
                  _const_spec((2, LANES, d)), _const_spec((2, LANES, d)), _const_spec((LANES, d))],
        out_specs=[out_spec] * 10,
        out_shape=[sds] * 10,
        compiler_params=_params(2),
        name="rwkv_proj",
    )(x, x, x, mods, gains, mix8, vec, w_rkv.astype(bf16), w_dn, padded_up(w_up), padded_up(a_up),
      g_up.astype(bf16))

    pairs = RWKV_PAIRS_PER_STEP
    rows = RWKV_ROWS_PER_STEP if b % RWKV_ROWS_PER_STEP == 0 else 1
    width = pairs * RWKV_PAIR
    groups = d // width

    def cspec(reverse):
        if reverse:
            return pl.BlockSpec((rows, SCAN_CHUNK, width), lambda bb, gg, i: (bb, geo.bwd_chunk(i), gg))
        return pl.BlockSpec((rows, SCAN_CHUNK, width), lambda bb, gg, i: (bb, i, gg))

    state = pltpu.VMEM((rows * pairs, RWKV_PAIR, RWKV_PAIR), f32)
    y_f, y_b = pl.pallas_call(
        functools.partial(_rwkv_scan_kernel, rows, pairs),
        grid=(b // rows, groups, geo.nc),
        in_specs=[cspec(False)] * 6 + [cspec(True)] * 6,
        out_specs=[cspec(False), cspec(True)],
        out_shape=[sds, sds],
        scratch_shapes=[state, state],
        compiler_params=_params(3),
        name="rwkv_scan",
    )(r, v, kk, lw0, a_0, k0, r, v, kk, lw1, a_1, k1)

    t0 = geo.ctx_tiles if skip_ctx else 0
    post_vec = jnp.stack([ln_w, ln_b, r_k.reshape(d)] + [jnp.zeros_like(ln_w)] * 5)
    mix_specs = [_tile_spec(geo, d, 0, t0)] * 7 + [_const_spec((8, d))]
    return _post_mlp(geo, x, mods, gains, _pre_rwkv, [y_f, y_b, r, k0, k1, v, gate, post_vec], mix_specs,
                     w_o.astype(bf16), w_in, w_out, skip_ctx, "rwkv_post")


def _hgrn_lower_bound(lb_param, layer):
    p = jax.nn.softmax(lb_param.astype(f32), axis=0)
    return (jnp.cumsum(p, axis=0) - p[0])[layer]


def kernel(x, c, ctx, c_ctx, w_mod, b_mod, g_pre_mix, g_post_mix, g_pre_mlp, g_post_mlp, w_mlp_in, w_mlp_out, attn_w_qkv, attn_w_o, attn_sink, gla_w_in, gla_w_gate_down, gla_w_gate_up, gla_gate_bias, gla_g_norm, gla_w_o, rwkv_mix, rwkv_w_rkv, rwkv_w0, rwkv_w_down, rwkv_w_up, rwkv_a0, rwkv_a_down, rwkv_a_up, rwkv_g_down, rwkv_g_up, rwkv_k_k, rwkv_k_a, rwkv_r_k, rwkv_ln_w, rwkv_ln_b, rwkv_w_o, hgrn_w_in, hgrn_w_f, hgrn_lb, hgrn_g_norm, hgrn_w_o):
    depth = w_mod.shape[0]
    geo = _Geom(x.shape[0], ctx.shape[1], x.shape[1])
    mods_all = _mod_vectors(c, c_ctx, w_mod, b_mod)
    xs = jnp.concatenate([ctx, x], axis=1)
    for i in range(depth):
        kind, j = i % 4, i // 4
        skip_ctx = i == depth - 1
        mods = mods_all[i]
        gains = jnp.stack([g_pre_mix[i], g_post_mix[i], g_pre_mlp[i], g_post_mlp[i]] + [jnp.zeros_like(g_pre_mix[i])] * 4)
        w_in, w_out = w_mlp_in[i].astype(bf16), w_mlp_out[i].astype(bf16)
        if kind == 0:
            xs = _attn_layer(geo, xs, mods, gains, attn_w_qkv[j], attn_w_o[j], attn_sink[j], w_in, w_out, skip_ctx)
        elif kind == 1:
            xs = _gla_layer(geo, xs, mods, gains, gla_w_in[j], gla_w_gate_down[j], gla_w_gate_up[j],
                            gla_gate_bias[j], gla_g_norm[j], gla_w_o[j], w_in, w_out, skip_ctx)
        elif kind == 2:
            xs = _rwkv_layer(geo, xs, mods, gains, rwkv_mix[j], rwkv_w_rkv[j], rwkv_w0[j], rwkv_w_down[j],
                             rwkv_w_up[j], rwkv_a0[j], rwkv_a_down[j], rwkv_a_up[j], rwkv_g_down[j], rwkv_g_up[j],
                             rwkv_k_k[j], rwkv_k_a[j], rwkv_r_k[j], rwkv_ln_w[j], rwkv_ln_b[j], rwkv_w_o[j],
                             w_in, w_out, skip_ctx)
        else:
            xs = _hgrn_layer(geo, xs, mods, gains, hgrn_w_in[j], hgrn_w_f[j], _hgrn_lower_bound(hgrn_lb, i),
                             hgrn_g_norm[j], hgrn_w_o[j], w_in, w_out, skip_ctx)
        if skip_ctx:
            return xs
    return xs[:, geo.n_ctx:]
```

```python
import functools

import jax
import jax.numpy as jnp
from jax import lax
from jax.experimental import pallas as pl
from jax.experimental.pallas import tpu as pltpu

f32 = jnp.float32
bf16 = jnp.bfloat16

D_MODEL = 1024
N_MOD = 6
MLP_HIDDEN = 4 * D_MODEL
NORM_EPS = 1e-6
NEG_INF = -1e30
GRID_W = 64

ATTN_HEADS = 16
ATTN_KV_HEADS = 4
ATTN_GROUP = ATTN_HEADS // ATTN_KV_HEADS
HEAD_DIM = 64
WINDOW = 128
ATTN_BLOCK = 128
ROPE_BASE = 10000.0
ROPE_AXIS_DIM = HEAD_DIM // 2
ROPE_FREQS = ROPE_AXIS_DIM // 2

GLA_HEADS = 4
GLA_KEY_DIM = D_MODEL // 2
GLA_DK = GLA_KEY_DIM // GLA_HEADS
GLA_DV = D_MODEL // GLA_HEADS
GLA_GATE_RANK = 16
GLA_TAU = 16.0
SCAN_CHUNK = 64

RWKV_HEAD_SIZE = 64
RWKV_LN_EPS = 64e-5
L2_EPS = 1e-12
RWKV_PAIR = 2 * RWKV_HEAD_SIZE

HGRN_EXPAND = 128
HGRN_HEADS = D_MODEL // HGRN_EXPAND

LANES = 128
MOD_ROWS = 8
VMEM_LIMIT = 56 * 1024 * 1024

NT = (((1,), (1,)), ((), ()))
TN = (((0,), (0,)), ((), ()))


def _params(n_grid):
    return pltpu.CompilerParams(dimension_semantics=("arbitrary",) * n_grid, vmem_limit_bytes=VMEM_LIMIT)


def _const_spec(shape):
    nd = len(shape)
    return pl.BlockSpec(shape, lambda *_: (0,) * nd, pipeline_mode=pl.Buffered(1))


def _dot(a, b):
    return jnp.dot(a, b, preferred_element_type=f32)


def _sigmoid(x):
    return 1.0 / (1.0 + jnp.exp(-x))


def _silu(x):
    return x * _sigmoid(x)


def _softplus(x):
    return jnp.maximum(x, 0.0) + jnp.log1p(jnp.exp(-jnp.abs(x)))


def _rms(x, gain):
    return x * lax.rsqrt(jnp.mean(x * x, axis=-1, keepdims=True) + NORM_EPS) * gain


def _modulate(x, gain, shift, scale):
    return _rms(x, gain) * (1.0 + scale) + shift


def _seg_sum(x, seg):
    r = lax.broadcasted_iota(jnp.int32, (LANES, LANES), 0) // seg
    c = lax.broadcasted_iota(jnp.int32, (LANES, LANES), 1) // seg
    ones_bd = (r == c).astype(bf16)
    hi = x.astype(bf16)
    lo = (x - hi.astype(f32)).astype(bf16)
    outs = []
    for j in range(x.shape[1] // LANES):
        sl = slice(j * LANES, (j + 1) * LANES)
        outs.append(_dot(hi[:, sl], ones_bd) + _dot(lo[:, sl], ones_bd))
    return jnp.concatenate(outs, axis=1)


def _cumsum_rows(g, reverse):
    n = g.shape[0]
    row = lax.broadcasted_iota(jnp.int32, g.shape, 0)
    x = g
    s = 1
    while s < n:
        if reverse:
            x = x + jnp.where(row < n - s, pltpu.roll(x, n - s, axis=0), 0.0)
        else:
            x = x + jnp.where(row >= s, pltpu.roll(x, s, axis=0), 0.0)
        s *= 2
    return x


def _mod_kernel(c_ref, w_ref, b_ref, o_ref):
    o_ref[0] = _dot(_silu(c_ref[...]).astype(bf16), w_ref[0]) + b_ref[0]


def _mod_vectors(c, c_ctx, w_mod, b_mod):
    depth, d, _ = w_mod.shape
    batch = c.shape[0]
    rows = -(-(batch + 1) // 8) * 8
    cc = jnp.zeros((rows, d), f32).at[:batch].set(c).at[batch].set(c_ctx)
    out = pl.pallas_call(
        _mod_kernel,
        grid=(depth, N_MOD),
        in_specs=[
            pl.BlockSpec((rows, d), lambda i, j: (0, 0)),
            pl.BlockSpec((1, d, d), lambda i, j: (i, 0, j)),
            pl.BlockSpec((1, 1, d), lambda i, j: (i, 0, j)),
        ],
        out_specs=pl.BlockSpec((1, rows, d), lambda i, j: (i, 0, j)),
        out_shape=jax.ShapeDtypeStruct((depth, rows, N_MOD * d), f32),
        compiler_params=_params(2),
        name="mod_vectors",
    )(cc, w_mod.astype(bf16), b_mod.reshape(depth, 1, N_MOD * d))
    lat = out[:, :batch].reshape(depth, batch, 1, N_MOD, d)
    con = jnp.broadcast_to(out[:, batch].reshape(depth, 1, 1, N_MOD, d), lat.shape)
    mods = jnp.concatenate([con, lat], axis=2)
    return jnp.pad(mods, ((0, 0), (0, 0), (0, 0), (0, MOD_ROWS - N_MOD), (0, 0)))


class _Geom:
    def __init__(self, batch, n_ctx, n_lat):
        self.batch, self.n_ctx, self.n_lat = batch, n_ctx, n_lat
        self.t = n_ctx + n_lat
        self.tm = min(256, n_ctx)
        assert n_ctx % self.tm == 0 and n_lat % self.tm == 0
        assert n_ctx % ATTN_BLOCK == 0 and n_lat % ATTN_BLOCK == 0 and n_lat % GRID_W == 0
        self.nt = self.t // self.tm
        self.ctx_tiles = n_ctx // self.tm
        self.nc = self.t // SCAN_CHUNK
        self.ctx_chunks = n_ctx // SCAN_CHUNK

    def x_spec(self, t0=0):
        return pl.BlockSpec((1, self.tm, D_MODEL), lambda b, t: (b, t + t0, 0))

    def mod_spec(self, t0=0):
        ct = self.ctx_tiles
        return pl.BlockSpec((1, 1, MOD_ROWS, D_MODEL), lambda b, t: (b, ((t + t0) >= ct).astype(jnp.int32), 0, 0))

    def bwd_chunk(self, i):
        cc = self.ctx_chunks
        return jnp.where(i < cc, cc - 1 - i, self.nc - 1 + cc - i)


def _lin_kernel(x_ref, m_ref, g_ref, w_ref, o_ref):
    h = _modulate(x_ref[0], g_ref[0:1], m_ref[0, 0, 0:1], m_ref[0, 0, 1:2])
    o_ref[0] = _dot(h.astype(bf16), w_ref[...])


def _mod_linear(geo, x, mods, gains, w, name):
    n = w.shape[1]
    return pl.pallas_call(
        _lin_kernel,
        grid=(geo.batch, geo.nt),
        in_specs=[geo.x_spec(), geo.mod_spec(), _const_spec((8, D_MODEL)), _const_spec((D_MODEL, n))],
        out_specs=pl.BlockSpec((1, geo.tm, n), lambda b, t: (b, t, 0)),
        out_shape=jax.ShapeDtypeStruct((geo.batch, geo.t, n), f32),
        compiler_params=_params(2),
        name=name,
    )(x, mods, gains, w)


def _head_rms(o, gain, width):
    outs = []
    for h in range(o.shape[1] // width):
        oh = o[:, h * width:(h + 1) * width]
        outs.append(oh * lax.rsqrt(jnp.mean(oh * oh, axis=-1, keepdims=True) + NORM_EPS) * gain)
    return jnp.concatenate(outs, axis=1)


def _pre_attn(refs):
    (o_ref,) = refs
    return o_ref[0]


def _pre_scan(width, refs):
    of_ref, ob_ref, gate_ref, gn_ref = refs
    o = _head_rms(of_ref[0] + ob_ref[0], gn_ref[...], width)
    return (o * _silu(gate_ref[0])).astype(bf16)


def _pre_rwkv(refs):
    yf_ref, yb_ref, r_ref, k0_ref, k1_ref, v_ref, g_ref, vec_ref = refs
    y = yf_ref[0] + yb_ref[0]
    inv_n = 1.0 / RWKV_HEAD_SIZE
    mu = _seg_sum(y, RWKV_HEAD_SIZE) * inv_n
    dlt = y - mu
    var = _seg_sum(dlt * dlt, RWKV_HEAD_SIZE) * inv_n
    yn = dlt * lax.rsqrt(var + RWKV_LN_EPS) * vec_ref[0:1] + vec_ref[1:2]
    bonus = _seg_sum(r_ref[0] * (0.5 * (k0_ref[0] + k1_ref[0])) * vec_ref[2:3], RWKV_HEAD_SIZE) * v_ref[0]
    return ((yn + bonus) * g_ref[0]).astype(bf16)


def _post_kernel(pre, n_mix, x_ref, m_ref, g_ref, *rest):
    mix_refs = rest[:n_mix]
    wo_ref, win_ref, wout_ref, out_ref = rest[n_mix:]
    o = pre(mix_refs)
    y = _dot(o, wo_ref[...])
    x1 = x_ref[0] + m_ref[0, 0, 2:3] * _rms(y, g_ref[1:2])
    h2 = _modulate(x1, g_ref[2:3], m_ref[0, 0, 3:4], m_ref[0, 0, 4:5]).astype(bf16)
    acc = jnp.zeros_like(x1)
    for c in range(MLP_HIDDEN // D_MODEL):
        u = jnp.square(jnp.maximum(_dot(h2, win_ref[:, c * D_MODEL:(c + 1) * D_MODEL]), 0.0))
        acc = acc + _dot(u.astype(bf16), wout_ref[c * D_MODEL:(c + 1) * D_MODEL, :])
    out_ref[0] = x1 + m_ref[0, 0, 5:6] * _rms(acc, g_ref[3:4])


def _post_mlp(geo, x, mods, gains, pre, mix_args, mix_specs, w_o, w_in, w_out, skip_ctx, name):
    t0 = geo.ctx_tiles if skip_ctx else 0
    return pl.pallas_call(
        functools.partial(_post_kernel, pre, len(mix_args)),
        grid=(geo.batch, geo.nt - t0),
        in_specs=[geo.x_spec(t0), geo.mod_spec(t0), _const_spec((8, D_MODEL))] + mix_specs + [
            _const_spec((D_MODEL, D_MODEL)), _const_spec((D_MODEL, MLP_HIDDEN)), _const_spec((MLP_HIDDEN, D_MODEL))],
        out_specs=pl.BlockSpec((1, geo.tm, D_MODEL), lambda b, t: (b, t, 0)),
        out_shape=jax.ShapeDtypeStruct((geo.batch, geo.t - t0 * geo.tm, D_MODEL), f32),
        compiler_params=_params(2),
        name=name,
    )(x, mods, gains, *mix_args, w_o, w_in, w_out)


def _tile_spec(geo, width, col, t0):
    return pl.BlockSpec((1, geo.tm, width), lambda b, t: (b, t + t0, col))


def _rope(x, cos, sin):
    w = x.shape[1]
    reps = w // LANES
    cw = jnp.concatenate([cos] * reps, axis=1)
    sw = jnp.concatenate([sin] * reps, axis=1)
    lane = lax.broadcasted_iota(jnp.int32, x.shape, 1)
    first = (lane % ROPE_AXIS_DIM) < ROPE_FREQS
    partner = jnp.where(first, pltpu.roll(x, w - ROPE_FREQS, axis=1), pltpu.roll(x, ROPE_FREQS, axis=1))
    return x * cw + partner * sw


def _attn_proj_kernel(x_ref, m_ref, g_ref, cos_ref, sin_ref, wq_ref, wk_ref, wv_ref, q_ref, k_ref, v_ref):
    h = _modulate(x_ref[0], g_ref[0:1], m_ref[0, 0, 0:1], m_ref[0, 0, 1:2]).astype(bf16)
    cos, sin = cos_ref[...], sin_ref[...]
    q = _rope(_dot(h, wq_ref[...]) * (HEAD_DIM ** -0.5), cos, sin)
    k = _rope(_dot(h, wk_ref[...]), cos, sin)
    v = _dot(h, wv_ref[...])
    for hd in range(ATTN_HEADS):
        q_ref[0, hd] = q[:, hd * HEAD_DIM:(hd + 1) * HEAD_DIM].astype(bf16)
    for hd in range(ATTN_KV_HEADS):
        k_ref[0, hd] = k[:, hd * HEAD_DIM:(hd + 1) * HEAD_DIM].astype(bf16)
        v_ref[0, hd] = v[:, hd * HEAD_DIM:(hd + 1) * HEAD_DIM].astype(bf16)


def _attn_core_kernel(geo, sink_ref, q_ref, kp_ref, kc_ref, kn_ref, kx_ref, vp_ref, vc_ref, vn_ref, vx_ref, o_ref):
    qb = pl.program_id(1)
    blk = ATTN_BLOCK
    n_loc = 3 * blk
    n_keys = n_loc + geo.n_ctx
    rows = ATTN_GROUP * blk
    first_lat = geo.n_ctx // blk
    n_blocks = geo.t // blk
    row = lax.broadcasted_iota(jnp.int32, (blk, n_keys), 0)
    col = lax.broadcasted_iota(jnp.int32, (blk, n_keys), 1)
    kblk = qb - 1 + col // blk
    ok_local = ((jnp.abs(col - blk - row) <= WINDOW) & (qb >= first_lat) & (kblk >= first_lat) & (kblk < n_blocks))
    ok = (col >= n_loc) | ok_local
    scores, vals = [], []
    for j in range(ATTN_KV_HEADS):
        q4 = q_ref[0, ATTN_GROUP * j:ATTN_GROUP * (j + 1)].reshape(rows, HEAD_DIM)
        keys = jnp.concatenate([kp_ref[0, j], kc_ref[0, j], kn_ref[0, j], kx_ref[0, j]], axis=0)
        vals.append(jnp.concatenate([vp_ref[0, j], vc_ref[0, j], vn_ref[0, j], vx_ref[0, j]], axis=0))
        scores.append(lax.dot_general(q4, keys, NT, preferred_element_type=f32))
    heads = [(j, g) for j in range(ATTN_KV_HEADS) for g in range(ATTN_GROUP)]
    sg = [jnp.where(ok, scores[j][g * blk:(g + 1) * blk], NEG_INF) for j, g in heads]
    sk = [sink_ref[ATTN_GROUP * j + g] for j, g in heads]
    m = [jnp.maximum(jnp.max(s, axis=-1, keepdims=True), k) for s, k in zip(sg, sk)]
    e = [jnp.exp(s - mm) for s, mm in zip(sg, m)]
    inv = [1.0 / (jnp.sum(ee, axis=-1, keepdims=True) + jnp.exp(k - mm)) for ee, k, mm in zip(e, sk, m)]
    p = [(ee * ii).astype(bf16) for ee, ii in zip(e, inv)]
    outs = []
    for j in range(ATTN_KV_HEADS):
        o4 = _dot(jnp.concatenate(p[ATTN_GROUP * j:ATTN_GROUP * (j + 1)], axis=0), vals[j])
        outs += [o4[g * blk:(g + 1) * blk] for g in range(ATTN_GROUP)]
    o_ref[0] = jnp.concatenate(outs, axis=1).astype(bf16)


def _rope_tables(geo):
    inv_freq = ROPE_BASE ** (-jnp.arange(ROPE_FREQS, dtype=f32) * 2.0 / ROPE_AXIS_DIM)
    pos = jnp.arange(geo.n_lat)
    row = (pos // GRID_W).astype(f32)
    col = (pos % GRID_W).astype(f32)
    ang = jnp.stack([row[:, None] * inv_freq, col[:, None] * inv_freq], axis=1)
    cos = jnp.cos(ang)
    sin = jnp.sin(ang)
    cos_h = jnp.concatenate([cos, cos], axis=2).reshape(geo.n_lat, HEAD_DIM)
    sin_h = jnp.concatenate([-sin, sin], axis=2).reshape(geo.n_lat, HEAD_DIM)
    cos_t = jnp.concatenate([jnp.ones((geo.n_ctx, HEAD_DIM), f32), cos_h], axis=0)
    sin_t = jnp.concatenate([jnp.zeros((geo.n_ctx, HEAD_DIM), f32), sin_h], axis=0)
    return jnp.tile(cos_t, (1, 2)), jnp.tile(sin_t, (1, 2))


def _attn_layer(geo, x, mods, gains, w_qkv, w_o, sink, w_in, w_out, skip_ctx):
    b, t, tm = geo.batch, geo.t, geo.tm
    q_cols = ATTN_HEADS * HEAD_DIM
    kv_cols = ATTN_KV_HEADS * HEAD_DIM
    wb = w_qkv.astype(bf16)
    cos_t, sin_t = _rope_tables(geo)
    tab_spec = pl.BlockSpec((tm, LANES), lambda bb, tt: (tt, 0))
    q, k, v = pl.pallas_call(
        _attn_proj_kernel,
        grid=(b, geo.nt),
        in_specs=[geo.x_spec(), geo.mod_spec(), _const_spec((8, D_MODEL)), tab_spec, tab_spec,
                  _const_spec((D_MODEL, q_cols)), _const_spec((D_MODEL, kv_cols)), _const_spec((D_MODEL, kv_cols))],
        out_specs=[pl.BlockSpec((1, ATTN_HEADS, tm, HEAD_DIM), lambda bb, tt: (bb, 0, tt, 0)),
                   pl.BlockSpec((1, ATTN_KV_HEADS, tm, HEAD_DIM), lambda bb, tt: (bb, 0, tt, 0)),
                   pl.BlockSpec((1, ATTN_KV_HEADS, tm, HEAD_DIM), lambda bb, tt: (bb, 0, tt, 0))],
        out_shape=[jax.ShapeDtypeStruct((b, ATTN_HEADS, t, HEAD_DIM), bf16),
                   jax.ShapeDtypeStruct((b, ATTN_KV_HEADS, t, HEAD_DIM), bf16),
                   jax.ShapeDtypeStruct((b, ATTN_KV_HEADS, t, HEAD_DIM), bf16)],
        compiler_params=_params(2),
        name="attn_proj",
    )(x, mods, gains, cos_t, sin_t, wb[:, :q_cols], wb[:, q_cols:q_cols + kv_cols], wb[:, q_cols + kv_cols:])

    blk = ATTN_BLOCK
    n_blocks = t // blk

    def kv_spec(off):
        return pl.BlockSpec((1, ATTN_KV_HEADS, blk, HEAD_DIM),
                            lambda bb, qb: (bb, 0, jnp.clip(qb + off, 0, n_blocks - 1), 0))

    ctx_spec = pl.BlockSpec((1, ATTN_KV_HEADS, geo.n_ctx, HEAD_DIM), lambda bb, qb: (bb, 0, 0, 0))
    o = pl.pallas_call(
        functools.partial(_attn_core_kernel, geo),
        grid=(b, n_blocks),
        in_specs=[pl.BlockSpec(memory_space=pltpu.SMEM),
                  pl.BlockSpec((1, ATTN_HEADS, blk, HEAD_DIM), lambda bb, qb: (bb, 0, qb, 0)),
                  kv_spec(-1), kv_spec(0), kv_spec(1), ctx_spec,
                  kv_spec(-1), kv_spec(0), kv_spec(1), ctx_spec],
        out_specs=pl.BlockSpec((1, blk, q_cols), lambda bb, qb: (bb, qb, 0)),
        out_shape=jax.ShapeDtypeStruct((b, t, q_cols), bf16),
        compiler_params=_params(2),
        name="attn_core",
    )(sink, q, k, k, k, k, v, v, v, v)

    t0 = geo.ctx_tiles if skip_ctx else 0
    return _post_mlp(geo, x, mods, gains, _pre_attn, [o], [_tile_spec(geo, D_MODEL, 0, t0)],
                     w_o.astype(bf16), w_in, w_out, skip_ctx, "attn_post")


def _gated_chunk(dirs, heads, dk, dv):
    units = []
    for q, k, v, g, st_ref, base, reverse in dirs:
        n = q.shape[0]
        b = _cumsum_rows(g, reverse)
        btot = b[0:1] if reverse else b[n - 1:n]
        q_dec = (q * jnp.exp(b)).astype(bf16)
        k_inv = (k * jnp.exp(-b)).astype(bf16)
        k_end = (k * jnp.exp(btot - b)).astype(bf16)
        dec = jnp.exp(btot)
        vb = v.astype(bf16)
        ri = lax.broadcasted_iota(jnp.int32, (n, n), 0)
        ci = lax.broadcasted_iota(jnp.int32, (n, n), 1)
        tri = (ci >= ri) if reverse else (ci <= ri)
        for h in range(heads):
            ks = slice(h * dk, (h + 1) * dk)
            vs = slice(h * dv, (h + 1) * dv)
            units.append(dict(q=q_dec[:, ks], ki=k_inv[:, ks], ke=k_end[:, ks], v=vb[:, vs], dec=dec[:, ks],
                              tri=tri, st=st_ref[base + h], ref=st_ref, h=base + h))
    for u in units:
        a = lax.dot_general(u["q"], u["ki"], NT, preferred_element_type=f32)
        u["a"] = jnp.where(u["tri"], a, 0.0).astype(bf16)
        u["qs"] = lax.dot_general(u["q"], u["st"].astype(bf16), NT, preferred_element_type=f32)
    for u in units:
        u["o"] = _dot(u["a"], u["v"]) + u["qs"]
        u["new"] = u["st"] * u["dec"] + lax.dot_general(u["v"], u["ke"], TN, preferred_element_type=f32)
    for u in units:
        u["ref"][u["h"]] = u["new"]
    return [jnp.concatenate([u["o"] for u in units[d * heads:(d + 1) * heads]], axis=1) for d in range(len(dirs))]


def _gla_gate(zd, wu_ref, bias_ref, d):
    zg = _dot(zd.astype(bf16), wu_ref[d]) + bias_ref[d:d + 1]
    return (jnp.minimum(zg, 0.0) - jnp.log1p(jnp.exp(-jnp.abs(zg)))) * (1.0 / GLA_TAU)


def _gla_scan_kernel(qkf_ref, vf_ref, zdf_ref, qkb_ref, vb_ref, zdb_ref, wu_ref, bias_ref,
                     of_ref, ob_ref, sf_ref, sb_ref):
    @pl.when(pl.program_id(1) == 0)
    def _():
        sf_ref[...] = jnp.zeros_like(sf_ref)
        sb_ref[...] = jnp.zeros_like(sb_ref)

    dirs, dests = [], []
    for row in range(qkf_ref.shape[0]):
        for d, (qk_ref, v_ref, zd_ref, s_ref, o_ref) in enumerate(
                ((qkf_ref, vf_ref, zdf_ref, sf_ref, of_ref), (qkb_ref, vb_ref, zdb_ref, sb_ref, ob_ref))):
            qk = qk_ref[row]
            q = qk[:, :GLA_KEY_DIM] * (GLA_DK ** -0.5)
            k = qk[:, GLA_KEY_DIM:]
            dirs.append((q, k, v_ref[row], _gla_gate(zd_ref[row], wu_ref, bias_ref, d), s_ref, row * GLA_HEADS,
                         d == 1))
            dests.append((o_ref, row))
    for o, (o_ref, row) in zip(_gated_chunk(dirs, GLA_HEADS, GLA_DK, GLA_DV), dests):
        o_ref[row] = o


def _hgrn_scan_kernel(qf_ref, if_ref, zff_ref, qb_ref, ib_ref, zfb_ref, lb_ref, of_ref, ob_ref, sf_ref, sb_ref):
    @pl.when(pl.program_id(1) == 0)
    def _():
        sf_ref[...] = jnp.zeros_like(sf_ref)
        sb_ref[...] = jnp.zeros_like(sb_ref)

    lb = lb_ref[...]
    dirs, dests = [], []
    for row in range(qf_ref.shape[0]):
        for d, (q_ref, i_ref, zf_ref, s_ref, o_ref) in enumerate(
                ((qf_ref, if_ref, zff_ref, sf_ref, of_ref), (qb_ref, ib_ref, zfb_ref, sb_ref, ob_ref))):
            f = lb + (1.0 - lb) * _sigmoid(zf_ref[row])
            dirs.append((_silu(q_ref[row]), 1.0 - f, i_ref[row], jnp.log(f), s_ref, row * HGRN_HEADS, d == 1))
            dests.append((o_ref, row))
    for o, (o_ref, row) in zip(_gated_chunk(dirs, HGRN_HEADS, HGRN_EXPAND, HGRN_EXPAND), dests):
        o_ref[row] = o


SCAN_ROWS_PER_STEP = 2


def _scan_rows(geo):
    return SCAN_ROWS_PER_STEP if geo.batch % SCAN_ROWS_PER_STEP == 0 else 1


def _chunk_spec(geo, width, col, reverse):
    rows = _scan_rows(geo)
    if reverse:
        return pl.BlockSpec((rows, SCAN_CHUNK, width), lambda b, i: (b, geo.bwd_chunk(i), col))
    return pl.BlockSpec((rows, SCAN_CHUNK, width), lambda b, i: (b, i, col))


def _scan_call(geo, body, args, specs, heads, dk, dv, name):
    out_sds = jax.ShapeDtypeStruct((geo.batch, geo.t, heads * dv), f32)
    rows = _scan_rows(geo)
    state = pltpu.VMEM((rows * heads, dv, dk), f32)
    return pl.pallas_call(
        body,
        grid=(geo.batch // rows, geo.nc),
        in_specs=specs,
        out_specs=[_chunk_spec(geo, heads * dv, 0, False), _chunk_spec(geo, heads * dv, 0, True)],
        out_shape=[out_sds, out_sds],
        scratch_shapes=[state, state],
        compiler_params=_params(2),
        name=name,
    )(*args)


def _gla_layer(geo, x, mods, gains, w_in_p, w_gd, w_gu, g_bias, g_norm, w_o, w_in, w_out, skip_ctx):
    r = GLA_GATE_RANK
    n_z = 2 * GLA_KEY_DIM + 2 * D_MODEL
    w_all = jnp.concatenate([w_in_p, w_gd[0], w_gd[1], jnp.zeros((D_MODEL, LANES - 2 * r), f32)], axis=1).astype(bf16)
    z = _mod_linear(geo, x, mods, gains, w_all, "gla_proj")
    wu = jnp.zeros((2, LANES, GLA_KEY_DIM), f32).at[0, :r].set(w_gu[0]).at[1, r:2 * r].set(w_gu[1]).astype(bf16)
    zd_col = n_z // LANES
    specs = []
    for rev in (False, True):
        specs += [_chunk_spec(geo, 2 * GLA_KEY_DIM, 0, rev), _chunk_spec(geo, D_MODEL, 1, rev),
                  _chunk_spec(geo, LANES, zd_col, rev)]
    specs += [_const_spec((2, LANES, GLA_KEY_DIM)), _const_spec((2, GLA_KEY_DIM))]
    o_f, o_b = _scan_call(geo, _gla_scan_kernel, [z, z, z, z, z, z, wu, g_bias], specs,
                          GLA_HEADS, GLA_DK, GLA_DV, "gla_scan")
    t0 = geo.ctx_tiles if skip_ctx else 0
    mix_specs = [_tile_spec(geo, D_MODEL, 0, t0), _tile_spec(geo, D_MODEL, 0, t0), _tile_spec(geo, D_MODEL, 2, t0),
                 _const_spec((1, GLA_DV))]
    return _post_mlp(geo, x, mods, gains, functools.partial(_pre_scan, GLA_DV),
                     [o_f, o_b, z, g_norm.reshape(1, GLA_DV)], mix_specs,
                     w_o.astype(bf16), w_in, w_out, skip_ctx, "gla_post")


def _hgrn_layer(geo, x, mods, gains, w_in_p, w_f, lower_bound, g_norm, w_o, w_in, w_out, skip_ctx):
    w_all = jnp.concatenate([w_in_p, w_f[0], w_f[1]], axis=1).astype(bf16)
    z = _mod_linear(geo, x, mods, gains, w_all, "hgrn_proj")
    specs = []
    for rev in (False, True):
        specs += [_chunk_spec(geo, D_MODEL, 0, rev), _chunk_spec(geo, D_MODEL, 1, rev),
                  _chunk_spec(geo, D_MODEL, 4 if rev else 3, rev)]
    specs += [_const_spec((1, D_MODEL))]
    o_f, o_b = _scan_call(geo, _hgrn_scan_kernel, [z, z, z, z, z, z, lower_bound.reshape(1, D_MODEL)], specs,
                          HGRN_HEADS, HGRN_EXPAND, HGRN_EXPAND, "hgrn_scan")
    t0 = geo.ctx_tiles if skip_ctx else 0
    mix_specs = [_tile_spec(geo, D_MODEL, 0, t0), _tile_spec(geo, D_MODEL, 0, t0), _tile_spec(geo, D_MODEL, 2, t0),
                 _const_spec((1, HGRN_EXPAND))]
    return _post_mlp(geo, x, mods, gains, functools.partial(_pre_scan, HGRN_EXPAND),
                     [o_f, o_b, z, g_norm.reshape(1, HGRN_EXPAND)], mix_specs,
                     w_o.astype(bf16), w_in, w_out, skip_ctx, "hgrn_post")


def _rwkv_proj_kernel(geo, x_ref, xp_ref, xn_ref, m_ref, g_ref, mix_ref, vec_ref, wrkv_ref, wdn_ref, wup_ref,
                      aup_ref, gup_ref, r_ref, v_ref, gate_ref, kk_ref, lw0_ref, lw1_ref, a0_ref, a1_ref,
                      k0_ref, k1_ref):
    t = pl.program_id(1)
    tm = geo.tm
    gain, shift, scale = g_ref[0:1], m_ref[0, 0, 0:1], m_ref[0, 0, 1:2]
    h = _modulate(x_ref[0], gain, shift, scale)
    seg_first = (t == 0) | (t == geo.ctx_tiles)
    seg_last = (t == geo.ctx_tiles - 1) | (t == geo.nt - 1)
    h_prev = jnp.where(seg_first, 0.0, _modulate(xp_ref[0], gain, shift, scale)[7:8])
    h_next = jnp.where(seg_last, 0.0, _modulate(xn_ref[0], gain, shift, scale)[0:1])
    row = lax.broadcasted_iota(jnp.int32, h.shape, 0)
    up = jnp.where(row == 0, h_prev, pltpu.roll(h, 1, axis=0))
    dn = jnp.where(row == tm - 1, h_next, pltpu.roll(h, tm - 1, axis=0))
    dx = 0.5 * (up + dn) - h

    def mixed(n):
        return (h + dx * mix_ref[n:n + 1]).astype(bf16)

    r = _dot(mixed(0), wrkv_ref[0])
    k = _dot(mixed(1), wrkv_ref[1])
    v = _dot(mixed(2), wrkv_ref[2])
    dw = jnp.tanh(_dot(mixed(3), wdn_ref[:, 0:LANES])).astype(bf16)
    da = _dot(mixed(4), wdn_ref[:, LANES:2 * LANES]).astype(bf16)
    dg = _sigmoid(_dot(mixed(5), wdn_ref[:, 2 * LANES:3 * LANES])).astype(bf16)
    r_ref[0] = r
    v_ref[0] = v
    gate_ref[0] = _dot(dg, gup_ref[...])
    kk = k * vec_ref[4:5]
    kk_ref[0] = kk / jnp.maximum(jnp.sqrt(_seg_sum(kk * kk, RWKV_HEAD_SIZE)), L2_EPS)
    for d, (lw_ref, a_ref, kd_ref) in enumerate(((lw0_ref, a0_ref, k0_ref), (lw1_ref, a1_ref, k1_ref))):
        w_log = -_softplus(-(vec_ref[d:d + 1] + _dot(dw, wup_ref[d]))) - 0.5
        lw_ref[0] = -jnp.exp(w_log)
        a = _sigmoid(vec_ref[2 + d:3 + d] + _dot(da, aup_ref[d]))
        a_ref[0] = a
        kd_ref[0] = k * (1.0 + (a - 1.0) * vec_ref[5:6])


def _rwkv_chunk_units(units):
    n = SCAN_CHUNK
    hs = RWKV_HEAD_SIZE
    lane = lax.broadcasted_iota(jnp.int32, (n, RWKV_PAIR), 1)
    t_i = lax.broadcasted_iota(jnp.int32, (n, RWKV_PAIR), 0)
    s_i = lane % hs
    head0 = lane < hs
    ri = lax.broadcasted_iota(jnp.int32, (RWKV_PAIR, RWKV_PAIR), 0)
    ci = lax.broadcasted_iota(jnp.int32, (RWKV_PAIR, RWKV_PAIR), 1)
    same_head = ri // hs == ci // hs
    top_rows = ri < hs

    def block_diag(x):
        return jnp.concatenate([jnp.where(head0, x, 0.0), jnp.where(head0, 0.0, x)], axis=0)

    def cross_blocks(b):
        return (t_i // (2 * b) == s_i // (2 * b)) & (t_i // b != s_i // b)

    st = []
    for r, kd, v, kk, a, lw, ht, reverse in units:
        c = _cumsum_rows(lw, reverse)
        tot = c[0:1] if reverse else c[n - 1:n]
        e_neg = jnp.exp(-c)
        e_end = jnp.exp(tot - c)
        beta = kk * a
        kb_t = jnp.concatenate([kd * e_neg, beta * e_neg], axis=0).T
        kb_sw = pltpu.roll(kb_t, hs, axis=1)
        rhs = jnp.concatenate([jnp.where(same_head, jnp.where(top_rows, kb_t, kb_sw), 0.0),
                               jnp.where(same_head, jnp.where(top_rows, kb_sw, kb_t), 0.0)], axis=1)
        st.append(dict(
            strict=(s_i > t_i) if reverse else (s_i < t_i), incl=(s_i >= t_i) if reverse else (s_i <= t_i),
            a_bar=(-kk * jnp.exp(c - lw)).astype(bf16), r_bar=(r * jnp.exp(c)).astype(bf16), rhs=rhs.astype(bf16),
            v_bd=block_diag(v).astype(bf16), h_t=ht.T.astype(bf16),
            ends=jnp.concatenate([kd * e_end, beta * e_end], axis=0).astype(bf16),
            v=v, ht=ht, dec=jnp.exp(tot)))
    for s in st:
        g = _dot(jnp.concatenate([s["a_bar"], s["r_bar"]], axis=0), s["rhs"])
        s["a_ak"] = jnp.where(s["strict"], g[0:n, 0:RWKV_PAIR], 0.0).astype(bf16)
        s["nmat"] = jnp.where(s["strict"], g[0:n, RWKV_PAIR:], 0.0)
        s["q_k"] = jnp.where(s["incl"], g[n:, 0:RWKV_PAIR], 0.0).astype(bf16)
        s["q_b"] = jnp.where(s["incl"], g[n:, RWKV_PAIR:], 0.0).astype(bf16)
    for s in st:
        s["inv"] = jnp.where(s_i == t_i, 1.0, jnp.where(cross_blocks(1), s["nmat"], 0.0))
    b = 2
    while 2 * b < n:
        for s in st:
            s["pc"] = _dot(s["inv"].astype(bf16), block_diag(jnp.where(cross_blocks(b), s["nmat"], 0.0)).astype(bf16))
        for s in st:
            s["inv"] = s["inv"] + _dot(s["pc"].astype(bf16), block_diag(s["inv"]).astype(bf16))
        b *= 2
    for s in st:
        x = _dot(jnp.concatenate([s["a_bar"], s["a_ak"]], axis=1), jnp.concatenate([s["h_t"], s["v_bd"]], axis=0))
        s["inv"] = s["inv"].astype(bf16)
        s["w"] = _dot(s["inv"], block_diag(x).astype(bf16))
    for s in st:
        s["cw"] = _dot(jnp.where(cross_blocks(n // 2), s["nmat"], 0.0).astype(bf16), block_diag(s["w"]).astype(bf16))
    outs = []
    for s in st:
        u = s["w"] + _dot(s["inv"], block_diag(s["cw"]).astype(bf16))
        y = _dot(jnp.concatenate([s["r_bar"], s["q_k"], s["q_b"]], axis=1),
                 jnp.concatenate([s["h_t"], s["v_bd"], block_diag(u).astype(bf16)], axis=0))
        upd = lax.dot_general(jnp.concatenate([s["v"], u], axis=0).astype(bf16), s["ends"], TN,
                              preferred_element_type=f32)
        outs.append((y, s["ht"] * s["dec"] + jnp.where(same_head, upd, 0.0)))
    return outs


def _rwkv_scan_kernel(rows, pairs, *refs):
    fwd, bwd = refs[0:6], refs[6:12]
    yf_ref, yb_ref, hf_ref, hb_ref = refs[12:]

    @pl.when(pl.program_id(2) == 0)
    def _():
        hf_ref[...] = jnp.zeros_like(hf_ref)
        hb_ref[...] = jnp.zeros_like(hb_ref)

    units, dests = [], []
    for ins, y_ref, h_ref, reverse in ((fwd, yf_ref, hf_ref, False), (bwd, yb_ref, hb_ref, True)):
        r_ref, v_ref, kk_ref, lw_ref, a_ref, kd_ref = ins
        for row in range(rows):
            for p in range(pairs):
                sl = slice(p * RWKV_PAIR, (p + 1) * RWKV_PAIR)
                slot = row * pairs + p
                units.append((r_ref[row, :, sl], kd_ref[row, :, sl], v_ref[row, :, sl], kk_ref[row, :, sl],
                              a_ref[row, :, sl], lw_ref[row, :, sl], h_ref[slot], reverse))
                dests.append((y_ref, h_ref, row, slot, sl))
    for (y, h_new), (y_ref, h_ref, row, slot, sl) in zip(_rwkv_chunk_units(units), dests):
        y_ref[row, :, sl] = y
        h_ref[slot] = h_new


RWKV_PAIRS_PER_STEP = 8
RWKV_ROWS_PER_STEP = 2


def _rwkv_layer(geo, x, mods, gains, mix, w_rkv, w0, w_down, w_up, a0, a_down, a_up, g_down, g_up, k_k, k_a,
                r_k, ln_w, ln_b, w_o, w_in, w_out, skip_ctx):
    b, t, tm = geo.batch, geo.t, geo.tm
    d = D_MODEL
    rank = w_down.shape[-1]
    assert 2 * rank == LANES and a_down.shape[-1] == rank and g_down.shape[-1] == LANES
    mix8 = jnp.pad(mix, ((0, 2), (0, 0)))
    vec = jnp.stack([w0[0], w0[1], a0[0], a0[1], k_k, k_a, jnp.zeros_like(k_k), jnp.zeros_like(k_k)])
    w_dn = jnp.concatenate([w_down[0], w_down[1], a_down[0], a_down[1], g_down], axis=1).astype(bf16)

    def padded_up(w):
        return jnp.zeros((2, LANES, d), f32).at[0, :rank].set(w[0]).at[1, rank:].set(w[1]).astype(bf16)

    halo = 8
    n_halo = t // halo
    per = tm // halo
    x_prev = pl.BlockSpec((1, halo, d), lambda bb, tt: (bb, jnp.maximum(tt * per - 1, 0), 0))
    x_next = pl.BlockSpec((1, halo, d), lambda bb, tt: (bb, jnp.minimum((tt + 1) * per, n_halo - 1), 0))
    out_spec = pl.BlockSpec((1, tm, d), lambda bb, tt: (bb, tt, 0))
    sds = jax.ShapeDtypeStruct((b, t, d), f32)
    r, v, gate, kk, lw0, lw1, a_0, a_1, k0, k1 = pl.pallas_call(
        functools.partial(_rwkv_proj_kernel, geo),
        grid=(b, geo.nt),
        in_specs=[geo.x_spec(), x_prev, x_next, geo.mod_spec(), _const_spec((8, d)), _const_spec((8, d)),
                  _const_spec((8, d)), _const_spec((3, d, d)), _const_spec((d, 3 * LANES)),
                  _const_spec((2, LANES, d)), _const_spec((2, LANES, d)), _const_spec((LANES, d))],
        out_specs=[out_spec] * 10,
        out_shape=[sds] * 10,
        compiler_params=_params(2),
        name="rwkv_proj",
    )(x, x, x, mods, gains, mix8, vec, w_rkv.astype(bf16), w_dn, padded_up(w_up), padded_up(a_up),
      g_up.astype(bf16))

    pairs = RWKV_PAIRS_PER_STEP
    rows = RWKV_ROWS_PER_STEP if b % RWKV_ROWS_PER_STEP == 0 else 1
    width = pairs * RWKV_PAIR
    groups = d // width

    def cspec(reverse):
        if reverse:
            return pl.BlockSpec((rows, SCAN_CHUNK, width), lambda bb, gg, i: (bb, geo.bwd_chunk(i), gg))
        return pl.BlockSpec((rows, SCAN_CHUNK, width), lambda bb, gg, i: (bb, i, gg))

    state = pltpu.VMEM((rows * pairs, RWKV_PAIR, RWKV_PAIR), f32)
    y_f, y_b = pl.pallas_call(
        functools.partial(_rwkv_scan_kernel, rows, pairs),
        grid=(b // rows, groups, geo.nc),
        in_specs=[cspec(False)] * 6 + [cspec(True)] * 6,
        out_specs=[cspec(False), cspec(True)],
        out_shape=[sds, sds],
        scratch_shapes=[state, state],
        compiler_params=_params(3),
        name="rwkv_scan",
    )(r, v, kk, lw0, a_0, k0, r, v, kk, lw1, a_1, k1)

    t0 = geo.ctx_tiles if skip_ctx else 0
    post_vec = jnp.stack([ln_w, ln_b, r_k.reshape(d)] + [jnp.zeros_like(ln_w)] * 5)
    mix_specs = [_tile_spec(geo, d, 0, t0)] * 7 + [_const_spec((8, d))]
    return _post_mlp(geo, x, mods, gains, _pre_rwkv, [y_f, y_b, r, k0, k1, v, gate, post_vec], mix_specs,
                     w_o.astype(bf16), w_in, w_out, skip_ctx, "rwkv_post")


def _hgrn_lower_bound(lb_param, layer):
    p = jax.nn.softmax(lb_param.astype(f32), axis=0)
    return (jnp.cumsum(p, axis=0) - p[0])[layer]


def kernel(x, c, ctx, c_ctx, w_mod, b_mod, g_pre_mix, g_post_mix, g_pre_mlp, g_post_mlp, w_mlp_in, w_mlp_out, attn_w_qkv, attn_w_o, attn_sink, gla_w_in, gla_w_gate_down, gla_w_gate_up, gla_gate_bias, gla_g_norm, gla_w_o, rwkv_mix, rwkv_w_rkv, rwkv_w0, rwkv_w_down, rwkv_w_up, rwkv_a0, rwkv_a_down, rwkv_a_up, rwkv_g_down, rwkv_g_up, rwkv_k_k, rwkv_k_a, rwkv_r_k, rwkv_ln_w, rwkv_ln_b, rwkv_w_o, hgrn_w_in, hgrn_w_f, hgrn_lb, hgrn_g_norm, hgrn_w_o):
    depth = w_mod.shape[0]
    geo = _Geom(x.shape[0], ctx.shape[1], x.shape[1])
    mods_all = _mod_vectors(c, c_ctx, w_mod, b_mod)
    xs = jnp.concatenate([ctx, x], axis=1)
    for i in range(depth):
        kind, j = i % 4, i // 4
        skip_ctx = i == depth - 1
        mods = mods_all[i]
        gains = jnp.stack([g_pre_mix[i], g_post_mix[i], g_pre_mlp[i], g_post_mlp[i]] + [jnp.zeros_like(g_pre_mix[i])] * 4)
        w_in, w_out = w_mlp_in[i].astype(bf16), w_mlp_out[i].astype(bf16)
        if kind == 0:
            xs = _attn_layer(geo, xs, mods, gains, attn_w_qkv[j], attn_w_o[j], attn_sink[j], w_in, w_out, skip_ctx)
        elif kind == 1:
            xs = _gla_layer(geo, xs, mods, gains, gla_w_in[j], gla_w_gate_down[j], gla_w_gate_up[j],
                            gla_gate_bias[j], gla_g_norm[j], gla_w_o[j], w_in, w_out, skip_ctx)
        elif kind == 2:
            xs = _rwkv_layer(geo, xs, mods, gains, rwkv_mix[j], rwkv_w_rkv[j], rwkv_w0[j], rwkv_w_down[j],
                             rwkv_w_up[j], rwkv_a0[j], rwkv_a_down[j], rwkv_a_up[j], rwkv_g_down[j], rwkv_g_up[j],
                             rwkv_k_k[j], rwkv_k_a[j], rwkv_r_k[j], rwkv_ln_w[j], rwkv_ln_b[j], rwkv_w_o[j],
                             w_in, w_out, skip_ctx)
        else:
            xs = _hgrn_layer(geo, xs, mods, gains, hgrn_w_in[j], hgrn_w_f[j], _hgrn_lower_bound(hgrn_lb, i),
                             hgrn_g_norm[j], hgrn_w_o[j], w_in, w_out, skip_ctx)
        if skip_ctx:
            return xs
    return xs[:, geo.n_ctx:]
```

```python
import functools

import jax
import jax.numpy as jnp
from jax import lax
from jax.experimental import pallas as pl
from jax.experimental.pallas import tpu as pltpu

f32 = jnp.float32
bf16 = jnp.bfloat16

D_MODEL = 1024
N_MOD = 6
MLP_HIDDEN = 4 * D_MODEL
NORM_EPS = 1e-6
NEG_INF = -1e30
GRID_W = 64

ATTN_HEADS = 16
ATTN_KV_HEADS = 4
ATTN_GROUP = ATTN_HEADS // ATTN_KV_HEADS
HEAD_DIM = 64
WINDOW = 128
ATTN_BLOCK = 128
ROPE_BASE = 10000.0
ROPE_AXIS_DIM = HEAD_DIM // 2
ROPE_FREQS = ROPE_AXIS_DIM // 2

GLA_HEADS = 4
GLA_KEY_DIM = D_MODEL // 2
GLA_DK = GLA_KEY_DIM // GLA_HEADS
GLA_DV = D_MODEL // GLA_HEADS
GLA_GATE_RANK = 16
GLA_TAU = 16.0
SCAN_CHUNK = 64

RWKV_HEAD_SIZE = 64
RWKV_LN_EPS = 64e-5
L2_EPS = 1e-12
RWKV_PAIR = 2 * RWKV_HEAD_SIZE

HGRN_EXPAND = 128
HGRN_HEADS = D_MODEL // HGRN_EXPAND

LANES = 128
MOD_ROWS = 8
VMEM_LIMIT = 56 * 1024 * 1024

NT = (((1,), (1,)), ((), ()))
TN = (((0,), (0,)), ((), ()))


def _params(n_grid):
    return pltpu.CompilerParams(dimension_semantics=("arbitrary",) * n_grid, vmem_limit_bytes=VMEM_LIMIT)


def _const_spec(shape):
    nd = len(shape)
    return pl.BlockSpec(shape, lambda *_: (0,) * nd, pipeline_mode=pl.Buffered(1))


def _dot(a, b):
    return jnp.dot(a, b, preferred_element_type=f32)


def _sigmoid(x):
    return 1.0 / (1.0 + jnp.exp(-x))


def _silu(x):
    return x * _sigmoid(x)


def _softplus(x):
    return jnp.maximum(x, 0.0) + jnp.log1p(jnp.exp(-jnp.abs(x)))


def _rms(x, gain):
    return x * lax.rsqrt(jnp.mean(x * x, axis=-1, keepdims=True) + NORM_EPS) * gain


def _modulate(x, gain, shift, scale):
    return _rms(x, gain) * (1.0 + scale) + shift


def _seg_sum(x, seg):
    r = lax.broadcasted_iota(jnp.int32, (LANES, LANES), 0) // seg
    c = lax.broadcasted_iota(jnp.int32, (LANES, LANES), 1) // seg
    ones_bd = (r == c).astype(bf16)
    hi = x.astype(bf16)
    lo = (x - hi.astype(f32)).astype(bf16)
    outs = []
    for j in range(x.shape[1] // LANES):
        sl = slice(j * LANES, (j + 1) * LANES)
        outs.append(_dot(hi[:, sl], ones_bd) + _dot(lo[:, sl], ones_bd))
    return jnp.concatenate(outs, axis=1)


def _cumsum_rows(g, reverse):
    n = g.shape[0]
    ri = lax.broadcasted_iota(jnp.int32, (n, n), 0)
    ci = lax.broadcasted_iota(jnp.int32, (n, n), 1)
    tri = ((ci >= ri) if reverse else (ci <= ri)).astype(bf16)
    hi = g.astype(bf16)
    lo = (g - hi.astype(f32)).astype(bf16)
    return _dot(tri, hi) + _dot(tri, lo)


def _mod_kernel(c_ref, w_ref, b_ref, o_ref):
    o_ref[0] = _dot(_silu(c_ref[...]).astype(bf16), w_ref[0]) + b_ref[0]


def _mod_vectors(c, c_ctx, w_mod, b_mod):
    depth, d, _ = w_mod.shape
    batch = c.shape[0]
    rows = -(-(batch + 1) // 8) * 8
    cc = jnp.zeros((rows, d), f32).at[:batch].set(c).at[batch].set(c_ctx)
    out = pl.pallas_call(
        _mod_kernel,
        grid=(depth, N_MOD),
        in_specs=[
            pl.BlockSpec((rows, d), lambda i, j: (0, 0)),
            pl.BlockSpec((1, d, d), lambda i, j: (i, 0, j)),
            pl.BlockSpec((1, 1, d), lambda i, j: (i, 0, j)),
        ],
        out_specs=pl.BlockSpec((1, rows, d), lambda i, j: (i, 0, j)),
        out_shape=jax.ShapeDtypeStruct((depth, rows, N_MOD * d), f32),
        compiler_params=_params(2),
        name="mod_vectors",
    )(cc, w_mod.astype(bf16), b_mod.reshape(depth, 1, N_MOD * d))
    lat = out[:, :batch].reshape(depth, batch, 1, N_MOD, d)
    con = jnp.broadcast_to(out[:, batch].reshape(depth, 1, 1, N_MOD, d), lat.shape)
    mods = jnp.concatenate([con, lat], axis=2)
    return jnp.pad(mods, ((0, 0), (0, 0), (0, 0), (0, MOD_ROWS - N_MOD), (0, 0)))


class _Geom:
    def __init__(self, batch, n_ctx, n_lat):
        self.batch, self.n_ctx, self.n_lat = batch, n_ctx, n_lat
        self.t = n_ctx + n_lat
        self.tm = min(256, n_ctx)
        assert n_ctx % self.tm == 0 and n_lat % self.tm == 0
        assert n_ctx % ATTN_BLOCK == 0 and n_lat % ATTN_BLOCK == 0 and n_lat % GRID_W == 0
        self.nt = self.t // self.tm
        self.ctx_tiles = n_ctx // self.tm
        self.nc = self.t // SCAN_CHUNK
        self.ctx_chunks = n_ctx // SCAN_CHUNK

    def x_spec(self, t0=0):
        return pl.BlockSpec((1, self.tm, D_MODEL), lambda b, t: (b, t + t0, 0))

    def mod_spec(self, t0=0):
        ct = self.ctx_tiles
        return pl.BlockSpec((1, 1, MOD_ROWS, D_MODEL), lambda b, t: (b, ((t + t0) >= ct).astype(jnp.int32), 0, 0))

    def bwd_chunk(self, i):
        cc = self.ctx_chunks
        return jnp.where(i < cc, cc - 1 - i, self.nc - 1 + cc - i)


def _lin_kernel(x_ref, m_ref, g_ref, w_ref, o_ref):
    h = _modulate(x_ref[0], g_ref[0:1], m_ref[0, 0, 0:1], m_ref[0, 0, 1:2])
    o_ref[0] = _dot(h.astype(bf16), w_ref[...])


def _mod_linear(geo, x, mods, gains, w, name):
    n = w.shape[1]
    return pl.pallas_call(
        _lin_kernel,
        grid=(geo.batch, geo.nt),
        in_specs=[geo.x_spec(), geo.mod_spec(), _const_spec((8, D_MODEL)), _const_spec((D_MODEL, n))],
        out_specs=pl.BlockSpec((1, geo.tm, n), lambda b, t: (b, t, 0)),
        out_shape=jax.ShapeDtypeStruct((geo.batch, geo.t, n), f32),
        compiler_params=_params(2),
        name=name,
    )(x, mods, gains, w)


def _head_rms(o, gain, width):
    outs = []
    for h in range(o.shape[1] // width):
        oh = o[:, h * width:(h + 1) * width]
        outs.append(oh * lax.rsqrt(jnp.mean(oh * oh, axis=-1, keepdims=True) + NORM_EPS) * gain)
    return jnp.concatenate(outs, axis=1)


def _pre_attn(refs):
    (o_ref,) = refs
    return o_ref[0]


def _pre_scan(width, refs):
    of_ref, ob_ref, gate_ref, gn_ref = refs
    o = _head_rms(of_ref[0] + ob_ref[0], gn_ref[...], width)
    return (o * _silu(gate_ref[0])).astype(bf16)


def _pre_rwkv(refs):
    yf_ref, yb_ref, r_ref, k0_ref, k1_ref, v_ref, g_ref, vec_ref = refs
    y = yf_ref[0] + yb_ref[0]
    inv_n = 1.0 / RWKV_HEAD_SIZE
    mu = _seg_sum(y, RWKV_HEAD_SIZE) * inv_n
    dlt = y - mu
    var = _seg_sum(dlt * dlt, RWKV_HEAD_SIZE) * inv_n
    yn = dlt * lax.rsqrt(var + RWKV_LN_EPS) * vec_ref[0:1] + vec_ref[1:2]
    bonus = _seg_sum(r_ref[0] * (0.5 * (k0_ref[0] + k1_ref[0])) * vec_ref[2:3], RWKV_HEAD_SIZE) * v_ref[0]
    return ((yn + bonus) * g_ref[0]).astype(bf16)


def _post_kernel(pre, n_mix, x_ref, m_ref, g_ref, *rest):
    mix_refs = rest[:n_mix]
    wo_ref, win_ref, wout_ref, out_ref = rest[n_mix:]
    o = pre(mix_refs)
    y = _dot(o, wo_ref[...])
    x1 = x_ref[0] + m_ref[0, 0, 2:3] * _rms(y, g_ref[1:2])
    h2 = _modulate(x1, g_ref[2:3], m_ref[0, 0, 3:4], m_ref[0, 0, 4:5]).astype(bf16)
    acc = jnp.zeros_like(x1)
    for c in range(MLP_HIDDEN // D_MODEL):
        u = jnp.square(jnp.maximum(_dot(h2, win_ref[:, c * D_MODEL:(c + 1) * D_MODEL]), 0.0))
        acc = acc + _dot(u.astype(bf16), wout_ref[c * D_MODEL:(c + 1) * D_MODEL, :])
    out_ref[0] = x1 + m_ref[0, 0, 5:6] * _rms(acc, g_ref[3:4])


def _post_mlp(geo, x, mods, gains, pre, mix_args, mix_specs, w_o, w_in, w_out, skip_ctx, name):
    t0 = geo.ctx_tiles if skip_ctx else 0
    return pl.pallas_call(
        functools.partial(_post_kernel, pre, len(mix_args)),
        grid=(geo.batch, geo.nt - t0),
        in_specs=[geo.x_spec(t0), geo.mod_spec(t0), _const_spec((8, D_MODEL))] + mix_specs + [
            _const_spec((D_MODEL, D_MODEL)), _const_spec((D_MODEL, MLP_HIDDEN)), _const_spec((MLP_HIDDEN, D_MODEL))],
        out_specs=pl.BlockSpec((1, geo.tm, D_MODEL), lambda b, t: (b, t, 0)),
        out_shape=jax.ShapeDtypeStruct((geo.batch, geo.t - t0 * geo.tm, D_MODEL), f32),
        compiler_params=_params(2),
        name=name,
    )(x, mods, gains, *mix_args, w_o, w_in, w_out)


def _tile_spec(geo, width, col, t0):
    return pl.BlockSpec((1, geo.tm, width), lambda b, t: (b, t + t0, col))


def _rope(x, cos, sin):
    w = x.shape[1]
    reps = w // LANES
    cw = jnp.concatenate([cos] * reps, axis=1)
    sw = jnp.concatenate([sin] * reps, axis=1)
    lane = lax.broadcasted_iota(jnp.int32, x.shape, 1)
    first = (lane % ROPE_AXIS_DIM) < ROPE_FREQS
    partner = jnp.where(first, pltpu.roll(x, w - ROPE_FREQS, axis=1), pltpu.roll(x, ROPE_FREQS, axis=1))
    return x * cw + partner * sw


def _attn_proj_kernel(x_ref, m_ref, g_ref, cos_ref, sin_ref, wq_ref, wk_ref, wv_ref, q_ref, k_ref, v_ref):
    h = _modulate(x_ref[0], g_ref[0:1], m_ref[0, 0, 0:1], m_ref[0, 0, 1:2]).astype(bf16)
    cos, sin = cos_ref[...], sin_ref[...]
    q = _rope(_dot(h, wq_ref[...]) * (HEAD_DIM ** -0.5), cos, sin)
    k = _rope(_dot(h, wk_ref[...]), cos, sin)
    v = _dot(h, wv_ref[...])
    for hd in range(ATTN_HEADS):
        q_ref[0, hd] = q[:, hd * HEAD_DIM:(hd + 1) * HEAD_DIM].astype(bf16)
    for hd in range(ATTN_KV_HEADS):
        k_ref[0, hd] = k[:, hd * HEAD_DIM:(hd + 1) * HEAD_DIM].astype(bf16)
        v_ref[0, hd] = v[:, hd * HEAD_DIM:(hd + 1) * HEAD_DIM].astype(bf16)


def _attn_core_kernel(geo, sink_ref, q_ref, kp_ref, kc_ref, kn_ref, kx_ref, vp_ref, vc_ref, vn_ref, vx_ref, o_ref):
    qb = pl.program_id(1)
    blk = ATTN_BLOCK
    n_loc = 3 * blk
    n_keys = n_loc + geo.n_ctx
    rows = ATTN_GROUP * blk
    first_lat = geo.n_ctx // blk
    n_blocks = geo.t // blk
    row = lax.broadcasted_iota(jnp.int32, (blk, n_keys), 0)
    col = lax.broadcasted_iota(jnp.int32, (blk, n_keys), 1)
    kblk = qb - 1 + col // blk
    ok_local = ((jnp.abs(col - blk - row) <= WINDOW) & (qb >= first_lat) & (kblk >= first_lat) & (kblk < n_blocks))
    ok = (col >= n_loc) | ok_local
    scores, vals = [], []
    for j in range(ATTN_KV_HEADS):
        q4 = q_ref[0, ATTN_GROUP * j:ATTN_GROUP * (j + 1)].reshape(rows, HEAD_DIM)
        keys = jnp.concatenate([kp_ref[0, j], kc_ref[0, j], kn_ref[0, j], kx_ref[0, j]], axis=0)
        vals.append(jnp.concatenate([vp_ref[0, j], vc_ref[0, j], vn_ref[0, j], vx_ref[0, j]], axis=0))
        scores.append(lax.dot_general(q4, keys, NT, preferred_element_type=f32))
    heads = [(j, g) for j in range(ATTN_KV_HEADS) for g in range(ATTN_GROUP)]
    sg = [jnp.where(ok, scores[j][g * blk:(g + 1) * blk], NEG_INF) for j, g in heads]
    sk = [sink_ref[ATTN_GROUP * j + g] for j, g in heads]
    m = [jnp.maximum(jnp.max(s, axis=-1, keepdims=True), k) for s, k in zip(sg, sk)]
    e = [jnp.exp(s - mm) for s, mm in zip(sg, m)]
    inv = [1.0 / (jnp.sum(ee, axis=-1, keepdims=True) + jnp.exp(k - mm)) for ee, k, mm in zip(e, sk, m)]
    p = [(ee * ii).astype(bf16) for ee, ii in zip(e, inv)]
    outs = []
    for j in range(ATTN_KV_HEADS):
        o4 = _dot(jnp.concatenate(p[ATTN_GROUP * j:ATTN_GROUP * (j + 1)], axis=0), vals[j])
        outs += [o4[g * blk:(g + 1) * blk] for g in range(ATTN_GROUP)]
    o_ref[0] = jnp.concatenate(outs, axis=1).astype(bf16)


def _rope_tables(geo):
    inv_freq = ROPE_BASE ** (-jnp.arange(ROPE_FREQS, dtype=f32) * 2.0 / ROPE_AXIS_DIM)
    pos = jnp.arange(geo.n_lat)
    row = (pos // GRID_W).astype(f32)
    col = (pos % GRID_W).astype(f32)
    ang = jnp.stack([row[:, None] * inv_freq, col[:, None] * inv_freq], axis=1)
    cos = jnp.cos(ang)
    sin = jnp.sin(ang)
    cos_h = jnp.concatenate([cos, cos], axis=2).reshape(geo.n_lat, HEAD_DIM)
    sin_h = jnp.concatenate([-sin, sin], axis=2).reshape(geo.n_lat, HEAD_DIM)
    cos_t = jnp.concatenate([jnp.ones((geo.n_ctx, HEAD_DIM), f32), cos_h], axis=0)
    sin_t = jnp.concatenate([jnp.zeros((geo.n_ctx, HEAD_DIM), f32), sin_h], axis=0)
    return jnp.tile(cos_t, (1, 2)), jnp.tile(sin_t, (1, 2))


def _attn_layer(geo, x, mods, gains, w_qkv, w_o, sink, w_in, w_out, skip_ctx):
    b, t, tm = geo.batch, geo.t, geo.tm
    q_cols = ATTN_HEADS * HEAD_DIM
    kv_cols = ATTN_KV_HEADS * HEAD_DIM
    wb = w_qkv.astype(bf16)
    cos_t, sin_t = _rope_tables(geo)
    tab_spec = pl.BlockSpec((tm, LANES), lambda bb, tt: (tt, 0))
    q, k, v = pl.pallas_call(
        _attn_proj_kernel,
        grid=(b, geo.nt),
        in_specs=[geo.x_spec(), geo.mod_spec(), _const_spec((8, D_MODEL)), tab_spec, tab_spec,
                  _const_spec((D_MODEL, q_cols)), _const_spec((D_MODEL, kv_cols)), _const_spec((D_MODEL, kv_cols))],
        out_specs=[pl.BlockSpec((1, ATTN_HEADS, tm, HEAD_DIM), lambda bb, tt: (bb, 0, tt, 0)),
                   pl.BlockSpec((1, ATTN_KV_HEADS, tm, HEAD_DIM), lambda bb, tt: (bb, 0, tt, 0)),
                   pl.BlockSpec((1, ATTN_KV_HEADS, tm, HEAD_DIM), lambda bb, tt: (bb, 0, tt, 0))],
        out_shape=[jax.ShapeDtypeStruct((b, ATTN_HEADS, t, HEAD_DIM), bf16),
                   jax.ShapeDtypeStruct((b, ATTN_KV_HEADS, t, HEAD_DIM), bf16),
                   jax.ShapeDtypeStruct((b, ATTN_KV_HEADS, t, HEAD_DIM), bf16)],
        compiler_params=_params(2),
        name="attn_proj",
    )(x, mods, gains, cos_t, sin_t, wb[:, :q_cols], wb[:, q_cols:q_cols + kv_cols], wb[:, q_cols + kv_cols:])

    blk = ATTN_BLOCK
    n_blocks = t // blk

    def kv_spec(off):
        return pl.BlockSpec((1, ATTN_KV_HEADS, blk, HEAD_DIM),
                            lambda bb, qb: (bb, 0, jnp.clip(qb + off, 0, n_blocks - 1), 0))

    ctx_spec = pl.BlockSpec((1, ATTN_KV_HEADS, geo.n_ctx, HEAD_DIM), lambda bb, qb: (bb, 0, 0, 0))
    o = pl.pallas_call(
        functools.partial(_attn_core_kernel, geo),
        grid=(b, n_blocks),
        in_specs=[pl.BlockSpec(memory_space=pltpu.SMEM),
                  pl.BlockSpec((1, ATTN_HEADS, blk, HEAD_DIM), lambda bb, qb: (bb, 0, qb, 0)),
                  kv_spec(-1), kv_spec(0), kv_spec(1), ctx_spec,
                  kv_spec(-1), kv_spec(0), kv_spec(1), ctx_spec],
        out_specs=pl.BlockSpec((1, blk, q_cols), lambda bb, qb: (bb, qb, 0)),
        out_shape=jax.ShapeDtypeStruct((b, t, q_cols), bf16),
        compiler_params=_params(2),
        name="attn_core",
    )(sink, q, k, k, k, k, v, v, v, v)

    t0 = geo.ctx_tiles if skip_ctx else 0
    return _post_mlp(geo, x, mods, gains, _pre_attn, [o], [_tile_spec(geo, D_MODEL, 0, t0)],
                     w_o.astype(bf16), w_in, w_out, skip_ctx, "attn_post")


def _gated_chunk(dirs, heads, dk, dv):
    units = []
    for q, k, v, g, st_ref, base, reverse in dirs:
        n = q.shape[0]
        b = _cumsum_rows(g, reverse)
        btot = b[0:1] if reverse else b[n - 1:n]
        q_dec = (q * jnp.exp(b)).astype(bf16)
        k_inv32 = k * jnp.exp(-b)
        k_inv = k_inv32.astype(bf16)
        dec = jnp.exp(btot)
        k_end = (k_inv32 * dec).astype(bf16)
        vb = v.astype(bf16)
        ri = lax.broadcasted_iota(jnp.int32, (n, n), 0)
        ci = lax.broadcasted_iota(jnp.int32, (n, n), 1)
        tri = (ci >= ri) if reverse else (ci <= ri)
        for h in range(heads):
            ks = slice(h * dk, (h + 1) * dk)
            vs = slice(h * dv, (h + 1) * dv)
            units.append(dict(q=q_dec[:, ks], ki=k_inv[:, ks], ke=k_end[:, ks], v=vb[:, vs], dec=dec[:, ks],
                              tri=tri, st=st_ref[base + h], ref=st_ref, h=base + h))
    for u in units:
        a = lax.dot_general(u["q"], u["ki"], NT, preferred_element_type=f32)
        u["a"] = jnp.where(u["tri"], a, 0.0).astype(bf16)
        u["qs"] = lax.dot_general(u["q"], u["st"].astype(bf16), NT, preferred_element_type=f32)
    for u in units:
        u["o"] = _dot(u["a"], u["v"]) + u["qs"]
        u["new"] = u["st"] * u["dec"] + lax.dot_general(u["v"], u["ke"], TN, preferred_element_type=f32)
    for u in units:
        u["ref"][u["h"]] = u["new"]
    return [jnp.concatenate([u["o"] for u in units[d * heads:(d + 1) * heads]], axis=1) for d in range(len(dirs))]


def _gla_gate(zd, wu_ref, bias_ref, d):
    zg = _dot(zd.astype(bf16), wu_ref[d]) + bias_ref[d:d + 1]
    return (jnp.minimum(zg, 0.0) - jnp.log1p(jnp.exp(-jnp.abs(zg)))) * (1.0 / GLA_TAU)


def _gla_scan_kernel(qkf_ref, vf_ref, zdf_ref, qkb_ref, vb_ref, zdb_ref, wu_ref, bias_ref,
                     of_ref, ob_ref, sf_ref, sb_ref):
    @pl.when(pl.program_id(1) == 0)
    def _():
        sf_ref[...] = jnp.zeros_like(sf_ref)
        sb_ref[...] = jnp.zeros_like(sb_ref)

    dirs, dests = [], []
    for row in range(qkf_ref.shape[0]):
        for d, (qk_ref, v_ref, zd_ref, s_ref, o_ref) in enumerate(
                ((qkf_ref, vf_ref, zdf_ref, sf_ref, of_ref), (qkb_ref, vb_ref, zdb_ref, sb_ref, ob_ref))):
            qk = qk_ref[row]
            q = qk[:, :GLA_KEY_DIM] * (GLA_DK ** -0.5)
            k = qk[:, GLA_KEY_DIM:]
            dirs.append((q, k, v_ref[row], _gla_gate(zd_ref[row], wu_ref, bias_ref, d), s_ref, row * GLA_HEADS,
                         d == 1))
            dests.append((o_ref, row))
    for o, (o_ref, row) in zip(_gated_chunk(dirs, GLA_HEADS, GLA_DK, GLA_DV), dests):
        o_ref[row] = o


def _hgrn_scan_kernel(qf_ref, if_ref, zff_ref, qb_ref, ib_ref, zfb_ref, lb_ref, of_ref, ob_ref, sf_ref, sb_ref):
    @pl.when(pl.program_id(1) == 0)
    def _():
        sf_ref[...] = jnp.zeros_like(sf_ref)
        sb_ref[...] = jnp.zeros_like(sb_ref)

    lb = lb_ref[...]
    dirs, dests = [], []
    for row in range(qf_ref.shape[0]):
        for d, (q_ref, i_ref, zf_ref, s_ref, o_ref) in enumerate(
                ((qf_ref, if_ref, zff_ref, sf_ref, of_ref), (qb_ref, ib_ref, zfb_ref, sb_ref, ob_ref))):
            f = lb + (1.0 - lb) * _sigmoid(zf_ref[row])
            dirs.append((_silu(q_ref[row]), 1.0 - f, i_ref[row], jnp.log(f), s_ref, row * HGRN_HEADS, d == 1))
            dests.append((o_ref, row))
    for o, (o_ref, row) in zip(_gated_chunk(dirs, HGRN_HEADS, HGRN_EXPAND, HGRN_EXPAND), dests):
        o_ref[row] = o


SCAN_ROWS_PER_STEP = 2


def _scan_rows(geo):
    return SCAN_ROWS_PER_STEP if geo.batch % SCAN_ROWS_PER_STEP == 0 else 1


def _chunk_spec(geo, width, col, reverse):
    rows = _scan_rows(geo)
    if reverse:
        return pl.BlockSpec((rows, SCAN_CHUNK, width), lambda b, i: (b, geo.bwd_chunk(i), col))
    return pl.BlockSpec((rows, SCAN_CHUNK, width), lambda b, i: (b, i, col))


def _scan_call(geo, body, args, specs, heads, dk, dv, name):
    out_sds = jax.ShapeDtypeStruct((geo.batch, geo.t, heads * dv), f32)
    rows = _scan_rows(geo)
    state = pltpu.VMEM((rows * heads, dv, dk), f32)
    return pl.pallas_call(
        body,
        grid=(geo.batch // rows, geo.nc),
        in_specs=specs,
        out_specs=[_chunk_spec(geo, heads * dv, 0, False), _chunk_spec(geo, heads * dv, 0, True)],
        out_shape=[out_sds, out_sds],
        scratch_shapes=[state, state],
        compiler_params=_params(2),
        name=name,
    )(*args)


def _gla_layer(geo, x, mods, gains, w_in_p, w_gd, w_gu, g_bias, g_norm, w_o, w_in, w_out, skip_ctx):
    r = GLA_GATE_RANK
    n_z = 2 * GLA_KEY_DIM + 2 * D_MODEL
    w_all = jnp.concatenate([w_in_p, w_gd[0], w_gd[1], jnp.zeros((D_MODEL, LANES - 2 * r), f32)], axis=1).astype(bf16)
    z = _mod_linear(geo, x, mods, gains, w_all, "gla_proj")
    wu = jnp.zeros((2, LANES, GLA_KEY_DIM), f32).at[0, :r].set(w_gu[0]).at[1, r:2 * r].set(w_gu[1]).astype(bf16)
    zd_col = n_z // LANES
    specs = []
    for rev in (False, True):
        specs += [_chunk_spec(geo, 2 * GLA_KEY_DIM, 0, rev), _chunk_spec(geo, D_MODEL, 1, rev),
                  _chunk_spec(geo, LANES, zd_col, rev)]
    specs += [_const_spec((2, LANES, GLA_KEY_DIM)), _const_spec((2, GLA_KEY_DIM))]
    o_f, o_b = _scan_call(geo, _gla_scan_kernel, [z, z, z, z, z, z, wu, g_bias], specs,
                          GLA_HEADS, GLA_DK, GLA_DV, "gla_scan")
    t0 = geo.ctx_tiles if skip_ctx else 0
    mix_specs = [_tile_spec(geo, D_MODEL, 0, t0), _tile_spec(geo, D_MODEL, 0, t0), _tile_spec(geo, D_MODEL, 2, t0),
                 _const_spec((1, GLA_DV))]
    return _post_mlp(geo, x, mods, gains, functools.partial(_pre_scan, GLA_DV),
                     [o_f, o_b, z, g_norm.reshape(1, GLA_DV)], mix_specs,
                     w_o.astype(bf16), w_in, w_out, skip_ctx, "gla_post")


def _hgrn_layer(geo, x, mods, gains, w_in_p, w_f, lower_bound, g_norm, w_o, w_in, w_out, skip_ctx):
    w_all = jnp.concatenate([w_in_p, w_f[0], w_f[1]], axis=1).astype(bf16)
    z = _mod_linear(geo, x, mods, gains, w_all, "hgrn_proj")
    specs = []
    for rev in (False, True):
        specs += [_chunk_spec(geo, D_MODEL, 0, rev), _chunk_spec(geo, D_MODEL, 1, rev),
                  _chunk_spec(geo, D_MODEL, 4 if rev else 3, rev)]
    specs += [_const_spec((1, D_MODEL))]
    o_f, o_b = _scan_call(geo, _hgrn_scan_kernel, [z, z, z, z, z, z, lower_bound.reshape(1, D_MODEL)], specs,
                          HGRN_HEADS, HGRN_EXPAND, HGRN_EXPAND, "hgrn_scan")
    t0 = geo.ctx_tiles if skip_ctx else 0
    mix_specs = [_tile_spec(geo, D_MODEL, 0, t0), _tile_spec(geo, D_MODEL, 0, t0), _tile_spec(geo, D_MODEL, 2, t0),
                 _const_spec((1, HGRN_EXPAND))]
    return _post_mlp(geo, x, mods, gains, functools.partial(_pre_scan, HGRN_EXPAND),
                     [o_f, o_b, z, g_norm.reshape(1, HGRN_EXPAND)], mix_specs,
                     w_o.astype(bf16), w_in, w_out, skip_ctx, "hgrn_post")


def _rwkv_proj_kernel(geo, x_ref, xp_ref, xn_ref, m_ref, g_ref, mix_ref, vec_ref, wrkv_ref, wdn_ref, wup_ref,
                      aup_ref, gup_ref, r_ref, v_ref, gate_ref, kk_ref, lw0_ref, lw1_ref, a0_ref, a1_ref,
                      k0_ref, k1_ref):
    t = pl.program_id(1)
    tm = geo.tm
    gain, shift, scale = g_ref[0:1], m_ref[0, 0, 0:1], m_ref[0, 0, 1:2]
    h = _modulate(x_ref[0], gain, shift, scale)
    seg_first = (t == 0) | (t == geo.ctx_tiles)
    seg_last = (t == geo.ctx_tiles - 1) | (t == geo.nt - 1)
    h_prev = jnp.where(seg_first, 0.0, _modulate(xp_ref[0], gain, shift, scale)[7:8])
    h_next = jnp.where(seg_last, 0.0, _modulate(xn_ref[0], gain, shift, scale)[0:1])
    row = lax.broadcasted_iota(jnp.int32, h.shape, 0)
    up = jnp.where(row == 0, h_prev, pltpu.roll(h, 1, axis=0))
    dn = jnp.where(row == tm - 1, h_next, pltpu.roll(h, tm - 1, axis=0))
    dx = 0.5 * (up + dn) - h

    def mixed(n):
        return (h + dx * mix_ref[n:n + 1]).astype(bf16)

    r = _dot(mixed(0), wrkv_ref[0])
    k = _dot(mixed(1), wrkv_ref[1])
    v = _dot(mixed(2), wrkv_ref[2])
    dw = jnp.tanh(_dot(mixed(3), wdn_ref[:, 0:LANES])).astype(bf16)
    da = _dot(mixed(4), wdn_ref[:, LANES:2 * LANES]).astype(bf16)
    dg = _sigmoid(_dot(mixed(5), wdn_ref[:, 2 * LANES:3 * LANES])).astype(bf16)
    r_ref[0] = r
    v_ref[0] = v
    gate_ref[0] = _dot(dg, gup_ref[...])
    kk = k * vec_ref[4:5]
    kk_ref[0] = kk / jnp.maximum(jnp.sqrt(_seg_sum(kk * kk, RWKV_HEAD_SIZE)), L2_EPS)
    for d, (lw_ref, a_ref, kd_ref) in enumerate(((lw0_ref, a0_ref, k0_ref), (lw1_ref, a1_ref, k1_ref))):
        w_log = -_softplus(-(vec_ref[d:d + 1] + _dot(dw, wup_ref[d]))) - 0.5
        lw_ref[0] = -jnp.exp(w_log)
        a = _sigmoid(vec_ref[2 + d:3 + d] + _dot(da, aup_ref[d]))
        a_ref[0] = a
        kd_ref[0] = k * (1.0 + (a - 1.0) * vec_ref[5:6])


def _rwkv_chunk_units(units):
    n = SCAN_CHUNK
    hs = RWKV_HEAD_SIZE
    lane = lax.broadcasted_iota(jnp.int32, (n, RWKV_PAIR), 1)
    t_i = lax.broadcasted_iota(jnp.int32, (n, RWKV_PAIR), 0)
    s_i = lane % hs
    head0 = lane < hs
    ri = lax.broadcasted_iota(jnp.int32, (RWKV_PAIR, RWKV_PAIR), 0)
    ci = lax.broadcasted_iota(jnp.int32, (RWKV_PAIR, RWKV_PAIR), 1)
    same_head = ri // hs == ci // hs
    top_rows = ri < hs

    def block_diag(x):
        return jnp.concatenate([jnp.where(head0, x, 0.0), jnp.where(head0, 0.0, x)], axis=0)

    def cross_blocks(b):
        return (t_i // (2 * b) == s_i // (2 * b)) & (t_i // b != s_i // b)

    st = []
    for r, kd, v, kk, a, lw, ht, reverse in units:
        c = _cumsum_rows(lw, reverse)
        tot = c[0:1] if reverse else c[n - 1:n]
        e_neg = jnp.exp(-c)
        dec = jnp.exp(tot)
        kb = jnp.concatenate([kd * e_neg, kk * a * e_neg], axis=0)
        kb_t = kb.T
        kb_sw = pltpu.roll(kb_t, hs, axis=1)
        rhs = jnp.concatenate([jnp.where(same_head, jnp.where(top_rows, kb_t, kb_sw), 0.0),
                               jnp.where(same_head, jnp.where(top_rows, kb_sw, kb_t), 0.0)], axis=1)
        st.append(dict(
            strict=(s_i > t_i) if reverse else (s_i < t_i), incl=(s_i >= t_i) if reverse else (s_i <= t_i),
            a_bar=(-kk * jnp.exp(c - lw)).astype(bf16), r_bar=(r * jnp.exp(c)).astype(bf16), rhs=rhs.astype(bf16),
            v_bd=block_diag(v).astype(bf16), h_t=ht.T.astype(bf16),
            ends=(kb * dec).astype(bf16),
            v=v, ht=ht, dec=dec))
    for s in st:
        g = _dot(jnp.concatenate([s["a_bar"], s["r_bar"]], axis=0), s["rhs"])
        s["a_ak"] = jnp.where(s["strict"], g[0:n, 0:RWKV_PAIR], 0.0).astype(bf16)
        s["nmat"] = jnp.where(s["strict"], g[0:n, RWKV_PAIR:], 0.0)
        s["q_k"] = jnp.where(s["incl"], g[n:, 0:RWKV_PAIR], 0.0).astype(bf16)
        s["q_b"] = jnp.where(s["incl"], g[n:, RWKV_PAIR:], 0.0).astype(bf16)
    for s in st:
        s["inv"] = jnp.where(s_i == t_i, 1.0, jnp.where(cross_blocks(1), s["nmat"], 0.0))
    b = 2
    while 2 * b < n:
        for s in st:
            s["pc"] = _dot(s["inv"].astype(bf16), block_diag(jnp.where(cross_blocks(b), s["nmat"], 0.0)).astype(bf16))
        for s in st:
            s["inv"] = s["inv"] + _dot(s["pc"].astype(bf16), block_diag(s["inv"]).astype(bf16))
        b *= 2
    for s in st:
        x = _dot(jnp.concatenate([s["a_bar"], s["a_ak"]], axis=1), jnp.concatenate([s["h_t"], s["v_bd"]], axis=0))
        s["inv"] = s["inv"].astype(bf16)
        s["w"] = _dot(s["inv"], block_diag(x).astype(bf16))
    for s in st:
        s["cw"] = _dot(jnp.where(cross_blocks(n // 2), s["nmat"], 0.0).astype(bf16), block_diag(s["w"]).astype(bf16))
    for s in st:
        s["u"] = s["w"] + _dot(s["inv"], block_diag(s["cw"]).astype(bf16))
    for s in st:
        s["vu_t"] = jnp.concatenate([s["v"], s["u"]], axis=0).T.astype(bf16)
    for s in st:
        s["y"] = _dot(jnp.concatenate([s["r_bar"], s["q_k"], s["q_b"]], axis=1),
                      jnp.concatenate([s["h_t"], s["v_bd"], block_diag(s["u"]).astype(bf16)], axis=0))
    for s in st:
        s["upd"] = _dot(s["vu_t"], s["ends"])
    return [(s["y"], s["ht"] * s["dec"] + jnp.where(same_head, s["upd"], 0.0)) for s in st]


def _rwkv_scan_kernel(rows, pairs, *refs):
    fwd, bwd = refs[0:6], refs[6:12]
    yf_ref, yb_ref, hf_ref, hb_ref = refs[12:]

    @pl.when(pl.program_id(2) == 0)
    def _():
        hf_ref[...] = jnp.zeros_like(hf_ref)
        hb_ref[...] = jnp.zeros_like(hb_ref)

    units, dests = [], []
    for ins, y_ref, h_ref, reverse in ((fwd, yf_ref, hf_ref, False), (bwd, yb_ref, hb_ref, True)):
        r_ref, v_ref, kk_ref, lw_ref, a_ref, kd_ref = ins
        for row in range(rows):
            for p in range(pairs):
                sl = slice(p * RWKV_PAIR, (p + 1) * RWKV_PAIR)
                slot = row * pairs + p
                units.append((r_ref[row, :, sl], kd_ref[row, :, sl], v_ref[row, :, sl], kk_ref[row, :, sl],
                              a_ref[row, :, sl], lw_ref[row, :, sl], h_ref[slot], reverse))
                dests.append((y_ref, h_ref, row, slot, sl))
    for (y, h_new), (y_ref, h_ref, row, slot, sl) in zip(_rwkv_chunk_units(units), dests):
        y_ref[row, :, sl] = y
        h_ref[slot] = h_new


RWKV_PAIRS_PER_STEP = 8
RWKV_ROWS_PER_STEP = 2


def _rwkv_layer(geo, x, mods, gains, mix, w_rkv, w0, w_down, w_up, a0, a_down, a_up, g_down, g_up, k_k, k_a,
                r_k, ln_w, ln_b, w_o, w_in, w_out, skip_ctx):
    b, t, tm = geo.batch, geo.t, geo.tm
    d = D_MODEL
    rank = w_down.shape[-1]
    assert 2 * rank == LANES and a_down.shape[-1] == rank and g_down.shape[-1] == LANES
    mix8 = jnp.pad(mix, ((0, 2), (0, 0)))
    vec = jnp.stack([w0[0], w0[1], a0[0], a0[1], k_k, k_a, jnp.zeros_like(k_k), jnp.zeros_like(k_k)])
    w_dn = jnp.concatenate([w_down[0], w_down[1], a_down[0], a_down[1], g_down], axis=1).astype(bf16)

    def padded_up(w):
        return jnp.zeros((2, LANES, d), f32).at[0, :rank].set(w[0]).at[1, rank:].set(w[1]).astype(bf16)

    halo = 8
    n_halo = t // halo
    per = tm // halo
    x_prev = pl.BlockSpec((1, halo, d), lambda bb, tt: (bb, jnp.maximum(tt * per - 1, 0), 0))
    x_next = pl.BlockSpec((1, halo, d), lambda bb, tt: (bb, jnp.minimum((tt + 1) * per, n_halo - 1), 0))
    out_spec = pl.BlockSpec((1, tm, d), lambda bb, tt: (bb, tt, 0))
    sds = jax.ShapeDtypeStruct((b, t, d), f32)
    r, v, gate, kk, lw0, lw1, a_0, a_1, k0, k1 = pl.pallas_call(
        functools.partial(_rwkv_proj_kernel, geo),
        grid=(b, geo.nt),
        in_specs=[geo.x_spec(), x_prev, x_next, geo.mod_spec(), _const_spec((8, d)), _const_spec((8, d)),
                  _const_spec((8, d)), _const_spec((3, d, d)), _const_spec((d, 3 * LANES)),
                  _const_spec((2, LANES, d)), _const_spec((2, LANES, d)), _const_spec((LANES, d))],
        out_specs=[out_spec] * 10,
        out_shape=[sds] * 10,
        compiler_params=_params(2),
        name="rwkv_proj",
    )(x, x, x, mods, gains, mix8, vec, w_rkv.astype(bf16), w_dn, padded_up(w_up), padded_up(a_up),
      g_up.astype(bf16))

    pairs = RWKV_PAIRS_PER_STEP
    rows = RWKV_ROWS_PER_STEP if b % RWKV_ROWS_PER_STEP == 0 else 1
    width = pairs * RWKV_PAIR
    groups = d // width

    def cspec(reverse):
        if reverse:
            return pl.BlockSpec((rows, SCAN_CHUNK, width), lambda bb, gg, i: (bb, geo.bwd_chunk(i), gg))
        return pl.BlockSpec((rows, SCAN_CHUNK, width), lambda bb, gg, i: (bb, i, gg))

    state = pltpu.VMEM((rows * pairs, RWKV_PAIR, RWKV_PAIR), f32)
    y_f, y_b = pl.pallas_call(
        functools.partial(_rwkv_scan_kernel, rows, pairs),
        grid=(b // rows, groups, geo.nc),
        in_specs=[cspec(False)] * 6 + [cspec(True)] * 6,
        out_specs=[cspec(False), cspec(True)],
        out_shape=[sds, sds],
        scratch_shapes=[state, state],
        compiler_params=_params(3),
        name="rwkv_scan",
    )(r, v, kk, lw0, a_0, k0, r, v, kk, lw1, a_1, k1)

    t0 = geo.ctx_tiles if skip_ctx else 0
    post_vec = jnp.stack([ln_w, ln_b, r_k.reshape(d)] + [jnp.zeros_like(ln_w)] * 5)
    mix_specs = [_tile_spec(geo, d, 0, t0)] * 7 + [_const_spec((8, d))]
    return _post_mlp(geo, x, mods, gains, _pre_rwkv, [y_f, y_b, r, k0, k1, v, gate, post_vec], mix_specs,
                     w_o.astype(bf16), w_in, w_out, skip_ctx, "rwkv_post")


def _hgrn_lower_bound(lb_param, layer):
    p = jax.nn.softmax(lb_param.astype(f32), axis=0)
    return (jnp.cumsum(p, axis=0) - p[0])[layer]


def kernel(x, c, ctx, c_ctx, w_mod, b_mod, g_pre_mix, g_post_mix, g_pre_mlp, g_post_mlp, w_mlp_in, w_mlp_out, attn_w_qkv, attn_w_o, attn_sink, gla_w_in, gla_w_gate_down, gla_w_gate_up, gla_gate_bias, gla_g_norm, gla_w_o, rwkv_mix, rwkv_w_rkv, rwkv_w0, rwkv_w_down, rwkv_w_up, rwkv_a0, rwkv_a_down, rwkv_a_up, rwkv_g_down, rwkv_g_up, rwkv_k_k, rwkv_k_a, rwkv_r_k, rwkv_ln_w, rwkv_ln_b, rwkv_w_o, hgrn_w_in, hgrn_w_f, hgrn_lb, hgrn_g_norm, hgrn_w_o):
    depth = w_mod.shape[0]
    geo = _Geom(x.shape[0], ctx.shape[1], x.shape[1])
    mods_all = _mod_vectors(c, c_ctx, w_mod, b_mod)
    xs = jnp.concatenate([ctx, x], axis=1)
    for i in range(depth):
        kind, j = i % 4, i // 4
        skip_ctx = i == depth - 1
        mods = mods_all[i]
        gains = jnp.stack([g_pre_mix[i], g_post_mix[i], g_pre_mlp[i], g_post_mlp[i]] + [jnp.zeros_like(g_pre_mix[i])] * 4)
        w_in, w_out = w_mlp_in[i].astype(bf16), w_mlp_out[i].astype(bf16)
        if kind == 0:
            xs = _attn_layer(geo, xs, mods, gains, attn_w_qkv[j], attn_w_o[j], attn_sink[j], w_in, w_out, skip_ctx)
        elif kind == 1:
            xs = _gla_layer(geo, xs, mods, gains, gla_w_in[j], gla_w_gate_down[j], gla_w_gate_up[j],
                            gla_gate_bias[j], gla_g_norm[j], gla_w_o[j], w_in, w_out, skip_ctx)
        elif kind == 2:
            xs = _rwkv_layer(geo, xs, mods, gains, rwkv_mix[j], rwkv_w_rkv[j], rwkv_w0[j], rwkv_w_down[j],
                             rwkv_w_up[j], rwkv_a0[j], rwkv_a_down[j], rwkv_a_up[j], rwkv_g_down[j], rwkv_g_up[j],
                             rwkv_k_k[j], rwkv_k_a[j], rwkv_r_k[j], rwkv_ln_w[j], rwkv_ln_b[j], rwkv_w_o[j],
                             w_in, w_out, skip_ctx)
        else:
            xs = _hgrn_layer(geo, xs, mods, gains, hgrn_w_in[j], hgrn_w_f[j], _hgrn_lower_bound(hgrn_lb, i),
                             hgrn_g_norm[j], hgrn_w_o[j], w_in, w_out, skip_ctx)
        if skip_ctx:
            return xs
    return xs[:, geo.n_ctx:]
```

```python
import functools

import jax
import jax.numpy as jnp
from jax import lax
from jax.experimental import pallas as pl
from jax.experimental.pallas import tpu as pltpu

f32 = jnp.float32
bf16 = jnp.bfloat16

D_MODEL = 1024
N_MOD = 6
MLP_HIDDEN = 4 * D_MODEL
NORM_EPS = 1e-6
NEG_INF = -1e30
GRID_W = 64

ATTN_HEADS = 16
ATTN_KV_HEADS = 4
ATTN_GROUP = ATTN_HEADS // ATTN_KV_HEADS
HEAD_DIM = 64
WINDOW = 128
ATTN_BLOCK = 128
ROPE_BASE = 10000.0
ROPE_AXIS_DIM = HEAD_DIM // 2
ROPE_FREQS = ROPE_AXIS_DIM // 2

GLA_HEADS = 4
GLA_KEY_DIM = D_MODEL // 2
GLA_DK = GLA_KEY_DIM // GLA_HEADS
GLA_DV = D_MODEL // GLA_HEADS
GLA_GATE_RANK = 16
GLA_TAU = 16.0
SCAN_CHUNK = 64

RWKV_HEAD_SIZE = 64
RWKV_LN_EPS = 64e-5
L2_EPS = 1e-12
RWKV_PAIR = 2 * RWKV_HEAD_SIZE
RWKV_DECAY_SCALE = 0.6065306597126334

HGRN_EXPAND = 128
HGRN_HEADS = D_MODEL // HGRN_EXPAND

LANES = 128
MOD_ROWS = 8
VMEM_LIMIT = 56 * 1024 * 1024

NT = (((1,), (1,)), ((), ()))
TN = (((0,), (0,)), ((), ()))


def _params(n_grid):
    return pltpu.CompilerParams(dimension_semantics=("arbitrary",) * n_grid, vmem_limit_bytes=VMEM_LIMIT)


def _const_spec(shape):
    nd = len(shape)
    return pl.BlockSpec(shape, lambda *_: (0,) * nd, pipeline_mode=pl.Buffered(1))


def _dot(a, b):
    return jnp.dot(a, b, preferred_element_type=f32)


def _sigmoid(x):
    return 0.5 * jnp.tanh(0.5 * x) + 0.5


def _silu(x):
    return x * _sigmoid(x)


def _rms(x, gain):
    return x * lax.rsqrt(jnp.mean(x * x, axis=-1, keepdims=True) + NORM_EPS) * gain


def _modulate(x, gain, shift, scale):
    return _rms(x, gain) * (1.0 + scale) + shift


def _seg_sum(x, seg):
    r = lax.broadcasted_iota(jnp.int32, (LANES, LANES), 0) // seg
    c = lax.broadcasted_iota(jnp.int32, (LANES, LANES), 1) // seg
    ones_bd = (r == c).astype(bf16)
    hi = x.astype(bf16)
    lo = (x - hi.astype(f32)).astype(bf16)
    outs = []
    for j in range(x.shape[1] // LANES):
        sl = slice(j * LANES, (j + 1) * LANES)
        outs.append(_dot(hi[:, sl], ones_bd) + _dot(lo[:, sl], ones_bd))
    return jnp.concatenate(outs, axis=1)


def _cumsum_rows(g, reverse):
    n = g.shape[0]
    ri = lax.broadcasted_iota(jnp.int32, (n, n), 0)
    ci = lax.broadcasted_iota(jnp.int32, (n, n), 1)
    tri = ((ci >= ri) if reverse else (ci <= ri)).astype(bf16)
    hi = g.astype(bf16)
    lo = (g - hi.astype(f32)).astype(bf16)
    return _dot(tri, hi) + _dot(tri, lo)


def _mod_kernel(c_ref, w_ref, b_ref, o_ref):
    o_ref[0] = _dot(_silu(c_ref[...]).astype(bf16), w_ref[0]) + b_ref[0]


def _mod_vectors(c, c_ctx, w_mod, b_mod):
    depth, d, _ = w_mod.shape
    batch = c.shape[0]
    rows = -(-(batch + 1) // 8) * 8
    cc = jnp.zeros((rows, d), f32).at[:batch].set(c).at[batch].set(c_ctx)
    out = pl.pallas_call(
        _mod_kernel,
        grid=(depth, N_MOD),
        in_specs=[
            pl.BlockSpec((rows, d), lambda i, j: (0, 0)),
            pl.BlockSpec((1, d, d), lambda i, j: (i, 0, j)),
            pl.BlockSpec((1, 1, d), lambda i, j: (i, 0, j)),
        ],
        out_specs=pl.BlockSpec((1, rows, d), lambda i, j: (i, 0, j)),
        out_shape=jax.ShapeDtypeStruct((depth, rows, N_MOD * d), f32),
        compiler_params=_params(2),
        name="mod_vectors",
    )(cc, w_mod.astype(bf16), b_mod.reshape(depth, 1, N_MOD * d))
    lat = out[:, :batch].reshape(depth, batch, 1, N_MOD, d)
    con = jnp.broadcast_to(out[:, batch].reshape(depth, 1, 1, N_MOD, d), lat.shape)
    mods = jnp.concatenate([con, lat], axis=2)
    return jnp.pad(mods, ((0, 0), (0, 0), (0, 0), (0, MOD_ROWS - N_MOD), (0, 0)))


class _Geom:
    def __init__(self, batch, n_ctx, n_lat):
        self.batch, self.n_ctx, self.n_lat = batch, n_ctx, n_lat
        self.t = n_ctx + n_lat
        self.tm = min(256, n_ctx)
        assert n_ctx % self.tm == 0 and n_lat % self.tm == 0
        assert n_ctx % ATTN_BLOCK == 0 and n_lat % ATTN_BLOCK == 0 and n_lat % GRID_W == 0
        self.nt = self.t // self.tm
        self.ctx_tiles = n_ctx // self.tm
        self.nc = self.t // SCAN_CHUNK
        self.ctx_chunks = n_ctx // SCAN_CHUNK

    def x_spec(self, t0=0):
        return pl.BlockSpec((1, self.tm, D_MODEL), lambda b, t: (b, t + t0, 0))

    def mod_spec(self, t0=0):
        ct = self.ctx_tiles
        return pl.BlockSpec((1, 1, MOD_ROWS, D_MODEL), lambda b, t: (b, ((t + t0) >= ct).astype(jnp.int32), 0, 0))

    def bwd_chunk(self, i):
        cc = self.ctx_chunks
        return jnp.where(i < cc, cc - 1 - i, self.nc - 1 + cc - i)


def _lin_kernel(x_ref, m_ref, g_ref, w_ref, o_ref):
    h = _modulate(x_ref[0], g_ref[0:1], m_ref[0, 0, 0:1], m_ref[0, 0, 1:2])
    o_ref[0] = _dot(h.astype(bf16), w_ref[...])


def _mod_linear(geo, x, mods, gains, w, name):
    n = w.shape[1]
    return pl.pallas_call(
        _lin_kernel,
        grid=(geo.batch, geo.nt),
        in_specs=[geo.x_spec(), geo.mod_spec(), _const_spec((8, D_MODEL)), _const_spec((D_MODEL, n))],
        out_specs=pl.BlockSpec((1, geo.tm, n), lambda b, t: (b, t, 0)),
        out_shape=jax.ShapeDtypeStruct((geo.batch, geo.t, n), f32),
        compiler_params=_params(2),
        name=name,
    )(x, mods, gains, w)


def _head_rms(o, gain, width):
    outs = []
    for h in range(o.shape[1] // width):
        oh = o[:, h * width:(h + 1) * width]
        outs.append(oh * lax.rsqrt(jnp.mean(oh * oh, axis=-1, keepdims=True) + NORM_EPS) * gain)
    return jnp.concatenate(outs, axis=1)


def _pre_attn(refs):
    (o_ref,) = refs
    return o_ref[0]


def _pre_scan(width, refs):
    of_ref, ob_ref, gate_ref, gn_ref = refs
    o = _head_rms(of_ref[0] + ob_ref[0], gn_ref[...], width)
    return (o * _silu(gate_ref[0])).astype(bf16)


def _pre_rwkv(refs):
    yf_ref, yb_ref, r_ref, k0_ref, k1_ref, v_ref, g_ref, vec_ref = refs
    y = yf_ref[0] + yb_ref[0]
    inv_n = 1.0 / RWKV_HEAD_SIZE
    mu = _seg_sum(y, RWKV_HEAD_SIZE) * inv_n
    dlt = y - mu
    var = _seg_sum(dlt * dlt, RWKV_HEAD_SIZE) * inv_n
    yn = dlt * lax.rsqrt(var + RWKV_LN_EPS) * vec_ref[0:1] + vec_ref[1:2]
    bonus = _seg_sum(r_ref[0] * (0.5 * (k0_ref[0] + k1_ref[0])) * vec_ref[2:3], RWKV_HEAD_SIZE) * v_ref[0]
    return ((yn + bonus) * g_ref[0]).astype(bf16)


def _post_kernel(pre, n_mix, x_ref, m_ref, g_ref, *rest):
    mix_refs = rest[:n_mix]
    wo_ref, win_ref, wout_ref, out_ref = rest[n_mix:]
    o = pre(mix_refs)
    y = _dot(o, wo_ref[...])
    x1 = x_ref[0] + m_ref[0, 0, 2:3] * _rms(y, g_ref[1:2])
    h2 = _modulate(x1, g_ref[2:3], m_ref[0, 0, 3:4], m_ref[0, 0, 4:5]).astype(bf16)
    acc = jnp.zeros_like(x1)
    for c in range(MLP_HIDDEN // D_MODEL):
        u = jnp.square(jnp.maximum(_dot(h2, win_ref[:, c * D_MODEL:(c + 1) * D_MODEL]), 0.0))
        acc = acc + _dot(u.astype(bf16), wout_ref[c * D_MODEL:(c + 1) * D_MODEL, :])
    out_ref[0] = x1 + m_ref[0, 0, 5:6] * _rms(acc, g_ref[3:4])


def _post_mlp(geo, x, mods, gains, pre, mix_args, mix_specs, w_o, w_in, w_out, skip_ctx, name):
    t0 = geo.ctx_tiles if skip_ctx else 0
    return pl.pallas_call(
        functools.partial(_post_kernel, pre, len(mix_args)),
        grid=(geo.batch, geo.nt - t0),
        in_specs=[geo.x_spec(t0), geo.mod_spec(t0), _const_spec((8, D_MODEL))] + mix_specs + [
            _const_spec((D_MODEL, D_MODEL)), _const_spec((D_MODEL, MLP_HIDDEN)), _const_spec((MLP_HIDDEN, D_MODEL))],
        out_specs=pl.BlockSpec((1, geo.tm, D_MODEL), lambda b, t: (b, t, 0)),
        out_shape=jax.ShapeDtypeStruct((geo.batch, geo.t - t0 * geo.tm, D_MODEL), f32),
        compiler_params=_params(2),
        name=name,
    )(x, mods, gains, *mix_args, w_o, w_in, w_out)


def _tile_spec(geo, width, col, t0):
    return pl.BlockSpec((1, geo.tm, width), lambda b, t: (b, t + t0, col))


def _rope(x, cos, sin):
    w = x.shape[1]
    reps = w // LANES
    cw = jnp.concatenate([cos] * reps, axis=1)
    sw = jnp.concatenate([sin] * reps, axis=1)
    lane = lax.broadcasted_iota(jnp.int32, x.shape, 1)
    first = (lane % ROPE_AXIS_DIM) < ROPE_FREQS
    partner = jnp.where(first, pltpu.roll(x, w - ROPE_FREQS, axis=1), pltpu.roll(x, ROPE_FREQS, axis=1))
    return x * cw + partner * sw


def _attn_proj_kernel(x_ref, m_ref, g_ref, cos_ref, sin_ref, wq_ref, wk_ref, wv_ref, q_ref, k_ref, v_ref):
    h = _modulate(x_ref[0], g_ref[0:1], m_ref[0, 0, 0:1], m_ref[0, 0, 1:2]).astype(bf16)
    cos, sin = cos_ref[...], sin_ref[...]
    q = _rope(_dot(h, wq_ref[...]) * (HEAD_DIM ** -0.5), cos, sin)
    k = _rope(_dot(h, wk_ref[...]), cos, sin)
    v = _dot(h, wv_ref[...])
    for hd in range(ATTN_HEADS):
        q_ref[0, hd] = q[:, hd * HEAD_DIM:(hd + 1) * HEAD_DIM].astype(bf16)
    for hd in range(ATTN_KV_HEADS):
        k_ref[0, hd] = k[:, hd * HEAD_DIM:(hd + 1) * HEAD_DIM].astype(bf16)
        v_ref[0, hd] = v[:, hd * HEAD_DIM:(hd + 1) * HEAD_DIM].astype(bf16)


def _attn_core_kernel(geo, sink_ref, q_ref, kp_ref, kc_ref, kn_ref, kx_ref, vp_ref, vc_ref, vn_ref, vx_ref, o_ref):
    qb = pl.program_id(1)
    blk = ATTN_BLOCK
    n_loc = 3 * blk
    n_keys = n_loc + geo.n_ctx
    rows = ATTN_GROUP * blk
    first_lat = geo.n_ctx // blk
    n_blocks = geo.t // blk
    row = lax.broadcasted_iota(jnp.int32, (blk, n_keys), 0)
    col = lax.broadcasted_iota(jnp.int32, (blk, n_keys), 1)
    kblk = qb - 1 + col // blk
    ok_local = ((jnp.abs(col - blk - row) <= WINDOW) & (qb >= first_lat) & (kblk >= first_lat) & (kblk < n_blocks))
    ok = (col >= n_loc) | ok_local
    scores, vals = [], []
    for j in range(ATTN_KV_HEADS):
        q4 = q_ref[0, ATTN_GROUP * j:ATTN_GROUP * (j + 1)].reshape(rows, HEAD_DIM)
        keys = jnp.concatenate([kp_ref[0, j], kc_ref[0, j], kn_ref[0, j], kx_ref[0, j]], axis=0)
        vals.append(jnp.concatenate([vp_ref[0, j], vc_ref[0, j], vn_ref[0, j], vx_ref[0, j]], axis=0))
        scores.append(lax.dot_general(q4, keys, NT, preferred_element_type=f32))
    heads = [(j, g) for j in range(ATTN_KV_HEADS) for g in range(ATTN_GROUP)]
    sg = [jnp.where(ok, scores[j][g * blk:(g + 1) * blk], NEG_INF) for j, g in heads]
    sk = [sink_ref[ATTN_GROUP * j + g] for j, g in heads]
    m = [jnp.maximum(jnp.max(s, axis=-1, keepdims=True), k) for s, k in zip(sg, sk)]
    e = [jnp.exp(s - mm) for s, mm in zip(sg, m)]
    inv = [1.0 / (jnp.sum(ee, axis=-1, keepdims=True) + jnp.exp(k - mm)) for ee, k, mm in zip(e, sk, m)]
    p = [(ee * ii).astype(bf16) for ee, ii in zip(e, inv)]
    outs = []
    for j in range(ATTN_KV_HEADS):
        o4 = _dot(jnp.concatenate(p[ATTN_GROUP * j:ATTN_GROUP * (j + 1)], axis=0), vals[j])
        outs += [o4[g * blk:(g + 1) * blk] for g in range(ATTN_GROUP)]
    o_ref[0] = jnp.concatenate(outs, axis=1).astype(bf16)


def _rope_tables(geo):
    inv_freq = ROPE_BASE ** (-jnp.arange(ROPE_FREQS, dtype=f32) * 2.0 / ROPE_AXIS_DIM)
    pos = jnp.arange(geo.n_lat)
    row = (pos // GRID_W).astype(f32)
    col = (pos % GRID_W).astype(f32)
    ang = jnp.stack([row[:, None] * inv_freq, col[:, None] * inv_freq], axis=1)
    cos = jnp.cos(ang)
    sin = jnp.sin(ang)
    cos_h = jnp.concatenate([cos, cos], axis=2).reshape(geo.n_lat, HEAD_DIM)
    sin_h = jnp.concatenate([-sin, sin], axis=2).reshape(geo.n_lat, HEAD_DIM)
    cos_t = jnp.concatenate([jnp.ones((geo.n_ctx, HEAD_DIM), f32), cos_h], axis=0)
    sin_t = jnp.concatenate([jnp.zeros((geo.n_ctx, HEAD_DIM), f32), sin_h], axis=0)
    return jnp.tile(cos_t, (1, 2)), jnp.tile(sin_t, (1, 2))


def _attn_layer(geo, x, mods, gains, w_qkv, w_o, sink, w_in, w_out, skip_ctx):
    b, t, tm = geo.batch, geo.t, geo.tm
    q_cols = ATTN_HEADS * HEAD_DIM
    kv_cols = ATTN_KV_HEADS * HEAD_DIM
    wb = w_qkv.astype(bf16)
    cos_t, sin_t = _rope_tables(geo)
    tab_spec = pl.BlockSpec((tm, LANES), lambda bb, tt: (tt, 0))
    q, k, v = pl.pallas_call(
        _attn_proj_kernel,
        grid=(b, geo.nt),
        in_specs=[geo.x_spec(), geo.mod_spec(), _const_spec((8, D_MODEL)), tab_spec, tab_spec,
                  _const_spec((D_MODEL, q_cols)), _const_spec((D_MODEL, kv_cols)), _const_spec((D_MODEL, kv_cols))],
        out_specs=[pl.BlockSpec((1, ATTN_HEADS, tm, HEAD_DIM), lambda bb, tt: (bb, 0, tt, 0)),
                   pl.BlockSpec((1, ATTN_KV_HEADS, tm, HEAD_DIM), lambda bb, tt: (bb, 0, tt, 0)),
                   pl.BlockSpec((1, ATTN_KV_HEADS, tm, HEAD_DIM), lambda bb, tt: (bb, 0, tt, 0))],
        out_shape=[jax.ShapeDtypeStruct((b, ATTN_HEADS, t, HEAD_DIM), bf16),
                   jax.ShapeDtypeStruct((b, ATTN_KV_HEADS, t, HEAD_DIM), bf16),
                   jax.ShapeDtypeStruct((b, ATTN_KV_HEADS, t, HEAD_DIM), bf16)],
        compiler_params=_params(2),
        name="attn_proj",
    )(x, mods, gains, cos_t, sin_t, wb[:, :q_cols], wb[:, q_cols:q_cols + kv_cols], wb[:, q_cols + kv_cols:])

    blk = ATTN_BLOCK
    n_blocks = t // blk

    def kv_spec(off):
        return pl.BlockSpec((1, ATTN_KV_HEADS, blk, HEAD_DIM),
                            lambda bb, qb: (bb, 0, jnp.clip(qb + off, 0, n_blocks - 1), 0))

    ctx_spec = pl.BlockSpec((1, ATTN_KV_HEADS, geo.n_ctx, HEAD_DIM), lambda bb, qb: (bb, 0, 0, 0))
    o = pl.pallas_call(
        functools.partial(_attn_core_kernel, geo),
        grid=(b, n_blocks),
        in_specs=[pl.BlockSpec(memory_space=pltpu.SMEM),
                  pl.BlockSpec((1, ATTN_HEADS, blk, HEAD_DIM), lambda bb, qb: (bb, 0, qb, 0)),
                  kv_spec(-1), kv_spec(0), kv_spec(1), ctx_spec,
                  kv_spec(-1), kv_spec(0), kv_spec(1), ctx_spec],
        out_specs=pl.BlockSpec((1, blk, q_cols), lambda bb, qb: (bb, qb, 0)),
        out_shape=jax.ShapeDtypeStruct((b, t, q_cols), bf16),
        compiler_params=_params(2),
        name="attn_core",
    )(sink, q, k, k, k, k, v, v, v, v)

    t0 = geo.ctx_tiles if skip_ctx else 0
    return _post_mlp(geo, x, mods, gains, _pre_attn, [o], [_tile_spec(geo, D_MODEL, 0, t0)],
                     w_o.astype(bf16), w_in, w_out, skip_ctx, "attn_post")


def _gated_chunk(dirs, heads, dk, dv):
    units = []
    for q, k, v, g, st_ref, base, reverse in dirs:
        n = q.shape[0]
        b = _cumsum_rows(g, reverse)
        btot = b[0:1] if reverse else b[n - 1:n]
        q_dec = (q * jnp.exp(b)).astype(bf16)
        k_inv32 = k * jnp.exp(-b)
        k_inv = k_inv32.astype(bf16)
        dec = jnp.exp(btot)
        k_end = (k_inv32 * dec).astype(bf16)
        vb = v.astype(bf16)
        ri = lax.broadcasted_iota(jnp.int32, (n, n), 0)
        ci = lax.broadcasted_iota(jnp.int32, (n, n), 1)
        tri = (ci >= ri) if reverse else (ci <= ri)
        for h in range(heads):
            ks = slice(h * dk, (h + 1) * dk)
            vs = slice(h * dv, (h + 1) * dv)
            units.append(dict(q=q_dec[:, ks], ki=k_inv[:, ks], ke=k_end[:, ks], v=vb[:, vs], dec=dec[:, ks],
                              tri=tri, st=st_ref[base + h], ref=st_ref, h=base + h))
    for u in units:
        a = lax.dot_general(u["q"], u["ki"], NT, preferred_element_type=f32)
        u["a"] = jnp.where(u["tri"], a, 0.0).astype(bf16)
        u["qs"] = lax.dot_general(u["q"], u["st"].astype(bf16), NT, preferred_element_type=f32)
    for u in units:
        u["o"] = _dot(u["a"], u["v"]) + u["qs"]
        u["new"] = u["st"] * u["dec"] + lax.dot_general(u["v"], u["ke"], TN, preferred_element_type=f32)
    for u in units:
        u["ref"][u["h"]] = u["new"]
    return [jnp.concatenate([u["o"] for u in units[d * heads:(d + 1) * heads]], axis=1) for d in range(len(dirs))]


def _gla_gate(zd, wu_ref, bias_ref, d):
    zg = _dot(zd.astype(bf16), wu_ref[d]) + bias_ref[d:d + 1]
    return (jnp.minimum(zg, 0.0) - jnp.log(1.0 + jnp.exp(-jnp.abs(zg)))) * (1.0 / GLA_TAU)


def _gla_scan_kernel(qkf_ref, vf_ref, zdf_ref, qkb_ref, vb_ref, zdb_ref, wu_ref, bias_ref,
                     of_ref, ob_ref, sf_ref, sb_ref):
    @pl.when(pl.program_id(1) == 0)
    def _():
        sf_ref[...] = jnp.zeros_like(sf_ref)
        sb_ref[...] = jnp.zeros_like(sb_ref)

    dirs, dests = [], []
    for row in range(qkf_ref.shape[0]):
        for d, (qk_ref, v_ref, zd_ref, s_ref, o_ref) in enumerate(
                ((qkf_ref, vf_ref, zdf_ref, sf_ref, of_ref), (qkb_ref, vb_ref, zdb_ref, sb_ref, ob_ref))):
            qk = qk_ref[row]
            q = qk[:, :GLA_KEY_DIM] * (GLA_DK ** -0.5)
            k = qk[:, GLA_KEY_DIM:]
            dirs.append((q, k, v_ref[row], _gla_gate(zd_ref[row], wu_ref, bias_ref, d), s_ref, row * GLA_HEADS,
                         d == 1))
            dests.append((o_ref, row))
    for o, (o_ref, row) in zip(_gated_chunk(dirs, GLA_HEADS, GLA_DK, GLA_DV), dests):
        o_ref[row] = o


def _hgrn_scan_kernel(qf_ref, if_ref, zff_ref, qb_ref, ib_ref, zfb_ref, lb_ref, of_ref, ob_ref, sf_ref, sb_ref):
    @pl.when(pl.program_id(1) == 0)
    def _():
        sf_ref[...] = jnp.zeros_like(sf_ref)
        sb_ref[...] = jnp.zeros_like(sb_ref)

    lb = lb_ref[...]
    dirs, dests = [], []
    for row in range(qf_ref.shape[0]):
        for d, (q_ref, i_ref, zf_ref, s_ref, o_ref) in enumerate(
                ((qf_ref, if_ref, zff_ref, sf_ref, of_ref), (qb_ref, ib_ref, zfb_ref, sb_ref, ob_ref))):
            f = lb + (1.0 - lb) * _sigmoid(zf_ref[row])
            dirs.append((_silu(q_ref[row]), 1.0 - f, i_ref[row], jnp.log(f), s_ref, row * HGRN_HEADS, d == 1))
            dests.append((o_ref, row))
    for o, (o_ref, row) in zip(_gated_chunk(dirs, HGRN_HEADS, HGRN_EXPAND, HGRN_EXPAND), dests):
        o_ref[row] = o


SCAN_ROWS_PER_STEP = 2


def _scan_rows(geo):
    return SCAN_ROWS_PER_STEP if geo.batch % SCAN_ROWS_PER_STEP == 0 else 1


def _chunk_spec(geo, width, col, reverse):
    rows = _scan_rows(geo)
    if reverse:
        return pl.BlockSpec((rows, SCAN_CHUNK, width), lambda b, i: (b, geo.bwd_chunk(i), col))
    return pl.BlockSpec((rows, SCAN_CHUNK, width), lambda b, i: (b, i, col))


def _scan_call(geo, body, args, specs, heads, dk, dv, name):
    out_sds = jax.ShapeDtypeStruct((geo.batch, geo.t, heads * dv), f32)
    rows = _scan_rows(geo)
    state = pltpu.VMEM((rows * heads, dv, dk), f32)
    return pl.pallas_call(
        body,
        grid=(geo.batch // rows, geo.nc),
        in_specs=specs,
        out_specs=[_chunk_spec(geo, heads * dv, 0, False), _chunk_spec(geo, heads * dv, 0, True)],
        out_shape=[out_sds, out_sds],
        scratch_shapes=[state, state],
        compiler_params=_params(2),
        name=name,
    )(*args)


def _gla_layer(geo, x, mods, gains, w_in_p, w_gd, w_gu, g_bias, g_norm, w_o, w_in, w_out, skip_ctx):
    r = GLA_GATE_RANK
    n_z = 2 * GLA_KEY_DIM + 2 * D_MODEL
    w_all = jnp.concatenate([w_in_p, w_gd[0], w_gd[1], jnp.zeros((D_MODEL, LANES - 2 * r), f32)], axis=1).astype(bf16)
    z = _mod_linear(geo, x, mods, gains, w_all, "gla_proj")
    wu = jnp.zeros((2, LANES, GLA_KEY_DIM), f32).at[0, :r].set(w_gu[0]).at[1, r:2 * r].set(w_gu[1]).astype(bf16)
    zd_col = n_z // LANES
    specs = []
    for rev in (False, True):
        specs += [_chunk_spec(geo, 2 * GLA_KEY_DIM, 0, rev), _chunk_spec(geo, D_MODEL, 1, rev),
                  _chunk_spec(geo, LANES, zd_col, rev)]
    specs += [_const_spec((2, LANES, GLA_KEY_DIM)), _const_spec((2, GLA_KEY_DIM))]
    o_f, o_b = _scan_call(geo, _gla_scan_kernel, [z, z, z, z, z, z, wu, g_bias], specs,
                          GLA_HEADS, GLA_DK, GLA_DV, "gla_scan")
    t0 = geo.ctx_tiles if skip_ctx else 0
    mix_specs = [_tile_spec(geo, D_MODEL, 0, t0), _tile_spec(geo, D_MODEL, 0, t0), _tile_spec(geo, D_MODEL, 2, t0),
                 _const_spec((1, GLA_DV))]
    return _post_mlp(geo, x, mods, gains, functools.partial(_pre_scan, GLA_DV),
                     [o_f, o_b, z, g_norm.reshape(1, GLA_DV)], mix_specs,
                     w_o.astype(bf16), w_in, w_out, skip_ctx, "gla_post")


def _hgrn_layer(geo, x, mods, gains, w_in_p, w_f, lower_bound, g_norm, w_o, w_in, w_out, skip_ctx):
    w_all = jnp.concatenate([w_in_p, w_f[0], w_f[1]], axis=1).astype(bf16)
    z = _mod_linear(geo, x, mods, gains, w_all, "hgrn_proj")
    specs = []
    for rev in (False, True):
        specs += [_chunk_spec(geo, D_MODEL, 0, rev), _chunk_spec(geo, D_MODEL, 1, rev),
                  _chunk_spec(geo, D_MODEL, 4 if rev else 3, rev)]
    specs += [_const_spec((1, D_MODEL))]
    o_f, o_b = _scan_call(geo, _hgrn_scan_kernel, [z, z, z, z, z, z, lower_bound.reshape(1, D_MODEL)], specs,
                          HGRN_HEADS, HGRN_EXPAND, HGRN_EXPAND, "hgrn_scan")
    t0 = geo.ctx_tiles if skip_ctx else 0
    mix_specs = [_tile_spec(geo, D_MODEL, 0, t0), _tile_spec(geo, D_MODEL, 0, t0), _tile_spec(geo, D_MODEL, 2, t0),
                 _const_spec((1, HGRN_EXPAND))]
    return _post_mlp(geo, x, mods, gains, functools.partial(_pre_scan, HGRN_EXPAND),
                     [o_f, o_b, z, g_norm.reshape(1, HGRN_EXPAND)], mix_specs,
                     w_o.astype(bf16), w_in, w_out, skip_ctx, "hgrn_post")


def _rwkv_proj_kernel(geo, x_ref, xp_ref, xn_ref, m_ref, g_ref, mix_ref, vec_ref, wrkv_ref, wdn_ref, wup_ref,
                      aup_ref, gup_ref, r_ref, v_ref, gate_ref, kk_ref, lw0_ref, lw1_ref, a0_ref, a1_ref,
                      k0_ref, k1_ref):
    t = pl.program_id(1)
    tm = geo.tm
    gain, shift, scale = g_ref[0:1], m_ref[0, 0, 0:1], m_ref[0, 0, 1:2]
    h = _modulate(x_ref[0], gain, shift, scale)
    seg_first = (t == 0) | (t == geo.ctx_tiles)
    seg_last = (t == geo.ctx_tiles - 1) | (t == geo.nt - 1)
    h_prev = jnp.where(seg_first, 0.0, _modulate(xp_ref[0], gain, shift, scale)[7:8])
    h_next = jnp.where(seg_last, 0.0, _modulate(xn_ref[0], gain, shift, scale)[0:1])
    row = lax.broadcasted_iota(jnp.int32, h.shape, 0)
    up = jnp.where(row == 0, h_prev, pltpu.roll(h, 1, axis=0))
    dn = jnp.where(row == tm - 1, h_next, pltpu.roll(h, tm - 1, axis=0))
    dx = 0.5 * (up + dn) - h

    def mixed(n):
        return (h + dx * mix_ref[n:n + 1]).astype(bf16)

    r = _dot(mixed(0), wrkv_ref[0])
    k = _dot(mixed(1), wrkv_ref[1])
    v = _dot(mixed(2), wrkv_ref[2])
    dw = jnp.tanh(_dot(mixed(3), wdn_ref[:, 0:LANES])).astype(bf16)
    da = _dot(mixed(4), wdn_ref[:, LANES:2 * LANES]).astype(bf16)
    dg = _sigmoid(_dot(mixed(5), wdn_ref[:, 2 * LANES:3 * LANES])).astype(bf16)
    r_ref[0] = r
    v_ref[0] = v
    gate_ref[0] = _dot(dg, gup_ref[...])
    kk = k * vec_ref[4:5]
    kk_ref[0] = kk * lax.rsqrt(jnp.maximum(_seg_sum(kk * kk, RWKV_HEAD_SIZE), L2_EPS * L2_EPS))
    for d, (lw_ref, a_ref, kd_ref) in enumerate(((lw0_ref, a0_ref, k0_ref), (lw1_ref, a1_ref, k1_ref))):
        lw_ref[0] = -RWKV_DECAY_SCALE * _sigmoid(vec_ref[d:d + 1] + _dot(dw, wup_ref[d]))
        a = _sigmoid(vec_ref[2 + d:3 + d] + _dot(da, aup_ref[d]))
        a_ref[0] = a
        kd_ref[0] = k * (1.0 + (a - 1.0) * vec_ref[5:6])


def _rwkv_chunk_units(units):
    n = SCAN_CHUNK
    hs = RWKV_HEAD_SIZE
    lane = lax.broadcasted_iota(jnp.int32, (n, RWKV_PAIR), 1)
    t_i = lax.broadcasted_iota(jnp.int32, (n, RWKV_PAIR), 0)
    s_i = lane % hs
    head0 = lane < hs
    ri = lax.broadcasted_iota(jnp.int32, (RWKV_PAIR, RWKV_PAIR), 0)
    ci = lax.broadcasted_iota(jnp.int32, (RWKV_PAIR, RWKV_PAIR), 1)
    same_head = ri // hs == ci // hs
    top_rows = ri < hs

    def block_diag(x):
        return jnp.concatenate([jnp.where(head0, x, 0.0), jnp.where(head0, 0.0, x)], axis=0)

    def cross_blocks(b):
        return (t_i // (2 * b) == s_i // (2 * b)) & (t_i // b != s_i // b)

    st = []
    for r, kd, v, kk, a, lw, ht, reverse in units:
        c = _cumsum_rows(lw, reverse)
        tot = c[0:1] if reverse else c[n - 1:n]
        e_neg = jnp.exp(-c)
        dec = jnp.exp(tot)
        kb = jnp.concatenate([kd * e_neg, kk * a * e_neg], axis=0)
        kb_t = kb.T
        kb_sw = pltpu.roll(kb_t, hs, axis=1)
        rhs = jnp.concatenate([jnp.where(same_head, jnp.where(top_rows, kb_t, kb_sw), 0.0),
                               jnp.where(same_head, jnp.where(top_rows, kb_sw, kb_t), 0.0)], axis=1)
        st.append(dict(
            strict=(s_i > t_i) if reverse else (s_i < t_i), incl=(s_i >= t_i) if reverse else (s_i <= t_i),
            a_bar=(-kk * jnp.exp(c - lw)).astype(bf16), r_bar=(r * jnp.exp(c)).astype(bf16), rhs=rhs.astype(bf16),
            v_bd=block_diag(v).astype(bf16), h_t=ht.T.astype(bf16),
            ends=(kb * dec).astype(bf16),
            v=v, ht=ht, dec=dec))
    for s in st:
        g = _dot(jnp.concatenate([s["a_bar"], s["r_bar"]], axis=0), s["rhs"])
        s["a_ak"] = jnp.where(s["strict"], g[0:n, 0:RWKV_PAIR], 0.0).astype(bf16)
        s["nmat"] = jnp.where(s["strict"], g[0:n, RWKV_PAIR:], 0.0)
        s["q_k"] = jnp.where(s["incl"], g[n:, 0:RWKV_PAIR], 0.0).astype(bf16)
        s["q_b"] = jnp.where(s["incl"], g[n:, RWKV_PAIR:], 0.0).astype(bf16)
    for s in st:
        s["inv"] = jnp.where(s_i == t_i, 1.0, jnp.where(cross_blocks(1), s["nmat"], 0.0))
    b = 2
    while 2 * b < n:
        for s in st:
            s["pc"] = _dot(s["inv"].astype(bf16), block_diag(jnp.where(cross_blocks(b), s["nmat"], 0.0)).astype(bf16))
        for s in st:
            s["inv"] = s["inv"] + _dot(s["pc"].astype(bf16), block_diag(s["inv"]).astype(bf16))
        b *= 2
    for s in st:
        x = _dot(jnp.concatenate([s["a_bar"], s["a_ak"]], axis=1), jnp.concatenate([s["h_t"], s["v_bd"]], axis=0))
        s["inv"] = s["inv"].astype(bf16)
        s["w"] = _dot(s["inv"], block_diag(x).astype(bf16))
    for s in st:
        s["cw"] = _dot(jnp.where(cross_blocks(n // 2), s["nmat"], 0.0).astype(bf16), block_diag(s["w"]).astype(bf16))
    for s in st:
        s["u"] = s["w"] + _dot(s["inv"], block_diag(s["cw"]).astype(bf16))
    for s in st:
        s["vu_t"] = jnp.concatenate([s["v"], s["u"]], axis=0).T.astype(bf16)
    for s in st:
        s["y"] = _dot(jnp.concatenate([s["r_bar"], s["q_k"], s["q_b"]], axis=1),
                      jnp.concatenate([s["h_t"], s["v_bd"], block_diag(s["u"]).astype(bf16)], axis=0))
    for s in st:
        s["upd"] = _dot(s["vu_t"], s["ends"])
    return [(s["y"], s["ht"] * s["dec"] + jnp.where(same_head, s["upd"], 0.0)) for s in st]


def _rwkv_scan_kernel(rows, pairs, *refs):
    fwd, bwd = refs[0:6], refs[6:12]
    yf_ref, yb_ref, hf_ref, hb_ref = refs[12:]

    @pl.when(pl.program_id(2) == 0)
    def _():
        hf_ref[...] = jnp.zeros_like(hf_ref)
        hb_ref[...] = jnp.zeros_like(hb_ref)

    units, dests = [], []
    for ins, y_ref, h_ref, reverse in ((fwd, yf_ref, hf_ref, False), (bwd, yb_ref, hb_ref, True)):
        r_ref, v_ref, kk_ref, lw_ref, a_ref, kd_ref = ins
        for row in range(rows):
            for p in range(pairs):
                sl = slice(p * RWKV_PAIR, (p + 1) * RWKV_PAIR)
                slot = row * pairs + p
                units.append((r_ref[row, :, sl], kd_ref[row, :, sl], v_ref[row, :, sl], kk_ref[row, :, sl],
                              a_ref[row, :, sl], lw_ref[row, :, sl], h_ref[slot], reverse))
                dests.append((y_ref, h_ref, row, slot, sl))
    for (y, h_new), (y_ref, h_ref, row, slot, sl) in zip(_rwkv_chunk_units(units), dests):
        y_ref[row, :, sl] = y
        h_ref[slot] = h_new


RWKV_PAIRS_PER_STEP = 8
RWKV_ROWS_PER_STEP = 2


def _rwkv_layer(geo, x, mods, gains, mix, w_rkv, w0, w_down, w_up, a0, a_down, a_up, g_down, g_up, k_k, k_a,
                r_k, ln_w, ln_b, w_o, w_in, w_out, skip_ctx):
    b, t, tm = geo.batch, geo.t, geo.tm
    d = D_MODEL
    rank = w_down.shape[-1]
    assert 2 * rank == LANES and a_down.shape[-1] == rank and g_down.shape[-1] == LANES
    mix8 = jnp.pad(mix, ((0, 2), (0, 0)))
    vec = jnp.stack([w0[0], w0[1], a0[0], a0[1], k_k, k_a, jnp.zeros_like(k_k), jnp.zeros_like(k_k)])
    w_dn = jnp.concatenate([w_down[0], w_down[1], a_down[0], a_down[1], g_down], axis=1).astype(bf16)

    def padded_up(w):
        return jnp.zeros((2, LANES, d), f32).at[0, :rank].set(w[0]).at[1, rank:].set(w[1]).astype(bf16)

    halo = 8
    n_halo = t // halo
    per = tm // halo
    x_prev = pl.BlockSpec((1, halo, d), lambda bb, tt: (bb, jnp.maximum(tt * per - 1, 0), 0))
    x_next = pl.BlockSpec((1, halo, d), lambda bb, tt: (bb, jnp.minimum((tt + 1) * per, n_halo - 1), 0))
    out_spec = pl.BlockSpec((1, tm, d), lambda bb, tt: (bb, tt, 0))
    sds = jax.ShapeDtypeStruct((b, t, d), f32)
    r, v, gate, kk, lw0, lw1, a_0, a_1, k0, k1 = pl.pallas_call(
        functools.partial(_rwkv_proj_kernel, geo),
        grid=(b, geo.nt),
        in_specs=[geo.x_spec(), x_prev, x_next, geo.mod_spec(), _const_spec((8, d)), _const_spec((8, d)),
                  _const_spec((8, d)), _const_spec((3, d, d)), _const_spec((d, 3 * LANES)),
                  _const_spec((2, LANES, d)), _const_spec((2, LANES, d)), _const_spec((LANES, d))],
        out_specs=[out_spec] * 10,
        out_shape=[sds] * 10,
        compiler_params=_params(2),
        name="rwkv_proj",
    )(x, x, x, mods, gains, mix8, vec, w_rkv.astype(bf16), w_dn, padded_up(w_up), padded_up(a_up),
      g_up.astype(bf16))

    pairs = RWKV_PAIRS_PER_STEP
    rows = RWKV_ROWS_PER_STEP if b % RWKV_ROWS_PER_STEP == 0 else 1
    width = pairs * RWKV_PAIR
    groups = d // width

    def cspec(reverse):
        if reverse:
            return pl.BlockSpec((rows, SCAN_CHUNK, width), lambda bb, gg, i: (bb, geo.bwd_chunk(i), gg))
        return pl.BlockSpec((rows, SCAN_CHUNK, width), lambda bb, gg, i: (bb, i, gg))

    state = pltpu.VMEM((rows * pairs, RWKV_PAIR, RWKV_PAIR), f32)
    y_f, y_b = pl.pallas_call(
        functools.partial(_rwkv_scan_kernel, rows, pairs),
        grid=(b // rows, groups, geo.nc),
        in_specs=[cspec(False)] * 6 + [cspec(True)] * 6,
        out_specs=[cspec(False), cspec(True)],
        out_shape=[sds, sds],
        scratch_shapes=[state, state],
        compiler_params=_params(3),
        name="rwkv_scan",
    )(r, v, kk, lw0, a_0, k0, r, v, kk, lw1, a_1, k1)

    t0 = geo.ctx_tiles if skip_ctx else 0
    post_vec = jnp.stack([ln_w, ln_b, r_k.reshape(d)] + [jnp.zeros_like(ln_w)] * 5)
    mix_specs = [_tile_spec(geo, d, 0, t0)] * 7 + [_const_spec((8, d))]
    return _post_mlp(geo, x, mods, gains, _pre_rwkv, [y_f, y_b, r, k0, k1, v, gate, post_vec], mix_specs,
                     w_o.astype(bf16), w_in, w_out, skip_ctx, "rwkv_post")


def _hgrn_lower_bound(lb_param, layer):
    p = jax.nn.softmax(lb_param.astype(f32), axis=0)
    return (jnp.cumsum(p, axis=0) - p[0])[layer]


def kernel(x, c, ctx, c_ctx, w_mod, b_mod, g_pre_mix, g_post_mix, g_pre_mlp, g_post_mlp, w_mlp_in, w_mlp_out, attn_w_qkv, attn_w_o, attn_sink, gla_w_in, gla_w_gate_down, gla_w_gate_up, gla_gate_bias, gla_g_norm, gla_w_o, rwkv_mix, rwkv_w_rkv, rwkv_w0, rwkv_w_down, rwkv_w_up, rwkv_a0, rwkv_a_down, rwkv_a_up, rwkv_g_down, rwkv_g_up, rwkv_k_k, rwkv_k_a, rwkv_r_k, rwkv_ln_w, rwkv_ln_b, rwkv_w_o, hgrn_w_in, hgrn_w_f, hgrn_lb, hgrn_g_norm, hgrn_w_o):
    depth = w_mod.shape[0]
    geo = _Geom(x.shape[0], ctx.shape[1], x.shape[1])
    mods_all = _mod_vectors(c, c_ctx, w_mod, b_mod)
    xs = jnp.concatenate([ctx, x], axis=1)
    for i in range(depth):
        kind, j = i % 4, i // 4
        skip_ctx = i == depth - 1
        mods = mods_all[i]
        gains = jnp.stack([g_pre_mix[i], g_post_mix[i], g_pre_mlp[i], g_post_mlp[i]] + [jnp.zeros_like(g_pre_mix[i])] * 4)
        w_in, w_out = w_mlp_in[i].astype(bf16), w_mlp_out[i].astype(bf16)
        if kind == 0:
            xs = _attn_layer(geo, xs, mods, gains, attn_w_qkv[j], attn_w_o[j], attn_sink[j], w_in, w_out, skip_ctx)
        elif kind == 1:
            xs = _gla_layer(geo, xs, mods, gains, gla_w_in[j], gla_w_gate_down[j], gla_w_gate_up[j],
                            gla_gate_bias[j], gla_g_norm[j], gla_w_o[j], w_in, w_out, skip_ctx)
        elif kind == 2:
            xs = _rwkv_layer(geo, xs, mods, gains, rwkv_mix[j], rwkv_w_rkv[j], rwkv_w0[j], rwkv_w_down[j],
                             rwkv_w_up[j], rwkv_a0[j], rwkv_a_down[j], rwkv_a_up[j], rwkv_g_down[j], rwkv_g_up[j],
                             rwkv_k_k[j], rwkv_k_a[j], rwkv_r_k[j], rwkv_ln_w[j], rwkv_ln_b[j], rwkv_w_o[j],
                             w_in, w_out, skip_ctx)
        else:
            xs = _hgrn_layer(geo, xs, mods, gains, hgrn_w_in[j], hgrn_w_f[j], _hgrn_lower_bound(hgrn_lb, i),
                             hgrn_g_norm[j], hgrn_w_o[j], w_in, w_out, skip_ctx)
        if skip_ctx:
            return xs
    return xs[:, geo.n_ctx:]
```

```python
import functools

import jax
import jax.numpy as jnp
from jax import lax
from jax.experimental import pallas as pl
from jax.experimental.pallas import tpu as pltpu

f32 = jnp.float32
bf16 = jnp.bfloat16

D_MODEL = 1024
N_MOD = 6
MLP_HIDDEN = 4 * D_MODEL
NORM_EPS = 1e-6
NEG_INF = -1e30
GRID_W = 64

ATTN_HEADS = 16
ATTN_KV_HEADS = 4
ATTN_GROUP = ATTN_HEADS // ATTN_KV_HEADS
HEAD_DIM = 64
WINDOW = 128
ATTN_BLOCK = 128
ROPE_BASE = 10000.0
ROPE_AXIS_DIM = HEAD_DIM // 2
ROPE_FREQS = ROPE_AXIS_DIM // 2

GLA_HEADS = 4
GLA_KEY_DIM = D_MODEL // 2
GLA_DK = GLA_KEY_DIM // GLA_HEADS
GLA_DV = D_MODEL // GLA_HEADS
GLA_GATE_RANK = 16
GLA_TAU = 16.0
SCAN_CHUNK = 64

RWKV_HEAD_SIZE = 64
RWKV_LN_EPS = 64e-5
L2_EPS = 1e-12
RWKV_PAIR = 2 * RWKV_HEAD_SIZE
RWKV_DECAY_SCALE = 0.6065306597126334

HGRN_EXPAND = 128
HGRN_HEADS = D_MODEL // HGRN_EXPAND

LANES = 128
MOD_ROWS = 8
VMEM_LIMIT = 56 * 1024 * 1024

NT = (((1,), (1,)), ((), ()))
TN = (((0,), (0,)), ((), ()))


def _params(n_grid):
    return pltpu.CompilerParams(dimension_semantics=("arbitrary",) * n_grid, vmem_limit_bytes=VMEM_LIMIT)


def _const_spec(shape):
    nd = len(shape)
    return pl.BlockSpec(shape, lambda *_: (0,) * nd, pipeline_mode=pl.Buffered(1))


def _dot(a, b):
    return jnp.dot(a, b, preferred_element_type=f32)


def _sigmoid(x):
    return 0.5 * jnp.tanh(0.5 * x) + 0.5


def _silu(x):
    return x * _sigmoid(x)


def _rms(x, gain):
    return x * lax.rsqrt(jnp.mean(x * x, axis=-1, keepdims=True) + NORM_EPS) * gain


def _modulate(x, gain, shift, scale):
    return _rms(x, gain) * (1.0 + scale) + shift


def _seg_sum(x, seg):
    r = lax.broadcasted_iota(jnp.int32, (LANES, LANES), 0) // seg
    c = lax.broadcasted_iota(jnp.int32, (LANES, LANES), 1) // seg
    ones_bd = (r == c).astype(bf16)
    hi = x.astype(bf16)
    lo = (x - hi.astype(f32)).astype(bf16)
    outs = []
    for j in range(x.shape[1] // LANES):
        sl = slice(j * LANES, (j + 1) * LANES)
        outs.append(_dot(hi[:, sl], ones_bd) + _dot(lo[:, sl], ones_bd))
    return jnp.concatenate(outs, axis=1)


def _cumsum_rows(g, reverse):
    n = g.shape[0]
    ri = lax.broadcasted_iota(jnp.int32, (n, n), 0)
    ci = lax.broadcasted_iota(jnp.int32, (n, n), 1)
    tri = ((ci >= ri) if reverse else (ci <= ri)).astype(bf16)
    hi = g.astype(bf16)
    lo = (g - hi.astype(f32)).astype(bf16)
    return _dot(tri, hi) + _dot(tri, lo)


def _mod_kernel(c_ref, w_ref, b_ref, o_ref):
    o_ref[0] = _dot(_silu(c_ref[...]).astype(bf16), w_ref[0]) + b_ref[0]


def _mod_vectors(c, c_ctx, w_mod, b_mod):
    depth, d, _ = w_mod.shape
    batch = c.shape[0]
    rows = -(-(batch + 1) // 8) * 8
    cc = jnp.zeros((rows, d), f32).at[:batch].set(c).at[batch].set(c_ctx)
    out = pl.pallas_call(
        _mod_kernel,
        grid=(depth, N_MOD),
        in_specs=[
            pl.BlockSpec((rows, d), lambda i, j: (0, 0)),
            pl.BlockSpec((1, d, d), lambda i, j: (i, 0, j)),
            pl.BlockSpec((1, 1, d), lambda i, j: (i, 0, j)),
        ],
        out_specs=pl.BlockSpec((1, rows, d), lambda i, j: (i, 0, j)),
        out_shape=jax.ShapeDtypeStruct((depth, rows, N_MOD * d), f32),
        compiler_params=_params(2),
        name="mod_vectors",
    )(cc, w_mod.astype(bf16), b_mod.reshape(depth, 1, N_MOD * d))
    lat = out[:, :batch].reshape(depth, batch, 1, N_MOD, d)
    con = jnp.broadcast_to(out[:, batch].reshape(depth, 1, 1, N_MOD, d), lat.shape)
    mods = jnp.concatenate([con, lat], axis=2)
    return jnp.pad(mods, ((0, 0), (0, 0), (0, 0), (0, MOD_ROWS - N_MOD), (0, 0)))


class _Geom:
    def __init__(self, batch, n_ctx, n_lat):
        self.batch, self.n_ctx, self.n_lat = batch, n_ctx, n_lat
        self.t = n_ctx + n_lat
        self.tm = min(256, n_ctx)
        assert n_ctx % self.tm == 0 and n_lat % self.tm == 0
        assert n_ctx % ATTN_BLOCK == 0 and n_lat % ATTN_BLOCK == 0 and n_lat % GRID_W == 0
        self.nt = self.t // self.tm
        self.ctx_tiles = n_ctx // self.tm
        self.nc = self.t // SCAN_CHUNK
        self.ctx_chunks = n_ctx // SCAN_CHUNK

    def x_spec(self, t0=0, rows=1):
        return pl.BlockSpec((rows, self.tm, D_MODEL), lambda b, t: (b, t + t0, 0))

    def mod_spec(self, t0=0, rows=1):
        ct = self.ctx_tiles
        return pl.BlockSpec((rows, 1, MOD_ROWS, D_MODEL), lambda b, t: (b, ((t + t0) >= ct).astype(jnp.int32), 0, 0))

    def bwd_chunk(self, i):
        cc = self.ctx_chunks
        return jnp.where(i < cc, cc - 1 - i, self.nc - 1 + cc - i)


LIN_ROWS_PER_STEP = 2


def _lin_kernel(x_ref, m_ref, g_ref, w_ref, o_ref):
    rows, tm = x_ref.shape[0], x_ref.shape[1]
    h = [_modulate(x_ref[r], g_ref[0:1], m_ref[r, 0, 0:1], m_ref[r, 0, 1:2]).astype(bf16) for r in range(rows)]
    z = _dot(jnp.concatenate(h, axis=0), w_ref[...])
    for r in range(rows):
        o_ref[r] = z[r * tm:(r + 1) * tm]


def _mod_linear(geo, x, mods, gains, w, name):
    n = w.shape[1]
    rows = LIN_ROWS_PER_STEP if geo.batch % LIN_ROWS_PER_STEP == 0 else 1
    return pl.pallas_call(
        _lin_kernel,
        grid=(geo.batch // rows, geo.nt),
        in_specs=[geo.x_spec(0, rows), geo.mod_spec(0, rows), _const_spec((8, D_MODEL)), _const_spec((D_MODEL, n))],
        out_specs=pl.BlockSpec((rows, geo.tm, n), lambda b, t: (b, t, 0)),
        out_shape=jax.ShapeDtypeStruct((geo.batch, geo.t, n), f32),
        compiler_params=_params(2),
        name=name,
    )(x, mods, gains, w)


def _head_rms(o, gain, width):
    outs = []
    for h in range(o.shape[1] // width):
        oh = o[:, h * width:(h + 1) * width]
        outs.append(oh * lax.rsqrt(jnp.mean(oh * oh, axis=-1, keepdims=True) + NORM_EPS) * gain)
    return jnp.concatenate(outs, axis=1)


def _pre_attn(refs, row):
    (o_ref,) = refs
    return o_ref[row]


def _pre_scan(width, refs, row):
    of_ref, ob_ref, gate_ref, gn_ref = refs
    o = _head_rms(of_ref[row] + ob_ref[row], gn_ref[...], width)
    return (o * _silu(gate_ref[row])).astype(bf16)


def _pre_rwkv(refs, row):
    yf_ref, yb_ref, r_ref, k0_ref, k1_ref, v_ref, g_ref, vec_ref = refs
    y = yf_ref[row] + yb_ref[row]
    inv_n = 1.0 / RWKV_HEAD_SIZE
    mu = _seg_sum(y, RWKV_HEAD_SIZE) * inv_n
    dlt = y - mu
    var = _seg_sum(dlt * dlt, RWKV_HEAD_SIZE) * inv_n
    yn = dlt * lax.rsqrt(var + RWKV_LN_EPS) * vec_ref[0:1] + vec_ref[1:2]
    bonus = _seg_sum(r_ref[row] * (0.5 * (k0_ref[row] + k1_ref[row])) * vec_ref[2:3], RWKV_HEAD_SIZE) * v_ref[row]
    return ((yn + bonus) * g_ref[row]).astype(bf16)


def _post_kernel(pre, n_mix, x_ref, m_ref, g_ref, *rest):
    mix_refs = rest[:n_mix]
    wo_ref, win_ref, wout_ref, out_ref = rest[n_mix:]
    rows = range(x_ref.shape[0])
    o = [pre(mix_refs, r) for r in rows]
    y = [_dot(o[r], wo_ref[...]) for r in rows]
    x1 = [x_ref[r] + m_ref[r, 0, 2:3] * _rms(y[r], g_ref[1:2]) for r in rows]
    h2 = [_modulate(x1[r], g_ref[2:3], m_ref[r, 0, 3:4], m_ref[r, 0, 4:5]).astype(bf16) for r in rows]
    acc = [None for _ in rows]
    for c in range(MLP_HIDDEN // D_MODEL):
        cols = slice(c * D_MODEL, (c + 1) * D_MODEL)
        u = [jnp.square(jnp.maximum(_dot(h2[r], win_ref[:, cols]), 0.0)).astype(bf16) for r in rows]
        for r in rows:
            part = _dot(u[r], wout_ref[cols, :])
            acc[r] = part if c == 0 else acc[r] + part
    for r in rows:
        out_ref[r] = x1[r] + m_ref[r, 0, 5:6] * _rms(acc[r], g_ref[3:4])


POST_ROWS_PER_STEP = 2


def _post_rows(geo, n_tile_inputs):
    tile = geo.tm * D_MODEL * 4
    weights = 2 * (D_MODEL * D_MODEL + 2 * D_MODEL * MLP_HIDDEN)
    for rows in (POST_ROWS_PER_STEP, 1):
        streamed = 2 * (n_tile_inputs + 2) * rows * tile
        live = 6 * rows * tile
        if geo.batch % rows == 0 and weights + streamed + live <= VMEM_LIMIT:
            return rows
    return 1


def _post_mlp(geo, x, mods, gains, pre, mix_args, mix_specs, w_o, w_in, w_out, skip_ctx, name, rows):
    t0 = geo.ctx_tiles if skip_ctx else 0
    return pl.pallas_call(
        functools.partial(_post_kernel, pre, len(mix_args)),
        grid=(geo.batch // rows, geo.nt - t0),
        in_specs=[geo.x_spec(t0, rows), geo.mod_spec(t0, rows), _const_spec((8, D_MODEL))] + mix_specs + [
            _const_spec((D_MODEL, D_MODEL)), _const_spec((D_MODEL, MLP_HIDDEN)), _const_spec((MLP_HIDDEN, D_MODEL))],
        out_specs=pl.BlockSpec((rows, geo.tm, D_MODEL), lambda b, t: (b, t, 0)),
        out_shape=jax.ShapeDtypeStruct((geo.batch, geo.t - t0 * geo.tm, D_MODEL), f32),
        compiler_params=_params(2),
        name=name,
    )(x, mods, gains, *mix_args, w_o, w_in, w_out)


def _tile_spec(geo, width, col, t0, rows):
    return pl.BlockSpec((rows, geo.tm, width), lambda b, t: (b, t + t0, col))


def _rope(x, cos, sin):
    w = x.shape[1]
    reps = w // LANES
    cw = jnp.concatenate([cos] * reps, axis=1)
    sw = jnp.concatenate([sin] * reps, axis=1)
    lane = lax.broadcasted_iota(jnp.int32, x.shape, 1)
    first = (lane % ROPE_AXIS_DIM) < ROPE_FREQS
    partner = jnp.where(first, pltpu.roll(x, w - ROPE_FREQS, axis=1), pltpu.roll(x, ROPE_FREQS, axis=1))
    return x * cw + partner * sw


def _attn_proj_kernel(x_ref, m_ref, g_ref, cos_ref, sin_ref, wq_ref, wk_ref, wv_ref, q_ref, k_ref, v_ref):
    h = _modulate(x_ref[0], g_ref[0:1], m_ref[0, 0, 0:1], m_ref[0, 0, 1:2]).astype(bf16)
    cos, sin = cos_ref[...], sin_ref[...]
    q = _rope(_dot(h, wq_ref[...]) * (HEAD_DIM ** -0.5), cos, sin)
    k = _rope(_dot(h, wk_ref[...]), cos, sin)
    v = _dot(h, wv_ref[...])
    for hd in range(ATTN_HEADS):
        q_ref[0, hd] = q[:, hd * HEAD_DIM:(hd + 1) * HEAD_DIM].astype(bf16)
    for hd in range(ATTN_KV_HEADS):
        k_ref[0, hd] = k[:, hd * HEAD_DIM:(hd + 1) * HEAD_DIM].astype(bf16)
        v_ref[0, hd] = v[:, hd * HEAD_DIM:(hd + 1) * HEAD_DIM].astype(bf16)


def _attn_core_kernel(geo, sink_ref, q_ref, kp_ref, kc_ref, kn_ref, kx_ref, vp_ref, vc_ref, vn_ref, vx_ref, o_ref):
    qb = pl.program_id(1)
    blk = ATTN_BLOCK
    n_loc = 3 * blk
    n_keys = n_loc + geo.n_ctx
    rows = ATTN_GROUP * blk
    first_lat = geo.n_ctx // blk
    n_blocks = geo.t // blk
    row = lax.broadcasted_iota(jnp.int32, (blk, n_keys), 0)
    col = lax.broadcasted_iota(jnp.int32, (blk, n_keys), 1)
    kblk = qb - 1 + col // blk
    ok_local = ((jnp.abs(col - blk - row) <= WINDOW) & (qb >= first_lat) & (kblk >= first_lat) & (kblk < n_blocks))
    ok = (col >= n_loc) | ok_local
    scores, vals = [], []
    for j in range(ATTN_KV_HEADS):
        q4 = q_ref[0, ATTN_GROUP * j:ATTN_GROUP * (j + 1)].reshape(rows, HEAD_DIM)
        keys = jnp.concatenate([kp_ref[0, j], kc_ref[0, j], kn_ref[0, j], kx_ref[0, j]], axis=0)
        vals.append(jnp.concatenate([vp_ref[0, j], vc_ref[0, j], vn_ref[0, j], vx_ref[0, j]], axis=0))
        scores.append(lax.dot_general(q4, keys, NT, preferred_element_type=f32))
    heads = [(j, g) for j in range(ATTN_KV_HEADS) for g in range(ATTN_GROUP)]
    sg = [jnp.where(ok, scores[j][g * blk:(g + 1) * blk], NEG_INF) for j, g in heads]
    sk = [sink_ref[ATTN_GROUP * j + g] for j, g in heads]
    m = [jnp.maximum(jnp.max(s, axis=-1, keepdims=True), k) for s, k in zip(sg, sk)]
    e = [jnp.exp(s - mm) for s, mm in zip(sg, m)]
    inv = [1.0 / (jnp.sum(ee, axis=-1, keepdims=True) + jnp.exp(k - mm)) for ee, k, mm in zip(e, sk, m)]
    p = [(ee * ii).astype(bf16) for ee, ii in zip(e, inv)]
    outs = []
    for j in range(ATTN_KV_HEADS):
        o4 = _dot(jnp.concatenate(p[ATTN_GROUP * j:ATTN_GROUP * (j + 1)], axis=0), vals[j])
        outs += [o4[g * blk:(g + 1) * blk] for g in range(ATTN_GROUP)]
    o_ref[0] = jnp.concatenate(outs, axis=1).astype(bf16)


def _rope_tables(geo):
    inv_freq = ROPE_BASE ** (-jnp.arange(ROPE_FREQS, dtype=f32) * 2.0 / ROPE_AXIS_DIM)
    pos = jnp.arange(geo.n_lat)
    row = (pos // GRID_W).astype(f32)
    col = (pos % GRID_W).astype(f32)
    ang = jnp.stack([row[:, None] * inv_freq, col[:, None] * inv_freq], axis=1)
    cos = jnp.cos(ang)
    sin = jnp.sin(ang)
    cos_h = jnp.concatenate([cos, cos], axis=2).reshape(geo.n_lat, HEAD_DIM)
    sin_h = jnp.concatenate([-sin, sin], axis=2).reshape(geo.n_lat, HEAD_DIM)
    cos_t = jnp.concatenate([jnp.ones((geo.n_ctx, HEAD_DIM), f32), cos_h], axis=0)
    sin_t = jnp.concatenate([jnp.zeros((geo.n_ctx, HEAD_DIM), f32), sin_h], axis=0)
    return jnp.tile(cos_t, (1, 2)), jnp.tile(sin_t, (1, 2))


def _attn_layer(geo, x, mods, gains, w_qkv, w_o, sink, w_in, w_out, skip_ctx):
    b, t, tm = geo.batch, geo.t, geo.tm
    q_cols = ATTN_HEADS * HEAD_DIM
    kv_cols = ATTN_KV_HEADS * HEAD_DIM
    wb = w_qkv.astype(bf16)
    cos_t, sin_t = _rope_tables(geo)
    tab_spec = pl.BlockSpec((tm, LANES), lambda bb, tt: (tt, 0))
    q, k, v = pl.pallas_call(
        _attn_proj_kernel,
        grid=(b, geo.nt),
        in_specs=[geo.x_spec(), geo.mod_spec(), _const_spec((8, D_MODEL)), tab_spec, tab_spec,
                  _const_spec((D_MODEL, q_cols)), _const_spec((D_MODEL, kv_cols)), _const_spec((D_MODEL, kv_cols))],
        out_specs=[pl.BlockSpec((1, ATTN_HEADS, tm, HEAD_DIM), lambda bb, tt: (bb, 0, tt, 0)),
                   pl.BlockSpec((1, ATTN_KV_HEADS, tm, HEAD_DIM), lambda bb, tt: (bb, 0, tt, 0)),
                   pl.BlockSpec((1, ATTN_KV_HEADS, tm, HEAD_DIM), lambda bb, tt: (bb, 0, tt, 0))],
        out_shape=[jax.ShapeDtypeStruct((b, ATTN_HEADS, t, HEAD_DIM), bf16),
                   jax.ShapeDtypeStruct((b, ATTN_KV_HEADS, t, HEAD_DIM), bf16),
                   jax.ShapeDtypeStruct((b, ATTN_KV_HEADS, t, HEAD_DIM), bf16)],
        compiler_params=_params(2),
        name="attn_proj",
    )(x, mods, gains, cos_t, sin_t, wb[:, :q_cols], wb[:, q_cols:q_cols + kv_cols], wb[:, q_cols + kv_cols:])

    blk = ATTN_BLOCK
    n_blocks = t // blk

    def kv_spec(off):
        return pl.BlockSpec((1, ATTN_KV_HEADS, blk, HEAD_DIM),
                            lambda bb, qb: (bb, 0, jnp.clip(qb + off, 0, n_blocks - 1), 0))

    ctx_spec = pl.BlockSpec((1, ATTN_KV_HEADS, geo.n_ctx, HEAD_DIM), lambda bb, qb: (bb, 0, 0, 0))
    o = pl.pallas_call(
        functools.partial(_attn_core_kernel, geo),
        grid=(b, n_blocks),
        in_specs=[pl.BlockSpec(memory_space=pltpu.SMEM),
                  pl.BlockSpec((1, ATTN_HEADS, blk, HEAD_DIM), lambda bb, qb: (bb, 0, qb, 0)),
                  kv_spec(-1), kv_spec(0), kv_spec(1), ctx_spec,
                  kv_spec(-1), kv_spec(0), kv_spec(1), ctx_spec],
        out_specs=pl.BlockSpec((1, blk, q_cols), lambda bb, qb: (bb, qb, 0)),
        out_shape=jax.ShapeDtypeStruct((b, t, q_cols), bf16),
        compiler_params=_params(2),
        name="attn_core",
    )(sink, q, k, k, k, k, v, v, v, v)

    t0 = geo.ctx_tiles if skip_ctx else 0
    rows = _post_rows(geo, 1)
    return _post_mlp(geo, x, mods, gains, _pre_attn, [o], [_tile_spec(geo, D_MODEL, 0, t0, rows)],
                     w_o.astype(bf16), w_in, w_out, skip_ctx, "attn_post", rows)


def _gated_chunk(dirs, heads, dk, dv):
    units = []
    for q, k, v, g, st_ref, base, reverse in dirs:
        n = q.shape[0]
        b = _cumsum_rows(g, reverse)
        btot = b[0:1] if reverse else b[n - 1:n]
        q_dec = (q * jnp.exp(b)).astype(bf16)
        k_inv32 = k * jnp.exp(-b)
        k_inv = k_inv32.astype(bf16)
        dec = jnp.exp(btot)
        k_end = (k_inv32 * dec).astype(bf16)
        vb = v.astype(bf16)
        ri = lax.broadcasted_iota(jnp.int32, (n, n), 0)
        ci = lax.broadcasted_iota(jnp.int32, (n, n), 1)
        tri = (ci >= ri) if reverse else (ci <= ri)
        for h in range(heads):
            ks = slice(h * dk, (h + 1) * dk)
            vs = slice(h * dv, (h + 1) * dv)
            units.append(dict(q=q_dec[:, ks], ki=k_inv[:, ks], ke=k_end[:, ks], v=vb[:, vs], dec=dec[:, ks],
                              tri=tri, st=st_ref[base + h], ref=st_ref, h=base + h))
    for u in units:
        a = lax.dot_general(u["q"], u["ki"], NT, preferred_element_type=f32)
        u["a"] = jnp.where(u["tri"], a, 0.0).astype(bf16)
        u["qs"] = lax.dot_general(u["q"], u["st"].astype(bf16), NT, preferred_element_type=f32)
    for u in units:
        u["o"] = _dot(u["a"], u["v"]) + u["qs"]
        u["new"] = u["st"] * u["dec"] + lax.dot_general(u["v"], u["ke"], TN, preferred_element_type=f32)
    for u in units:
        u["ref"][u["h"]] = u["new"]
    return [jnp.concatenate([u["o"] for u in units[d * heads:(d + 1) * heads]], axis=1) for d in range(len(dirs))]


def _gla_gate(zd, wu_ref, bias_ref, d):
    zg = _dot(zd.astype(bf16), wu_ref[d]) + bias_ref[d:d + 1]
    return (jnp.minimum(zg, 0.0) - jnp.log(1.0 + jnp.exp(-jnp.abs(zg)))) * (1.0 / GLA_TAU)


def _gla_scan_kernel(qkf_ref, vf_ref, zdf_ref, qkb_ref, vb_ref, zdb_ref, wu_ref, bias_ref,
                     of_ref, ob_ref, sf_ref, sb_ref):
    @pl.when(pl.program_id(1) == 0)
    def _():
        sf_ref[...] = jnp.zeros_like(sf_ref)
        sb_ref[...] = jnp.zeros_like(sb_ref)

    dirs, dests = [], []
    for row in range(qkf_ref.shape[0]):
        for d, (qk_ref, v_ref, zd_ref, s_ref, o_ref) in enumerate(
                ((qkf_ref, vf_ref, zdf_ref, sf_ref, of_ref), (qkb_ref, vb_ref, zdb_ref, sb_ref, ob_ref))):
            qk = qk_ref[row]
            q = qk[:, :GLA_KEY_DIM] * (GLA_DK ** -0.5)
            k = qk[:, GLA_KEY_DIM:]
            dirs.append((q, k, v_ref[row], _gla_gate(zd_ref[row], wu_ref, bias_ref, d), s_ref, row * GLA_HEADS,
                         d == 1))
            dests.append((o_ref, row))
    for o, (o_ref, row) in zip(_gated_chunk(dirs, GLA_HEADS, GLA_DK, GLA_DV), dests):
        o_ref[row] = o


def _hgrn_scan_kernel(qf_ref, if_ref, zff_ref, qb_ref, ib_ref, zfb_ref, lb_ref, of_ref, ob_ref, sf_ref, sb_ref):
    @pl.when(pl.program_id(1) == 0)
    def _():
        sf_ref[...] = jnp.zeros_like(sf_ref)
        sb_ref[...] = jnp.zeros_like(sb_ref)

    lb = lb_ref[...]
    dirs, dests = [], []
    for row in range(qf_ref.shape[0]):
        for d, (q_ref, i_ref, zf_ref, s_ref, o_ref) in enumerate(
                ((qf_ref, if_ref, zff_ref, sf_ref, of_ref), (qb_ref, ib_ref, zfb_ref, sb_ref, ob_ref))):
            f = lb + (1.0 - lb) * _sigmoid(zf_ref[row])
            dirs.append((_silu(q_ref[row]), 1.0 - f, i_ref[row], jnp.log(f), s_ref, row * HGRN_HEADS, d == 1))
            dests.append((o_ref, row))
    for o, (o_ref, row) in zip(_gated_chunk(dirs, HGRN_HEADS, HGRN_EXPAND, HGRN_EXPAND), dests):
        o_ref[row] = o


SCAN_ROWS_PER_STEP = 2


def _scan_rows(geo):
    return SCAN_ROWS_PER_STEP if geo.batch % SCAN_ROWS_PER_STEP == 0 else 1


def _chunk_spec(geo, width, col, reverse):
    rows = _scan_rows(geo)
    if reverse:
        return pl.BlockSpec((rows, SCAN_CHUNK, width), lambda b, i: (b, geo.bwd_chunk(i), col))
    return pl.BlockSpec((rows, SCAN_CHUNK, width), lambda b, i: (b, i, col))


def _scan_call(geo, body, args, specs, heads, dk, dv, name):
    out_sds = jax.ShapeDtypeStruct((geo.batch, geo.t, heads * dv), f32)
    rows = _scan_rows(geo)
    state = pltpu.VMEM((rows * heads, dv, dk), f32)
    return pl.pallas_call(
        body,
        grid=(geo.batch // rows, geo.nc),
        in_specs=specs,
        out_specs=[_chunk_spec(geo, heads * dv, 0, False), _chunk_spec(geo, heads * dv, 0, True)],
        out_shape=[out_sds, out_sds],
        scratch_shapes=[state, state],
        compiler_params=_params(2),
        name=name,
    )(*args)


def _gla_layer(geo, x, mods, gains, w_in_p, w_gd, w_gu, g_bias, g_norm, w_o, w_in, w_out, skip_ctx):
    r = GLA_GATE_RANK
    n_z = 2 * GLA_KEY_DIM + 2 * D_MODEL
    w_all = jnp.concatenate([w_in_p, w_gd[0], w_gd[1], jnp.zeros((D_MODEL, LANES - 2 * r), f32)], axis=1).astype(bf16)
    z = _mod_linear(geo, x, mods, gains, w_all, "gla_proj")
    wu = jnp.zeros((2, LANES, GLA_KEY_DIM), f32).at[0, :r].set(w_gu[0]).at[1, r:2 * r].set(w_gu[1]).astype(bf16)
    zd_col = n_z // LANES
    specs = []
    for rev in (False, True):
        specs += [_chunk_spec(geo, 2 * GLA_KEY_DIM, 0, rev), _chunk_spec(geo, D_MODEL, 1, rev),
                  _chunk_spec(geo, LANES, zd_col, rev)]
    specs += [_const_spec((2, LANES, GLA_KEY_DIM)), _const_spec((2, GLA_KEY_DIM))]
    o_f, o_b = _scan_call(geo, _gla_scan_kernel, [z, z, z, z, z, z, wu, g_bias], specs,
                          GLA_HEADS, GLA_DK, GLA_DV, "gla_scan")
    t0 = geo.ctx_tiles if skip_ctx else 0
    rows = _post_rows(geo, 3)
    mix_specs = [_tile_spec(geo, D_MODEL, 0, t0, rows), _tile_spec(geo, D_MODEL, 0, t0, rows),
                 _tile_spec(geo, D_MODEL, 2, t0, rows), _const_spec((1, GLA_DV))]
    return _post_mlp(geo, x, mods, gains, functools.partial(_pre_scan, GLA_DV),
                     [o_f, o_b, z, g_norm.reshape(1, GLA_DV)], mix_specs,
                     w_o.astype(bf16), w_in, w_out, skip_ctx, "gla_post", rows)


def _hgrn_layer(geo, x, mods, gains, w_in_p, w_f, lower_bound, g_norm, w_o, w_in, w_out, skip_ctx):
    w_all = jnp.concatenate([w_in_p, w_f[0], w_f[1]], axis=1).astype(bf16)
    z = _mod_linear(geo, x, mods, gains, w_all, "hgrn_proj")
    specs = []
    for rev in (False, True):
        specs += [_chunk_spec(geo, D_MODEL, 0, rev), _chunk_spec(geo, D_MODEL, 1, rev),
                  _chunk_spec(geo, D_MODEL, 4 if rev else 3, rev)]
    specs += [_const_spec((1, D_MODEL))]
    o_f, o_b = _scan_call(geo, _hgrn_scan_kernel, [z, z, z, z, z, z, lower_bound.reshape(1, D_MODEL)], specs,
                          HGRN_HEADS, HGRN_EXPAND, HGRN_EXPAND, "hgrn_scan")
    t0 = geo.ctx_tiles if skip_ctx else 0
    rows = _post_rows(geo, 3)
    mix_specs = [_tile_spec(geo, D_MODEL, 0, t0, rows), _tile_spec(geo, D_MODEL, 0, t0, rows),
                 _tile_spec(geo, D_MODEL, 2, t0, rows), _const_spec((1, HGRN_EXPAND))]
    return _post_mlp(geo, x, mods, gains, functools.partial(_pre_scan, HGRN_EXPAND),
                     [o_f, o_b, z, g_norm.reshape(1, HGRN_EXPAND)], mix_specs,
                     w_o.astype(bf16), w_in, w_out, skip_ctx, "hgrn_post", rows)


def _rwkv_proj_kernel(geo, x_ref, xp_ref, xn_ref, m_ref, g_ref, mix_ref, vec_ref, wrkv_ref, wdn_ref, wup_ref,
                      aup_ref, gup_ref, r_ref, v_ref, gate_ref, kk_ref, lw0_ref, lw1_ref, a0_ref, a1_ref,
                      k0_ref, k1_ref):
    t = pl.program_id(1)
    tm = geo.tm
    gain, shift, scale = g_ref[0:1], m_ref[0, 0, 0:1], m_ref[0, 0, 1:2]
    h = _modulate(x_ref[0], gain, shift, scale)
    seg_first = (t == 0) | (t == geo.ctx_tiles)
    seg_last = (t == geo.ctx_tiles - 1) | (t == geo.nt - 1)
    h_prev = jnp.where(seg_first, 0.0, _modulate(xp_ref[0], gain, shift, scale)[7:8])
    h_next = jnp.where(seg_last, 0.0, _modulate(xn_ref[0], gain, shift, scale)[0:1])
    row = lax.broadcasted_iota(jnp.int32, h.shape, 0)
    up = jnp.where(row == 0, h_prev, pltpu.roll(h, 1, axis=0))
    dn = jnp.where(row == tm - 1, h_next, pltpu.roll(h, tm - 1, axis=0))
    dx = 0.5 * (up + dn) - h

    def mixed(n):
        return (h + dx * mix_ref[n:n + 1]).astype(bf16)

    r = _dot(mixed(0), wrkv_ref[0])
    k = _dot(mixed(1), wrkv_ref[1])
    v = _dot(mixed(2), wrkv_ref[2])
    dw = jnp.tanh(_dot(mixed(3), wdn_ref[:, 0:LANES])).astype(bf16)
    da = _dot(mixed(4), wdn_ref[:, LANES:2 * LANES]).astype(bf16)
    dg = _sigmoid(_dot(mixed(5), wdn_ref[:, 2 * LANES:3 * LANES])).astype(bf16)
    r_ref[0] = r
    v_ref[0] = v
    gate_ref[0] = _dot(dg, gup_ref[...])
    kk = k * vec_ref[4:5]
    kk_ref[0] = kk * lax.rsqrt(jnp.maximum(_seg_sum(kk * kk, RWKV_HEAD_SIZE), L2_EPS * L2_EPS))
    for d, (lw_ref, a_ref, kd_ref) in enumerate(((lw0_ref, a0_ref, k0_ref), (lw1_ref, a1_ref, k1_ref))):
        lw_ref[0] = -RWKV_DECAY_SCALE * _sigmoid(vec_ref[d:d + 1] + _dot(dw, wup_ref[d]))
        a = _sigmoid(vec_ref[2 + d:3 + d] + _dot(da, aup_ref[d]))
        a_ref[0] = a
        kd_ref[0] = k * (1.0 + (a - 1.0) * vec_ref[5:6])


def _rwkv_chunk_units(units):
    n = SCAN_CHUNK
    hs = RWKV_HEAD_SIZE
    lane = lax.broadcasted_iota(jnp.int32, (n, RWKV_PAIR), 1)
    t_i = lax.broadcasted_iota(jnp.int32, (n, RWKV_PAIR), 0)
    s_i = lane % hs
    head0 = lane < hs
    ri = lax.broadcasted_iota(jnp.int32, (RWKV_PAIR, RWKV_PAIR), 0)
    ci = lax.broadcasted_iota(jnp.int32, (RWKV_PAIR, RWKV_PAIR), 1)
    same_head = ri // hs == ci // hs
    top_rows = ri < hs

    def block_diag(x):
        return jnp.concatenate([jnp.where(head0, x, 0.0), jnp.where(head0, 0.0, x)], axis=0)

    def cross_blocks(b):
        return (t_i // (2 * b) == s_i // (2 * b)) & (t_i // b != s_i // b)

    st = []
    for r, kd, v, kk, a, lw, ht, reverse in units:
        c = _cumsum_rows(lw, reverse)
        tot = c[0:1] if reverse else c[n - 1:n]
        e_neg = jnp.exp(-c)
        dec = jnp.exp(tot)
        kb = jnp.concatenate([kd * e_neg, kk * a * e_neg], axis=0)
        kb_t = kb.T
        kb_sw = pltpu.roll(kb_t, hs, axis=1)
        rhs = jnp.concatenate([jnp.where(same_head, jnp.where(top_rows, kb_t, kb_sw), 0.0),
                               jnp.where(same_head, jnp.where(top_rows, kb_sw, kb_t), 0.0)], axis=1)
        st.append(dict(
            strict=(s_i > t_i) if reverse else (s_i < t_i), incl=(s_i >= t_i) if reverse else (s_i <= t_i),
            a_bar=(-kk * jnp.exp(c - lw)).astype(bf16), r_bar=(r * jnp.exp(c)).astype(bf16), rhs=rhs.astype(bf16),
            v_bd=block_diag(v).astype(bf16), h_t=ht.T.astype(bf16),
            ends=(kb * dec).astype(bf16),
            v=v, ht=ht, dec=dec))
    for s in st:
        g = _dot(jnp.concatenate([s["a_bar"], s["r_bar"]], axis=0), s["rhs"])
        s["a_ak"] = jnp.where(s["strict"], g[0:n, 0:RWKV_PAIR], 0.0).astype(bf16)
        s["nmat"] = jnp.where(s["strict"], g[0:n, RWKV_PAIR:], 0.0)
        s["q_k"] = jnp.where(s["incl"], g[n:, 0:RWKV_PAIR], 0.0).astype(bf16)
        s["q_b"] = jnp.where(s["incl"], g[n:, RWKV_PAIR:], 0.0).astype(bf16)
    for s in st:
        s["inv"] = jnp.where(s_i == t_i, 1.0, jnp.where(cross_blocks(1), s["nmat"], 0.0))
    b = 2
    while 2 * b < n:
        for s in st:
            s["pc"] = _dot(s["inv"].astype(bf16), block_diag(jnp.where(cross_blocks(b), s["nmat"], 0.0)).astype(bf16))
        for s in st:
            s["inv"] = s["inv"] + _dot(s["pc"].astype(bf16), block_diag(s["inv"]).astype(bf16))
        b *= 2
    for s in st:
        x = _dot(jnp.concatenate([s["a_bar"], s["a_ak"]], axis=1), jnp.concatenate([s["h_t"], s["v_bd"]], axis=0))
        s["inv"] = s["inv"].astype(bf16)
        s["w"] = _dot(s["inv"], block_diag(x).astype(bf16))
    for s in st:
        s["cw"] = _dot(jnp.where(cross_blocks(n // 2), s["nmat"], 0.0).astype(bf16), block_diag(s["w"]).astype(bf16))
    for s in st:
        s["u"] = s["w"] + _dot(s["inv"], block_diag(s["cw"]).astype(bf16))
    for s in st:
        s["vu_t"] = jnp.concatenate([s["v"], s["u"]], axis=0).T.astype(bf16)
    for s in st:
        s["y"] = _dot(jnp.concatenate([s["r_bar"], s["q_k"], s["q_b"]], axis=1),
                      jnp.concatenate([s["h_t"], s["v_bd"], block_diag(s["u"]).astype(bf16)], axis=0))
    for s in st:
        s["upd"] = _dot(s["vu_t"], s["ends"])
    return [(s["y"], s["ht"] * s["dec"] + jnp.where(same_head, s["upd"], 0.0)) for s in st]


def _rwkv_scan_kernel(rows, pairs, *refs):
    fwd, bwd = refs[0:6], refs[6:12]
    yf_ref, yb_ref, hf_ref, hb_ref = refs[12:]

    @pl.when(pl.program_id(2) == 0)
    def _():
        hf_ref[...] = jnp.zeros_like(hf_ref)
        hb_ref[...] = jnp.zeros_like(hb_ref)

    units, dests = [], []
    for ins, y_ref, h_ref, reverse in ((fwd, yf_ref, hf_ref, False), (bwd, yb_ref, hb_ref, True)):
        r_ref, v_ref, kk_ref, lw_ref, a_ref, kd_ref = ins
        for row in range(rows):
            for p in range(pairs):
                sl = slice(p * RWKV_PAIR, (p + 1) * RWKV_PAIR)
                slot = row * pairs + p
                units.append((r_ref[row, :, sl], kd_ref[row, :, sl], v_ref[row, :, sl], kk_ref[row, :, sl],
                              a_ref[row, :, sl], lw_ref[row, :, sl], h_ref[slot], reverse))
                dests.append((y_ref, h_ref, row, slot, sl))
    for (y, h_new), (y_ref, h_ref, row, slot, sl) in zip(_rwkv_chunk_units(units), dests):
        y_ref[row, :, sl] = y
        h_ref[slot] = h_new


RWKV_PAIRS_PER_STEP = 8
RWKV_ROWS_PER_STEP = 2


def _rwkv_layer(geo, x, mods, gains, mix, w_rkv, w0, w_down, w_up, a0, a_down, a_up, g_down, g_up, k_k, k_a,
                r_k, ln_w, ln_b, w_o, w_in, w_out, skip_ctx):
    b, t, tm = geo.batch, geo.t, geo.tm
    d = D_MODEL
    rank = w_down.shape[-1]
    assert 2 * rank == LANES and a_down.shape[-1] == rank and g_down.shape[-1] == LANES
    mix8 = jnp.pad(mix, ((0, 2), (0, 0)))
    vec = jnp.stack([w0[0], w0[1], a0[0], a0[1], k_k, k_a, jnp.zeros_like(k_k), jnp.zeros_like(k_k)])
    w_dn = jnp.concatenate([w_down[0], w_down[1], a_down[0], a_down[1], g_down], axis=1).astype(bf16)

    def padded_up(w):
        return jnp.zeros((2, LANES, d), f32).at[0, :rank].set(w[0]).at[1, rank:].set(w[1]).astype(bf16)

    halo = 8
    n_halo = t // halo
    per = tm // halo
    x_prev = pl.BlockSpec((1, halo, d), lambda bb, tt: (bb, jnp.maximum(tt * per - 1, 0), 0))
    x_next = pl.BlockSpec((1, halo, d), lambda bb, tt: (bb, jnp.minimum((tt + 1) * per, n_halo - 1), 0))
    out_spec = pl.BlockSpec((1, tm, d), lambda bb, tt: (bb, tt, 0))
    sds = jax.ShapeDtypeStruct((b, t, d), f32)
    r, v, gate, kk, lw0, lw1, a_0, a_1, k0, k1 = pl.pallas_call(
        functools.partial(_rwkv_proj_kernel, geo),
        grid=(b, geo.nt),
        in_specs=[geo.x_spec(), x_prev, x_next, geo.mod_spec(), _const_spec((8, d)), _const_spec((8, d)),
                  _const_spec((8, d)), _const_spec((3, d, d)), _const_spec((d, 3 * LANES)),
                  _const_spec((2, LANES, d)), _const_spec((2, LANES, d)), _const_spec((LANES, d))],
        out_specs=[out_spec] * 10,
        out_shape=[sds] * 10,
        compiler_params=_params(2),
        name="rwkv_proj",
    )(x, x, x, mods, gains, mix8, vec, w_rkv.astype(bf16), w_dn, padded_up(w_up), padded_up(a_up),
      g_up.astype(bf16))

    pairs = RWKV_PAIRS_PER_STEP
    rows = RWKV_ROWS_PER_STEP if b % RWKV_ROWS_PER_STEP == 0 else 1
    width = pairs * RWKV_PAIR
    groups = d // width

    def cspec(reverse):
        if reverse:
            return pl.BlockSpec((rows, SCAN_CHUNK, width), lambda bb, gg, i: (bb, geo.bwd_chunk(i), gg))
        return pl.BlockSpec((rows, SCAN_CHUNK, width), lambda bb, gg, i: (bb, i, gg))

    state = pltpu.VMEM((rows * pairs, RWKV_PAIR, RWKV_PAIR), f32)
    y_f, y_b = pl.pallas_call(
        functools.partial(_rwkv_scan_kernel, rows, pairs),
        grid=(b // rows, groups, geo.nc),
        in_specs=[cspec(False)] * 6 + [cspec(True)] * 6,
        out_specs=[cspec(False), cspec(True)],
        out_shape=[sds, sds],
        scratch_shapes=[state, state],
        compiler_params=_params(3),
        name="rwkv_scan",
    )(r, v, kk, lw0, a_0, k0, r, v, kk, lw1, a_1, k1)

    t0 = geo.ctx_tiles if skip_ctx else 0
    post_vec = jnp.stack([ln_w, ln_b, r_k.reshape(d)] + [jnp.zeros_like(ln_w)] * 5)
    rows = _post_rows(geo, 7)
    mix_specs = [_tile_spec(geo, d, 0, t0, rows)] * 7 + [_const_spec((8, d))]
    return _post_mlp(geo, x, mods, gains, _pre_rwkv, [y_f, y_b, r, k0, k1, v, gate, post_vec], mix_specs,
                     w_o.astype(bf16), w_in, w_out, skip_ctx, "rwkv_post", rows)


def _hgrn_lower_bound(lb_param, layer):
    p = jax.nn.softmax(lb_param.astype(f32), axis=0)
    return (jnp.cumsum(p, axis=0) - p[0])[layer]


def kernel(x, c, ctx, c_ctx, w_mod, b_mod, g_pre_mix, g_post_mix, g_pre_mlp, g_post_mlp, w_mlp_in, w_mlp_out, attn_w_qkv, attn_w_o, attn_sink, gla_w_in, gla_w_gate_down, gla_w_gate_up, gla_gate_bias, gla_g_norm, gla_w_o, rwkv_mix, rwkv_w_rkv, rwkv_w0, rwkv_w_down, rwkv_w_up, rwkv_a0, rwkv_a_down, rwkv_a_up, rwkv_g_down, rwkv_g_up, rwkv_k_k, rwkv_k_a, rwkv_r_k, rwkv_ln_w, rwkv_ln_b, rwkv_w_o, hgrn_w_in, hgrn_w_f, hgrn_lb, hgrn_g_norm, hgrn_w_o):
    depth = w_mod.shape[0]
    geo = _Geom(x.shape[0], ctx.shape[1], x.shape[1])
    mods_all = _mod_vectors(c, c_ctx, w_mod, b_mod)
    xs = jnp.concatenate([ctx, x], axis=1)
    for i in range(depth):
        kind, j = i % 4, i // 4
        skip_ctx = i == depth - 1
        mods = mods_all[i]
        gains = jnp.stack([g_pre_mix[i], g_post_mix[i], g_pre_mlp[i], g_post_mlp[i]] + [jnp.zeros_like(g_pre_mix[i])] * 4)
        w_in, w_out = w_mlp_in[i].astype(bf16), w_mlp_out[i].astype(bf16)
        if kind == 0:
            xs = _attn_layer(geo, xs, mods, gains, attn_w_qkv[j], attn_w_o[j], attn_sink[j], w_in, w_out, skip_ctx)
        elif kind == 1:
            xs = _gla_layer(geo, xs, mods, gains, gla_w_in[j], gla_w_gate_down[j], gla_w_gate_up[j],
                            gla_gate_bias[j], gla_g_norm[j], gla_w_o[j], w_in, w_out, skip_ctx)
        elif kind == 2:
            xs = _rwkv_layer(geo, xs, mods, gains, rwkv_mix[j], rwkv_w_rkv[j], rwkv_w0[j], rwkv_w_down[j],
                             rwkv_w_up[j], rwkv_a0[j], rwkv_a_down[j], rwkv_a_up[j], rwkv_g_down[j], rwkv_g_up[j],
                             rwkv_k_k[j], rwkv_k_a[j], rwkv_r_k[j], rwkv_ln_w[j], rwkv_ln_b[j], rwkv_w_o[j],
                             w_in, w_out, skip_ctx)
        else:
            xs = _hgrn_layer(geo, xs, mods, gains, hgrn_w_in[j], hgrn_w_f[j], _hgrn_lower_bound(hgrn_lb, i),
                             hgrn_g_norm[j], hgrn_w_o[j], w_in, w_out, skip_ctx)
        if skip_ctx:
            return xs
    return xs[:, geo.n_ctx:]
```

```python
import functools

import jax
import jax.numpy as jnp
from jax import lax
from jax.experimental import pallas as pl
from jax.experimental.pallas import tpu as pltpu

f32 = jnp.float32
bf16 = jnp.bfloat16

D_MODEL = 1024
N_MOD = 6
MLP_HIDDEN = 4 * D_MODEL
NORM_EPS = 1e-6
NEG_INF = -1e30
GRID_W = 64

ATTN_HEADS = 16
ATTN_KV_HEADS = 4
ATTN_GROUP = ATTN_HEADS // ATTN_KV_HEADS
HEAD_DIM = 64
WINDOW = 128
ATTN_BLOCK = 128
ROPE_BASE = 10000.0
ROPE_AXIS_DIM = HEAD_DIM // 2
ROPE_FREQS = ROPE_AXIS_DIM // 2

GLA_HEADS = 4
GLA_KEY_DIM = D_MODEL // 2
GLA_DK = GLA_KEY_DIM // GLA_HEADS
GLA_DV = D_MODEL // GLA_HEADS
GLA_GATE_RANK = 16
GLA_TAU = 16.0
SCAN_CHUNK = 64

RWKV_HEAD_SIZE = 64
RWKV_LN_EPS = 64e-5
L2_EPS = 1e-12
RWKV_PAIR = 2 * RWKV_HEAD_SIZE
RWKV_DECAY_SCALE = 0.6065306597126334

HGRN_EXPAND = 128
HGRN_HEADS = D_MODEL // HGRN_EXPAND

LANES = 128
MOD_ROWS = 8
VMEM_LIMIT = 56 * 1024 * 1024

NT = (((1,), (1,)), ((), ()))
TN = (((0,), (0,)), ((), ()))


def _params(n_grid):
    return pltpu.CompilerParams(dimension_semantics=("arbitrary",) * n_grid, vmem_limit_bytes=VMEM_LIMIT)


def _const_spec(shape):
    nd = len(shape)
    return pl.BlockSpec(shape, lambda *_: (0,) * nd, pipeline_mode=pl.Buffered(1))


def _dot(a, b):
    return jnp.dot(a, b, preferred_element_type=f32)


def _sigmoid(x):
    return 0.5 * jnp.tanh(0.5 * x) + 0.5


def _silu(x):
    return x * _sigmoid(x)


def _rms(x, gain):
    return x * lax.rsqrt(jnp.mean(x * x, axis=-1, keepdims=True) + NORM_EPS) * gain


def _modulate(x, gain, shift, scale):
    return _rms(x, gain) * (1.0 + scale) + shift


def _seg_sum(x, seg):
    r = lax.broadcasted_iota(jnp.int32, (LANES, LANES), 0) // seg
    c = lax.broadcasted_iota(jnp.int32, (LANES, LANES), 1) // seg
    ones_bd = (r == c).astype(bf16)
    hi = x.astype(bf16)
    lo = (x - hi.astype(f32)).astype(bf16)
    outs = []
    for j in range(x.shape[1] // LANES):
        sl = slice(j * LANES, (j + 1) * LANES)
        outs.append(_dot(hi[:, sl], ones_bd) + _dot(lo[:, sl], ones_bd))
    return jnp.concatenate(outs, axis=1)


def _cumsum_rows(g, reverse):
    n = g.shape[0]
    ri = lax.broadcasted_iota(jnp.int32, (n, n), 0)
    ci = lax.broadcasted_iota(jnp.int32, (n, n), 1)
    tri = ((ci >= ri) if reverse else (ci <= ri)).astype(bf16)
    hi = g.astype(bf16)
    lo = (g - hi.astype(f32)).astype(bf16)
    return _dot(tri, hi) + _dot(tri, lo)


def _mod_kernel(c_ref, w_ref, b_ref, o_ref):
    o_ref[0] = _dot(_silu(c_ref[...]).astype(bf16), w_ref[0]) + b_ref[0]


def _mod_vectors(c, c_ctx, w_mod, b_mod):
    depth, d, _ = w_mod.shape
    batch = c.shape[0]
    rows = -(-(batch + 1) // 8) * 8
    cc = jnp.zeros((rows, d), f32).at[:batch].set(c).at[batch].set(c_ctx)
    out = pl.pallas_call(
        _mod_kernel,
        grid=(depth, N_MOD),
        in_specs=[
            pl.BlockSpec((rows, d), lambda i, j: (0, 0)),
            pl.BlockSpec((1, d, d), lambda i, j: (i, 0, j)),
            pl.BlockSpec((1, 1, d), lambda i, j: (i, 0, j)),
        ],
        out_specs=pl.BlockSpec((1, rows, d), lambda i, j: (i, 0, j)),
        out_shape=jax.ShapeDtypeStruct((depth, rows, N_MOD * d), f32),
        compiler_params=_params(2),
        name="mod_vectors",
    )(cc, w_mod.astype(bf16), b_mod.reshape(depth, 1, N_MOD * d))
    lat = out[:, :batch].reshape(depth, batch, 1, N_MOD, d)
    con = jnp.broadcast_to(out[:, batch].reshape(depth, 1, 1, N_MOD, d), lat.shape)
    mods = jnp.concatenate([con, lat], axis=2)
    return jnp.pad(mods, ((0, 0), (0, 0), (0, 0), (0, MOD_ROWS - N_MOD), (0, 0)))


class _Geom:
    def __init__(self, batch, n_ctx, n_lat):
        self.batch, self.n_ctx, self.n_lat = batch, n_ctx, n_lat
        self.t = n_ctx + n_lat
        self.tm = min(256, n_ctx)
        assert n_ctx % self.tm == 0 and n_lat % self.tm == 0
        assert n_ctx % ATTN_BLOCK == 0 and n_lat % ATTN_BLOCK == 0 and n_lat % GRID_W == 0
        self.nt = self.t // self.tm
        self.ctx_tiles = n_ctx // self.tm
        self.nc = self.t // SCAN_CHUNK
        self.ctx_chunks = n_ctx // SCAN_CHUNK

    def x_spec(self, t0=0, rows=1):
        return pl.BlockSpec((rows, self.tm, D_MODEL), lambda b, t: (b, t + t0, 0))

    def mod_spec(self, t0=0, rows=1):
        ct = self.ctx_tiles
        return pl.BlockSpec((rows, 1, MOD_ROWS, D_MODEL), lambda b, t: (b, ((t + t0) >= ct).astype(jnp.int32), 0, 0))

    def bwd_chunk(self, i):
        cc = self.ctx_chunks
        return jnp.where(i < cc, cc - 1 - i, self.nc - 1 + cc - i)


LIN_ROWS_PER_STEP = 2


def _lin_kernel(x_ref, m_ref, g_ref, w_ref, o_ref):
    rows, tm = x_ref.shape[0], x_ref.shape[1]
    h = [_modulate(x_ref[r], g_ref[0:1], m_ref[r, 0, 0:1], m_ref[r, 0, 1:2]).astype(bf16) for r in range(rows)]
    z = _dot(jnp.concatenate(h, axis=0), w_ref[...])
    for r in range(rows):
        o_ref[r] = z[r * tm:(r + 1) * tm]


def _mod_linear(geo, x, mods, gains, w, name):
    n = w.shape[1]
    rows = LIN_ROWS_PER_STEP if geo.batch % LIN_ROWS_PER_STEP == 0 else 1
    return pl.pallas_call(
        _lin_kernel,
        grid=(geo.batch // rows, geo.nt),
        in_specs=[geo.x_spec(0, rows), geo.mod_spec(0, rows), _const_spec((8, D_MODEL)), _const_spec((D_MODEL, n))],
        out_specs=pl.BlockSpec((rows, geo.tm, n), lambda b, t: (b, t, 0)),
        out_shape=jax.ShapeDtypeStruct((geo.batch, geo.t, n), f32),
        compiler_params=_params(2),
        name=name,
    )(x, mods, gains, w)


def _head_rms(o, gain, width):
    outs = []
    for h in range(o.shape[1] // width):
        oh = o[:, h * width:(h + 1) * width]
        outs.append(oh * lax.rsqrt(jnp.mean(oh * oh, axis=-1, keepdims=True) + NORM_EPS) * gain)
    return jnp.concatenate(outs, axis=1)


def _pre_attn(refs, row):
    (o_ref,) = refs
    return o_ref[row]


def _pre_scan(width, refs, row):
    of_ref, ob_ref, gate_ref, gn_ref = refs
    o = _head_rms(of_ref[row] + ob_ref[row], gn_ref[...], width)
    return (o * _silu(gate_ref[row])).astype(bf16)


def _pre_rwkv(refs, row):
    yf_ref, yb_ref, g_ref, bg_ref, vec_ref = refs
    y = yf_ref[row] + yb_ref[row]
    inv_n = 1.0 / RWKV_HEAD_SIZE
    mu = _seg_sum(y, RWKV_HEAD_SIZE) * inv_n
    dlt = y - mu
    var = _seg_sum(dlt * dlt, RWKV_HEAD_SIZE) * inv_n
    yn = dlt * lax.rsqrt(var + RWKV_LN_EPS) * vec_ref[0:1] + vec_ref[1:2]
    return (yn * g_ref[row] + bg_ref[row]).astype(bf16)


def _post_kernel(pre, n_mix, x_ref, m_ref, g_ref, *rest):
    mix_refs = rest[:n_mix]
    wo_ref, win_ref, wout_ref, out_ref = rest[n_mix:]
    rows = range(x_ref.shape[0])
    o = [pre(mix_refs, r) for r in rows]
    y = [_dot(o[r], wo_ref[...]) for r in rows]
    x1 = [x_ref[r] + m_ref[r, 0, 2:3] * _rms(y[r], g_ref[1:2]) for r in rows]
    h2 = [_modulate(x1[r], g_ref[2:3], m_ref[r, 0, 3:4], m_ref[r, 0, 4:5]).astype(bf16) for r in rows]
    acc = [None for _ in rows]
    for c in range(MLP_HIDDEN // D_MODEL):
        cols = slice(c * D_MODEL, (c + 1) * D_MODEL)
        u = [jnp.square(jnp.maximum(_dot(h2[r], win_ref[:, cols]), 0.0)).astype(bf16) for r in rows]
        for r in rows:
            part = _dot(u[r], wout_ref[cols, :])
            acc[r] = part if c == 0 else acc[r] + part
    for r in rows:
        out_ref[r] = x1[r] + m_ref[r, 0, 5:6] * _rms(acc[r], g_ref[3:4])


POST_ROWS_PER_STEP = 2


def _post_rows(geo, n_tile_inputs):
    tile = geo.tm * D_MODEL * 4
    weights = 2 * (D_MODEL * D_MODEL + 2 * D_MODEL * MLP_HIDDEN)
    for rows in (POST_ROWS_PER_STEP, 1):
        streamed = 2 * (n_tile_inputs + 2) * rows * tile
        live = 6 * rows * tile
        if geo.batch % rows == 0 and weights + streamed + live <= VMEM_LIMIT:
            return rows
    return 1


def _post_mlp(geo, x, mods, gains, pre, mix_args, mix_specs, w_o, w_in, w_out, skip_ctx, name, rows):
    t0 = geo.ctx_tiles if skip_ctx else 0
    return pl.pallas_call(
        functools.partial(_post_kernel, pre, len(mix_args)),
        grid=(geo.batch // rows, geo.nt - t0),
        in_specs=[geo.x_spec(t0, rows), geo.mod_spec(t0, rows), _const_spec((8, D_MODEL))] + mix_specs + [
            _const_spec((D_MODEL, D_MODEL)), _const_spec((D_MODEL, MLP_HIDDEN)), _const_spec((MLP_HIDDEN, D_MODEL))],
        out_specs=pl.BlockSpec((rows, geo.tm, D_MODEL), lambda b, t: (b, t, 0)),
        out_shape=jax.ShapeDtypeStruct((geo.batch, geo.t - t0 * geo.tm, D_MODEL), f32),
        compiler_params=_params(2),
        name=name,
    )(x, mods, gains, *mix_args, w_o, w_in, w_out)


def _tile_spec(geo, width, col, t0, rows):
    return pl.BlockSpec((rows, geo.tm, width), lambda b, t: (b, t + t0, col))


def _rope(x, cos, sin):
    w = x.shape[1]
    reps = w // LANES
    cw = jnp.concatenate([cos] * reps, axis=1)
    sw = jnp.concatenate([sin] * reps, axis=1)
    lane = lax.broadcasted_iota(jnp.int32, x.shape, 1)
    first = (lane % ROPE_AXIS_DIM) < ROPE_FREQS
    partner = jnp.where(first, pltpu.roll(x, w - ROPE_FREQS, axis=1), pltpu.roll(x, ROPE_FREQS, axis=1))
    return x * cw + partner * sw


def _attn_proj_kernel(x_ref, m_ref, g_ref, cos_ref, sin_ref, wq_ref, wk_ref, wv_ref, q_ref, k_ref, v_ref):
    h = _modulate(x_ref[0], g_ref[0:1], m_ref[0, 0, 0:1], m_ref[0, 0, 1:2]).astype(bf16)
    cos, sin = cos_ref[...], sin_ref[...]
    q = _rope(_dot(h, wq_ref[...]) * (HEAD_DIM ** -0.5), cos, sin)
    k = _rope(_dot(h, wk_ref[...]), cos, sin)
    v = _dot(h, wv_ref[...])
    for hd in range(ATTN_HEADS):
        q_ref[0, hd] = q[:, hd * HEAD_DIM:(hd + 1) * HEAD_DIM].astype(bf16)
    for hd in range(ATTN_KV_HEADS):
        k_ref[0, hd] = k[:, hd * HEAD_DIM:(hd + 1) * HEAD_DIM].astype(bf16)
        v_ref[0, hd] = v[:, hd * HEAD_DIM:(hd + 1) * HEAD_DIM].astype(bf16)


def _attn_core_kernel(geo, sink_ref, q_ref, kp_ref, kc_ref, kn_ref, kx_ref, vp_ref, vc_ref, vn_ref, vx_ref, o_ref):
    qb = pl.program_id(1)
    blk = ATTN_BLOCK
    n_loc = 3 * blk
    n_keys = n_loc + geo.n_ctx
    rows = ATTN_GROUP * blk
    first_lat = geo.n_ctx // blk
    n_blocks = geo.t // blk
    row = lax.broadcasted_iota(jnp.int32, (blk, n_keys), 0)
    col = lax.broadcasted_iota(jnp.int32, (blk, n_keys), 1)
    kblk = qb - 1 + col // blk
    ok_local = ((jnp.abs(col - blk - row) <= WINDOW) & (qb >= first_lat) & (kblk >= first_lat) & (kblk < n_blocks))
    ok = (col >= n_loc) | ok_local
    scores, vals = [], []
    for j in range(ATTN_KV_HEADS):
        q4 = q_ref[0, ATTN_GROUP * j:ATTN_GROUP * (j + 1)].reshape(rows, HEAD_DIM)
        keys = jnp.concatenate([kp_ref[0, j], kc_ref[0, j], kn_ref[0, j], kx_ref[0, j]], axis=0)
        vals.append(jnp.concatenate([vp_ref[0, j], vc_ref[0, j], vn_ref[0, j], vx_ref[0, j]], axis=0))
        scores.append(lax.dot_general(q4, keys, NT, preferred_element_type=f32))
    heads = [(j, g) for j in range(ATTN_KV_HEADS) for g in range(ATTN_GROUP)]
    sg = [jnp.where(ok, scores[j][g * blk:(g + 1) * blk], NEG_INF) for j, g in heads]
    sk = [sink_ref[ATTN_GROUP * j + g] for j, g in heads]
    m = [jnp.maximum(jnp.max(s, axis=-1, keepdims=True), k) for s, k in zip(sg, sk)]
    e = [jnp.exp(s - mm) for s, mm in zip(sg, m)]
    inv = [1.0 / (jnp.sum(ee, axis=-1, keepdims=True) + jnp.exp(k - mm)) for ee, k, mm in zip(e, sk, m)]
    p = [ee.astype(bf16) for ee in e]
    outs = []
    for j in range(ATTN_KV_HEADS):
        o4 = _dot(jnp.concatenate(p[ATTN_GROUP * j:ATTN_GROUP * (j + 1)], axis=0), vals[j])
        outs += [o4[g * blk:(g + 1) * blk] * inv[ATTN_GROUP * j + g] for g in range(ATTN_GROUP)]
    o_ref[0] = jnp.concatenate(outs, axis=1).astype(bf16)


def _rope_tables(geo):
    inv_freq = ROPE_BASE ** (-jnp.arange(ROPE_FREQS, dtype=f32) * 2.0 / ROPE_AXIS_DIM)
    pos = jnp.arange(geo.n_lat)
    row = (pos // GRID_W).astype(f32)
    col = (pos % GRID_W).astype(f32)
    ang = jnp.stack([row[:, None] * inv_freq, col[:, None] * inv_freq], axis=1)
    cos = jnp.cos(ang)
    sin = jnp.sin(ang)
    cos_h = jnp.concatenate([cos, cos], axis=2).reshape(geo.n_lat, HEAD_DIM)
    sin_h = jnp.concatenate([-sin, sin], axis=2).reshape(geo.n_lat, HEAD_DIM)
    cos_t = jnp.concatenate([jnp.ones((geo.n_ctx, HEAD_DIM), f32), cos_h], axis=0)
    sin_t = jnp.concatenate([jnp.zeros((geo.n_ctx, HEAD_DIM), f32), sin_h], axis=0)
    return jnp.tile(cos_t, (1, 2)), jnp.tile(sin_t, (1, 2))


def _attn_layer(geo, x, mods, gains, w_qkv, w_o, sink, w_in, w_out, skip_ctx):
    b, t, tm = geo.batch, geo.t, geo.tm
    q_cols = ATTN_HEADS * HEAD_DIM
    kv_cols = ATTN_KV_HEADS * HEAD_DIM
    wb = w_qkv.astype(bf16)
    cos_t, sin_t = _rope_tables(geo)
    tab_spec = pl.BlockSpec((tm, LANES), lambda bb, tt: (tt, 0))
    q, k, v = pl.pallas_call(
        _attn_proj_kernel,
        grid=(b, geo.nt),
        in_specs=[geo.x_spec(), geo.mod_spec(), _const_spec((8, D_MODEL)), tab_spec, tab_spec,
                  _const_spec((D_MODEL, q_cols)), _const_spec((D_MODEL, kv_cols)), _const_spec((D_MODEL, kv_cols))],
        out_specs=[pl.BlockSpec((1, ATTN_HEADS, tm, HEAD_DIM), lambda bb, tt: (bb, 0, tt, 0)),
                   pl.BlockSpec((1, ATTN_KV_HEADS, tm, HEAD_DIM), lambda bb, tt: (bb, 0, tt, 0)),
                   pl.BlockSpec((1, ATTN_KV_HEADS, tm, HEAD_DIM), lambda bb, tt: (bb, 0, tt, 0))],
        out_shape=[jax.ShapeDtypeStruct((b, ATTN_HEADS, t, HEAD_DIM), bf16),
                   jax.ShapeDtypeStruct((b, ATTN_KV_HEADS, t, HEAD_DIM), bf16),
                   jax.ShapeDtypeStruct((b, ATTN_KV_HEADS, t, HEAD_DIM), bf16)],
        compiler_params=_params(2),
        name="attn_proj",
    )(x, mods, gains, cos_t, sin_t, wb[:, :q_cols], wb[:, q_cols:q_cols + kv_cols], wb[:, q_cols + kv_cols:])

    blk = ATTN_BLOCK
    n_blocks = t // blk

    def kv_spec(off):
        return pl.BlockSpec((1, ATTN_KV_HEADS, blk, HEAD_DIM),
                            lambda bb, qb: (bb, 0, jnp.clip(qb + off, 0, n_blocks - 1), 0))

    ctx_spec = pl.BlockSpec((1, ATTN_KV_HEADS, geo.n_ctx, HEAD_DIM), lambda bb, qb: (bb, 0, 0, 0))
    o = pl.pallas_call(
        functools.partial(_attn_core_kernel, geo),
        grid=(b, n_blocks),
        in_specs=[pl.BlockSpec(memory_space=pltpu.SMEM),
                  pl.BlockSpec((1, ATTN_HEADS, blk, HEAD_DIM), lambda bb, qb: (bb, 0, qb, 0)),
                  kv_spec(-1), kv_spec(0), kv_spec(1), ctx_spec,
                  kv_spec(-1), kv_spec(0), kv_spec(1), ctx_spec],
        out_specs=pl.BlockSpec((1, blk, q_cols), lambda bb, qb: (bb, qb, 0)),
        out_shape=jax.ShapeDtypeStruct((b, t, q_cols), bf16),
        compiler_params=_params(2),
        name="attn_core",
    )(sink, q, k, k, k, k, v, v, v, v)

    t0 = geo.ctx_tiles if skip_ctx else 0
    rows = _post_rows(geo, 1)
    return _post_mlp(geo, x, mods, gains, _pre_attn, [o], [_tile_spec(geo, D_MODEL, 0, t0, rows)],
                     w_o.astype(bf16), w_in, w_out, skip_ctx, "attn_post", rows)


def _gated_chunk(dirs, heads, dk, dv):
    units = []
    for q, k, v, g, st_ref, base, reverse in dirs:
        n = q.shape[0]
        b = _cumsum_rows(g, reverse)
        btot = b[0:1] if reverse else b[n - 1:n]
        q_dec = (q * jnp.exp(b)).astype(bf16)
        k_inv32 = k * jnp.exp(-b)
        k_inv = k_inv32.astype(bf16)
        dec = jnp.exp(btot)
        k_end = (k_inv32 * dec).astype(bf16)
        vb = v.astype(bf16)
        ri = lax.broadcasted_iota(jnp.int32, (n, n), 0)
        ci = lax.broadcasted_iota(jnp.int32, (n, n), 1)
        tri = (ci >= ri) if reverse else (ci <= ri)
        for h in range(heads):
            ks = slice(h * dk, (h + 1) * dk)
            vs = slice(h * dv, (h + 1) * dv)
            units.append(dict(q=q_dec[:, ks], ki=k_inv[:, ks], ke=k_end[:, ks], v=vb[:, vs], dec=dec[:, ks],
                              tri=tri, st=st_ref[base + h], ref=st_ref, h=base + h))
    for u in units:
        a = lax.dot_general(u["q"], u["ki"], NT, preferred_element_type=f32)
        u["a"] = jnp.where(u["tri"], a, 0.0).astype(bf16)
        u["qs"] = lax.dot_general(u["q"], u["st"].astype(bf16), NT, preferred_element_type=f32)
    for u in units:
        u["o"] = _dot(u["a"], u["v"]) + u["qs"]
        u["new"] = u["st"] * u["dec"] + lax.dot_general(u["v"], u["ke"], TN, preferred_element_type=f32)
    for u in units:
        u["ref"][u["h"]] = u["new"]
    return [jnp.concatenate([u["o"] for u in units[d * heads:(d + 1) * heads]], axis=1) for d in range(len(dirs))]


def _gla_gate(zd, wu_ref, bias_ref, d):
    zg = _dot(zd.astype(bf16), wu_ref[d]) + bias_ref[d:d + 1]
    return (jnp.minimum(zg, 0.0) - jnp.log(1.0 + jnp.exp(-jnp.abs(zg)))) * (1.0 / GLA_TAU)


def _gla_scan_kernel(qkf_ref, vf_ref, zdf_ref, qkb_ref, vb_ref, zdb_ref, wu_ref, bias_ref,
                     of_ref, ob_ref, sf_ref, sb_ref):
    @pl.when(pl.program_id(1) == 0)
    def _():
        sf_ref[...] = jnp.zeros_like(sf_ref)
        sb_ref[...] = jnp.zeros_like(sb_ref)

    dirs, dests = [], []
    for row in range(qkf_ref.shape[0]):
        for d, (qk_ref, v_ref, zd_ref, s_ref, o_ref) in enumerate(
                ((qkf_ref, vf_ref, zdf_ref, sf_ref, of_ref), (qkb_ref, vb_ref, zdb_ref, sb_ref, ob_ref))):
            qk = qk_ref[row]
            q = qk[:, :GLA_KEY_DIM] * (GLA_DK ** -0.5)
            k = qk[:, GLA_KEY_DIM:]
            dirs.append((q, k, v_ref[row], _gla_gate(zd_ref[row], wu_ref, bias_ref, d), s_ref, row * GLA_HEADS,
                         d == 1))
            dests.append((o_ref, row))
    for o, (o_ref, row) in zip(_gated_chunk(dirs, GLA_HEADS, GLA_DK, GLA_DV), dests):
        o_ref[row] = o


def _hgrn_scan_kernel(qf_ref, if_ref, zff_ref, qb_ref, ib_ref, zfb_ref, lb_ref, of_ref, ob_ref, sf_ref, sb_ref):
    @pl.when(pl.program_id(1) == 0)
    def _():
        sf_ref[...] = jnp.zeros_like(sf_ref)
        sb_ref[...] = jnp.zeros_like(sb_ref)

    lb = lb_ref[...]
    dirs, dests = [], []
    for row in range(qf_ref.shape[0]):
        for d, (q_ref, i_ref, zf_ref, s_ref, o_ref) in enumerate(
                ((qf_ref, if_ref, zff_ref, sf_ref, of_ref), (qb_ref, ib_ref, zfb_ref, sb_ref, ob_ref))):
            f = lb + (1.0 - lb) * _sigmoid(zf_ref[row])
            dirs.append((_silu(q_ref[row]), 1.0 - f, i_ref[row], jnp.log(f), s_ref, row * HGRN_HEADS, d == 1))
            dests.append((o_ref, row))
    for o, (o_ref, row) in zip(_gated_chunk(dirs, HGRN_HEADS, HGRN_EXPAND, HGRN_EXPAND), dests):
        o_ref[row] = o


SCAN_ROWS_PER_STEP = 2


def _scan_rows(geo):
    return SCAN_ROWS_PER_STEP if geo.batch % SCAN_ROWS_PER_STEP == 0 else 1


def _chunk_spec(geo, width, col, reverse):
    rows = _scan_rows(geo)
    if reverse:
        return pl.BlockSpec((rows, SCAN_CHUNK, width), lambda b, i: (b, geo.bwd_chunk(i), col))
    return pl.BlockSpec((rows, SCAN_CHUNK, width), lambda b, i: (b, i, col))


def _scan_call(geo, body, args, specs, heads, dk, dv, name):
    out_sds = jax.ShapeDtypeStruct((geo.batch, geo.t, heads * dv), f32)
    rows = _scan_rows(geo)
    state = pltpu.VMEM((rows * heads, dv, dk), f32)
    return pl.pallas_call(
        body,
        grid=(geo.batch // rows, geo.nc),
        in_specs=specs,
        out_specs=[_chunk_spec(geo, heads * dv, 0, False), _chunk_spec(geo, heads * dv, 0, True)],
        out_shape=[out_sds, out_sds],
        scratch_shapes=[state, state],
        compiler_params=_params(2),
        name=name,
    )(*args)


def _gla_layer(geo, x, mods, gains, w_in_p, w_gd, w_gu, g_bias, g_norm, w_o, w_in, w_out, skip_ctx):
    r = GLA_GATE_RANK
    n_z = 2 * GLA_KEY_DIM + 2 * D_MODEL
    w_all = jnp.concatenate([w_in_p, w_gd[0], w_gd[1], jnp.zeros((D_MODEL, LANES - 2 * r), f32)], axis=1).astype(bf16)
    z = _mod_linear(geo, x, mods, gains, w_all, "gla_proj")
    wu = jnp.zeros((2, LANES, GLA_KEY_DIM), f32).at[0, :r].set(w_gu[0]).at[1, r:2 * r].set(w_gu[1]).astype(bf16)
    zd_col = n_z // LANES
    specs = []
    for rev in (False, True):
        specs += [_chunk_spec(geo, 2 * GLA_KEY_DIM, 0, rev), _chunk_spec(geo, D_MODEL, 1, rev),
                  _chunk_spec(geo, LANES, zd_col, rev)]
    specs += [_const_spec((2, LANES, GLA_KEY_DIM)), _const_spec((2, GLA_KEY_DIM))]
    o_f, o_b = _scan_call(geo, _gla_scan_kernel, [z, z, z, z, z, z, wu, g_bias], specs,
                          GLA_HEADS, GLA_DK, GLA_DV, "gla_scan")
    t0 = geo.ctx_tiles if skip_ctx else 0
    rows = _post_rows(geo, 3)
    mix_specs = [_tile_spec(geo, D_MODEL, 0, t0, rows), _tile_spec(geo, D_MODEL, 0, t0, rows),
                 _tile_spec(geo, D_MODEL, 2, t0, rows), _const_spec((1, GLA_DV))]
    return _post_mlp(geo, x, mods, gains, functools.partial(_pre_scan, GLA_DV),
                     [o_f, o_b, z, g_norm.reshape(1, GLA_DV)], mix_specs,
                     w_o.astype(bf16), w_in, w_out, skip_ctx, "gla_post", rows)


def _hgrn_layer(geo, x, mods, gains, w_in_p, w_f, lower_bound, g_norm, w_o, w_in, w_out, skip_ctx):
    w_all = jnp.concatenate([w_in_p, w_f[0], w_f[1]], axis=1).astype(bf16)
    z = _mod_linear(geo, x, mods, gains, w_all, "hgrn_proj")
    specs = []
    for rev in (False, True):
        specs += [_chunk_spec(geo, D_MODEL, 0, rev), _chunk_spec(geo, D_MODEL, 1, rev),
                  _chunk_spec(geo, D_MODEL, 4 if rev else 3, rev)]
    specs += [_const_spec((1, D_MODEL))]
    o_f, o_b = _scan_call(geo, _hgrn_scan_kernel, [z, z, z, z, z, z, lower_bound.reshape(1, D_MODEL)], specs,
                          HGRN_HEADS, HGRN_EXPAND, HGRN_EXPAND, "hgrn_scan")
    t0 = geo.ctx_tiles if skip_ctx else 0
    rows = _post_rows(geo, 3)
    mix_specs = [_tile_spec(geo, D_MODEL, 0, t0, rows), _tile_spec(geo, D_MODEL, 0, t0, rows),
                 _tile_spec(geo, D_MODEL, 2, t0, rows), _const_spec((1, HGRN_EXPAND))]
    return _post_mlp(geo, x, mods, gains, functools.partial(_pre_scan, HGRN_EXPAND),
                     [o_f, o_b, z, g_norm.reshape(1, HGRN_EXPAND)], mix_specs,
                     w_o.astype(bf16), w_in, w_out, skip_ctx, "hgrn_post", rows)


def _rwkv_proj_kernel(geo, x_ref, xp_ref, xn_ref, m_ref, g_ref, mix_ref, vec_ref, wrkv_ref, wdn_ref, wup_ref,
                      aup_ref, gup_ref, r_ref, v_ref, gate_ref, kk_ref, lw0_ref, lw1_ref, a0_ref, a1_ref,
                      k0_ref, k1_ref, bg_ref):
    t = pl.program_id(1)
    tm = geo.tm
    gain, shift, scale = g_ref[0:1], m_ref[0, 0, 0:1], m_ref[0, 0, 1:2]
    h = _modulate(x_ref[0], gain, shift, scale)
    seg_first = (t == 0) | (t == geo.ctx_tiles)
    seg_last = (t == geo.ctx_tiles - 1) | (t == geo.nt - 1)
    h_prev = jnp.where(seg_first, 0.0, _modulate(xp_ref[0], gain, shift, scale)[7:8])
    h_next = jnp.where(seg_last, 0.0, _modulate(xn_ref[0], gain, shift, scale)[0:1])
    row = lax.broadcasted_iota(jnp.int32, h.shape, 0)
    up = jnp.where(row == 0, h_prev, pltpu.roll(h, 1, axis=0))
    dn = jnp.where(row == tm - 1, h_next, pltpu.roll(h, tm - 1, axis=0))
    dx = 0.5 * (up + dn) - h

    def mixed(n):
        return (h + dx * mix_ref[n:n + 1]).astype(bf16)

    r = _dot(mixed(0), wrkv_ref[0])
    k = _dot(mixed(1), wrkv_ref[1])
    v = _dot(mixed(2), wrkv_ref[2])
    dw = jnp.tanh(_dot(mixed(3), wdn_ref[:, 0:LANES])).astype(bf16)
    da = _dot(mixed(4), wdn_ref[:, LANES:2 * LANES]).astype(bf16)
    dg = _sigmoid(_dot(mixed(5), wdn_ref[:, 2 * LANES:3 * LANES])).astype(bf16)
    gate = _dot(dg, gup_ref[...])
    r_ref[0] = r
    v_ref[0] = v
    gate_ref[0] = gate
    kk = k * vec_ref[4:5]
    kk_ref[0] = kk * lax.rsqrt(jnp.maximum(_seg_sum(kk * kk, RWKV_HEAD_SIZE), L2_EPS * L2_EPS))
    kds = []
    for d, (lw_ref, a_ref, kd_ref) in enumerate(((lw0_ref, a0_ref, k0_ref), (lw1_ref, a1_ref, k1_ref))):
        lw_ref[0] = -RWKV_DECAY_SCALE * _sigmoid(vec_ref[d:d + 1] + _dot(dw, wup_ref[d]))
        a = _sigmoid(vec_ref[2 + d:3 + d] + _dot(da, aup_ref[d]))
        a_ref[0] = a
        kds.append(k * (1.0 + (a - 1.0) * vec_ref[5:6]))
        kd_ref[0] = kds[d]
    bg_ref[0] = _seg_sum(r * (0.5 * (kds[0] + kds[1])) * vec_ref[6:7], RWKV_HEAD_SIZE) * v * gate


def _rwkv_chunk_units(units):
    n = SCAN_CHUNK
    hs = RWKV_HEAD_SIZE
    lane = lax.broadcasted_iota(jnp.int32, (n, RWKV_PAIR), 1)
    t_i = lax.broadcasted_iota(jnp.int32, (n, RWKV_PAIR), 0)
    s_i = lane % hs
    head0 = lane < hs
    ri = lax.broadcasted_iota(jnp.int32, (RWKV_PAIR, RWKV_PAIR), 0)
    ci = lax.broadcasted_iota(jnp.int32, (RWKV_PAIR, RWKV_PAIR), 1)
    same_head = ri // hs == ci // hs
    top_rows = ri < hs

    def block_diag(x):
        return jnp.concatenate([jnp.where(head0, x, 0.0), jnp.where(head0, 0.0, x)], axis=0)

    def cross_blocks(b):
        return (t_i // (2 * b) == s_i // (2 * b)) & (t_i // b != s_i // b)

    st = []
    for r, kd, v, kk, a, lw, ht, reverse in units:
        c = _cumsum_rows(lw, reverse)
        tot = c[0:1] if reverse else c[n - 1:n]
        e_neg = jnp.exp(-c)
        dec = jnp.exp(tot)
        kb = jnp.concatenate([kd * e_neg, kk * a * e_neg], axis=0)
        kb_t = kb.T
        kb_sw = pltpu.roll(kb_t, hs, axis=1)
        rhs = jnp.concatenate([jnp.where(same_head, jnp.where(top_rows, kb_t, kb_sw), 0.0),
                               jnp.where(same_head, jnp.where(top_rows, kb_sw, kb_t), 0.0)], axis=1)
        st.append(dict(
            strict=(s_i > t_i) if reverse else (s_i < t_i), incl=(s_i >= t_i) if reverse else (s_i <= t_i),
            a_bar=(-kk * jnp.exp(c - lw)).astype(bf16), r_bar=(r * jnp.exp(c)).astype(bf16), rhs=rhs.astype(bf16),
            v_bd=block_diag(v).astype(bf16), h_t=ht.T.astype(bf16),
            ends=(kb * dec).astype(bf16),
            v=v, ht=ht, dec=dec))
    for s in st:
        g = _dot(jnp.concatenate([s["a_bar"], s["r_bar"]], axis=0), s["rhs"])
        s["a_ak"] = jnp.where(s["strict"], g[0:n, 0:RWKV_PAIR], 0.0).astype(bf16)
        s["nmat"] = jnp.where(s["strict"], g[0:n, RWKV_PAIR:], 0.0)
        s["q_k"] = jnp.where(s["incl"], g[n:, 0:RWKV_PAIR], 0.0).astype(bf16)
        s["q_b"] = jnp.where(s["incl"], g[n:, RWKV_PAIR:], 0.0).astype(bf16)
    for s in st:
        s["inv"] = jnp.where(s_i == t_i, 1.0, jnp.where(cross_blocks(1), s["nmat"], 0.0))
    b = 2
    while 2 * b < n:
        for s in st:
            s["pc"] = _dot(s["inv"].astype(bf16), block_diag(jnp.where(cross_blocks(b), s["nmat"], 0.0)).astype(bf16))
        for s in st:
            s["inv"] = s["inv"] + _dot(s["pc"].astype(bf16), block_diag(s["inv"]).astype(bf16))
        b *= 2
    for s in st:
        x = _dot(jnp.concatenate([s["a_bar"], s["a_ak"]], axis=1), jnp.concatenate([s["h_t"], s["v_bd"]], axis=0))
        s["inv"] = s["inv"].astype(bf16)
        s["w"] = _dot(s["inv"], block_diag(x).astype(bf16))
    for s in st:
        s["cw"] = _dot(jnp.where(cross_blocks(n // 2), s["nmat"], 0.0).astype(bf16), block_diag(s["w"]).astype(bf16))
    for s in st:
        s["u"] = s["w"] + _dot(s["inv"], block_diag(s["cw"]).astype(bf16))
    for s in st:
        s["vu_t"] = jnp.concatenate([s["v"], s["u"]], axis=0).T.astype(bf16)
    for s in st:
        s["y"] = _dot(jnp.concatenate([s["r_bar"], s["q_k"], s["q_b"]], axis=1),
                      jnp.concatenate([s["h_t"], s["v_bd"], block_diag(s["u"]).astype(bf16)], axis=0))
    for s in st:
        s["upd"] = _dot(s["vu_t"], s["ends"])
    return [(s["y"], s["ht"] * s["dec"] + jnp.where(same_head, s["upd"], 0.0)) for s in st]


def _rwkv_scan_kernel(rows, pairs, *refs):
    fwd, bwd = refs[0:6], refs[6:12]
    yf_ref, yb_ref, hf_ref, hb_ref = refs[12:]

    @pl.when(pl.program_id(2) == 0)
    def _():
        hf_ref[...] = jnp.zeros_like(hf_ref)
        hb_ref[...] = jnp.zeros_like(hb_ref)

    units, dests = [], []
    for ins, y_ref, h_ref, reverse in ((fwd, yf_ref, hf_ref, False), (bwd, yb_ref, hb_ref, True)):
        r_ref, v_ref, kk_ref, lw_ref, a_ref, kd_ref = ins
        for row in range(rows):
            for p in range(pairs):
                sl = slice(p * RWKV_PAIR, (p + 1) * RWKV_PAIR)
                slot = row * pairs + p
                units.append((r_ref[row, :, sl], kd_ref[row, :, sl], v_ref[row, :, sl], kk_ref[row, :, sl],
                              a_ref[row, :, sl], lw_ref[row, :, sl], h_ref[slot], reverse))
                dests.append((y_ref, h_ref, row, slot, sl))
    for (y, h_new), (y_ref, h_ref, row, slot, sl) in zip(_rwkv_chunk_units(units), dests):
        y_ref[row, :, sl] = y
        h_ref[slot] = h_new


RWKV_PAIRS_PER_STEP = 8
RWKV_ROWS_PER_STEP = 2


def _rwkv_layer(geo, x, mods, gains, mix, w_rkv, w0, w_down, w_up, a0, a_down, a_up, g_down, g_up, k_k, k_a,
                r_k, ln_w, ln_b, w_o, w_in, w_out, skip_ctx):
    b, t, tm = geo.batch, geo.t, geo.tm
    d = D_MODEL
    rank = w_down.shape[-1]
    assert 2 * rank == LANES and a_down.shape[-1] == rank and g_down.shape[-1] == LANES
    mix8 = jnp.pad(mix, ((0, 2), (0, 0)))
    vec = jnp.stack([w0[0], w0[1], a0[0], a0[1], k_k, k_a, r_k.reshape(d), jnp.zeros_like(k_k)])
    w_dn = jnp.concatenate([w_down[0], w_down[1], a_down[0], a_down[1], g_down], axis=1).astype(bf16)

    def padded_up(w):
        return jnp.zeros((2, LANES, d), f32).at[0, :rank].set(w[0]).at[1, rank:].set(w[1]).astype(bf16)

    halo = 8
    n_halo = t // halo
    per = tm // halo
    x_prev = pl.BlockSpec((1, halo, d), lambda bb, tt: (bb, jnp.maximum(tt * per - 1, 0), 0))
    x_next = pl.BlockSpec((1, halo, d), lambda bb, tt: (bb, jnp.minimum((tt + 1) * per, n_halo - 1), 0))
    out_spec = pl.BlockSpec((1, tm, d), lambda bb, tt: (bb, tt, 0))
    sds = jax.ShapeDtypeStruct((b, t, d), f32)
    r, v, gate, kk, lw0, lw1, a_0, a_1, k0, k1, bonus_g = pl.pallas_call(
        functools.partial(_rwkv_proj_kernel, geo),
        grid=(b, geo.nt),
        in_specs=[geo.x_spec(), x_prev, x_next, geo.mod_spec(), _const_spec((8, d)), _const_spec((8, d)),
                  _const_spec((8, d)), _const_spec((3, d, d)), _const_spec((d, 3 * LANES)),
                  _const_spec((2, LANES, d)), _const_spec((2, LANES, d)), _const_spec((LANES, d))],
        out_specs=[out_spec] * 11,
        out_shape=[sds] * 11,
        compiler_params=_params(2),
        name="rwkv_proj",
    )(x, x, x, mods, gains, mix8, vec, w_rkv.astype(bf16), w_dn, padded_up(w_up), padded_up(a_up),
      g_up.astype(bf16))

    pairs = RWKV_PAIRS_PER_STEP
    rows = RWKV_ROWS_PER_STEP if b % RWKV_ROWS_PER_STEP == 0 else 1
    width = pairs * RWKV_PAIR
    groups = d // width

    def cspec(reverse):
        if reverse:
            return pl.BlockSpec((rows, SCAN_CHUNK, width), lambda bb, gg, i: (bb, geo.bwd_chunk(i), gg))
        return pl.BlockSpec((rows, SCAN_CHUNK, width), lambda bb, gg, i: (bb, i, gg))

    state = pltpu.VMEM((rows * pairs, RWKV_PAIR, RWKV_PAIR), f32)
    y_f, y_b = pl.pallas_call(
        functools.partial(_rwkv_scan_kernel, rows, pairs),
        grid=(b // rows, groups, geo.nc),
        in_specs=[cspec(False)] * 6 + [cspec(True)] * 6,
        out_specs=[cspec(False), cspec(True)],
        out_shape=[sds, sds],
        scratch_shapes=[state, state],
        compiler_params=_params(3),
        name="rwkv_scan",
    )(r, v, kk, lw0, a_0, k0, r, v, kk, lw1, a_1, k1)

    t0 = geo.ctx_tiles if skip_ctx else 0
    post_vec = jnp.stack([ln_w, ln_b] + [jnp.zeros_like(ln_w)] * 6)
    rows = _post_rows(geo, 4)
    mix_specs = [_tile_spec(geo, d, 0, t0, rows)] * 4 + [_const_spec((8, d))]
    return _post_mlp(geo, x, mods, gains, _pre_rwkv, [y_f, y_b, gate, bonus_g, post_vec], mix_specs,
                     w_o.astype(bf16), w_in, w_out, skip_ctx, "rwkv_post", rows)


def _hgrn_lower_bound(lb_param, layer):
    p = jax.nn.softmax(lb_param.astype(f32), axis=0)
    return (jnp.cumsum(p, axis=0) - p[0])[layer]


def kernel(x, c, ctx, c_ctx, w_mod, b_mod, g_pre_mix, g_post_mix, g_pre_mlp, g_post_mlp, w_mlp_in, w_mlp_out, attn_w_qkv, attn_w_o, attn_sink, gla_w_in, gla_w_gate_down, gla_w_gate_up, gla_gate_bias, gla_g_norm, gla_w_o, rwkv_mix, rwkv_w_rkv, rwkv_w0, rwkv_w_down, rwkv_w_up, rwkv_a0, rwkv_a_down, rwkv_a_up, rwkv_g_down, rwkv_g_up, rwkv_k_k, rwkv_k_a, rwkv_r_k, rwkv_ln_w, rwkv_ln_b, rwkv_w_o, hgrn_w_in, hgrn_w_f, hgrn_lb, hgrn_g_norm, hgrn_w_o):
    depth = w_mod.shape[0]
    geo = _Geom(x.shape[0], ctx.shape[1], x.shape[1])
    mods_all = _mod_vectors(c, c_ctx, w_mod, b_mod)
    xs = jnp.concatenate([ctx, x], axis=1)
    for i in range(depth):
        kind, j = i % 4, i // 4
        skip_ctx = i == depth - 1
        mods = mods_all[i]
        gains = jnp.stack([g_pre_mix[i], g_post_mix[i], g_pre_mlp[i], g_post_mlp[i]] + [jnp.zeros_like(g_pre_mix[i])] * 4)
        w_in, w_out = w_mlp_in[i].astype(bf16), w_mlp_out[i].astype(bf16)
        if kind == 0:
            xs = _attn_layer(geo, xs, mods, gains, attn_w_qkv[j], attn_w_o[j], attn_sink[j], w_in, w_out, skip_ctx)
        elif kind == 1:
            xs = _gla_layer(geo, xs, mods, gains, gla_w_in[j], gla_w_gate_down[j], gla_w_gate_up[j],
                            gla_gate_bias[j], gla_g_norm[j], gla_w_o[j], w_in, w_out, skip_ctx)
        elif kind == 2:
            xs = _rwkv_layer(geo, xs, mods, gains, rwkv_mix[j], rwkv_w_rkv[j], rwkv_w0[j], rwkv_w_down[j],
                             rwkv_w_up[j], rwkv_a0[j], rwkv_a_down[j], rwkv_a_up[j], rwkv_g_down[j], rwkv_g_up[j],
                             rwkv_k_k[j], rwkv_k_a[j], rwkv_r_k[j], rwkv_ln_w[j], rwkv_ln_b[j], rwkv_w_o[j],
                             w_in, w_out, skip_ctx)
        else:
            xs = _hgrn_layer(geo, xs, mods, gains, hgrn_w_in[j], hgrn_w_f[j], _hgrn_lower_bound(hgrn_lb, i),
                             hgrn_g_norm[j], hgrn_w_o[j], w_in, w_out, skip_ctx)
        if skip_ctx:
            return xs
    return xs[:, geo.n_ctx:]
```

```python
import functools

import jax
import jax.numpy as jnp
from jax import lax
from jax.experimental import pallas as pl
from jax.experimental.pallas import tpu as pltpu

f32 = jnp.float32
bf16 = jnp.bfloat16

D_MODEL = 1024
N_MOD = 6
MLP_HIDDEN = 4 * D_MODEL
NORM_EPS = 1e-6
NEG_INF = -1e30
GRID_W = 64

ATTN_HEADS = 16
ATTN_KV_HEADS = 4
ATTN_GROUP = ATTN_HEADS // ATTN_KV_HEADS
HEAD_DIM = 64
WINDOW = 128
ATTN_BLOCK = 128
ROPE_BASE = 10000.0
ROPE_AXIS_DIM = HEAD_DIM // 2
ROPE_FREQS = ROPE_AXIS_DIM // 2

GLA_HEADS = 4
GLA_KEY_DIM = D_MODEL // 2
GLA_DK = GLA_KEY_DIM // GLA_HEADS
GLA_DV = D_MODEL // GLA_HEADS
GLA_GATE_RANK = 16
GLA_TAU = 16.0
SCAN_CHUNK = 64

RWKV_HEAD_SIZE = 64
RWKV_LN_EPS = 64e-5
L2_EPS = 1e-12
RWKV_PAIR = 2 * RWKV_HEAD_SIZE
RWKV_DECAY_SCALE = 0.6065306597126334

HGRN_EXPAND = 128
HGRN_HEADS = D_MODEL // HGRN_EXPAND

LANES = 128
MOD_ROWS = 8
VMEM_LIMIT = 56 * 1024 * 1024

NT = (((1,), (1,)), ((), ()))
TN = (((0,), (0,)), ((), ()))


def _params(n_grid):
    return pltpu.CompilerParams(dimension_semantics=("arbitrary",) * n_grid, vmem_limit_bytes=VMEM_LIMIT)


def _const_spec(shape):
    nd = len(shape)
    return pl.BlockSpec(shape, lambda *_: (0,) * nd, pipeline_mode=pl.Buffered(1))


def _dot(a, b):
    return jnp.dot(a, b, preferred_element_type=f32)


def _sigmoid(x):
    return 0.5 * jnp.tanh(0.5 * x) + 0.5


def _silu(x):
    return x * _sigmoid(x)


def _rms(x, gain):
    return x * lax.rsqrt(jnp.mean(x * x, axis=-1, keepdims=True) + NORM_EPS) * gain


def _modulate(x, gain, shift, scale):
    return _rms(x, gain) * (1.0 + scale) + shift


def _seg_sum(x, seg):
    r = lax.broadcasted_iota(jnp.int32, (LANES, LANES), 0) // seg
    c = lax.broadcasted_iota(jnp.int32, (LANES, LANES), 1) // seg
    ones_bd = (r == c).astype(bf16)
    hi = x.astype(bf16)
    lo = (x - hi.astype(f32)).astype(bf16)
    outs = []
    for j in range(x.shape[1] // LANES):
        sl = slice(j * LANES, (j + 1) * LANES)
        outs.append(_dot(hi[:, sl], ones_bd) + _dot(lo[:, sl], ones_bd))
    return jnp.concatenate(outs, axis=1)


def _cumsum_rows(g, reverse):
    n = g.shape[0]
    ri = lax.broadcasted_iota(jnp.int32, (n, n), 0)
    ci = lax.broadcasted_iota(jnp.int32, (n, n), 1)
    tri = ((ci >= ri) if reverse else (ci <= ri)).astype(bf16)
    hi = g.astype(bf16)
    lo = (g - hi.astype(f32)).astype(bf16)
    return _dot(tri, hi) + _dot(tri, lo)


def _mod_kernel(c_ref, w_ref, b_ref, o_ref):
    o_ref[0] = _dot(_silu(c_ref[...]).astype(bf16), w_ref[0]) + b_ref[0]


def _mod_vectors(c, c_ctx, w_mod, b_mod):
    depth, d, _ = w_mod.shape
    batch = c.shape[0]
    rows = -(-(batch + 1) // 8) * 8
    cc = jnp.zeros((rows, d), f32).at[:batch].set(c).at[batch].set(c_ctx)
    out = pl.pallas_call(
        _mod_kernel,
        grid=(depth, N_MOD),
        in_specs=[
            pl.BlockSpec((rows, d), lambda i, j: (0, 0)),
            pl.BlockSpec((1, d, d), lambda i, j: (i, 0, j)),
            pl.BlockSpec((1, 1, d), lambda i, j: (i, 0, j)),
        ],
        out_specs=pl.BlockSpec((1, rows, d), lambda i, j: (i, 0, j)),
        out_shape=jax.ShapeDtypeStruct((depth, rows, N_MOD * d), f32),
        compiler_params=_params(2),
        name="mod_vectors",
    )(cc, w_mod.astype(bf16), b_mod.reshape(depth, 1, N_MOD * d))
    lat = out[:, :batch].reshape(depth, batch, 1, N_MOD, d)
    con = jnp.broadcast_to(out[:, batch].reshape(depth, 1, 1, N_MOD, d), lat.shape)
    mods = jnp.concatenate([con, lat], axis=2)
    return jnp.pad(mods, ((0, 0), (0, 0), (0, 0), (0, MOD_ROWS - N_MOD), (0, 0)))


class _Geom:
    def __init__(self, batch, n_ctx, n_lat):
        self.batch, self.n_ctx, self.n_lat = batch, n_ctx, n_lat
        self.t = n_ctx + n_lat
        self.tm = min(256, n_ctx)
        assert n_ctx % self.tm == 0 and n_lat % self.tm == 0
        assert n_ctx % ATTN_BLOCK == 0 and n_lat % ATTN_BLOCK == 0 and n_lat % GRID_W == 0
        self.nt = self.t // self.tm
        self.ctx_tiles = n_ctx // self.tm
        self.nc = self.t // SCAN_CHUNK
        self.ctx_chunks = n_ctx // SCAN_CHUNK

    def x_spec(self, t0=0, rows=1):
        return pl.BlockSpec((rows, self.tm, D_MODEL), lambda b, t: (b, t + t0, 0))

    def mod_spec(self, t0=0, rows=1):
        ct = self.ctx_tiles
        return pl.BlockSpec((rows, 1, MOD_ROWS, D_MODEL), lambda b, t: (b, ((t + t0) >= ct).astype(jnp.int32), 0, 0))

    def bwd_chunk(self, i):
        cc = self.ctx_chunks
        return jnp.where(i < cc, cc - 1 - i, self.nc - 1 + cc - i)


LIN_ROWS_PER_STEP = 2


def _lin_kernel(x_ref, m_ref, g_ref, w_ref, o_ref):
    rows, tm = x_ref.shape[0], x_ref.shape[1]
    h = [_modulate(x_ref[r], g_ref[0:1], m_ref[r, 0, 0:1], m_ref[r, 0, 1:2]).astype(bf16) for r in range(rows)]
    z = _dot(jnp.concatenate(h, axis=0), w_ref[...])
    for r in range(rows):
        o_ref[r] = z[r * tm:(r + 1) * tm]


def _mod_linear(geo, x, mods, gains, w, name):
    n = w.shape[1]
    rows = LIN_ROWS_PER_STEP if geo.batch % LIN_ROWS_PER_STEP == 0 else 1
    return pl.pallas_call(
        _lin_kernel,
        grid=(geo.batch // rows, geo.nt),
        in_specs=[geo.x_spec(0, rows), geo.mod_spec(0, rows), _const_spec((8, D_MODEL)), _const_spec((D_MODEL, n))],
        out_specs=pl.BlockSpec((rows, geo.tm, n), lambda b, t: (b, t, 0)),
        out_shape=jax.ShapeDtypeStruct((geo.batch, geo.t, n), f32),
        compiler_params=_params(2),
        name=name,
    )(x, mods, gains, w)


def _head_rms(o, gain, width):
    outs = []
    for h in range(o.shape[1] // width):
        oh = o[:, h * width:(h + 1) * width]
        outs.append(oh * lax.rsqrt(jnp.mean(oh * oh, axis=-1, keepdims=True) + NORM_EPS) * gain)
    return jnp.concatenate(outs, axis=1)


def _pre_attn(refs, row):
    (o_ref,) = refs
    return o_ref[row]


def _pre_scan(width, refs, row):
    of_ref, ob_ref, gate_ref, gn_ref = refs
    o = _head_rms(of_ref[row] + ob_ref[row], gn_ref[...], width)
    return (o * _silu(gate_ref[row])).astype(bf16)


def _pre_rwkv(refs, row):
    yf_ref, yb_ref, g_ref, bg_ref, vec_ref = refs
    y = yf_ref[row] + yb_ref[row]
    inv_n = 1.0 / RWKV_HEAD_SIZE
    mu = _seg_sum(y, RWKV_HEAD_SIZE) * inv_n
    dlt = y - mu
    var = _seg_sum(dlt * dlt, RWKV_HEAD_SIZE) * inv_n
    yn = dlt * lax.rsqrt(var + RWKV_LN_EPS) * vec_ref[0:1] + vec_ref[1:2]
    return (yn * g_ref[row] + bg_ref[row]).astype(bf16)


def _post_kernel(pre, n_mix, x_ref, m_ref, g_ref, *rest):
    mix_refs = rest[:n_mix]
    wo_ref, win_ref, wout_ref, out_ref = rest[n_mix:]
    rows = range(x_ref.shape[0])
    o = [pre(mix_refs, r) for r in rows]
    y = [_dot(o[r], wo_ref[...]) for r in rows]
    x1 = [x_ref[r] + m_ref[r, 0, 2:3] * _rms(y[r], g_ref[1:2]) for r in rows]
    h2 = [_modulate(x1[r], g_ref[2:3], m_ref[r, 0, 3:4], m_ref[r, 0, 4:5]).astype(bf16) for r in rows]
    acc = [None for _ in rows]
    for c in range(MLP_HIDDEN // D_MODEL):
        cols = slice(c * D_MODEL, (c + 1) * D_MODEL)
        u = [jnp.square(jnp.maximum(_dot(h2[r], win_ref[:, cols]), 0.0)).astype(bf16) for r in rows]
        for r in rows:
            part = _dot(u[r], wout_ref[cols, :])
            acc[r] = part if c == 0 else acc[r] + part
    for r in rows:
        out_ref[r] = x1[r] + m_ref[r, 0, 5:6] * _rms(acc[r], g_ref[3:4])


POST_ROWS_PER_STEP = 2


def _post_rows(geo, n_tile_inputs):
    tile = geo.tm * D_MODEL * 4
    weights = 2 * (D_MODEL * D_MODEL + 2 * D_MODEL * MLP_HIDDEN)
    for rows in (POST_ROWS_PER_STEP, 1):
        streamed = 2 * (n_tile_inputs + 2) * rows * tile
        live = 6 * rows * tile
        if geo.batch % rows == 0 and weights + streamed + live <= VMEM_LIMIT:
            return rows
    return 1


def _post_mlp(geo, x, mods, gains, pre, mix_args, mix_specs, w_o, w_in, w_out, skip_ctx, name, rows):
    t0 = geo.ctx_tiles if skip_ctx else 0
    return pl.pallas_call(
        functools.partial(_post_kernel, pre, len(mix_args)),
        grid=(geo.batch // rows, geo.nt - t0),
        in_specs=[geo.x_spec(t0, rows), geo.mod_spec(t0, rows), _const_spec((8, D_MODEL))] + mix_specs + [
            _const_spec((D_MODEL, D_MODEL)), _const_spec((D_MODEL, MLP_HIDDEN)), _const_spec((MLP_HIDDEN, D_MODEL))],
        out_specs=pl.BlockSpec((rows, geo.tm, D_MODEL), lambda b, t: (b, t, 0)),
        out_shape=jax.ShapeDtypeStruct((geo.batch, geo.t - t0 * geo.tm, D_MODEL), f32),
        compiler_params=_params(2),
        name=name,
    )(x, mods, gains, *mix_args, w_o, w_in, w_out)


def _tile_spec(geo, width, col, t0, rows):
    return pl.BlockSpec((rows, geo.tm, width), lambda b, t: (b, t + t0, col))


def _rope(x, cos, sin):
    w = x.shape[1]
    reps = w // LANES
    cw = jnp.concatenate([cos] * reps, axis=1)
    sw = jnp.concatenate([sin] * reps, axis=1)
    lane = lax.broadcasted_iota(jnp.int32, x.shape, 1)
    first = (lane % ROPE_AXIS_DIM) < ROPE_FREQS
    partner = jnp.where(first, pltpu.roll(x, w - ROPE_FREQS, axis=1), pltpu.roll(x, ROPE_FREQS, axis=1))
    return x * cw + partner * sw


def _attn_proj_kernel(x_ref, m_ref, g_ref, cos_ref, sin_ref, wq_ref, wk_ref, wv_ref, q_ref, k_ref, v_ref):
    h = _modulate(x_ref[0], g_ref[0:1], m_ref[0, 0, 0:1], m_ref[0, 0, 1:2]).astype(bf16)
    cos, sin = cos_ref[...], sin_ref[...]
    q = _rope(_dot(h, wq_ref[...]) * (HEAD_DIM ** -0.5), cos, sin)
    k = _rope(_dot(h, wk_ref[...]), cos, sin)
    v = _dot(h, wv_ref[...])
    for hd in range(ATTN_HEADS):
        q_ref[0, hd] = q[:, hd * HEAD_DIM:(hd + 1) * HEAD_DIM].astype(bf16)
    for hd in range(ATTN_KV_HEADS):
        k_ref[0, hd] = k[:, hd * HEAD_DIM:(hd + 1) * HEAD_DIM].astype(bf16)
        v_ref[0, hd] = v[:, hd * HEAD_DIM:(hd + 1) * HEAD_DIM].astype(bf16)


def _attn_core_kernel(geo, sink_ref, q_ref, kp_ref, kc_ref, kn_ref, kx_ref, vp_ref, vc_ref, vn_ref, vx_ref, o_ref):
    qb = pl.program_id(1)
    blk = ATTN_BLOCK
    n_loc = 3 * blk
    n_keys = n_loc + geo.n_ctx
    rows = ATTN_GROUP * blk
    first_lat = geo.n_ctx // blk
    n_blocks = geo.t // blk
    row = lax.broadcasted_iota(jnp.int32, (blk, n_keys), 0)
    col = lax.broadcasted_iota(jnp.int32, (blk, n_keys), 1)
    kblk = qb - 1 + col // blk
    ok_local = ((jnp.abs(col - blk - row) <= WINDOW) & (qb >= first_lat) & (kblk >= first_lat) & (kblk < n_blocks))
    ok = (col >= n_loc) | ok_local
    scores, vals = [], []
    for j in range(ATTN_KV_HEADS):
        q4 = q_ref[0, ATTN_GROUP * j:ATTN_GROUP * (j + 1)].reshape(rows, HEAD_DIM)
        keys = jnp.concatenate([kp_ref[0, j], kc_ref[0, j], kn_ref[0, j], kx_ref[0, j]], axis=0)
        vals.append(jnp.concatenate([vp_ref[0, j], vc_ref[0, j], vn_ref[0, j], vx_ref[0, j]], axis=0))
        scores.append(lax.dot_general(q4, keys, NT, preferred_element_type=f32))
    heads = [(j, g) for j in range(ATTN_KV_HEADS) for g in range(ATTN_GROUP)]
    sg = [jnp.where(ok, scores[j][g * blk:(g + 1) * blk], NEG_INF) for j, g in heads]
    sk = [sink_ref[ATTN_GROUP * j + g] for j, g in heads]
    m = [jnp.maximum(jnp.max(s, axis=-1, keepdims=True), k) for s, k in zip(sg, sk)]
    e = [jnp.exp(s - mm) for s, mm in zip(sg, m)]
    inv = [1.0 / (jnp.sum(ee, axis=-1, keepdims=True) + jnp.exp(k - mm)) for ee, k, mm in zip(e, sk, m)]
    p = [ee.astype(bf16) for ee in e]
    outs = []
    for j in range(ATTN_KV_HEADS):
        o4 = _dot(jnp.concatenate(p[ATTN_GROUP * j:ATTN_GROUP * (j + 1)], axis=0), vals[j])
        outs += [o4[g * blk:(g + 1) * blk] * inv[ATTN_GROUP * j + g] for g in range(ATTN_GROUP)]
    o_ref[0] = jnp.concatenate(outs, axis=1).astype(bf16)


def _rope_tables(geo):
    inv_freq = ROPE_BASE ** (-jnp.arange(ROPE_FREQS, dtype=f32) * 2.0 / ROPE_AXIS_DIM)
    pos = jnp.arange(geo.n_lat)
    row = (pos // GRID_W).astype(f32)
    col = (pos % GRID_W).astype(f32)
    ang = jnp.stack([row[:, None] * inv_freq, col[:, None] * inv_freq], axis=1)
    cos = jnp.cos(ang)
    sin = jnp.sin(ang)
    cos_h = jnp.concatenate([cos, cos], axis=2).reshape(geo.n_lat, HEAD_DIM)
    sin_h = jnp.concatenate([-sin, sin], axis=2).reshape(geo.n_lat, HEAD_DIM)
    cos_t = jnp.concatenate([jnp.ones((geo.n_ctx, HEAD_DIM), f32), cos_h], axis=0)
    sin_t = jnp.concatenate([jnp.zeros((geo.n_ctx, HEAD_DIM), f32), sin_h], axis=0)
    return jnp.tile(cos_t, (1, 2)), jnp.tile(sin_t, (1, 2))


def _attn_layer(geo, x, mods, gains, w_qkv, w_o, sink, w_in, w_out, skip_ctx):
    b, t, tm = geo.batch, geo.t, geo.tm
    q_cols = ATTN_HEADS * HEAD_DIM
    kv_cols = ATTN_KV_HEADS * HEAD_DIM
    wb = w_qkv.astype(bf16)
    cos_t, sin_t = _rope_tables(geo)
    tab_spec = pl.BlockSpec((tm, LANES), lambda bb, tt: (tt, 0))
    q, k, v = pl.pallas_call(
        _attn_proj_kernel,
        grid=(b, geo.nt),
        in_specs=[geo.x_spec(), geo.mod_spec(), _const_spec((8, D_MODEL)), tab_spec, tab_spec,
                  _const_spec((D_MODEL, q_cols)), _const_spec((D_MODEL, kv_cols)), _const_spec((D_MODEL, kv_cols))],
        out_specs=[pl.BlockSpec((1, ATTN_HEADS, tm, HEAD_DIM), lambda bb, tt: (bb, 0, tt, 0)),
                   pl.BlockSpec((1, ATTN_KV_HEADS, tm, HEAD_DIM), lambda bb, tt: (bb, 0, tt, 0)),
                   pl.BlockSpec((1, ATTN_KV_HEADS, tm, HEAD_DIM), lambda bb, tt: (bb, 0, tt, 0))],
        out_shape=[jax.ShapeDtypeStruct((b, ATTN_HEADS, t, HEAD_DIM), bf16),
                   jax.ShapeDtypeStruct((b, ATTN_KV_HEADS, t, HEAD_DIM), bf16),
                   jax.ShapeDtypeStruct((b, ATTN_KV_HEADS, t, HEAD_DIM), bf16)],
        compiler_params=_params(2),
        name="attn_proj",
    )(x, mods, gains, cos_t, sin_t, wb[:, :q_cols], wb[:, q_cols:q_cols + kv_cols], wb[:, q_cols + kv_cols:])

    blk = ATTN_BLOCK
    n_blocks = t // blk

    def kv_spec(off):
        return pl.BlockSpec((1, ATTN_KV_HEADS, blk, HEAD_DIM),
                            lambda bb, qb: (bb, 0, jnp.clip(qb + off, 0, n_blocks - 1), 0))

    ctx_spec = pl.BlockSpec((1, ATTN_KV_HEADS, geo.n_ctx, HEAD_DIM), lambda bb, qb: (bb, 0, 0, 0))
    o = pl.pallas_call(
        functools.partial(_attn_core_kernel, geo),
        grid=(b, n_blocks),
        in_specs=[pl.BlockSpec(memory_space=pltpu.SMEM),
                  pl.BlockSpec((1, ATTN_HEADS, blk, HEAD_DIM), lambda bb, qb: (bb, 0, qb, 0)),
                  kv_spec(-1), kv_spec(0), kv_spec(1), ctx_spec,
                  kv_spec(-1), kv_spec(0), kv_spec(1), ctx_spec],
        out_specs=pl.BlockSpec((1, blk, q_cols), lambda bb, qb: (bb, qb, 0)),
        out_shape=jax.ShapeDtypeStruct((b, t, q_cols), bf16),
        compiler_params=_params(2),
        name="attn_core",
    )(sink, q, k, k, k, k, v, v, v, v)

    t0 = geo.ctx_tiles if skip_ctx else 0
    rows = _post_rows(geo, 1)
    return _post_mlp(geo, x, mods, gains, _pre_attn, [o], [_tile_spec(geo, D_MODEL, 0, t0, rows)],
                     w_o.astype(bf16), w_in, w_out, skip_ctx, "attn_post", rows)


def _gated_prep(entries, heads, dk, dv):
    sums = [_cumsum_rows(g, reverse) for _, _, _, g, _, _, reverse in entries]
    units = []
    for (q, k, v, g, st_ref, base, reverse), b in zip(entries, sums):
        n = q.shape[0]
        btot = b[0:1] if reverse else b[n - 1:n]
        q_dec = (q * jnp.exp(b)).astype(bf16)
        k_inv32 = k * jnp.exp(-b)
        k_inv = k_inv32.astype(bf16)
        dec = jnp.exp(btot)
        k_end = (k_inv32 * dec).astype(bf16)
        vb = v.astype(bf16)
        ri = lax.broadcasted_iota(jnp.int32, (n, n), 0)
        ci = lax.broadcasted_iota(jnp.int32, (n, n), 1)
        tri = (ci >= ri) if reverse else (ci <= ri)
        for h in range(heads):
            ks = slice(h * dk, (h + 1) * dk)
            vs = slice(h * dv, (h + 1) * dv)
            units.append(dict(q=q_dec[:, ks], ki=k_inv[:, ks], ke=k_end[:, ks], v=vb[:, vs], dec=dec[:, ks],
                              tri=tri, st=st_ref[base + h], ref=st_ref, h=base + h))
    return units


def _gated_matmuls(units, heads):
    for u in units:
        a = lax.dot_general(u["q"], u["ki"], NT, preferred_element_type=f32)
        u["a"] = jnp.where(u["tri"], a, 0.0).astype(bf16)
        u["qs"] = lax.dot_general(u["q"], u["st"].astype(bf16), NT, preferred_element_type=f32)
    for u in units:
        u["o"] = _dot(u["a"], u["v"]) + u["qs"]
        u["new"] = u["st"] * u["dec"] + lax.dot_general(u["v"], u["ke"], TN, preferred_element_type=f32)
    for u in units:
        u["ref"][u["h"]] = u["new"]
    return [jnp.concatenate([u["o"] for u in units[i:i + heads]], axis=1) for i in range(0, len(units), heads)]


def _gated_chunk(groups, heads, dk, dv, staggered):
    def prep(group):
        return _gated_prep([thunk() for thunk in group], heads, dk, dv)

    if not staggered:
        units = prep([thunk for group in groups for thunk in group])
        per_row = len(groups[0])
        outs = _gated_matmuls(units, heads)
        return [outs[i:i + per_row] for i in range(0, len(outs), per_row)]
    outs = []
    ready = prep(groups[0])
    for i in range(len(groups)):
        nxt = prep(groups[i + 1]) if i + 1 < len(groups) else None
        outs.append(_gated_matmuls(ready, heads))
        ready = nxt
    return outs


def _gla_gate(zd, wu_ref, bias_ref, d):
    zg = _dot(zd.astype(bf16), wu_ref[d]) + bias_ref[d:d + 1]
    return (jnp.minimum(zg, 0.0) - jnp.log(1.0 + jnp.exp(-jnp.abs(zg)))) * (1.0 / GLA_TAU)


def _gla_scan_kernel(qkf_ref, vf_ref, zdf_ref, qkb_ref, vb_ref, zdb_ref, wu_ref, bias_ref,
                     of_ref, ob_ref, sf_ref, sb_ref):
    @pl.when(pl.program_id(1) == 0)
    def _():
        sf_ref[...] = jnp.zeros_like(sf_ref)
        sb_ref[...] = jnp.zeros_like(sb_ref)

    def entry(row, d, qk_ref, v_ref, zd_ref, s_ref):
        qk = qk_ref[row]
        return (qk[:, :GLA_KEY_DIM] * (GLA_DK ** -0.5), qk[:, GLA_KEY_DIM:], v_ref[row],
                _gla_gate(zd_ref[row], wu_ref, bias_ref, d), s_ref, row * GLA_HEADS, d == 1)

    groups = [[functools.partial(entry, row, 0, qkf_ref, vf_ref, zdf_ref, sf_ref),
               functools.partial(entry, row, 1, qkb_ref, vb_ref, zdb_ref, sb_ref)]
              for row in range(qkf_ref.shape[0])]
    for row, (o_f, o_b) in enumerate(_gated_chunk(groups, GLA_HEADS, GLA_DK, GLA_DV, staggered=False)):
        of_ref[row] = o_f
        ob_ref[row] = o_b


def _hgrn_scan_kernel(qf_ref, if_ref, zff_ref, qb_ref, ib_ref, zfb_ref, lb_ref, of_ref, ob_ref, sf_ref, sb_ref):
    @pl.when(pl.program_id(1) == 0)
    def _():
        sf_ref[...] = jnp.zeros_like(sf_ref)
        sb_ref[...] = jnp.zeros_like(sb_ref)

    lb = lb_ref[...]
    def entry(row, d, q_ref, i_ref, zf_ref, s_ref):
        f = lb + (1.0 - lb) * _sigmoid(zf_ref[row])
        return (_silu(q_ref[row]), 1.0 - f, i_ref[row], jnp.log(f), s_ref, row * HGRN_HEADS, d == 1)

    groups = [[functools.partial(entry, row, 0, qf_ref, if_ref, zff_ref, sf_ref),
               functools.partial(entry, row, 1, qb_ref, ib_ref, zfb_ref, sb_ref)]
              for row in range(qf_ref.shape[0])]
    for row, (o_f, o_b) in enumerate(_gated_chunk(groups, HGRN_HEADS, HGRN_EXPAND, HGRN_EXPAND, staggered=True)):
        of_ref[row] = o_f
        ob_ref[row] = o_b


SCAN_ROWS_PER_STEP = 4


def _scan_rows(geo):
    return SCAN_ROWS_PER_STEP if geo.batch % SCAN_ROWS_PER_STEP == 0 else 1


def _chunk_spec(geo, width, col, reverse):
    rows = _scan_rows(geo)
    if reverse:
        return pl.BlockSpec((rows, SCAN_CHUNK, width), lambda b, i: (b, geo.bwd_chunk(i), col))
    return pl.BlockSpec((rows, SCAN_CHUNK, width), lambda b, i: (b, i, col))


def _scan_call(geo, body, args, specs, heads, dk, dv, name):
    out_sds = jax.ShapeDtypeStruct((geo.batch, geo.t, heads * dv), f32)
    rows = _scan_rows(geo)
    state = pltpu.VMEM((rows * heads, dv, dk), f32)
    return pl.pallas_call(
        body,
        grid=(geo.batch // rows, geo.nc),
        in_specs=specs,
        out_specs=[_chunk_spec(geo, heads * dv, 0, False), _chunk_spec(geo, heads * dv, 0, True)],
        out_shape=[out_sds, out_sds],
        scratch_shapes=[state, state],
        compiler_params=_params(2),
        name=name,
    )(*args)


def _gla_layer(geo, x, mods, gains, w_in_p, w_gd, w_gu, g_bias, g_norm, w_o, w_in, w_out, skip_ctx):
    r = GLA_GATE_RANK
    n_z = 2 * GLA_KEY_DIM + 2 * D_MODEL
    w_all = jnp.concatenate([w_in_p, w_gd[0], w_gd[1], jnp.zeros((D_MODEL, LANES - 2 * r), f32)], axis=1).astype(bf16)
    z = _mod_linear(geo, x, mods, gains, w_all, "gla_proj")
    wu = jnp.zeros((2, LANES, GLA_KEY_DIM), f32).at[0, :r].set(w_gu[0]).at[1, r:2 * r].set(w_gu[1]).astype(bf16)
    zd_col = n_z // LANES
    specs = []
    for rev in (False, True):
        specs += [_chunk_spec(geo, 2 * GLA_KEY_DIM, 0, rev), _chunk_spec(geo, D_MODEL, 1, rev),
                  _chunk_spec(geo, LANES, zd_col, rev)]
    specs += [_const_spec((2, LANES, GLA_KEY_DIM)), _const_spec((2, GLA_KEY_DIM))]
    o_f, o_b = _scan_call(geo, _gla_scan_kernel, [z, z, z, z, z, z, wu, g_bias], specs,
                          GLA_HEADS, GLA_DK, GLA_DV, "gla_scan")
    t0 = geo.ctx_tiles if skip_ctx else 0
    rows = _post_rows(geo, 3)
    mix_specs = [_tile_spec(geo, D_MODEL, 0, t0, rows), _tile_spec(geo, D_MODEL, 0, t0, rows),
                 _tile_spec(geo, D_MODEL, 2, t0, rows), _const_spec((1, GLA_DV))]
    return _post_mlp(geo, x, mods, gains, functools.partial(_pre_scan, GLA_DV),
                     [o_f, o_b, z, g_norm.reshape(1, GLA_DV)], mix_specs,
                     w_o.astype(bf16), w_in, w_out, skip_ctx, "gla_post", rows)


def _hgrn_layer(geo, x, mods, gains, w_in_p, w_f, lower_bound, g_norm, w_o, w_in, w_out, skip_ctx):
    w_all = jnp.concatenate([w_in_p, w_f[0], w_f[1]], axis=1).astype(bf16)
    z = _mod_linear(geo, x, mods, gains, w_all, "hgrn_proj")
    specs = []
    for rev in (False, True):
        specs += [_chunk_spec(geo, D_MODEL, 0, rev), _chunk_spec(geo, D_MODEL, 1, rev),
                  _chunk_spec(geo, D_MODEL, 4 if rev else 3, rev)]
    specs += [_const_spec((1, D_MODEL))]
    o_f, o_b = _scan_call(geo, _hgrn_scan_kernel, [z, z, z, z, z, z, lower_bound.reshape(1, D_MODEL)], specs,
                          HGRN_HEADS, HGRN_EXPAND, HGRN_EXPAND, "hgrn_scan")
    t0 = geo.ctx_tiles if skip_ctx else 0
    rows = _post_rows(geo, 3)
    mix_specs = [_tile_spec(geo, D_MODEL, 0, t0, rows), _tile_spec(geo, D_MODEL, 0, t0, rows),
                 _tile_spec(geo, D_MODEL, 2, t0, rows), _const_spec((1, HGRN_EXPAND))]
    return _post_mlp(geo, x, mods, gains, functools.partial(_pre_scan, HGRN_EXPAND),
                     [o_f, o_b, z, g_norm.reshape(1, HGRN_EXPAND)], mix_specs,
                     w_o.astype(bf16), w_in, w_out, skip_ctx, "hgrn_post", rows)


def _rwkv_proj_kernel(geo, x_ref, xp_ref, xn_ref, m_ref, g_ref, mix_ref, vec_ref, wrkv_ref, wdn_ref, wup_ref,
                      aup_ref, gup_ref, r_ref, v_ref, gate_ref, kk_ref, lw0_ref, lw1_ref, a0_ref, a1_ref,
                      k0_ref, k1_ref, bg_ref):
    t = pl.program_id(1)
    tm = geo.tm
    gain, shift, scale = g_ref[0:1], m_ref[0, 0, 0:1], m_ref[0, 0, 1:2]
    h = _modulate(x_ref[0], gain, shift, scale)
    seg_first = (t == 0) | (t == geo.ctx_tiles)
    seg_last = (t == geo.ctx_tiles - 1) | (t == geo.nt - 1)
    h_prev = jnp.where(seg_first, 0.0, _modulate(xp_ref[0], gain, shift, scale)[7:8])
    h_next = jnp.where(seg_last, 0.0, _modulate(xn_ref[0], gain, shift, scale)[0:1])
    row = lax.broadcasted_iota(jnp.int32, h.shape, 0)
    up = jnp.where(row == 0, h_prev, pltpu.roll(h, 1, axis=0))
    dn = jnp.where(row == tm - 1, h_next, pltpu.roll(h, tm - 1, axis=0))
    dx = 0.5 * (up + dn) - h

    def mixed(n):
        return (h + dx * mix_ref[n:n + 1]).astype(bf16)

    r = _dot(mixed(0), wrkv_ref[0])
    k = _dot(mixed(1), wrkv_ref[1])
    v = _dot(mixed(2), wrkv_ref[2])
    dw = jnp.tanh(_dot(mixed(3), wdn_ref[:, 0:LANES])).astype(bf16)
    da = _dot(mixed(4), wdn_ref[:, LANES:2 * LANES]).astype(bf16)
    dg = _sigmoid(_dot(mixed(5), wdn_ref[:, 2 * LANES:3 * LANES])).astype(bf16)
    gate = _dot(dg, gup_ref[...])
    r_ref[0] = r
    v_ref[0] = v
    gate_ref[0] = gate
    kk = k * vec_ref[4:5]
    kk_ref[0] = kk * lax.rsqrt(jnp.maximum(_seg_sum(kk * kk, RWKV_HEAD_SIZE), L2_EPS * L2_EPS))
    kds = []
    for d, (lw_ref, a_ref, kd_ref) in enumerate(((lw0_ref, a0_ref, k0_ref), (lw1_ref, a1_ref, k1_ref))):
        lw_ref[0] = -RWKV_DECAY_SCALE * _sigmoid(vec_ref[d:d + 1] + _dot(dw, wup_ref[d]))
        a = _sigmoid(vec_ref[2 + d:3 + d] + _dot(da, aup_ref[d]))
        a_ref[0] = a
        kds.append(k * (1.0 + (a - 1.0) * vec_ref[5:6]))
        kd_ref[0] = kds[d]
    bg_ref[0] = _seg_sum(r * (0.5 * (kds[0] + kds[1])) * vec_ref[6:7], RWKV_HEAD_SIZE) * v * gate


def _rwkv_chunk_units(units):
    n = SCAN_CHUNK
    hs = RWKV_HEAD_SIZE
    lane = lax.broadcasted_iota(jnp.int32, (n, RWKV_PAIR), 1)
    t_i = lax.broadcasted_iota(jnp.int32, (n, RWKV_PAIR), 0)
    s_i = lane % hs
    head0 = lane < hs
    ri = lax.broadcasted_iota(jnp.int32, (RWKV_PAIR, RWKV_PAIR), 0)
    ci = lax.broadcasted_iota(jnp.int32, (RWKV_PAIR, RWKV_PAIR), 1)
    same_head = ri // hs == ci // hs
    top_rows = ri < hs

    def block_diag(x):
        return jnp.concatenate([jnp.where(head0, x, 0.0), jnp.where(head0, 0.0, x)], axis=0)

    def cross_blocks(b):
        return (t_i // (2 * b) == s_i // (2 * b)) & (t_i // b != s_i // b)

    st = []
    for r, kd, v, kk, a, lw, ht, reverse in units:
        c = _cumsum_rows(lw, reverse)
        tot = c[0:1] if reverse else c[n - 1:n]
        e_neg = jnp.exp(-c)
        dec = jnp.exp(tot)
        kb = jnp.concatenate([kd * e_neg, kk * a * e_neg], axis=0)
        kb_t = kb.T
        kb_sw = pltpu.roll(kb_t, hs, axis=1)
        rhs = jnp.concatenate([jnp.where(same_head, jnp.where(top_rows, kb_t, kb_sw), 0.0),
                               jnp.where(same_head, jnp.where(top_rows, kb_sw, kb_t), 0.0)], axis=1)
        st.append(dict(
            strict=(s_i > t_i) if reverse else (s_i < t_i), incl=(s_i >= t_i) if reverse else (s_i <= t_i),
            a_bar=(-kk * jnp.exp(c - lw)).astype(bf16), r_bar=(r * jnp.exp(c)).astype(bf16), rhs=rhs.astype(bf16),
            v_bd=block_diag(v).astype(bf16), h_t=ht.T.astype(bf16),
            ends=(kb * dec).astype(bf16),
            v=v, ht=ht, dec=dec))
    for s in st:
        g = _dot(jnp.concatenate([s["a_bar"], s["r_bar"]], axis=0), s["rhs"])
        s["a_ak"] = jnp.where(s["strict"], g[0:n, 0:RWKV_PAIR], 0.0).astype(bf16)
        s["nmat"] = jnp.where(s["strict"], g[0:n, RWKV_PAIR:], 0.0)
        s["q_k"] = jnp.where(s["incl"], g[n:, 0:RWKV_PAIR], 0.0).astype(bf16)
        s["q_b"] = jnp.where(s["incl"], g[n:, RWKV_PAIR:], 0.0).astype(bf16)
    for s in st:
        s["inv"] = jnp.where(s_i == t_i, 1.0, jnp.where(cross_blocks(1), s["nmat"], 0.0))
    b = 2
    while 2 * b < n:
        for s in st:
            s["pc"] = _dot(s["inv"].astype(bf16), block_diag(jnp.where(cross_blocks(b), s["nmat"], 0.0)).astype(bf16))
        for s in st:
            s["inv"] = s["inv"] + _dot(s["pc"].astype(bf16), block_diag(s["inv"]).astype(bf16))
        b *= 2
    for s in st:
        x = _dot(jnp.concatenate([s["a_bar"], s["a_ak"]], axis=1), jnp.concatenate([s["h_t"], s["v_bd"]], axis=0))
        s["inv"] = s["inv"].astype(bf16)
        s["w"] = _dot(s["inv"], block_diag(x).astype(bf16))
    for s in st:
        s["cw"] = _dot(jnp.where(cross_blocks(n // 2), s["nmat"], 0.0).astype(bf16), block_diag(s["w"]).astype(bf16))
    for s in st:
        s["u"] = s["w"] + _dot(s["inv"], block_diag(s["cw"]).astype(bf16))
    for s in st:
        s["vu_t"] = jnp.concatenate([s["v"], s["u"]], axis=0).T.astype(bf16)
    for s in st:
        s["y"] = _dot(jnp.concatenate([s["r_bar"], s["q_k"], s["q_b"]], axis=1),
                      jnp.concatenate([s["h_t"], s["v_bd"], block_diag(s["u"]).astype(bf16)], axis=0))
    for s in st:
        s["upd"] = _dot(s["vu_t"], s["ends"])
    return [(s["y"], s["ht"] * s["dec"] + jnp.where(same_head, s["upd"], 0.0)) for s in st]


def _rwkv_scan_kernel(rows, pairs, *refs):
    fwd, bwd = refs[0:6], refs[6:12]
    yf_ref, yb_ref, hf_ref, hb_ref = refs[12:]

    @pl.when(pl.program_id(2) == 0)
    def _():
        hf_ref[...] = jnp.zeros_like(hf_ref)
        hb_ref[...] = jnp.zeros_like(hb_ref)

    units, dests = [], []
    for ins, y_ref, h_ref, reverse in ((fwd, yf_ref, hf_ref, False), (bwd, yb_ref, hb_ref, True)):
        r_ref, v_ref, kk_ref, lw_ref, a_ref, kd_ref = ins
        for row in range(rows):
            for p in range(pairs):
                sl = slice(p * RWKV_PAIR, (p + 1) * RWKV_PAIR)
                slot = row * pairs + p
                units.append((r_ref[row, :, sl], kd_ref[row, :, sl], v_ref[row, :, sl], kk_ref[row, :, sl],
                              a_ref[row, :, sl], lw_ref[row, :, sl], h_ref[slot], reverse))
                dests.append((y_ref, h_ref, row, slot, sl))
    for (y, h_new), (y_ref, h_ref, row, slot, sl) in zip(_rwkv_chunk_units(units), dests):
        y_ref[row, :, sl] = y
        h_ref[slot] = h_new


RWKV_PAIRS_PER_STEP = 8
RWKV_ROWS_PER_STEP = 2


def _rwkv_layer(geo, x, mods, gains, mix, w_rkv, w0, w_down, w_up, a0, a_down, a_up, g_down, g_up, k_k, k_a,
                r_k, ln_w, ln_b, w_o, w_in, w_out, skip_ctx):
    b, t, tm = geo.batch, geo.t, geo.tm
    d = D_MODEL
    rank = w_down.shape[-1]
    assert 2 * rank == LANES and a_down.shape[-1] == rank and g_down.shape[-1] == LANES
    mix8 = jnp.pad(mix, ((0, 2), (0, 0)))
    vec = jnp.stack([w0[0], w0[1], a0[0], a0[1], k_k, k_a, r_k.reshape(d), jnp.zeros_like(k_k)])
    w_dn = jnp.concatenate([w_down[0], w_down[1], a_down[0], a_down[1], g_down], axis=1).astype(bf16)

    def padded_up(w):
        return jnp.zeros((2, LANES, d), f32).at[0, :rank].set(w[0]).at[1, rank:].set(w[1]).astype(bf16)

    halo = 8
    n_halo = t // halo
    per = tm // halo
    x_prev = pl.BlockSpec((1, halo, d), lambda bb, tt: (bb, jnp.maximum(tt * per - 1, 0), 0))
    x_next = pl.BlockSpec((1, halo, d), lambda bb, tt: (bb, jnp.minimum((tt + 1) * per, n_halo - 1), 0))
    out_spec = pl.BlockSpec((1, tm, d), lambda bb, tt: (bb, tt, 0))
    sds = jax.ShapeDtypeStruct((b, t, d), f32)
    r, v, gate, kk, lw0, lw1, a_0, a_1, k0, k1, bonus_g = pl.pallas_call(
        functools.partial(_rwkv_proj_kernel, geo),
        grid=(b, geo.nt),
        in_specs=[geo.x_spec(), x_prev, x_next, geo.mod_spec(), _const_spec((8, d)), _const_spec((8, d)),
                  _const_spec((8, d)), _const_spec((3, d, d)), _const_spec((d, 3 * LANES)),
                  _const_spec((2, LANES, d)), _const_spec((2, LANES, d)), _const_spec((LANES, d))],
        out_specs=[out_spec] * 11,
        out_shape=[sds] * 11,
        compiler_params=_params(2),
        name="rwkv_proj",
    )(x, x, x, mods, gains, mix8, vec, w_rkv.astype(bf16), w_dn, padded_up(w_up), padded_up(a_up),
      g_up.astype(bf16))

    pairs = RWKV_PAIRS_PER_STEP
    rows = RWKV_ROWS_PER_STEP if b % RWKV_ROWS_PER_STEP == 0 else 1
    width = pairs * RWKV_PAIR
    groups = d // width

    def cspec(reverse):
        if reverse:
            return pl.BlockSpec((rows, SCAN_CHUNK, width), lambda bb, gg, i: (bb, geo.bwd_chunk(i), gg))
        return pl.BlockSpec((rows, SCAN_CHUNK, width), lambda bb, gg, i: (bb, i, gg))

    state = pltpu.VMEM((rows * pairs, RWKV_PAIR, RWKV_PAIR), f32)
    y_f, y_b = pl.pallas_call(
        functools.partial(_rwkv_scan_kernel, rows, pairs),
        grid=(b // rows, groups, geo.nc),
        in_specs=[cspec(False)] * 6 + [cspec(True)] * 6,
        out_specs=[cspec(False), cspec(True)],
        out_shape=[sds, sds],
        scratch_shapes=[state, state],
        compiler_params=_params(3),
        name="rwkv_scan",
    )(r, v, kk, lw0, a_0, k0, r, v, kk, lw1, a_1, k1)

    t0 = geo.ctx_tiles if skip_ctx else 0
    post_vec = jnp.stack([ln_w, ln_b] + [jnp.zeros_like(ln_w)] * 6)
    rows = _post_rows(geo, 4)
    mix_specs = [_tile_spec(geo, d, 0, t0, rows)] * 4 + [_const_spec((8, d))]
    return _post_mlp(geo, x, mods, gains, _pre_rwkv, [y_f, y_b, gate, bonus_g, post_vec], mix_specs,
                     w_o.astype(bf16), w_in, w_out, skip_ctx, "rwkv_post", rows)


def _hgrn_lower_bound(lb_param, layer):
    p = jax.nn.softmax(lb_param.astype(f32), axis=0)
    return (jnp.cumsum(p, axis=0) - p[0])[layer]


def kernel(x, c, ctx, c_ctx, w_mod, b_mod, g_pre_mix, g_post_mix, g_pre_mlp, g_post_mlp, w_mlp_in, w_mlp_out, attn_w_qkv, attn_w_o, attn_sink, gla_w_in, gla_w_gate_down, gla_w_gate_up, gla_gate_bias, gla_g_norm, gla_w_o, rwkv_mix, rwkv_w_rkv, rwkv_w0, rwkv_w_down, rwkv_w_up, rwkv_a0, rwkv_a_down, rwkv_a_up, rwkv_g_down, rwkv_g_up, rwkv_k_k, rwkv_k_a, rwkv_r_k, rwkv_ln_w, rwkv_ln_b, rwkv_w_o, hgrn_w_in, hgrn_w_f, hgrn_lb, hgrn_g_norm, hgrn_w_o):
    depth = w_mod.shape[0]
    geo = _Geom(x.shape[0], ctx.shape[1], x.shape[1])
    mods_all = _mod_vectors(c, c_ctx, w_mod, b_mod)
    xs = jnp.concatenate([ctx, x], axis=1)
    for i in range(depth):
        kind, j = i % 4, i // 4
        skip_ctx = i == depth - 1
        mods = mods_all[i]
        gains = jnp.stack([g_pre_mix[i], g_post_mix[i], g_pre_mlp[i], g_post_mlp[i]] + [jnp.zeros_like(g_pre_mix[i])] * 4)
        w_in, w_out = w_mlp_in[i].astype(bf16), w_mlp_out[i].astype(bf16)
        if kind == 0:
            xs = _attn_layer(geo, xs, mods, gains, attn_w_qkv[j], attn_w_o[j], attn_sink[j], w_in, w_out, skip_ctx)
        elif kind == 1:
            xs = _gla_layer(geo, xs, mods, gains, gla_w_in[j], gla_w_gate_down[j], gla_w_gate_up[j],
                            gla_gate_bias[j], gla_g_norm[j], gla_w_o[j], w_in, w_out, skip_ctx)
        elif kind == 2:
            xs = _rwkv_layer(geo, xs, mods, gains, rwkv_mix[j], rwkv_w_rkv[j], rwkv_w0[j], rwkv_w_down[j],
                             rwkv_w_up[j], rwkv_a0[j], rwkv_a_down[j], rwkv_a_up[j], rwkv_g_down[j], rwkv_g_up[j],
                             rwkv_k_k[j], rwkv_k_a[j], rwkv_r_k[j], rwkv_ln_w[j], rwkv_ln_b[j], rwkv_w_o[j],
                             w_in, w_out, skip_ctx)
        else:
            xs = _hgrn_layer(geo, xs, mods, gains, hgrn_w_in[j], hgrn_w_f[j], _hgrn_lower_bound(hgrn_lb, i),
                             hgrn_g_norm[j], hgrn_w_o[j], w_in, w_out, skip_ctx)
        if skip_ctx:
            return xs
    return xs[:, geo.n_ctx:]
```

```python
import functools

import jax
import jax.numpy as jnp
from jax import lax
from jax.experimental import pallas as pl
from jax.experimental.pallas import tpu as pltpu

f32 = jnp.float32
bf16 = jnp.bfloat16

D_MODEL = 1024
N_MOD = 6
MLP_HIDDEN = 4 * D_MODEL
NORM_EPS = 1e-6
NEG_INF = -1e30
GRID_W = 64

ATTN_HEADS = 16
ATTN_KV_HEADS = 4
ATTN_GROUP = ATTN_HEADS // ATTN_KV_HEADS
HEAD_DIM = 64
WINDOW = 128
ATTN_BLOCK = 128
ROPE_BASE = 10000.0
ROPE_AXIS_DIM = HEAD_DIM // 2
ROPE_FREQS = ROPE_AXIS_DIM // 2

GLA_HEADS = 4
GLA_KEY_DIM = D_MODEL // 2
GLA_DK = GLA_KEY_DIM // GLA_HEADS
GLA_DV = D_MODEL // GLA_HEADS
GLA_GATE_RANK = 16
GLA_TAU = 16.0
SCAN_CHUNK = 64

RWKV_HEAD_SIZE = 64
RWKV_LN_EPS = 64e-5
L2_EPS = 1e-12
RWKV_PAIR = 2 * RWKV_HEAD_SIZE
RWKV_DECAY_SCALE = 0.6065306597126334

HGRN_EXPAND = 128
HGRN_HEADS = D_MODEL // HGRN_EXPAND

LANES = 128
MOD_ROWS = 8
VMEM_LIMIT = 56 * 1024 * 1024

NT = (((1,), (1,)), ((), ()))
TN = (((0,), (0,)), ((), ()))


def _params(n_grid):
    return pltpu.CompilerParams(dimension_semantics=("arbitrary",) * n_grid, vmem_limit_bytes=VMEM_LIMIT)


def _const_spec(shape):
    nd = len(shape)
    return pl.BlockSpec(shape, lambda *_: (0,) * nd, pipeline_mode=pl.Buffered(1))


def _dot(a, b):
    return jnp.dot(a, b, preferred_element_type=f32)


def _sigmoid(x):
    return 0.5 * jnp.tanh(0.5 * x) + 0.5


def _silu(x):
    return x * _sigmoid(x)


def _rms(x, gain):
    return x * lax.rsqrt(jnp.mean(x * x, axis=-1, keepdims=True) + NORM_EPS) * gain


def _modulate(x, gain, shift, scale):
    return _rms(x, gain) * (1.0 + scale) + shift


def _seg_sum(x, seg):
    r = lax.broadcasted_iota(jnp.int32, (LANES, LANES), 0) // seg
    c = lax.broadcasted_iota(jnp.int32, (LANES, LANES), 1) // seg
    ones_bd = (r == c).astype(bf16)
    hi = x.astype(bf16)
    lo = (x - hi.astype(f32)).astype(bf16)
    outs = []
    for j in range(x.shape[1] // LANES):
        sl = slice(j * LANES, (j + 1) * LANES)
        outs.append(_dot(hi[:, sl], ones_bd) + _dot(lo[:, sl], ones_bd))
    return jnp.concatenate(outs, axis=1)


def _cumsum_rows(g, reverse):
    n = g.shape[0]
    ri = lax.broadcasted_iota(jnp.int32, (n, n), 0)
    ci = lax.broadcasted_iota(jnp.int32, (n, n), 1)
    tri = ((ci >= ri) if reverse else (ci <= ri)).astype(bf16)
    hi = g.astype(bf16)
    lo = (g - hi.astype(f32)).astype(bf16)
    return _dot(tri, hi) + _dot(tri, lo)


def _mod_kernel(c_ref, w_ref, b_ref, o_ref):
    o_ref[0] = _dot(_silu(c_ref[...]).astype(bf16), w_ref[0]) + b_ref[0]


def _mod_vectors(c, c_ctx, w_mod, b_mod):
    depth, d, _ = w_mod.shape
    batch = c.shape[0]
    rows = -(-(batch + 1) // 8) * 8
    cc = jnp.zeros((rows, d), f32).at[:batch].set(c).at[batch].set(c_ctx)
    out = pl.pallas_call(
        _mod_kernel,
        grid=(depth, N_MOD),
        in_specs=[
            pl.BlockSpec((rows, d), lambda i, j: (0, 0)),
            pl.BlockSpec((1, d, d), lambda i, j: (i, 0, j)),
            pl.BlockSpec((1, 1, d), lambda i, j: (i, 0, j)),
        ],
        out_specs=pl.BlockSpec((1, rows, d), lambda i, j: (i, 0, j)),
        out_shape=jax.ShapeDtypeStruct((depth, rows, N_MOD * d), f32),
        compiler_params=_params(2),
        name="mod_vectors",
    )(cc, w_mod.astype(bf16), b_mod.reshape(depth, 1, N_MOD * d))
    lat = out[:, :batch].reshape(depth, batch, 1, N_MOD, d)
    con = jnp.broadcast_to(out[:, batch].reshape(depth, 1, 1, N_MOD, d), lat.shape)
    mods = jnp.concatenate([con, lat], axis=2)
    return jnp.pad(mods, ((0, 0), (0, 0), (0, 0), (0, MOD_ROWS - N_MOD), (0, 0)))


class _Geom:
    def __init__(self, batch, n_ctx, n_lat):
        self.batch, self.n_ctx, self.n_lat = batch, n_ctx, n_lat
        self.t = n_ctx + n_lat
        self.tm = min(256, n_ctx)
        assert n_ctx % self.tm == 0 and n_lat % self.tm == 0
        assert n_ctx % ATTN_BLOCK == 0 and n_lat % ATTN_BLOCK == 0 and n_lat % GRID_W == 0
        self.nt = self.t // self.tm
        self.ctx_tiles = n_ctx // self.tm
        self.nc = self.t // SCAN_CHUNK
        self.ctx_chunks = n_ctx // SCAN_CHUNK

    def x_spec(self, t0=0, rows=1):
        return pl.BlockSpec((rows, self.tm, D_MODEL), lambda b, t: (b, t + t0, 0))

    def x_in(self, x, t0=0, rows=1):
        if not isinstance(x, tuple):
            return [x], [self.x_spec(t0, rows)]
        ct = self.ctx_tiles
        block = (rows, self.tm, D_MODEL)
        return list(x), [pl.BlockSpec(block, lambda b, t: (b, jnp.minimum(t + t0, ct - 1), 0)),
                         pl.BlockSpec(block, lambda b, t: (b, jnp.maximum(t + t0 - ct, 0), 0))]

    def mod_spec(self, t0=0, rows=1):
        ct = self.ctx_tiles
        return pl.BlockSpec((rows, 1, MOD_ROWS, D_MODEL), lambda b, t: (b, ((t + t0) >= ct).astype(jnp.int32), 0, 0))

    def bwd_chunk(self, i):
        cc = self.ctx_chunks
        return jnp.where(i < cc, cc - 1 - i, self.nc - 1 + cc - i)


LIN_ROWS_PER_STEP = 2


def _lin_kernel(x_ref, m_ref, g_ref, w_ref, o_ref):
    rows, tm = x_ref.shape[0], x_ref.shape[1]
    h = [_modulate(x_ref[r], g_ref[0:1], m_ref[r, 0, 0:1], m_ref[r, 0, 1:2]).astype(bf16) for r in range(rows)]
    z = _dot(jnp.concatenate(h, axis=0), w_ref[...])
    for r in range(rows):
        o_ref[r] = z[r * tm:(r + 1) * tm]


def _mod_linear(geo, x, mods, gains, w, name):
    n = w.shape[1]
    rows = LIN_ROWS_PER_STEP if geo.batch % LIN_ROWS_PER_STEP == 0 else 1
    return pl.pallas_call(
        _lin_kernel,
        grid=(geo.batch // rows, geo.nt),
        in_specs=[geo.x_spec(0, rows), geo.mod_spec(0, rows), _const_spec((8, D_MODEL)), _const_spec((D_MODEL, n))],
        out_specs=pl.BlockSpec((rows, geo.tm, n), lambda b, t: (b, t, 0)),
        out_shape=jax.ShapeDtypeStruct((geo.batch, geo.t, n), f32),
        compiler_params=_params(2),
        name=name,
    )(x, mods, gains, w)


def _read_tile(x_refs, row, is_ctx):
    if len(x_refs) == 1:
        return x_refs[0][row]
    return jnp.where(is_ctx, x_refs[0][row], x_refs[1][row])


def _head_rms(o, gain, width):
    outs = []
    for h in range(o.shape[1] // width):
        oh = o[:, h * width:(h + 1) * width]
        outs.append(oh * lax.rsqrt(jnp.mean(oh * oh, axis=-1, keepdims=True) + NORM_EPS) * gain)
    return jnp.concatenate(outs, axis=1)


def _pre_attn(refs, row):
    (o_ref,) = refs
    return o_ref[row]


def _pre_scan(width, refs, row):
    of_ref, ob_ref, gate_ref, gn_ref = refs
    o = _head_rms(of_ref[row] + ob_ref[row], gn_ref[...], width)
    return (o * _silu(gate_ref[row])).astype(bf16)


def _pre_rwkv(refs, row):
    yf_ref, yb_ref, g_ref, bg_ref, vec_ref = refs
    y = yf_ref[row] + yb_ref[row]
    inv_n = 1.0 / RWKV_HEAD_SIZE
    mu = _seg_sum(y, RWKV_HEAD_SIZE) * inv_n
    dlt = y - mu
    var = _seg_sum(dlt * dlt, RWKV_HEAD_SIZE) * inv_n
    yn = dlt * lax.rsqrt(var + RWKV_LN_EPS) * vec_ref[0:1] + vec_ref[1:2]
    return (yn * g_ref[row] + bg_ref[row]).astype(bf16)


def _post_kernel(pre, n_x, n_mix, first_ctx_tiles, *refs):
    x_refs, (m_ref, g_ref), mix_refs = refs[:n_x], refs[n_x:n_x + 2], refs[n_x + 2:n_x + 2 + n_mix]
    wo_ref, win_ref, wout_ref, out_ref = refs[n_x + 2 + n_mix:]
    rows = range(out_ref.shape[0])
    is_ctx = pl.program_id(1) < first_ctx_tiles
    o = [pre(mix_refs, r) for r in rows]
    y = [_dot(o[r], wo_ref[...]) for r in rows]
    x1 = [_read_tile(x_refs, r, is_ctx) + m_ref[r, 0, 2:3] * _rms(y[r], g_ref[1:2]) for r in rows]
    h2 = [_modulate(x1[r], g_ref[2:3], m_ref[r, 0, 3:4], m_ref[r, 0, 4:5]).astype(bf16) for r in rows]
    us = [[] for _ in rows]
    for c in range(MLP_HIDDEN // D_MODEL):
        cols = slice(c * D_MODEL, (c + 1) * D_MODEL)
        for r in rows:
            us[r].append(jnp.square(jnp.maximum(_dot(h2[r], win_ref[:, cols]), 0.0)).astype(bf16))
    acc = [_dot(jnp.concatenate(us[r], axis=1), wout_ref[...]) for r in rows]
    for r in rows:
        out_ref[r] = x1[r] + m_ref[r, 0, 5:6] * _rms(acc[r], g_ref[3:4])


POST_ROWS_PER_STEP = 2


def _post_rows(geo, n_tile_inputs):
    tile = geo.tm * D_MODEL * 4
    weights = 2 * (D_MODEL * D_MODEL + 2 * D_MODEL * MLP_HIDDEN)
    for rows in (POST_ROWS_PER_STEP, 1):
        streamed = 2 * (n_tile_inputs + 2) * rows * tile
        live = 6 * rows * tile
        if geo.batch % rows == 0 and weights + streamed + live <= VMEM_LIMIT:
            return rows
    return 1


def _post_mlp(geo, x, mods, gains, pre, mix_args, mix_specs, w_o, w_in, w_out, skip_ctx, name, rows):
    t0 = geo.ctx_tiles if skip_ctx else 0
    x_args, x_specs = geo.x_in(x, t0, rows)
    return pl.pallas_call(
        functools.partial(_post_kernel, pre, len(x_args), len(mix_args), geo.ctx_tiles - t0),
        grid=(geo.batch // rows, geo.nt - t0),
        in_specs=x_specs + [geo.mod_spec(t0, rows), _const_spec((8, D_MODEL))] + mix_specs + [
            _const_spec((D_MODEL, D_MODEL)), _const_spec((D_MODEL, MLP_HIDDEN)), _const_spec((MLP_HIDDEN, D_MODEL))],
        out_specs=pl.BlockSpec((rows, geo.tm, D_MODEL), lambda b, t: (b, t, 0)),
        out_shape=jax.ShapeDtypeStruct((geo.batch, geo.t - t0 * geo.tm, D_MODEL), f32),
        compiler_params=_params(2),
        name=name,
    )(*x_args, mods, gains, *mix_args, w_o, w_in, w_out)


def _tile_spec(geo, width, col, t0, rows):
    return pl.BlockSpec((rows, geo.tm, width), lambda b, t: (b, t + t0, col))


def _rope(x, cos, sin):
    w = x.shape[1]
    reps = w // LANES
    cw = jnp.concatenate([cos] * reps, axis=1)
    sw = jnp.concatenate([sin] * reps, axis=1)
    lane = lax.broadcasted_iota(jnp.int32, x.shape, 1)
    first = (lane % ROPE_AXIS_DIM) < ROPE_FREQS
    partner = jnp.where(first, pltpu.roll(x, w - ROPE_FREQS, axis=1), pltpu.roll(x, ROPE_FREQS, axis=1))
    return x * cw + partner * sw


def _attn_proj_kernel(ctx_tiles, *refs):
    x_refs, (m_ref, g_ref, cos_ref, sin_ref, wq_ref, wk_ref, wv_ref, q_ref, k_ref, v_ref) = refs[:-10], refs[-10:]
    x = _read_tile(x_refs, 0, pl.program_id(1) < ctx_tiles)
    h = _modulate(x, g_ref[0:1], m_ref[0, 0, 0:1], m_ref[0, 0, 1:2]).astype(bf16)
    cos, sin = cos_ref[...], sin_ref[...]
    q = _rope(_dot(h, wq_ref[...]) * (HEAD_DIM ** -0.5), cos, sin)
    k = _rope(_dot(h, wk_ref[...]), cos, sin)
    v = _dot(h, wv_ref[...])
    for hd in range(ATTN_HEADS):
        q_ref[0, hd] = q[:, hd * HEAD_DIM:(hd + 1) * HEAD_DIM].astype(bf16)
    for hd in range(ATTN_KV_HEADS):
        k_ref[0, hd] = k[:, hd * HEAD_DIM:(hd + 1) * HEAD_DIM].astype(bf16)
        v_ref[0, hd] = v[:, hd * HEAD_DIM:(hd + 1) * HEAD_DIM].astype(bf16)


def _attn_core_kernel(geo, sink_ref, q_ref, kp_ref, kc_ref, kn_ref, kx_ref, vp_ref, vc_ref, vn_ref, vx_ref, o_ref):
    qb = pl.program_id(1)
    blk = ATTN_BLOCK
    n_loc = 3 * blk
    n_keys = n_loc + geo.n_ctx
    rows = ATTN_GROUP * blk
    first_lat = geo.n_ctx // blk
    n_blocks = geo.t // blk
    row = lax.broadcasted_iota(jnp.int32, (blk, n_keys), 0)
    col = lax.broadcasted_iota(jnp.int32, (blk, n_keys), 1)
    kblk = qb - 1 + col // blk
    ok_local = ((jnp.abs(col - blk - row) <= WINDOW) & (qb >= first_lat) & (kblk >= first_lat) & (kblk < n_blocks))
    ok = (col >= n_loc) | ok_local
    scores, vals = [], []
    for j in range(ATTN_KV_HEADS):
        q4 = q_ref[0, ATTN_GROUP * j:ATTN_GROUP * (j + 1)].reshape(rows, HEAD_DIM)
        keys = jnp.concatenate([kp_ref[0, j], kc_ref[0, j], kn_ref[0, j], kx_ref[0, j]], axis=0)
        vals.append(jnp.concatenate([vp_ref[0, j], vc_ref[0, j], vn_ref[0, j], vx_ref[0, j]], axis=0))
        scores.append(lax.dot_general(q4, keys, NT, preferred_element_type=f32))
    heads = [(j, g) for j in range(ATTN_KV_HEADS) for g in range(ATTN_GROUP)]
    sg = [jnp.where(ok, scores[j][g * blk:(g + 1) * blk], NEG_INF) for j, g in heads]
    sk = [sink_ref[ATTN_GROUP * j + g] for j, g in heads]
    m = [jnp.maximum(jnp.max(s, axis=-1, keepdims=True), k) for s, k in zip(sg, sk)]
    e = [jnp.exp(s - mm) for s, mm in zip(sg, m)]
    inv = [1.0 / (jnp.sum(ee, axis=-1, keepdims=True) + jnp.exp(k - mm)) for ee, k, mm in zip(e, sk, m)]
    p = [ee.astype(bf16) for ee in e]
    outs = []
    for j in range(ATTN_KV_HEADS):
        o4 = _dot(jnp.concatenate(p[ATTN_GROUP * j:ATTN_GROUP * (j + 1)], axis=0), vals[j])
        outs += [o4[g * blk:(g + 1) * blk] * inv[ATTN_GROUP * j + g] for g in range(ATTN_GROUP)]
    o_ref[0] = jnp.concatenate(outs, axis=1).astype(bf16)


def _rope_tables(geo):
    inv_freq = ROPE_BASE ** (-jnp.arange(ROPE_FREQS, dtype=f32) * 2.0 / ROPE_AXIS_DIM)
    pos = jnp.arange(geo.n_lat)
    row = (pos // GRID_W).astype(f32)
    col = (pos % GRID_W).astype(f32)
    ang = jnp.stack([row[:, None] * inv_freq, col[:, None] * inv_freq], axis=1)
    cos = jnp.cos(ang)
    sin = jnp.sin(ang)
    cos_h = jnp.concatenate([cos, cos], axis=2).reshape(geo.n_lat, HEAD_DIM)
    sin_h = jnp.concatenate([-sin, sin], axis=2).reshape(geo.n_lat, HEAD_DIM)
    cos_t = jnp.concatenate([jnp.ones((geo.n_ctx, HEAD_DIM), f32), cos_h], axis=0)
    sin_t = jnp.concatenate([jnp.zeros((geo.n_ctx, HEAD_DIM), f32), sin_h], axis=0)
    return jnp.tile(cos_t, (1, 2)), jnp.tile(sin_t, (1, 2))


def _attn_layer(geo, x, mods, gains, w_qkv, w_o, sink, w_in, w_out, skip_ctx):
    b, t, tm = geo.batch, geo.t, geo.tm
    q_cols = ATTN_HEADS * HEAD_DIM
    kv_cols = ATTN_KV_HEADS * HEAD_DIM
    wb = w_qkv.astype(bf16)
    cos_t, sin_t = _rope_tables(geo)
    tab_spec = pl.BlockSpec((tm, LANES), lambda bb, tt: (tt, 0))
    x_args, x_specs = geo.x_in(x)
    q, k, v = pl.pallas_call(
        functools.partial(_attn_proj_kernel, geo.ctx_tiles),
        grid=(b, geo.nt),
        in_specs=x_specs + [geo.mod_spec(), _const_spec((8, D_MODEL)), tab_spec, tab_spec,
                  _const_spec((D_MODEL, q_cols)), _const_spec((D_MODEL, kv_cols)), _const_spec((D_MODEL, kv_cols))],
        out_specs=[pl.BlockSpec((1, ATTN_HEADS, tm, HEAD_DIM), lambda bb, tt: (bb, 0, tt, 0)),
                   pl.BlockSpec((1, ATTN_KV_HEADS, tm, HEAD_DIM), lambda bb, tt: (bb, 0, tt, 0)),
                   pl.BlockSpec((1, ATTN_KV_HEADS, tm, HEAD_DIM), lambda bb, tt: (bb, 0, tt, 0))],
        out_shape=[jax.ShapeDtypeStruct((b, ATTN_HEADS, t, HEAD_DIM), bf16),
                   jax.ShapeDtypeStruct((b, ATTN_KV_HEADS, t, HEAD_DIM), bf16),
                   jax.ShapeDtypeStruct((b, ATTN_KV_HEADS, t, HEAD_DIM), bf16)],
        compiler_params=_params(2),
        name="attn_proj",
    )(*x_args, mods, gains, cos_t, sin_t, wb[:, :q_cols], wb[:, q_cols:q_cols + kv_cols], wb[:, q_cols + kv_cols:])

    blk = ATTN_BLOCK
    n_blocks = t // blk

    def kv_spec(off):
        return pl.BlockSpec((1, ATTN_KV_HEADS, blk, HEAD_DIM),
                            lambda bb, qb: (bb, 0, jnp.clip(qb + off, 0, n_blocks - 1), 0))

    ctx_spec = pl.BlockSpec((1, ATTN_KV_HEADS, geo.n_ctx, HEAD_DIM), lambda bb, qb: (bb, 0, 0, 0))
    o = pl.pallas_call(
        functools.partial(_attn_core_kernel, geo),
        grid=(b, n_blocks),
        in_specs=[pl.BlockSpec(memory_space=pltpu.SMEM),
                  pl.BlockSpec((1, ATTN_HEADS, blk, HEAD_DIM), lambda bb, qb: (bb, 0, qb, 0)),
                  kv_spec(-1), kv_spec(0), kv_spec(1), ctx_spec,
                  kv_spec(-1), kv_spec(0), kv_spec(1), ctx_spec],
        out_specs=pl.BlockSpec((1, blk, q_cols), lambda bb, qb: (bb, qb, 0)),
        out_shape=jax.ShapeDtypeStruct((b, t, q_cols), bf16),
        compiler_params=_params(2),
        name="attn_core",
    )(sink, q, k, k, k, k, v, v, v, v)

    t0 = geo.ctx_tiles if skip_ctx else 0
    rows = _post_rows(geo, 1)
    return _post_mlp(geo, x, mods, gains, _pre_attn, [o], [_tile_spec(geo, D_MODEL, 0, t0, rows)],
                     w_o.astype(bf16), w_in, w_out, skip_ctx, "attn_post", rows)


def _gated_prep(entries, heads, dk, dv):
    sums = [_cumsum_rows(g, reverse) for _, _, _, g, _, _, reverse in entries]
    units = []
    for (q, k, v, g, st_ref, base, reverse), b in zip(entries, sums):
        n = q.shape[0]
        btot = b[0:1] if reverse else b[n - 1:n]
        q_dec = (q * jnp.exp(b)).astype(bf16)
        k_inv32 = k * jnp.exp(-b)
        k_inv = k_inv32.astype(bf16)
        dec = jnp.exp(btot)
        k_end = (k_inv32 * dec).astype(bf16)
        vb = v.astype(bf16)
        ri = lax.broadcasted_iota(jnp.int32, (n, n), 0)
        ci = lax.broadcasted_iota(jnp.int32, (n, n), 1)
        tri = (ci >= ri) if reverse else (ci <= ri)
        for h in range(heads):
            ks = slice(h * dk, (h + 1) * dk)
            vs = slice(h * dv, (h + 1) * dv)
            units.append(dict(q=q_dec[:, ks], ki=k_inv[:, ks], ke=k_end[:, ks], v=vb[:, vs], dec=dec[:, ks],
                              tri=tri, st=st_ref[base + h], ref=st_ref, h=base + h))
    return units


def _gated_matmuls(units, heads):
    for u in units:
        a = lax.dot_general(u["q"], u["ki"], NT, preferred_element_type=f32)
        u["a"] = jnp.where(u["tri"], a, 0.0).astype(bf16)
        u["qs"] = lax.dot_general(u["q"], u["st"].astype(bf16), NT, preferred_element_type=f32)
    for u in units:
        u["o"] = _dot(u["a"], u["v"]) + u["qs"]
        u["new"] = u["st"] * u["dec"] + lax.dot_general(u["v"], u["ke"], TN, preferred_element_type=f32)
    for u in units:
        u["ref"][u["h"]] = u["new"]
    return [jnp.concatenate([u["o"] for u in units[i:i + heads]], axis=1) for i in range(0, len(units), heads)]


def _gated_chunk(groups, heads, dk, dv, staggered):
    def prep(group):
        return _gated_prep([thunk() for thunk in group], heads, dk, dv)

    if not staggered:
        units = prep([thunk for group in groups for thunk in group])
        per_row = len(groups[0])
        outs = _gated_matmuls(units, heads)
        return [outs[i:i + per_row] for i in range(0, len(outs), per_row)]
    outs = []
    ready = prep(groups[0])
    for i in range(len(groups)):
        nxt = prep(groups[i + 1]) if i + 1 < len(groups) else None
        outs.append(_gated_matmuls(ready, heads))
        ready = nxt
    return outs


def _gla_gate(zd, wu_ref, bias_ref, d):
    zg = _dot(zd.astype(bf16), wu_ref[d]) + bias_ref[d:d + 1]
    return (jnp.minimum(zg, 0.0) - jnp.log(1.0 + jnp.exp(-jnp.abs(zg)))) * (1.0 / GLA_TAU)


def _gla_scan_kernel(qkf_ref, vf_ref, zdf_ref, qkb_ref, vb_ref, zdb_ref, wu_ref, bias_ref,
                     of_ref, ob_ref, sf_ref, sb_ref):
    @pl.when(pl.program_id(1) == 0)
    def _():
        sf_ref[...] = jnp.zeros_like(sf_ref)
        sb_ref[...] = jnp.zeros_like(sb_ref)

    def entry(row, d, qk_ref, v_ref, zd_ref, s_ref):
        qk = qk_ref[row]
        return (qk[:, :GLA_KEY_DIM] * (GLA_DK ** -0.5), qk[:, GLA_KEY_DIM:], v_ref[row],
                _gla_gate(zd_ref[row], wu_ref, bias_ref, d), s_ref, row * GLA_HEADS, d == 1)

    groups = [[functools.partial(entry, row, 0, qkf_ref, vf_ref, zdf_ref, sf_ref),
               functools.partial(entry, row, 1, qkb_ref, vb_ref, zdb_ref, sb_ref)]
              for row in range(qkf_ref.shape[0])]
    for row, (o_f, o_b) in enumerate(_gated_chunk(groups, GLA_HEADS, GLA_DK, GLA_DV, staggered=False)):
        of_ref[row] = o_f
        ob_ref[row] = o_b


def _hgrn_scan_kernel(qf_ref, if_ref, zff_ref, qb_ref, ib_ref, zfb_ref, lb_ref, of_ref, ob_ref, sf_ref, sb_ref):
    @pl.when(pl.program_id(1) == 0)
    def _():
        sf_ref[...] = jnp.zeros_like(sf_ref)
        sb_ref[...] = jnp.zeros_like(sb_ref)

    lb = lb_ref[...]
    def entry(row, d, q_ref, i_ref, zf_ref, s_ref):
        f = lb + (1.0 - lb) * _sigmoid(zf_ref[row])
        return (_silu(q_ref[row]), 1.0 - f, i_ref[row], jnp.log(f), s_ref, row * HGRN_HEADS, d == 1)

    groups = [[functools.partial(entry, row, 0, qf_ref, if_ref, zff_ref, sf_ref),
               functools.partial(entry, row, 1, qb_ref, ib_ref, zfb_ref, sb_ref)]
              for row in range(qf_ref.shape[0])]
    for row, (o_f, o_b) in enumerate(_gated_chunk(groups, HGRN_HEADS, HGRN_EXPAND, HGRN_EXPAND, staggered=True)):
        of_ref[row] = o_f
        ob_ref[row] = o_b


SCAN_ROWS_PER_STEP = 4


def _scan_rows(geo):
    return SCAN_ROWS_PER_STEP if geo.batch % SCAN_ROWS_PER_STEP == 0 else 1


def _chunk_spec(geo, width, col, reverse):
    rows = _scan_rows(geo)
    if reverse:
        return pl.BlockSpec((rows, SCAN_CHUNK, width), lambda b, i: (b, geo.bwd_chunk(i), col))
    return pl.BlockSpec((rows, SCAN_CHUNK, width), lambda b, i: (b, i, col))


def _scan_call(geo, body, args, specs, heads, dk, dv, name):
    out_sds = jax.ShapeDtypeStruct((geo.batch, geo.t, heads * dv), f32)
    rows = _scan_rows(geo)
    state = pltpu.VMEM((rows * heads, dv, dk), f32)
    return pl.pallas_call(
        body,
        grid=(geo.batch // rows, geo.nc),
        in_specs=specs,
        out_specs=[_chunk_spec(geo, heads * dv, 0, False), _chunk_spec(geo, heads * dv, 0, True)],
        out_shape=[out_sds, out_sds],
        scratch_shapes=[state, state],
        compiler_params=_params(2),
        name=name,
    )(*args)


def _gla_layer(geo, x, mods, gains, w_in_p, w_gd, w_gu, g_bias, g_norm, w_o, w_in, w_out, skip_ctx):
    r = GLA_GATE_RANK
    n_z = 2 * GLA_KEY_DIM + 2 * D_MODEL
    w_all = jnp.concatenate([w_in_p, w_gd[0], w_gd[1], jnp.zeros((D_MODEL, LANES - 2 * r), f32)], axis=1).astype(bf16)
    z = _mod_linear(geo, x, mods, gains, w_all, "gla_proj")
    wu = jnp.zeros((2, LANES, GLA_KEY_DIM), f32).at[0, :r].set(w_gu[0]).at[1, r:2 * r].set(w_gu[1]).astype(bf16)
    zd_col = n_z // LANES
    specs = []
    for rev in (False, True):
        specs += [_chunk_spec(geo, 2 * GLA_KEY_DIM, 0, rev), _chunk_spec(geo, D_MODEL, 1, rev),
                  _chunk_spec(geo, LANES, zd_col, rev)]
    specs += [_const_spec((2, LANES, GLA_KEY_DIM)), _const_spec((2, GLA_KEY_DIM))]
    o_f, o_b = _scan_call(geo, _gla_scan_kernel, [z, z, z, z, z, z, wu, g_bias], specs,
                          GLA_HEADS, GLA_DK, GLA_DV, "gla_scan")
    t0 = geo.ctx_tiles if skip_ctx else 0
    rows = _post_rows(geo, 3)
    mix_specs = [_tile_spec(geo, D_MODEL, 0, t0, rows), _tile_spec(geo, D_MODEL, 0, t0, rows),
                 _tile_spec(geo, D_MODEL, 2, t0, rows), _const_spec((1, GLA_DV))]
    return _post_mlp(geo, x, mods, gains, functools.partial(_pre_scan, GLA_DV),
                     [o_f, o_b, z, g_norm.reshape(1, GLA_DV)], mix_specs,
                     w_o.astype(bf16), w_in, w_out, skip_ctx, "gla_post", rows)


def _hgrn_layer(geo, x, mods, gains, w_in_p, w_f, lower_bound, g_norm, w_o, w_in, w_out, skip_ctx):
    w_all = jnp.concatenate([w_in_p, w_f[0], w_f[1]], axis=1).astype(bf16)
    z = _mod_linear(geo, x, mods, gains, w_all, "hgrn_proj")
    specs = []
    for rev in (False, True):
        specs += [_chunk_spec(geo, D_MODEL, 0, rev), _chunk_spec(geo, D_MODEL, 1, rev),
                  _chunk_spec(geo, D_MODEL, 4 if rev else 3, rev)]
    specs += [_const_spec((1, D_MODEL))]
    o_f, o_b = _scan_call(geo, _hgrn_scan_kernel, [z, z, z, z, z, z, lower_bound.reshape(1, D_MODEL)], specs,
                          HGRN_HEADS, HGRN_EXPAND, HGRN_EXPAND, "hgrn_scan")
    t0 = geo.ctx_tiles if skip_ctx else 0
    rows = _post_rows(geo, 3)
    mix_specs = [_tile_spec(geo, D_MODEL, 0, t0, rows), _tile_spec(geo, D_MODEL, 0, t0, rows),
                 _tile_spec(geo, D_MODEL, 2, t0, rows), _const_spec((1, HGRN_EXPAND))]
    return _post_mlp(geo, x, mods, gains, functools.partial(_pre_scan, HGRN_EXPAND),
                     [o_f, o_b, z, g_norm.reshape(1, HGRN_EXPAND)], mix_specs,
                     w_o.astype(bf16), w_in, w_out, skip_ctx, "hgrn_post", rows)


def _rwkv_proj_kernel(geo, x_ref, xp_ref, xn_ref, m_ref, g_ref, mix_ref, vec_ref, wrkv_ref, wdn_ref, wup_ref,
                      aup_ref, gup_ref, r_ref, v_ref, gate_ref, kk_ref, lw0_ref, lw1_ref, a0_ref, a1_ref,
                      k0_ref, k1_ref, bg_ref):
    t = pl.program_id(1)
    tm = geo.tm
    gain, shift, scale = g_ref[0:1], m_ref[0, 0, 0:1], m_ref[0, 0, 1:2]
    h = _modulate(x_ref[0], gain, shift, scale)
    seg_first = (t == 0) | (t == geo.ctx_tiles)
    seg_last = (t == geo.ctx_tiles - 1) | (t == geo.nt - 1)
    h_prev = jnp.where(seg_first, 0.0, _modulate(xp_ref[0], gain, shift, scale)[7:8])
    h_next = jnp.where(seg_last, 0.0, _modulate(xn_ref[0], gain, shift, scale)[0:1])
    row = lax.broadcasted_iota(jnp.int32, h.shape, 0)
    up = jnp.where(row == 0, h_prev, pltpu.roll(h, 1, axis=0))
    dn = jnp.where(row == tm - 1, h_next, pltpu.roll(h, tm - 1, axis=0))
    dx = 0.5 * (up + dn) - h

    def mixed(n):
        return (h + dx * mix_ref[n:n + 1]).astype(bf16)

    r = _dot(mixed(0), wrkv_ref[0])
    k = _dot(mixed(1), wrkv_ref[1])
    v = _dot(mixed(2), wrkv_ref[2])
    dw = jnp.tanh(_dot(mixed(3), wdn_ref[:, 0:LANES])).astype(bf16)
    da = _dot(mixed(4), wdn_ref[:, LANES:2 * LANES]).astype(bf16)
    dg = _sigmoid(_dot(mixed(5), wdn_ref[:, 2 * LANES:3 * LANES])).astype(bf16)
    gate = _dot(dg, gup_ref[...])
    r_ref[0] = r
    v_ref[0] = v
    gate_ref[0] = gate
    kk = k * vec_ref[4:5]
    kk_ref[0] = kk * lax.rsqrt(jnp.maximum(_seg_sum(kk * kk, RWKV_HEAD_SIZE), L2_EPS * L2_EPS))
    kds = []
    for d, (lw_ref, a_ref, kd_ref) in enumerate(((lw0_ref, a0_ref, k0_ref), (lw1_ref, a1_ref, k1_ref))):
        lw_ref[0] = -RWKV_DECAY_SCALE * _sigmoid(vec_ref[d:d + 1] + _dot(dw, wup_ref[d]))
        a = _sigmoid(vec_ref[2 + d:3 + d] + _dot(da, aup_ref[d]))
        a_ref[0] = a
        kds.append(k * (1.0 + (a - 1.0) * vec_ref[5:6]))
        kd_ref[0] = kds[d]
    bg_ref[0] = _seg_sum(r * (0.5 * (kds[0] + kds[1])) * vec_ref[6:7], RWKV_HEAD_SIZE) * v * gate


def _rwkv_chunk_units(units):
    n = SCAN_CHUNK
    hs = RWKV_HEAD_SIZE
    lane = lax.broadcasted_iota(jnp.int32, (n, RWKV_PAIR), 1)
    t_i = lax.broadcasted_iota(jnp.int32, (n, RWKV_PAIR), 0)
    s_i = lane % hs
    head0 = lane < hs
    ri = lax.broadcasted_iota(jnp.int32, (RWKV_PAIR, RWKV_PAIR), 0)
    ci = lax.broadcasted_iota(jnp.int32, (RWKV_PAIR, RWKV_PAIR), 1)
    same_head = ri // hs == ci // hs
    top_rows = ri < hs

    def block_diag(x):
        return jnp.concatenate([jnp.where(head0, x, 0.0), jnp.where(head0, 0.0, x)], axis=0)

    def cross_blocks(b):
        return (t_i // (2 * b) == s_i // (2 * b)) & (t_i // b != s_i // b)

    st = []
    for r, kd, v, kk, a, lw, ht, reverse in units:
        c = _cumsum_rows(lw, reverse)
        tot = c[0:1] if reverse else c[n - 1:n]
        e_neg = jnp.exp(-c)
        dec = jnp.exp(tot)
        kb = jnp.concatenate([kd * e_neg, kk * a * e_neg], axis=0)
        kb_t = kb.T
        kb_sw = pltpu.roll(kb_t, hs, axis=1)
        rhs = jnp.concatenate([jnp.where(same_head, jnp.where(top_rows, kb_t, kb_sw), 0.0),
                               jnp.where(same_head, jnp.where(top_rows, kb_sw, kb_t), 0.0)], axis=1)
        st.append(dict(
            strict=(s_i > t_i) if reverse else (s_i < t_i), incl=(s_i >= t_i) if reverse else (s_i <= t_i),
            a_bar=(-kk * jnp.exp(c - lw)).astype(bf16), r_bar=(r * jnp.exp(c)).astype(bf16), rhs=rhs.astype(bf16),
            v_bd=block_diag(v).astype(bf16), h_t=ht.T.astype(bf16),
            ends=(kb * dec).astype(bf16),
            v=v, ht=ht, dec=dec))
    for s in st:
        g = _dot(jnp.concatenate([s["a_bar"], s["r_bar"]], axis=0), s["rhs"])
        s["a_ak"] = jnp.where(s["strict"], g[0:n, 0:RWKV_PAIR], 0.0).astype(bf16)
        s["nmat"] = jnp.where(s["strict"], g[0:n, RWKV_PAIR:], 0.0)
        s["q_k"] = jnp.where(s["incl"], g[n:, 0:RWKV_PAIR], 0.0).astype(bf16)
        s["q_b"] = jnp.where(s["incl"], g[n:, RWKV_PAIR:], 0.0).astype(bf16)
    for s in st:
        s["inv"] = jnp.where(s_i == t_i, 1.0, jnp.where(cross_blocks(1), s["nmat"], 0.0))
    b = 2
    while 2 * b < n:
        for s in st:
            s["pc"] = _dot(s["inv"].astype(bf16), block_diag(jnp.where(cross_blocks(b), s["nmat"], 0.0)).astype(bf16))
        for s in st:
            s["inv"] = s["inv"] + _dot(s["pc"].astype(bf16), block_diag(s["inv"]).astype(bf16))
        b *= 2
    for s in st:
        x = _dot(jnp.concatenate([s["a_bar"], s["a_ak"]], axis=1), jnp.concatenate([s["h_t"], s["v_bd"]], axis=0))
        s["inv"] = s["inv"].astype(bf16)
        s["w"] = _dot(s["inv"], block_diag(x).astype(bf16))
    for s in st:
        s["cw"] = _dot(jnp.where(cross_blocks(n // 2), s["nmat"], 0.0).astype(bf16), block_diag(s["w"]).astype(bf16))
    for s in st:
        s["u"] = s["w"] + _dot(s["inv"], block_diag(s["cw"]).astype(bf16))
    for s in st:
        s["vu_t"] = jnp.concatenate([s["v"], s["u"]], axis=0).T.astype(bf16)
    for s in st:
        s["y"] = _dot(jnp.concatenate([s["r_bar"], s["q_k"], s["q_b"]], axis=1),
                      jnp.concatenate([s["h_t"], s["v_bd"], block_diag(s["u"]).astype(bf16)], axis=0))
    for s in st:
        s["upd"] = _dot(s["vu_t"], s["ends"])
    return [(s["y"], s["ht"] * s["dec"] + jnp.where(same_head, s["upd"], 0.0)) for s in st]


def _rwkv_scan_kernel(rows, pairs, *refs):
    fwd, bwd = refs[0:6], refs[6:12]
    yf_ref, yb_ref, hf_ref, hb_ref = refs[12:]

    @pl.when(pl.program_id(2) == 0)
    def _():
        hf_ref[...] = jnp.zeros_like(hf_ref)
        hb_ref[...] = jnp.zeros_like(hb_ref)

    units, dests = [], []
    for ins, y_ref, h_ref, reverse in ((fwd, yf_ref, hf_ref, False), (bwd, yb_ref, hb_ref, True)):
        r_ref, v_ref, kk_ref, lw_ref, a_ref, kd_ref = ins
        for row in range(rows):
            for p in range(pairs):
                sl = slice(p * RWKV_PAIR, (p + 1) * RWKV_PAIR)
                slot = row * pairs + p
                units.append((r_ref[row, :, sl], kd_ref[row, :, sl], v_ref[row, :, sl], kk_ref[row, :, sl],
                              a_ref[row, :, sl], lw_ref[row, :, sl], h_ref[slot], reverse))
                dests.append((y_ref, h_ref, row, slot, sl))
    for (y, h_new), (y_ref, h_ref, row, slot, sl) in zip(_rwkv_chunk_units(units), dests):
        y_ref[row, :, sl] = y
        h_ref[slot] = h_new


RWKV_PAIRS_PER_STEP = 8
RWKV_ROWS_PER_STEP = 2


def _rwkv_layer(geo, x, mods, gains, mix, w_rkv, w0, w_down, w_up, a0, a_down, a_up, g_down, g_up, k_k, k_a,
                r_k, ln_w, ln_b, w_o, w_in, w_out, skip_ctx):
    b, t, tm = geo.batch, geo.t, geo.tm
    d = D_MODEL
    rank = w_down.shape[-1]
    assert 2 * rank == LANES and a_down.shape[-1] == rank and g_down.shape[-1] == LANES
    mix8 = jnp.pad(mix, ((0, 2), (0, 0)))
    vec = jnp.stack([w0[0], w0[1], a0[0], a0[1], k_k, k_a, r_k.reshape(d), jnp.zeros_like(k_k)])
    w_dn = jnp.concatenate([w_down[0], w_down[1], a_down[0], a_down[1], g_down], axis=1).astype(bf16)

    def padded_up(w):
        return jnp.zeros((2, LANES, d), f32).at[0, :rank].set(w[0]).at[1, rank:].set(w[1]).astype(bf16)

    halo = 8
    n_halo = t // halo
    per = tm // halo
    x_prev = pl.BlockSpec((1, halo, d), lambda bb, tt: (bb, jnp.maximum(tt * per - 1, 0), 0))
    x_next = pl.BlockSpec((1, halo, d), lambda bb, tt: (bb, jnp.minimum((tt + 1) * per, n_halo - 1), 0))
    out_spec = pl.BlockSpec((1, tm, d), lambda bb, tt: (bb, tt, 0))
    sds = jax.ShapeDtypeStruct((b, t, d), f32)
    r, v, gate, kk, lw0, lw1, a_0, a_1, k0, k1, bonus_g = pl.pallas_call(
        functools.partial(_rwkv_proj_kernel, geo),
        grid=(b, geo.nt),
        in_specs=[geo.x_spec(), x_prev, x_next, geo.mod_spec(), _const_spec((8, d)), _const_spec((8, d)),
                  _const_spec((8, d)), _const_spec((3, d, d)), _const_spec((d, 3 * LANES)),
                  _const_spec((2, LANES, d)), _const_spec((2, LANES, d)), _const_spec((LANES, d))],
        out_specs=[out_spec] * 11,
        out_shape=[sds] * 11,
        compiler_params=_params(2),
        name="rwkv_proj",
    )(x, x, x, mods, gains, mix8, vec, w_rkv.astype(bf16), w_dn, padded_up(w_up), padded_up(a_up),
      g_up.astype(bf16))

    pairs = RWKV_PAIRS_PER_STEP
    rows = RWKV_ROWS_PER_STEP if b % RWKV_ROWS_PER_STEP == 0 else 1
    width = pairs * RWKV_PAIR
    groups = d // width

    def cspec(reverse):
        if reverse:
            return pl.BlockSpec((rows, SCAN_CHUNK, width), lambda bb, gg, i: (bb, geo.bwd_chunk(i), gg))
        return pl.BlockSpec((rows, SCAN_CHUNK, width), lambda bb, gg, i: (bb, i, gg))

    state = pltpu.VMEM((rows * pairs, RWKV_PAIR, RWKV_PAIR), f32)
    y_f, y_b = pl.pallas_call(
        functools.partial(_rwkv_scan_kernel, rows, pairs),
        grid=(b // rows, groups, geo.nc),
        in_specs=[cspec(False)] * 6 + [cspec(True)] * 6,
        out_specs=[cspec(False), cspec(True)],
        out_shape=[sds, sds],
        scratch_shapes=[state, state],
        compiler_params=_params(3),
        name="rwkv_scan",
    )(r, v, kk, lw0, a_0, k0, r, v, kk, lw1, a_1, k1)

    t0 = geo.ctx_tiles if skip_ctx else 0
    post_vec = jnp.stack([ln_w, ln_b] + [jnp.zeros_like(ln_w)] * 6)
    rows = _post_rows(geo, 4)
    mix_specs = [_tile_spec(geo, d, 0, t0, rows)] * 4 + [_const_spec((8, d))]
    return _post_mlp(geo, x, mods, gains, _pre_rwkv, [y_f, y_b, gate, bonus_g, post_vec], mix_specs,
                     w_o.astype(bf16), w_in, w_out, skip_ctx, "rwkv_post", rows)


def _hgrn_lower_bound(lb_param, layer):
    p = jax.nn.softmax(lb_param.astype(f32), axis=0)
    return (jnp.cumsum(p, axis=0) - p[0])[layer]


def kernel(x, c, ctx, c_ctx, w_mod, b_mod, g_pre_mix, g_post_mix, g_pre_mlp, g_post_mlp, w_mlp_in, w_mlp_out, attn_w_qkv, attn_w_o, attn_sink, gla_w_in, gla_w_gate_down, gla_w_gate_up, gla_gate_bias, gla_g_norm, gla_w_o, rwkv_mix, rwkv_w_rkv, rwkv_w0, rwkv_w_down, rwkv_w_up, rwkv_a0, rwkv_a_down, rwkv_a_up, rwkv_g_down, rwkv_g_up, rwkv_k_k, rwkv_k_a, rwkv_r_k, rwkv_ln_w, rwkv_ln_b, rwkv_w_o, hgrn_w_in, hgrn_w_f, hgrn_lb, hgrn_g_norm, hgrn_w_o):
    depth = w_mod.shape[0]
    geo = _Geom(x.shape[0], ctx.shape[1], x.shape[1])
    mods_all = _mod_vectors(c, c_ctx, w_mod, b_mod)
    xs = (ctx, x)
    for i in range(depth):
        kind, j = i % 4, i // 4
        skip_ctx = i == depth - 1
        mods = mods_all[i]
        gains = jnp.stack([g_pre_mix[i], g_post_mix[i], g_pre_mlp[i], g_post_mlp[i]] + [jnp.zeros_like(g_pre_mix[i])] * 4)
        w_in, w_out = w_mlp_in[i].astype(bf16), w_mlp_out[i].astype(bf16)
        if kind == 0:
            xs = _attn_layer(geo, xs, mods, gains, attn_w_qkv[j], attn_w_o[j], attn_sink[j], w_in, w_out, skip_ctx)
        elif kind == 1:
            xs = _gla_layer(geo, xs, mods, gains, gla_w_in[j], gla_w_gate_down[j], gla_w_gate_up[j],
                            gla_gate_bias[j], gla_g_norm[j], gla_w_o[j], w_in, w_out, skip_ctx)
        elif kind == 2:
            xs = _rwkv_layer(geo, xs, mods, gains, rwkv_mix[j], rwkv_w_rkv[j], rwkv_w0[j], rwkv_w_down[j],
                             rwkv_w_up[j], rwkv_a0[j], rwkv_a_down[j], rwkv_a_up[j], rwkv_g_down[j], rwkv_g_up[j],
                             rwkv_k_k[j], rwkv_k_a[j], rwkv_r_k[j], rwkv_ln_w[j], rwkv_ln_b[j], rwkv_w_o[j],
                             w_in, w_out, skip_ctx)
        else:
            xs = _hgrn_layer(geo, xs, mods, gains, hgrn_w_in[j], hgrn_w_f[j], _hgrn_lower_bound(hgrn_lb, i),
                             hgrn_g_norm[j], hgrn_w_o[j], w_in, w_out, skip_ctx)
        if skip_ctx:
            return xs
    return xs[:, geo.n_ctx:]
```

```python
import functools

import jax
import jax.numpy as jnp
from jax import lax
from jax.experimental import pallas as pl
from jax.experimental.pallas import tpu as pltpu

f32 = jnp.float32
bf16 = jnp.bfloat16

D_MODEL = 1024
N_MOD = 6
MLP_HIDDEN = 4 * D_MODEL
NORM_EPS = 1e-6
NEG_INF = -1e30
GRID_W = 64

ATTN_HEADS = 16
ATTN_KV_HEADS = 4
ATTN_GROUP = ATTN_HEADS // ATTN_KV_HEADS
HEAD_DIM = 64
WINDOW = 128
ATTN_BLOCK = 128
ROPE_BASE = 10000.0
ROPE_AXIS_DIM = HEAD_DIM // 2
ROPE_FREQS = ROPE_AXIS_DIM // 2

GLA_HEADS = 4
GLA_KEY_DIM = D_MODEL // 2
GLA_DK = GLA_KEY_DIM // GLA_HEADS
GLA_DV = D_MODEL // GLA_HEADS
GLA_GATE_RANK = 16
GLA_TAU = 16.0
SCAN_CHUNK = 64

RWKV_HEAD_SIZE = 64
RWKV_LN_EPS = 64e-5
L2_EPS = 1e-12
RWKV_PAIR = 2 * RWKV_HEAD_SIZE
RWKV_DECAY_SCALE = 0.6065306597126334

HGRN_EXPAND = 128
HGRN_HEADS = D_MODEL // HGRN_EXPAND

LANES = 128
MOD_ROWS = 8
VMEM_LIMIT = 56 * 1024 * 1024

NT = (((1,), (1,)), ((), ()))
TN = (((0,), (0,)), ((), ()))


def _params(n_grid):
    return pltpu.CompilerParams(dimension_semantics=("arbitrary",) * n_grid, vmem_limit_bytes=VMEM_LIMIT)


def _const_spec(shape):
    nd = len(shape)
    return pl.BlockSpec(shape, lambda *_: (0,) * nd, pipeline_mode=pl.Buffered(1))


def _dot(a, b):
    return jnp.dot(a, b, preferred_element_type=f32)


def _sigmoid(x):
    return 0.5 * jnp.tanh(0.5 * x) + 0.5


def _silu(x):
    return x * _sigmoid(x)


def _rms(x, gain):
    return x * lax.rsqrt(jnp.mean(x * x, axis=-1, keepdims=True) + NORM_EPS) * gain


def _modulate(x, gain, shift, scale):
    return _rms(x, gain) * (1.0 + scale) + shift


def _seg_sum(x, seg):
    r = lax.broadcasted_iota(jnp.int32, (LANES, LANES), 0) // seg
    c = lax.broadcasted_iota(jnp.int32, (LANES, LANES), 1) // seg
    ones_bd = (r == c).astype(bf16)
    hi = x.astype(bf16)
    lo = (x - hi.astype(f32)).astype(bf16)
    outs = []
    for j in range(x.shape[1] // LANES):
        sl = slice(j * LANES, (j + 1) * LANES)
        outs.append(_dot(hi[:, sl], ones_bd) + _dot(lo[:, sl], ones_bd))
    return jnp.concatenate(outs, axis=1)


def _cumsum_rows(g, reverse):
    n = g.shape[0]
    ri = lax.broadcasted_iota(jnp.int32, (n, n), 0)
    ci = lax.broadcasted_iota(jnp.int32, (n, n), 1)
    tri = ((ci >= ri) if reverse else (ci <= ri)).astype(bf16)
    hi = g.astype(bf16)
    lo = (g - hi.astype(f32)).astype(bf16)
    return _dot(tri, hi) + _dot(tri, lo)


def _mod_kernel(c_ref, w_ref, b_ref, o_ref):
    o_ref[0] = _dot(_silu(c_ref[...]).astype(bf16), w_ref[0]) + b_ref[0]


def _mod_vectors(c, c_ctx, w_mod, b_mod):
    depth, d, _ = w_mod.shape
    batch = c.shape[0]
    rows = -(-(batch + 1) // 8) * 8
    cc = jnp.zeros((rows, d), f32).at[:batch].set(c).at[batch].set(c_ctx)
    out = pl.pallas_call(
        _mod_kernel,
        grid=(depth, N_MOD),
        in_specs=[
            pl.BlockSpec((rows, d), lambda i, j: (0, 0)),
            pl.BlockSpec((1, d, d), lambda i, j: (i, 0, j)),
            pl.BlockSpec((1, 1, d), lambda i, j: (i, 0, j)),
        ],
        out_specs=pl.BlockSpec((1, rows, d), lambda i, j: (i, 0, j)),
        out_shape=jax.ShapeDtypeStruct((depth, rows, N_MOD * d), f32),
        compiler_params=_params(2),
        name="mod_vectors",
    )(cc, w_mod.astype(bf16), b_mod.reshape(depth, 1, N_MOD * d))
    lat = out[:, :batch].reshape(depth, batch, 1, N_MOD, d)
    con = jnp.broadcast_to(out[:, batch].reshape(depth, 1, 1, N_MOD, d), lat.shape)
    mods = jnp.concatenate([con, lat], axis=2)
    return jnp.pad(mods, ((0, 0), (0, 0), (0, 0), (0, MOD_ROWS - N_MOD), (0, 0)))


class _Geom:
    def __init__(self, batch, n_ctx, n_lat):
        self.batch, self.n_ctx, self.n_lat = batch, n_ctx, n_lat
        self.t = n_ctx + n_lat
        self.tm = min(256, n_ctx)
        assert n_ctx % self.tm == 0 and n_lat % self.tm == 0
        assert n_ctx % ATTN_BLOCK == 0 and n_lat % ATTN_BLOCK == 0 and n_lat % GRID_W == 0
        self.nt = self.t // self.tm
        self.ctx_tiles = n_ctx // self.tm
        self.nc = self.t // SCAN_CHUNK
        self.ctx_chunks = n_ctx // SCAN_CHUNK

    def x_spec(self, t0=0, rows=1):
        return pl.BlockSpec((rows, self.tm, D_MODEL), lambda b, t: (b, t + t0, 0))

    def x_in(self, x, t0=0, rows=1):
        if not isinstance(x, tuple):
            return [x], [self.x_spec(t0, rows)]
        ct = self.ctx_tiles
        block = (rows, self.tm, D_MODEL)
        return list(x), [pl.BlockSpec(block, lambda b, t: (b, jnp.minimum(t + t0, ct - 1), 0)),
                         pl.BlockSpec(block, lambda b, t: (b, jnp.maximum(t + t0 - ct, 0), 0))]

    def mod_spec(self, t0=0, rows=1):
        ct = self.ctx_tiles
        return pl.BlockSpec((rows, 1, MOD_ROWS, D_MODEL), lambda b, t: (b, ((t + t0) >= ct).astype(jnp.int32), 0, 0))

    def bwd_chunk(self, i):
        cc = self.ctx_chunks
        return jnp.where(i < cc, cc - 1 - i, self.nc - 1 + cc - i)


LIN_ROWS_PER_STEP = 2


def _lin_kernel(x_ref, m_ref, g_ref, w_ref, o_ref):
    rows, tm = x_ref.shape[0], x_ref.shape[1]
    h = [_modulate(x_ref[r], g_ref[0:1], m_ref[r, 0, 0:1], m_ref[r, 0, 1:2]).astype(bf16) for r in range(rows)]
    z = _dot(jnp.concatenate(h, axis=0), w_ref[...])
    for r in range(rows):
        o_ref[r] = z[r * tm:(r + 1) * tm]


def _mod_linear(geo, x, mods, gains, w, name):
    n = w.shape[1]
    rows = LIN_ROWS_PER_STEP if geo.batch % LIN_ROWS_PER_STEP == 0 else 1
    return pl.pallas_call(
        _lin_kernel,
        grid=(geo.batch // rows, geo.nt),
        in_specs=[geo.x_spec(0, rows), geo.mod_spec(0, rows), _const_spec((8, D_MODEL)), _const_spec((D_MODEL, n))],
        out_specs=pl.BlockSpec((rows, geo.tm, n), lambda b, t: (b, t, 0)),
        out_shape=jax.ShapeDtypeStruct((geo.batch, geo.t, n), f32),
        compiler_params=_params(2),
        name=name,
    )(x, mods, gains, w)


def _read_tile(x_refs, row, is_ctx):
    if len(x_refs) == 1:
        return x_refs[0][row]
    return jnp.where(is_ctx, x_refs[0][row], x_refs[1][row])


def _head_rms(o, gain, width):
    outs = []
    for h in range(o.shape[1] // width):
        oh = o[:, h * width:(h + 1) * width]
        outs.append(oh * lax.rsqrt(jnp.mean(oh * oh, axis=-1, keepdims=True) + NORM_EPS) * gain)
    return jnp.concatenate(outs, axis=1)


def _pre_attn(refs, row):
    (o_ref,) = refs
    return o_ref[row]


def _pre_scan(width, refs, row):
    of_ref, ob_ref, gate_ref, gn_ref = refs
    o = _head_rms(of_ref[row] + ob_ref[row], gn_ref[...], width)
    return (o * _silu(gate_ref[row])).astype(bf16)


def _pre_rwkv(refs, row):
    yf_ref, yb_ref, g_ref, bg_ref, vec_ref = refs
    y = yf_ref[row] + yb_ref[row]
    inv_n = 1.0 / RWKV_HEAD_SIZE
    mu = _seg_sum(y, RWKV_HEAD_SIZE) * inv_n
    dlt = y - mu
    var = _seg_sum(dlt * dlt, RWKV_HEAD_SIZE) * inv_n
    yn = dlt * lax.rsqrt(var + RWKV_LN_EPS) * vec_ref[0:1] + vec_ref[1:2]
    return (yn * g_ref[row] + bg_ref[row]).astype(bf16)


def _post_kernel(pre, n_x, n_mix, first_ctx_tiles, *refs):
    x_refs, (m_ref, g_ref), mix_refs = refs[:n_x], refs[n_x:n_x + 2], refs[n_x + 2:n_x + 2 + n_mix]
    wo_ref, win_ref, wout_ref, out_ref = refs[n_x + 2 + n_mix:]
    rows = range(out_ref.shape[0])
    is_ctx = pl.program_id(1) < first_ctx_tiles
    o = [pre(mix_refs, r) for r in rows]
    y = [_dot(o[r], wo_ref[...]) for r in rows]
    x1 = [_read_tile(x_refs, r, is_ctx) + m_ref[r, 0, 2:3] * _rms(y[r], g_ref[1:2]) for r in rows]
    h2 = [_modulate(x1[r], g_ref[2:3], m_ref[r, 0, 3:4], m_ref[r, 0, 4:5]).astype(bf16) for r in rows]
    us = [[] for _ in rows]
    for c in range(MLP_HIDDEN // D_MODEL):
        cols = slice(c * D_MODEL, (c + 1) * D_MODEL)
        for r in rows:
            us[r].append(jnp.square(jnp.maximum(_dot(h2[r], win_ref[:, cols]), 0.0)).astype(bf16))
    acc = [_dot(jnp.concatenate(us[r], axis=1), wout_ref[...]) for r in rows]
    for r in rows:
        out_ref[r] = x1[r] + m_ref[r, 0, 5:6] * _rms(acc[r], g_ref[3:4])


POST_ROWS_PER_STEP = 2


def _post_rows(geo, n_tile_inputs):
    tile = geo.tm * D_MODEL * 4
    weights = 2 * (D_MODEL * D_MODEL + 2 * D_MODEL * MLP_HIDDEN)
    for rows in (POST_ROWS_PER_STEP, 1):
        streamed = 2 * (n_tile_inputs + 2) * rows * tile
        live = 6 * rows * tile
        if geo.batch % rows == 0 and weights + streamed + live <= VMEM_LIMIT:
            return rows
    return 1


def _post_mlp(geo, x, mods, gains, pre, mix_args, mix_specs, w_o, w_in, w_out, skip_ctx, name, rows):
    t0 = geo.ctx_tiles if skip_ctx else 0
    x_args, x_specs = geo.x_in(x, t0, rows)
    return pl.pallas_call(
        functools.partial(_post_kernel, pre, len(x_args), len(mix_args), geo.ctx_tiles - t0),
        grid=(geo.batch // rows, geo.nt - t0),
        in_specs=x_specs + [geo.mod_spec(t0, rows), _const_spec((8, D_MODEL))] + mix_specs + [
            _const_spec((D_MODEL, D_MODEL)), _const_spec((D_MODEL, MLP_HIDDEN)), _const_spec((MLP_HIDDEN, D_MODEL))],
        out_specs=pl.BlockSpec((rows, geo.tm, D_MODEL), lambda b, t: (b, t, 0)),
        out_shape=jax.ShapeDtypeStruct((geo.batch, geo.t - t0 * geo.tm, D_MODEL), f32),
        compiler_params=_params(2),
        name=name,
    )(*x_args, mods, gains, *mix_args, w_o, w_in, w_out)


def _tile_spec(geo, width, col, t0, rows):
    return pl.BlockSpec((rows, geo.tm, width), lambda b, t: (b, t + t0, col))


def _rope(x, cos, sin):
    w = x.shape[1]
    reps = w // LANES
    cw = jnp.concatenate([cos] * reps, axis=1)
    sw = jnp.concatenate([sin] * reps, axis=1)
    lane = lax.broadcasted_iota(jnp.int32, x.shape, 1)
    first = (lane % ROPE_AXIS_DIM) < ROPE_FREQS
    partner = jnp.where(first, pltpu.roll(x, w - ROPE_FREQS, axis=1), pltpu.roll(x, ROPE_FREQS, axis=1))
    return x * cw + partner * sw


def _attn_proj_kernel(ctx_tiles, *refs):
    x_refs, (m_ref, g_ref, cos_ref, sin_ref, wq_ref, wk_ref, wv_ref, q_ref, k_ref, v_ref) = refs[:-10], refs[-10:]
    rows = range(q_ref.shape[0])
    is_ctx = pl.program_id(1) < ctx_tiles
    cos, sin = cos_ref[...], sin_ref[...]
    h = [_modulate(_read_tile(x_refs, r, is_ctx), g_ref[0:1], m_ref[r, 0, 0:1], m_ref[r, 0, 1:2]).astype(bf16)
         for r in rows]
    q = [_dot(h[r], wq_ref[...]) * (HEAD_DIM ** -0.5) for r in rows]
    k = [_dot(h[r], wk_ref[...]) for r in rows]
    v = [_dot(h[r], wv_ref[...]) for r in rows]
    q = [_rope(q[r], cos, sin) for r in rows]
    k = [_rope(k[r], cos, sin) for r in rows]
    for r in rows:
        for hd in range(ATTN_HEADS):
            q_ref[r, hd] = q[r][:, hd * HEAD_DIM:(hd + 1) * HEAD_DIM].astype(bf16)
        for hd in range(ATTN_KV_HEADS):
            k_ref[r, hd] = k[r][:, hd * HEAD_DIM:(hd + 1) * HEAD_DIM].astype(bf16)
            v_ref[r, hd] = v[r][:, hd * HEAD_DIM:(hd + 1) * HEAD_DIM].astype(bf16)


def _attn_core_kernel(geo, sink_ref, q_ref, kp_ref, kc_ref, kn_ref, kx_ref, vp_ref, vc_ref, vn_ref, vx_ref, o_ref):
    qb = pl.program_id(1)
    blk = ATTN_BLOCK
    n_loc = 3 * blk
    n_keys = n_loc + geo.n_ctx
    rows = ATTN_GROUP * blk
    first_lat = geo.n_ctx // blk
    n_blocks = geo.t // blk
    row = lax.broadcasted_iota(jnp.int32, (blk, n_keys), 0)
    col = lax.broadcasted_iota(jnp.int32, (blk, n_keys), 1)
    kblk = qb - 1 + col // blk
    ok_local = ((jnp.abs(col - blk - row) <= WINDOW) & (qb >= first_lat) & (kblk >= first_lat) & (kblk < n_blocks))
    ok = (col >= n_loc) | ok_local
    scores, vals = [], []
    for j in range(ATTN_KV_HEADS):
        q4 = q_ref[0, ATTN_GROUP * j:ATTN_GROUP * (j + 1)].reshape(rows, HEAD_DIM)
        keys = jnp.concatenate([kp_ref[0, j], kc_ref[0, j], kn_ref[0, j], kx_ref[0, j]], axis=0)
        vals.append(jnp.concatenate([vp_ref[0, j], vc_ref[0, j], vn_ref[0, j], vx_ref[0, j]], axis=0))
        scores.append(lax.dot_general(q4, keys, NT, preferred_element_type=f32))
    heads = [(j, g) for j in range(ATTN_KV_HEADS) for g in range(ATTN_GROUP)]
    sg = [jnp.where(ok, scores[j][g * blk:(g + 1) * blk], NEG_INF) for j, g in heads]
    sk = [sink_ref[ATTN_GROUP * j + g] for j, g in heads]
    m = [jnp.maximum(jnp.max(s, axis=-1, keepdims=True), k) for s, k in zip(sg, sk)]
    e = [jnp.exp(s - mm) for s, mm in zip(sg, m)]
    inv = [1.0 / (jnp.sum(ee, axis=-1, keepdims=True) + jnp.exp(k - mm)) for ee, k, mm in zip(e, sk, m)]
    p = [ee.astype(bf16) for ee in e]
    outs = []
    for j in range(ATTN_KV_HEADS):
        o4 = _dot(jnp.concatenate(p[ATTN_GROUP * j:ATTN_GROUP * (j + 1)], axis=0), vals[j])
        outs += [o4[g * blk:(g + 1) * blk] * inv[ATTN_GROUP * j + g] for g in range(ATTN_GROUP)]
    o_ref[0] = jnp.concatenate(outs, axis=1).astype(bf16)


def _rope_tables(geo):
    inv_freq = ROPE_BASE ** (-jnp.arange(ROPE_FREQS, dtype=f32) * 2.0 / ROPE_AXIS_DIM)
    pos = jnp.arange(geo.n_lat)
    row = (pos // GRID_W).astype(f32)
    col = (pos % GRID_W).astype(f32)
    ang = jnp.stack([row[:, None] * inv_freq, col[:, None] * inv_freq], axis=1)
    cos = jnp.cos(ang)
    sin = jnp.sin(ang)
    cos_h = jnp.concatenate([cos, cos], axis=2).reshape(geo.n_lat, HEAD_DIM)
    sin_h = jnp.concatenate([-sin, sin], axis=2).reshape(geo.n_lat, HEAD_DIM)
    cos_t = jnp.concatenate([jnp.ones((geo.n_ctx, HEAD_DIM), f32), cos_h], axis=0)
    sin_t = jnp.concatenate([jnp.zeros((geo.n_ctx, HEAD_DIM), f32), sin_h], axis=0)
    return jnp.tile(cos_t, (1, 2)), jnp.tile(sin_t, (1, 2))


def _attn_layer(geo, x, mods, gains, w_qkv, w_o, sink, w_in, w_out, skip_ctx):
    b, t, tm = geo.batch, geo.t, geo.tm
    q_cols = ATTN_HEADS * HEAD_DIM
    kv_cols = ATTN_KV_HEADS * HEAD_DIM
    wb = w_qkv.astype(bf16)
    cos_t, sin_t = _rope_tables(geo)
    tab_spec = pl.BlockSpec((tm, LANES), lambda bb, tt: (tt, 0))
    rows = LIN_ROWS_PER_STEP if b % LIN_ROWS_PER_STEP == 0 else 1
    x_args, x_specs = geo.x_in(x, 0, rows)
    q, k, v = pl.pallas_call(
        functools.partial(_attn_proj_kernel, geo.ctx_tiles),
        grid=(b // rows, geo.nt),
        in_specs=x_specs + [geo.mod_spec(0, rows), _const_spec((8, D_MODEL)), tab_spec, tab_spec,
                  _const_spec((D_MODEL, q_cols)), _const_spec((D_MODEL, kv_cols)), _const_spec((D_MODEL, kv_cols))],
        out_specs=[pl.BlockSpec((rows, ATTN_HEADS, tm, HEAD_DIM), lambda bb, tt: (bb, 0, tt, 0)),
                   pl.BlockSpec((rows, ATTN_KV_HEADS, tm, HEAD_DIM), lambda bb, tt: (bb, 0, tt, 0)),
                   pl.BlockSpec((rows, ATTN_KV_HEADS, tm, HEAD_DIM), lambda bb, tt: (bb, 0, tt, 0))],
        out_shape=[jax.ShapeDtypeStruct((b, ATTN_HEADS, t, HEAD_DIM), bf16),
                   jax.ShapeDtypeStruct((b, ATTN_KV_HEADS, t, HEAD_DIM), bf16),
                   jax.ShapeDtypeStruct((b, ATTN_KV_HEADS, t, HEAD_DIM), bf16)],
        compiler_params=_params(2),
        name="attn_proj",
    )(*x_args, mods, gains, cos_t, sin_t, wb[:, :q_cols], wb[:, q_cols:q_cols + kv_cols], wb[:, q_cols + kv_cols:])

    blk = ATTN_BLOCK
    n_blocks = t // blk

    def kv_spec(off):
        return pl.BlockSpec((1, ATTN_KV_HEADS, blk, HEAD_DIM),
                            lambda bb, qb: (bb, 0, jnp.clip(qb + off, 0, n_blocks - 1), 0))

    ctx_spec = pl.BlockSpec((1, ATTN_KV_HEADS, geo.n_ctx, HEAD_DIM), lambda bb, qb: (bb, 0, 0, 0))
    o = pl.pallas_call(
        functools.partial(_attn_core_kernel, geo),
        grid=(b, n_blocks),
        in_specs=[pl.BlockSpec(memory_space=pltpu.SMEM),
                  pl.BlockSpec((1, ATTN_HEADS, blk, HEAD_DIM), lambda bb, qb: (bb, 0, qb, 0)),
                  kv_spec(-1), kv_spec(0), kv_spec(1), ctx_spec,
                  kv_spec(-1), kv_spec(0), kv_spec(1), ctx_spec],
        out_specs=pl.BlockSpec((1, blk, q_cols), lambda bb, qb: (bb, qb, 0)),
        out_shape=jax.ShapeDtypeStruct((b, t, q_cols), bf16),
        compiler_params=_params(2),
        name="attn_core",
    )(sink, q, k, k, k, k, v, v, v, v)

    t0 = geo.ctx_tiles if skip_ctx else 0
    rows = _post_rows(geo, 1)
    return _post_mlp(geo, x, mods, gains, _pre_attn, [o], [_tile_spec(geo, D_MODEL, 0, t0, rows)],
                     w_o.astype(bf16), w_in, w_out, skip_ctx, "attn_post", rows)


def _gated_prep(entries, heads, dk, dv):
    sums = [_cumsum_rows(g, reverse) for _, _, _, g, _, _, reverse in entries]
    units = []
    for (q, k, v, g, st_ref, base, reverse), b in zip(entries, sums):
        n = q.shape[0]
        btot = b[0:1] if reverse else b[n - 1:n]
        q_dec = (q * jnp.exp(b)).astype(bf16)
        k_inv32 = k * jnp.exp(-b)
        k_inv = k_inv32.astype(bf16)
        dec = jnp.exp(btot)
        k_end = (k_inv32 * dec).astype(bf16)
        vb = v.astype(bf16)
        ri = lax.broadcasted_iota(jnp.int32, (n, n), 0)
        ci = lax.broadcasted_iota(jnp.int32, (n, n), 1)
        tri = (ci >= ri) if reverse else (ci <= ri)
        for h in range(heads):
            ks = slice(h * dk, (h + 1) * dk)
            vs = slice(h * dv, (h + 1) * dv)
            units.append(dict(q=q_dec[:, ks], ki=k_inv[:, ks], ke=k_end[:, ks], v=vb[:, vs], dec=dec[:, ks],
                              tri=tri, st=st_ref[base + h], ref=st_ref, h=base + h))
    return units


def _gated_matmuls(units, heads):
    for u in units:
        a = lax.dot_general(u["q"], u["ki"], NT, preferred_element_type=f32)
        u["a"] = jnp.where(u["tri"], a, 0.0).astype(bf16)
        u["qs"] = lax.dot_general(u["q"], u["st"].astype(bf16), NT, preferred_element_type=f32)
    for u in units:
        u["o"] = _dot(u["a"], u["v"]) + u["qs"]
        u["new"] = u["st"] * u["dec"] + lax.dot_general(u["v"], u["ke"], TN, preferred_element_type=f32)
    for u in units:
        u["ref"][u["h"]] = u["new"]
    return [jnp.concatenate([u["o"] for u in units[i:i + heads]], axis=1) for i in range(0, len(units), heads)]


def _gated_chunk(groups, heads, dk, dv, staggered):
    def prep(group):
        return _gated_prep([thunk() for thunk in group], heads, dk, dv)

    if not staggered:
        units = prep([thunk for group in groups for thunk in group])
        per_row = len(groups[0])
        outs = _gated_matmuls(units, heads)
        return [outs[i:i + per_row] for i in range(0, len(outs), per_row)]
    outs = []
    ready = prep(groups[0])
    for i in range(len(groups)):
        nxt = prep(groups[i + 1]) if i + 1 < len(groups) else None
        outs.append(_gated_matmuls(ready, heads))
        ready = nxt
    return outs


def _gla_gate(zd, wu_ref, bias_ref, d):
    zg = _dot(zd.astype(bf16), wu_ref[d]) + bias_ref[d:d + 1]
    return (jnp.minimum(zg, 0.0) - jnp.log(1.0 + jnp.exp(-jnp.abs(zg)))) * (1.0 / GLA_TAU)


def _gla_scan_kernel(qkf_ref, vf_ref, zdf_ref, qkb_ref, vb_ref, zdb_ref, wu_ref, bias_ref,
                     of_ref, ob_ref, sf_ref, sb_ref):
    @pl.when(pl.program_id(1) == 0)
    def _():
        sf_ref[...] = jnp.zeros_like(sf_ref)
        sb_ref[...] = jnp.zeros_like(sb_ref)

    def entry(row, d, qk_ref, v_ref, zd_ref, s_ref):
        qk = qk_ref[row]
        return (qk[:, :GLA_KEY_DIM] * (GLA_DK ** -0.5), qk[:, GLA_KEY_DIM:], v_ref[row],
                _gla_gate(zd_ref[row], wu_ref, bias_ref, d), s_ref, row * GLA_HEADS, d == 1)

    groups = [[functools.partial(entry, row, 0, qkf_ref, vf_ref, zdf_ref, sf_ref),
               functools.partial(entry, row, 1, qkb_ref, vb_ref, zdb_ref, sb_ref)]
              for row in range(qkf_ref.shape[0])]
    for row, (o_f, o_b) in enumerate(_gated_chunk(groups, GLA_HEADS, GLA_DK, GLA_DV, staggered=False)):
        of_ref[row] = o_f
        ob_ref[row] = o_b


def _hgrn_scan_kernel(qf_ref, if_ref, zff_ref, qb_ref, ib_ref, zfb_ref, lb_ref, of_ref, ob_ref, sf_ref, sb_ref):
    @pl.when(pl.program_id(1) == 0)
    def _():
        sf_ref[...] = jnp.zeros_like(sf_ref)
        sb_ref[...] = jnp.zeros_like(sb_ref)

    lb = lb_ref[...]
    def entry(row, d, q_ref, i_ref, zf_ref, s_ref):
        f = lb + (1.0 - lb) * _sigmoid(zf_ref[row])
        return (_silu(q_ref[row]), 1.0 - f, i_ref[row], jnp.log(f), s_ref, row * HGRN_HEADS, d == 1)

    groups = [[functools.partial(entry, row, 0, qf_ref, if_ref, zff_ref, sf_ref),
               functools.partial(entry, row, 1, qb_ref, ib_ref, zfb_ref, sb_ref)]
              for row in range(qf_ref.shape[0])]
    for row, (o_f, o_b) in enumerate(_gated_chunk(groups, HGRN_HEADS, HGRN_EXPAND, HGRN_EXPAND, staggered=True)):
        of_ref[row] = o_f
        ob_ref[row] = o_b


SCAN_ROWS_PER_STEP = 8


def _scan_rows(geo):
    return SCAN_ROWS_PER_STEP if geo.batch % SCAN_ROWS_PER_STEP == 0 else 1


def _chunk_spec(geo, width, col, reverse):
    rows = _scan_rows(geo)
    if reverse:
        return pl.BlockSpec((rows, SCAN_CHUNK, width), lambda b, i: (b, geo.bwd_chunk(i), col))
    return pl.BlockSpec((rows, SCAN_CHUNK, width), lambda b, i: (b, i, col))


def _scan_call(geo, body, args, specs, heads, dk, dv, name):
    out_sds = jax.ShapeDtypeStruct((geo.batch, geo.t, heads * dv), f32)
    rows = _scan_rows(geo)
    state = pltpu.VMEM((rows * heads, dv, dk), f32)
    return pl.pallas_call(
        body,
        grid=(geo.batch // rows, geo.nc),
        in_specs=specs,
        out_specs=[_chunk_spec(geo, heads * dv, 0, False), _chunk_spec(geo, heads * dv, 0, True)],
        out_shape=[out_sds, out_sds],
        scratch_shapes=[state, state],
        compiler_params=_params(2),
        name=name,
    )(*args)


def _gla_layer(geo, x, mods, gains, w_in_p, w_gd, w_gu, g_bias, g_norm, w_o, w_in, w_out, skip_ctx):
    r = GLA_GATE_RANK
    n_z = 2 * GLA_KEY_DIM + 2 * D_MODEL
    w_all = jnp.concatenate([w_in_p, w_gd[0], w_gd[1], jnp.zeros((D_MODEL, LANES - 2 * r), f32)], axis=1).astype(bf16)
    z = _mod_linear(geo, x, mods, gains, w_all, "gla_proj")
    wu = jnp.zeros((2, LANES, GLA_KEY_DIM), f32).at[0, :r].set(w_gu[0]).at[1, r:2 * r].set(w_gu[1]).astype(bf16)
    zd_col = n_z // LANES
    specs = []
    for rev in (False, True):
        specs += [_chunk_spec(geo, 2 * GLA_KEY_DIM, 0, rev), _chunk_spec(geo, D_MODEL, 1, rev),
                  _chunk_spec(geo, LANES, zd_col, rev)]
    specs += [_const_spec((2, LANES, GLA_KEY_DIM)), _const_spec((2, GLA_KEY_DIM))]
    o_f, o_b = _scan_call(geo, _gla_scan_kernel, [z, z, z, z, z, z, wu, g_bias], specs,
                          GLA_HEADS, GLA_DK, GLA_DV, "gla_scan")
    t0 = geo.ctx_tiles if skip_ctx else 0
    rows = _post_rows(geo, 3)
    mix_specs = [_tile_spec(geo, D_MODEL, 0, t0, rows), _tile_spec(geo, D_MODEL, 0, t0, rows),
                 _tile_spec(geo, D_MODEL, 2, t0, rows), _const_spec((1, GLA_DV))]
    return _post_mlp(geo, x, mods, gains, functools.partial(_pre_scan, GLA_DV),
                     [o_f, o_b, z, g_norm.reshape(1, GLA_DV)], mix_specs,
                     w_o.astype(bf16), w_in, w_out, skip_ctx, "gla_post", rows)


def _hgrn_layer(geo, x, mods, gains, w_in_p, w_f, lower_bound, g_norm, w_o, w_in, w_out, skip_ctx):
    w_all = jnp.concatenate([w_in_p, w_f[0], w_f[1]], axis=1).astype(bf16)
    z = _mod_linear(geo, x, mods, gains, w_all, "hgrn_proj")
    specs = []
    for rev in (False, True):
        specs += [_chunk_spec(geo, D_MODEL, 0, rev), _chunk_spec(geo, D_MODEL, 1, rev),
                  _chunk_spec(geo, D_MODEL, 4 if rev else 3, rev)]
    specs += [_const_spec((1, D_MODEL))]
    o_f, o_b = _scan_call(geo, _hgrn_scan_kernel, [z, z, z, z, z, z, lower_bound.reshape(1, D_MODEL)], specs,
                          HGRN_HEADS, HGRN_EXPAND, HGRN_EXPAND, "hgrn_scan")
    t0 = geo.ctx_tiles if skip_ctx else 0
    rows = _post_rows(geo, 3)
    mix_specs = [_tile_spec(geo, D_MODEL, 0, t0, rows), _tile_spec(geo, D_MODEL, 0, t0, rows),
                 _tile_spec(geo, D_MODEL, 2, t0, rows), _const_spec((1, HGRN_EXPAND))]
    return _post_mlp(geo, x, mods, gains, functools.partial(_pre_scan, HGRN_EXPAND),
                     [o_f, o_b, z, g_norm.reshape(1, HGRN_EXPAND)], mix_specs,
                     w_o.astype(bf16), w_in, w_out, skip_ctx, "hgrn_post", rows)


def _rwkv_proj_kernel(geo, x_ref, xp_ref, xn_ref, m_ref, g_ref, mix_ref, vec_ref, wrkv_ref, wdn_ref, wup_ref,
                      aup_ref, gup_ref, r_ref, v_ref, gate_ref, kk_ref, lw0_ref, lw1_ref, a0_ref, a1_ref,
                      k0_ref, k1_ref, bg_ref):
    t = pl.program_id(1)
    tm = geo.tm
    gain, shift, scale = g_ref[0:1], m_ref[0, 0, 0:1], m_ref[0, 0, 1:2]
    h = _modulate(x_ref[0], gain, shift, scale)
    seg_first = (t == 0) | (t == geo.ctx_tiles)
    seg_last = (t == geo.ctx_tiles - 1) | (t == geo.nt - 1)
    h_prev = jnp.where(seg_first, 0.0, _modulate(xp_ref[0], gain, shift, scale)[7:8])
    h_next = jnp.where(seg_last, 0.0, _modulate(xn_ref[0], gain, shift, scale)[0:1])
    row = lax.broadcasted_iota(jnp.int32, h.shape, 0)
    up = jnp.where(row == 0, h_prev, pltpu.roll(h, 1, axis=0))
    dn = jnp.where(row == tm - 1, h_next, pltpu.roll(h, tm - 1, axis=0))
    dx = 0.5 * (up + dn) - h

    def mixed(n):
        return (h + dx * mix_ref[n:n + 1]).astype(bf16)

    r = _dot(mixed(0), wrkv_ref[0])
    k = _dot(mixed(1), wrkv_ref[1])
    v = _dot(mixed(2), wrkv_ref[2])
    dw = jnp.tanh(_dot(mixed(3), wdn_ref[:, 0:LANES])).astype(bf16)
    da = _dot(mixed(4), wdn_ref[:, LANES:2 * LANES]).astype(bf16)
    dg = _sigmoid(_dot(mixed(5), wdn_ref[:, 2 * LANES:3 * LANES])).astype(bf16)
    gate = _dot(dg, gup_ref[...])
    r_ref[0] = r
    v_ref[0] = v
    gate_ref[0] = gate
    kk = k * vec_ref[4:5]
    kk_ref[0] = kk * lax.rsqrt(jnp.maximum(_seg_sum(kk * kk, RWKV_HEAD_SIZE), L2_EPS * L2_EPS))
    kds = []
    for d, (lw_ref, a_ref, kd_ref) in enumerate(((lw0_ref, a0_ref, k0_ref), (lw1_ref, a1_ref, k1_ref))):
        lw_ref[0] = -RWKV_DECAY_SCALE * _sigmoid(vec_ref[d:d + 1] + _dot(dw, wup_ref[d]))
        a = _sigmoid(vec_ref[2 + d:3 + d] + _dot(da, aup_ref[d]))
        a_ref[0] = a
        kds.append(k * (1.0 + (a - 1.0) * vec_ref[5:6]))
        kd_ref[0] = kds[d]
    bg_ref[0] = _seg_sum(r * (0.5 * (kds[0] + kds[1])) * vec_ref[6:7], RWKV_HEAD_SIZE) * v * gate


def _rwkv_chunk_units(units):
    n = SCAN_CHUNK
    hs = RWKV_HEAD_SIZE
    lane = lax.broadcasted_iota(jnp.int32, (n, RWKV_PAIR), 1)
    t_i = lax.broadcasted_iota(jnp.int32, (n, RWKV_PAIR), 0)
    s_i = lane % hs
    head0 = lane < hs
    ri = lax.broadcasted_iota(jnp.int32, (RWKV_PAIR, RWKV_PAIR), 0)
    ci = lax.broadcasted_iota(jnp.int32, (RWKV_PAIR, RWKV_PAIR), 1)
    same_head = ri // hs == ci // hs
    top_rows = ri < hs

    def block_diag(x):
        return jnp.concatenate([jnp.where(head0, x, 0.0), jnp.where(head0, 0.0, x)], axis=0)

    def cross_blocks(b):
        return (t_i // (2 * b) == s_i // (2 * b)) & (t_i // b != s_i // b)

    st = []
    for r, kd, v, kk, a, lw, ht, reverse in units:
        c = _cumsum_rows(lw, reverse)
        tot = c[0:1] if reverse else c[n - 1:n]
        e_neg = jnp.exp(-c)
        dec = jnp.exp(tot)
        kb = jnp.concatenate([kd * e_neg, kk * a * e_neg], axis=0)
        kb_t = kb.T
        kb_sw = pltpu.roll(kb_t, hs, axis=1)
        rhs = jnp.concatenate([jnp.where(same_head, jnp.where(top_rows, kb_t, kb_sw), 0.0),
                               jnp.where(same_head, jnp.where(top_rows, kb_sw, kb_t), 0.0)], axis=1)
        st.append(dict(
            strict=(s_i > t_i) if reverse else (s_i < t_i), incl=(s_i >= t_i) if reverse else (s_i <= t_i),
            a_bar=(-kk * jnp.exp(c - lw)).astype(bf16), r_bar=(r * jnp.exp(c)).astype(bf16), rhs=rhs.astype(bf16),
            v_bd=block_diag(v).astype(bf16), h_t=ht.T.astype(bf16),
            ends=(kb * dec).astype(bf16),
            v=v, ht=ht, dec=dec))
    for s in st:
        g = _dot(jnp.concatenate([s["a_bar"], s["r_bar"]], axis=0), s["rhs"])
        s["a_ak"] = jnp.where(s["strict"], g[0:n, 0:RWKV_PAIR], 0.0).astype(bf16)
        s["nmat"] = jnp.where(s["strict"], g[0:n, RWKV_PAIR:], 0.0)
        s["q_k"] = jnp.where(s["incl"], g[n:, 0:RWKV_PAIR], 0.0).astype(bf16)
        s["q_b"] = jnp.where(s["incl"], g[n:, RWKV_PAIR:], 0.0).astype(bf16)
    for s in st:
        s["inv"] = jnp.where(s_i == t_i, 1.0, jnp.where(cross_blocks(1), s["nmat"], 0.0))
    b = 2
    while 2 * b < n:
        for s in st:
            s["pc"] = _dot(s["inv"].astype(bf16), block_diag(jnp.where(cross_blocks(b), s["nmat"], 0.0)).astype(bf16))
        for s in st:
            s["inv"] = s["inv"] + _dot(s["pc"].astype(bf16), block_diag(s["inv"]).astype(bf16))
        b *= 2
    for s in st:
        x = _dot(jnp.concatenate([s["a_bar"], s["a_ak"]], axis=1), jnp.concatenate([s["h_t"], s["v_bd"]], axis=0))
        s["inv"] = s["inv"].astype(bf16)
        s["w"] = _dot(s["inv"], block_diag(x).astype(bf16))
    for s in st:
        s["cw"] = _dot(jnp.where(cross_blocks(n // 2), s["nmat"], 0.0).astype(bf16), block_diag(s["w"]).astype(bf16))
    for s in st:
        s["u"] = s["w"] + _dot(s["inv"], block_diag(s["cw"]).astype(bf16))
    for s in st:
        s["vu_t"] = jnp.concatenate([s["v"], s["u"]], axis=0).T.astype(bf16)
    for s in st:
        s["y"] = _dot(jnp.concatenate([s["r_bar"], s["q_k"], s["q_b"]], axis=1),
                      jnp.concatenate([s["h_t"], s["v_bd"], block_diag(s["u"]).astype(bf16)], axis=0))
    for s in st:
        s["upd"] = _dot(s["vu_t"], s["ends"])
    return [(s["y"], s["ht"] * s["dec"] + jnp.where(same_head, s["upd"], 0.0)) for s in st]


def _rwkv_scan_kernel(rows, pairs, *refs):
    fwd, bwd = refs[0:6], refs[6:12]
    yf_ref, yb_ref, hf_ref, hb_ref = refs[12:]

    @pl.when(pl.program_id(2) == 0)
    def _():
        hf_ref[...] = jnp.zeros_like(hf_ref)
        hb_ref[...] = jnp.zeros_like(hb_ref)

    units, dests = [], []
    for ins, y_ref, h_ref, reverse in ((fwd, yf_ref, hf_ref, False), (bwd, yb_ref, hb_ref, True)):
        r_ref, v_ref, kk_ref, lw_ref, a_ref, kd_ref = ins
        for row in range(rows):
            for p in range(pairs):
                sl = slice(p * RWKV_PAIR, (p + 1) * RWKV_PAIR)
                slot = row * pairs + p
                units.append((r_ref[row, :, sl], kd_ref[row, :, sl], v_ref[row, :, sl], kk_ref[row, :, sl],
                              a_ref[row, :, sl], lw_ref[row, :, sl], h_ref[slot], reverse))
                dests.append((y_ref, h_ref, row, slot, sl))
    for (y, h_new), (y_ref, h_ref, row, slot, sl) in zip(_rwkv_chunk_units(units), dests):
        y_ref[row, :, sl] = y
        h_ref[slot] = h_new


RWKV_PAIRS_PER_STEP = 8
RWKV_ROWS_PER_STEP = 2


def _rwkv_layer(geo, x, mods, gains, mix, w_rkv, w0, w_down, w_up, a0, a_down, a_up, g_down, g_up, k_k, k_a,
                r_k, ln_w, ln_b, w_o, w_in, w_out, skip_ctx):
    b, t, tm = geo.batch, geo.t, geo.tm
    d = D_MODEL
    rank = w_down.shape[-1]
    assert 2 * rank == LANES and a_down.shape[-1] == rank and g_down.shape[-1] == LANES
    mix8 = jnp.pad(mix, ((0, 2), (0, 0)))
    vec = jnp.stack([w0[0], w0[1], a0[0], a0[1], k_k, k_a, r_k.reshape(d), jnp.zeros_like(k_k)])
    w_dn = jnp.concatenate([w_down[0], w_down[1], a_down[0], a_down[1], g_down], axis=1).astype(bf16)

    def padded_up(w):
        return jnp.zeros((2, LANES, d), f32).at[0, :rank].set(w[0]).at[1, rank:].set(w[1]).astype(bf16)

    halo = 8
    n_halo = t // halo
    per = tm // halo
    x_prev = pl.BlockSpec((1, halo, d), lambda bb, tt: (bb, jnp.maximum(tt * per - 1, 0), 0))
    x_next = pl.BlockSpec((1, halo, d), lambda bb, tt: (bb, jnp.minimum((tt + 1) * per, n_halo - 1), 0))
    out_spec = pl.BlockSpec((1, tm, d), lambda bb, tt: (bb, tt, 0))
    sds = jax.ShapeDtypeStruct((b, t, d), f32)
    r, v, gate, kk, lw0, lw1, a_0, a_1, k0, k1, bonus_g = pl.pallas_call(
        functools.partial(_rwkv_proj_kernel, geo),
        grid=(b, geo.nt),
        in_specs=[geo.x_spec(), x_prev, x_next, geo.mod_spec(), _const_spec((8, d)), _const_spec((8, d)),
                  _const_spec((8, d)), _const_spec((3, d, d)), _const_spec((d, 3 * LANES)),
                  _const_spec((2, LANES, d)), _const_spec((2, LANES, d)), _const_spec((LANES, d))],
        out_specs=[out_spec] * 11,
        out_shape=[sds] * 11,
        compiler_params=_params(2),
        name="rwkv_proj",
    )(x, x, x, mods, gains, mix8, vec, w_rkv.astype(bf16), w_dn, padded_up(w_up), padded_up(a_up),
      g_up.astype(bf16))

    pairs = RWKV_PAIRS_PER_STEP
    rows = RWKV_ROWS_PER_STEP if b % RWKV_ROWS_PER_STEP == 0 else 1
    width = pairs * RWKV_PAIR
    groups = d // width

    def cspec(reverse):
        if reverse:
            return pl.BlockSpec((rows, SCAN_CHUNK, width), lambda bb, gg, i: (bb, geo.bwd_chunk(i), gg))
        return pl.BlockSpec((rows, SCAN_CHUNK, width), lambda bb, gg, i: (bb, i, gg))

    state = pltpu.VMEM((rows * pairs, RWKV_PAIR, RWKV_PAIR), f32)
    y_f, y_b = pl.pallas_call(
        functools.partial(_rwkv_scan_kernel, rows, pairs),
        grid=(b // rows, groups, geo.nc),
        in_specs=[cspec(False)] * 6 + [cspec(True)] * 6,
        out_specs=[cspec(False), cspec(True)],
        out_shape=[sds, sds],
        scratch_shapes=[state, state],
        compiler_params=_params(3),
        name="rwkv_scan",
    )(r, v, kk, lw0, a_0, k0, r, v, kk, lw1, a_1, k1)

    t0 = geo.ctx_tiles if skip_ctx else 0
    post_vec = jnp.stack([ln_w, ln_b] + [jnp.zeros_like(ln_w)] * 6)
    rows = _post_rows(geo, 4)
    mix_specs = [_tile_spec(geo, d, 0, t0, rows)] * 4 + [_const_spec((8, d))]
    return _post_mlp(geo, x, mods, gains, _pre_rwkv, [y_f, y_b, gate, bonus_g, post_vec], mix_specs,
                     w_o.astype(bf16), w_in, w_out, skip_ctx, "rwkv_post", rows)


def _hgrn_lower_bound(lb_param, layer):
    p = jax.nn.softmax(lb_param.astype(f32), axis=0)
    return (jnp.cumsum(p, axis=0) - p[0])[layer]


def kernel(x, c, ctx, c_ctx, w_mod, b_mod, g_pre_mix, g_post_mix, g_pre_mlp, g_post_mlp, w_mlp_in, w_mlp_out, attn_w_qkv, attn_w_o, attn_sink, gla_w_in, gla_w_gate_down, gla_w_gate_up, gla_gate_bias, gla_g_norm, gla_w_o, rwkv_mix, rwkv_w_rkv, rwkv_w0, rwkv_w_down, rwkv_w_up, rwkv_a0, rwkv_a_down, rwkv_a_up, rwkv_g_down, rwkv_g_up, rwkv_k_k, rwkv_k_a, rwkv_r_k, rwkv_ln_w, rwkv_ln_b, rwkv_w_o, hgrn_w_in, hgrn_w_f, hgrn_lb, hgrn_g_norm, hgrn_w_o):
    depth = w_mod.shape[0]
    geo = _Geom(x.shape[0], ctx.shape[1], x.shape[1])
    mods_all = _mod_vectors(c, c_ctx, w_mod, b_mod)
    xs = (ctx, x)
    for i in range(depth):
        kind, j = i % 4, i // 4
        skip_ctx = i == depth - 1
        mods = mods_all[i]
        gains = jnp.stack([g_pre_mix[i], g_post_mix[i], g_pre_mlp[i], g_post_mlp[i]] + [jnp.zeros_like(g_pre_mix[i])] * 4)
        w_in, w_out = w_mlp_in[i].astype(bf16), w_mlp_out[i].astype(bf16)
        if kind == 0:
            xs = _attn_layer(geo, xs, mods, gains, attn_w_qkv[j], attn_w_o[j], attn_sink[j], w_in, w_out, skip_ctx)
        elif kind == 1:
            xs = _gla_layer(geo, xs, mods, gains, gla_w_in[j], gla_w_gate_down[j], gla_w_gate_up[j],
                            gla_gate_bias[j], gla_g_norm[j], gla_w_o[j], w_in, w_out, skip_ctx)
        elif kind == 2:
            xs = _rwkv_layer(geo, xs, mods, gains, rwkv_mix[j], rwkv_w_rkv[j], rwkv_w0[j], rwkv_w_down[j],
                             rwkv_w_up[j], rwkv_a0[j], rwkv_a_down[j], rwkv_a_up[j], rwkv_g_down[j], rwkv_g_up[j],
                             rwkv_k_k[j], rwkv_k_a[j], rwkv_r_k[j], rwkv_ln_w[j], rwkv_ln_b[j], rwkv_w_o[j],
                             w_in, w_out, skip_ctx)
        else:
            xs = _hgrn_layer(geo, xs, mods, gains, hgrn_w_in[j], hgrn_w_f[j], _hgrn_lower_bound(hgrn_lb, i),
                             hgrn_g_norm[j], hgrn_w_o[j], w_in, w_out, skip_ctx)
        if skip_ctx:
            return xs
    return xs[:, geo.n_ctx:]
```

```python
import functools

import jax
import jax.numpy as jnp
from jax import lax
from jax.experimental import pallas as pl
from jax.experimental.pallas import tpu as pltpu

f32 = jnp.float32
bf16 = jnp.bfloat16

D_MODEL = 1024
N_MOD = 6
MLP_HIDDEN = 4 * D_MODEL
NORM_EPS = 1e-6
NEG_INF = -1e30
GRID_W = 64

ATTN_HEADS = 16
ATTN_KV_HEADS = 4
ATTN_GROUP = ATTN_HEADS // ATTN_KV_HEADS
HEAD_DIM = 64
WINDOW = 128
ATTN_BLOCK = 128
ROPE_BASE = 10000.0
ROPE_AXIS_DIM = HEAD_DIM // 2
ROPE_FREQS = ROPE_AXIS_DIM // 2

GLA_HEADS = 4
GLA_KEY_DIM = D_MODEL // 2
GLA_DK = GLA_KEY_DIM // GLA_HEADS
GLA_DV = D_MODEL // GLA_HEADS
GLA_GATE_RANK = 16
GLA_TAU = 16.0
SCAN_CHUNK = 64

RWKV_HEAD_SIZE = 64
RWKV_LN_EPS = 64e-5
L2_EPS = 1e-12
RWKV_PAIR = 2 * RWKV_HEAD_SIZE
RWKV_DECAY_SCALE = 0.6065306597126334

HGRN_EXPAND = 128
HGRN_HEADS = D_MODEL // HGRN_EXPAND

LANES = 128
MOD_ROWS = 8
VMEM_LIMIT = 56 * 1024 * 1024

NT = (((1,), (1,)), ((), ()))
TN = (((0,), (0,)), ((), ()))


def _params(n_grid):
    return pltpu.CompilerParams(dimension_semantics=("arbitrary",) * n_grid, vmem_limit_bytes=VMEM_LIMIT)


def _const_spec(shape):
    nd = len(shape)
    return pl.BlockSpec(shape, lambda *_: (0,) * nd, pipeline_mode=pl.Buffered(1))


def _dot(a, b):
    return jnp.dot(a, b, preferred_element_type=f32)


def _sigmoid(x):
    return 0.5 * jnp.tanh(0.5 * x) + 0.5


def _silu(x):
    return x * _sigmoid(x)


def _rms(x, gain):
    return x * lax.rsqrt(jnp.mean(x * x, axis=-1, keepdims=True) + NORM_EPS) * gain


def _modulate(x, gain, shift, scale):
    return _rms(x, gain) * (1.0 + scale) + shift


def _seg_sum(x, seg):
    r = lax.broadcasted_iota(jnp.int32, (LANES, LANES), 0) // seg
    c = lax.broadcasted_iota(jnp.int32, (LANES, LANES), 1) // seg
    ones_bd = (r == c).astype(bf16)
    hi = x.astype(bf16)
    lo = (x - hi.astype(f32)).astype(bf16)
    outs = []
    for j in range(x.shape[1] // LANES):
        sl = slice(j * LANES, (j + 1) * LANES)
        outs.append(_dot(hi[:, sl], ones_bd) + _dot(lo[:, sl], ones_bd))
    return jnp.concatenate(outs, axis=1)


def _cumsum_rows(g, reverse):
    n = g.shape[0]
    ri = lax.broadcasted_iota(jnp.int32, (n, n), 0)
    ci = lax.broadcasted_iota(jnp.int32, (n, n), 1)
    tri = ((ci >= ri) if reverse else (ci <= ri)).astype(bf16)
    hi = g.astype(bf16)
    lo = (g - hi.astype(f32)).astype(bf16)
    return _dot(tri, hi) + _dot(tri, lo)


def _mod_kernel(c_ref, w_ref, b_ref, o_ref):
    o_ref[0] = _dot(_silu(c_ref[...]).astype(bf16), w_ref[0]) + b_ref[0]


def _mod_vectors(c, c_ctx, w_mod, b_mod):
    depth, d, _ = w_mod.shape
    batch = c.shape[0]
    rows = -(-(batch + 1) // 8) * 8
    cc = jnp.zeros((rows, d), f32).at[:batch].set(c).at[batch].set(c_ctx)
    out = pl.pallas_call(
        _mod_kernel,
        grid=(depth, N_MOD),
        in_specs=[
            pl.BlockSpec((rows, d), lambda i, j: (0, 0)),
            pl.BlockSpec((1, d, d), lambda i, j: (i, 0, j)),
            pl.BlockSpec((1, 1, d), lambda i, j: (i, 0, j)),
        ],
        out_specs=pl.BlockSpec((1, rows, d), lambda i, j: (i, 0, j)),
        out_shape=jax.ShapeDtypeStruct((depth, rows, N_MOD * d), f32),
        compiler_params=_params(2),
        name="mod_vectors",
    )(cc, w_mod.astype(bf16), b_mod.reshape(depth, 1, N_MOD * d))
    lat = out[:, :batch].reshape(depth, batch, 1, N_MOD, d)
    con = jnp.broadcast_to(out[:, batch].reshape(depth, 1, 1, N_MOD, d), lat.shape)
    mods = jnp.concatenate([con, lat], axis=2)
    return jnp.pad(mods, ((0, 0), (0, 0), (0, 0), (0, MOD_ROWS - N_MOD), (0, 0)))


class _Geom:
    def __init__(self, batch, n_ctx, n_lat):
        self.batch, self.n_ctx, self.n_lat = batch, n_ctx, n_lat
        self.t = n_ctx + n_lat
        self.tm = min(256, n_ctx)
        assert n_ctx % self.tm == 0 and n_lat % self.tm == 0
        assert n_ctx % ATTN_BLOCK == 0 and n_lat % ATTN_BLOCK == 0 and n_lat % GRID_W == 0
        self.nt = self.t // self.tm
        self.ctx_tiles = n_ctx // self.tm
        self.nc = self.t // SCAN_CHUNK
        self.ctx_chunks = n_ctx // SCAN_CHUNK

    def x_spec(self, t0=0, rows=1):
        return pl.BlockSpec((rows, self.tm, D_MODEL), lambda b, t: (b, t + t0, 0))

    def x_in(self, x, t0=0, rows=1):
        if not isinstance(x, tuple):
            return [x], [self.x_spec(t0, rows)]
        ct = self.ctx_tiles
        block = (rows, self.tm, D_MODEL)
        return list(x), [pl.BlockSpec(block, lambda b, t: (b, jnp.minimum(t + t0, ct - 1), 0)),
                         pl.BlockSpec(block, lambda b, t: (b, jnp.maximum(t + t0 - ct, 0), 0))]

    def mod_spec(self, t0=0, rows=1):
        ct = self.ctx_tiles
        return pl.BlockSpec((rows, 1, MOD_ROWS, D_MODEL), lambda b, t: (b, ((t + t0) >= ct).astype(jnp.int32), 0, 0))

    def bwd_chunk(self, i):
        cc = self.ctx_chunks
        return jnp.where(i < cc, cc - 1 - i, self.nc - 1 + cc - i)


LIN_ROWS_PER_STEP = 2


def _lin_kernel(x_ref, m_ref, g_ref, w_ref, o_ref):
    rows, tm = x_ref.shape[0], x_ref.shape[1]
    h = [_modulate(x_ref[r], g_ref[0:1], m_ref[r, 0, 0:1], m_ref[r, 0, 1:2]).astype(bf16) for r in range(rows)]
    z = _dot(jnp.concatenate(h, axis=0), w_ref[...])
    for r in range(rows):
        o_ref[r] = z[r * tm:(r + 1) * tm]


def _mod_linear(geo, x, mods, gains, w, name):
    n = w.shape[1]
    rows = LIN_ROWS_PER_STEP if geo.batch % LIN_ROWS_PER_STEP == 0 else 1
    return pl.pallas_call(
        _lin_kernel,
        grid=(geo.batch // rows, geo.nt),
        in_specs=[geo.x_spec(0, rows), geo.mod_spec(0, rows), _const_spec((8, D_MODEL)), _const_spec((D_MODEL, n))],
        out_specs=pl.BlockSpec((rows, geo.tm, n), lambda b, t: (b, t, 0)),
        out_shape=jax.ShapeDtypeStruct((geo.batch, geo.t, n), f32),
        compiler_params=_params(2),
        name=name,
    )(x, mods, gains, w)


def _read_tile(x_refs, row, is_ctx):
    if len(x_refs) == 1:
        return x_refs[0][row]
    return jnp.where(is_ctx, x_refs[0][row], x_refs[1][row])


def _head_rms(o, gain, width):
    outs = []
    for h in range(o.shape[1] // width):
        oh = o[:, h * width:(h + 1) * width]
        outs.append(oh * lax.rsqrt(jnp.mean(oh * oh, axis=-1, keepdims=True) + NORM_EPS) * gain)
    return jnp.concatenate(outs, axis=1)


def _pre_attn(refs, row):
    (o_ref,) = refs
    return o_ref[row]


def _pre_scan(width, refs, row):
    of_ref, ob_ref, gate_ref, gn_ref = refs
    o = _head_rms(of_ref[row] + ob_ref[row], gn_ref[...], width)
    return (o * _silu(gate_ref[row])).astype(bf16)


def _pre_rwkv(refs, row):
    yf_ref, yb_ref, g_ref, bg_ref, vec_ref = refs
    y = yf_ref[row] + yb_ref[row]
    inv_n = 1.0 / RWKV_HEAD_SIZE
    mu = _seg_sum(y, RWKV_HEAD_SIZE) * inv_n
    dlt = y - mu
    var = _seg_sum(dlt * dlt, RWKV_HEAD_SIZE) * inv_n
    yn = dlt * lax.rsqrt(var + RWKV_LN_EPS) * vec_ref[0:1] + vec_ref[1:2]
    return (yn * g_ref[row] + bg_ref[row]).astype(bf16)


def _post_kernel(pre, n_x, n_mix, first_ctx_tiles, *refs):
    x_refs, (m_ref, g_ref), mix_refs = refs[:n_x], refs[n_x:n_x + 2], refs[n_x + 2:n_x + 2 + n_mix]
    wo_ref, win_ref, wout_ref, out_ref = refs[n_x + 2 + n_mix:]
    rows = range(out_ref.shape[0])
    is_ctx = pl.program_id(1) < first_ctx_tiles
    o = [pre(mix_refs, r) for r in rows]
    y = [_dot(o[r], wo_ref[...]) for r in rows]
    x1 = [_read_tile(x_refs, r, is_ctx) + m_ref[r, 0, 2:3] * _rms(y[r], g_ref[1:2]) for r in rows]
    h2 = [_modulate(x1[r], g_ref[2:3], m_ref[r, 0, 3:4], m_ref[r, 0, 4:5]).astype(bf16) for r in rows]
    us = [[] for _ in rows]
    for c in range(MLP_HIDDEN // D_MODEL):
        cols = slice(c * D_MODEL, (c + 1) * D_MODEL)
        for r in rows:
            us[r].append(jnp.square(jnp.maximum(_dot(h2[r], win_ref[:, cols]), 0.0)).astype(bf16))
    acc = [_dot(jnp.concatenate(us[r], axis=1), wout_ref[...]) for r in rows]
    for r in rows:
        out_ref[r] = x1[r] + m_ref[r, 0, 5:6] * _rms(acc[r], g_ref[3:4])


POST_ROWS_PER_STEP = 2


def _post_rows(geo, n_tile_inputs):
    tile = geo.tm * D_MODEL * 4
    weights = 2 * (D_MODEL * D_MODEL + 2 * D_MODEL * MLP_HIDDEN)
    for rows in (POST_ROWS_PER_STEP, 1):
        streamed = 2 * (n_tile_inputs + 2) * rows * tile
        live = 6 * rows * tile
        if geo.batch % rows == 0 and weights + streamed + live <= VMEM_LIMIT:
            return rows
    return 1


def _post_mlp(geo, x, mods, gains, pre, mix_args, mix_specs, w_o, w_in, w_out, skip_ctx, name, rows):
    t0 = geo.ctx_tiles if skip_ctx else 0
    x_args, x_specs = geo.x_in(x, t0, rows)
    return pl.pallas_call(
        functools.partial(_post_kernel, pre, len(x_args), len(mix_args), geo.ctx_tiles - t0),
        grid=(geo.batch // rows, geo.nt - t0),
        in_specs=x_specs + [geo.mod_spec(t0, rows), _const_spec((8, D_MODEL))] + mix_specs + [
            _const_spec((D_MODEL, D_MODEL)), _const_spec((D_MODEL, MLP_HIDDEN)), _const_spec((MLP_HIDDEN, D_MODEL))],
        out_specs=pl.BlockSpec((rows, geo.tm, D_MODEL), lambda b, t: (b, t, 0)),
        out_shape=jax.ShapeDtypeStruct((geo.batch, geo.t - t0 * geo.tm, D_MODEL), f32),
        compiler_params=_params(2),
        name=name,
    )(*x_args, mods, gains, *mix_args, w_o, w_in, w_out)


def _tile_spec(geo, width, col, t0, rows):
    return pl.BlockSpec((rows, geo.tm, width), lambda b, t: (b, t + t0, col))


def _rope(x, cos, sin):
    w = x.shape[1]
    reps = w // LANES
    cw = jnp.concatenate([cos] * reps, axis=1)
    sw = jnp.concatenate([sin] * reps, axis=1)
    lane = lax.broadcasted_iota(jnp.int32, x.shape, 1)
    first = (lane % ROPE_AXIS_DIM) < ROPE_FREQS
    partner = jnp.where(first, pltpu.roll(x, w - ROPE_FREQS, axis=1), pltpu.roll(x, ROPE_FREQS, axis=1))
    return x * cw + partner * sw


def _attn_proj_kernel(ctx_tiles, *refs):
    x_refs, (m_ref, g_ref, cos_ref, sin_ref, wq_ref, wk_ref, wv_ref, q_ref, k_ref, v_ref) = refs[:-10], refs[-10:]
    rows = range(q_ref.shape[0])
    is_ctx = pl.program_id(1) < ctx_tiles
    cos, sin = cos_ref[...], sin_ref[...]
    h = [_modulate(_read_tile(x_refs, r, is_ctx), g_ref[0:1], m_ref[r, 0, 0:1], m_ref[r, 0, 1:2]).astype(bf16)
         for r in rows]
    q = [_dot(h[r], wq_ref[...]) * (HEAD_DIM ** -0.5) for r in rows]
    k = [_dot(h[r], wk_ref[...]) for r in rows]
    v = [_dot(h[r], wv_ref[...]) for r in rows]
    q = [_rope(q[r], cos, sin) for r in rows]
    k = [_rope(k[r], cos, sin) for r in rows]
    for r in rows:
        for hd in range(ATTN_HEADS):
            q_ref[r, hd] = q[r][:, hd * HEAD_DIM:(hd + 1) * HEAD_DIM].astype(bf16)
        for hd in range(ATTN_KV_HEADS):
            k_ref[r, hd] = k[r][:, hd * HEAD_DIM:(hd + 1) * HEAD_DIM].astype(bf16)
            v_ref[r, hd] = v[r][:, hd * HEAD_DIM:(hd + 1) * HEAD_DIM].astype(bf16)


def _attn_core_kernel(geo, sink_ref, q_ref, kp_ref, kc_ref, kn_ref, kx_ref, vp_ref, vc_ref, vn_ref, vx_ref, o_ref):
    qb = pl.program_id(1)
    blk = ATTN_BLOCK
    n_loc = 3 * blk
    n_keys = n_loc + geo.n_ctx
    rows = ATTN_GROUP * blk
    first_lat = geo.n_ctx // blk
    n_blocks = geo.t // blk
    row = lax.broadcasted_iota(jnp.int32, (blk, n_keys), 0)
    col = lax.broadcasted_iota(jnp.int32, (blk, n_keys), 1)
    kblk = qb - 1 + col // blk
    ok_local = ((jnp.abs(col - blk - row) <= WINDOW) & (qb >= first_lat) & (kblk >= first_lat) & (kblk < n_blocks))
    ok = (col >= n_loc) | ok_local
    scores, vals = [], []
    for j in range(ATTN_KV_HEADS):
        q4 = q_ref[0, ATTN_GROUP * j:ATTN_GROUP * (j + 1)].reshape(rows, HEAD_DIM)
        keys = jnp.concatenate([kp_ref[0, j], kc_ref[0, j], kn_ref[0, j], kx_ref[0, j]], axis=0)
        vals.append(jnp.concatenate([vp_ref[0, j], vc_ref[0, j], vn_ref[0, j], vx_ref[0, j]], axis=0))
        scores.append(lax.dot_general(q4, keys, NT, preferred_element_type=f32))
    heads = [(j, g) for j in range(ATTN_KV_HEADS) for g in range(ATTN_GROUP)]
    sg = [jnp.where(ok, scores[j][g * blk:(g + 1) * blk], NEG_INF) for j, g in heads]
    sk = [sink_ref[ATTN_GROUP * j + g] for j, g in heads]
    m = [jnp.maximum(jnp.max(s, axis=-1, keepdims=True), k) for s, k in zip(sg, sk)]
    e = [jnp.exp(s - mm) for s, mm in zip(sg, m)]
    inv = [1.0 / (jnp.sum(ee, axis=-1, keepdims=True) + jnp.exp(k - mm)) for ee, k, mm in zip(e, sk, m)]
    p = [ee.astype(bf16) for ee in e]
    outs = []
    for j in range(ATTN_KV_HEADS):
        o4 = _dot(jnp.concatenate(p[ATTN_GROUP * j:ATTN_GROUP * (j + 1)], axis=0), vals[j])
        outs += [o4[g * blk:(g + 1) * blk] * inv[ATTN_GROUP * j + g] for g in range(ATTN_GROUP)]
    o_ref[0] = jnp.concatenate(outs, axis=1).astype(bf16)


def _rope_tables(geo):
    inv_freq = ROPE_BASE ** (-jnp.arange(ROPE_FREQS, dtype=f32) * 2.0 / ROPE_AXIS_DIM)
    pos = jnp.arange(geo.n_lat)
    row = (pos // GRID_W).astype(f32)
    col = (pos % GRID_W).astype(f32)
    ang = jnp.stack([row[:, None] * inv_freq, col[:, None] * inv_freq], axis=1)
    cos = jnp.cos(ang)
    sin = jnp.sin(ang)
    cos_h = jnp.concatenate([cos, cos], axis=2).reshape(geo.n_lat, HEAD_DIM)
    sin_h = jnp.concatenate([-sin, sin], axis=2).reshape(geo.n_lat, HEAD_DIM)
    cos_t = jnp.concatenate([jnp.ones((geo.n_ctx, HEAD_DIM), f32), cos_h], axis=0)
    sin_t = jnp.concatenate([jnp.zeros((geo.n_ctx, HEAD_DIM), f32), sin_h], axis=0)
    return jnp.tile(cos_t, (1, 2)), jnp.tile(sin_t, (1, 2))


def _attn_layer(geo, x, mods, gains, w_qkv, w_o, sink, w_in, w_out, skip_ctx):
    b, t, tm = geo.batch, geo.t, geo.tm
    q_cols = ATTN_HEADS * HEAD_DIM
    kv_cols = ATTN_KV_HEADS * HEAD_DIM
    wb = w_qkv.astype(bf16)
    cos_t, sin_t = _rope_tables(geo)
    tab_spec = pl.BlockSpec((tm, LANES), lambda bb, tt: (tt, 0))
    rows = LIN_ROWS_PER_STEP if b % LIN_ROWS_PER_STEP == 0 else 1
    x_args, x_specs = geo.x_in(x, 0, rows)
    q, k, v = pl.pallas_call(
        functools.partial(_attn_proj_kernel, geo.ctx_tiles),
        grid=(b // rows, geo.nt),
        in_specs=x_specs + [geo.mod_spec(0, rows), _const_spec((8, D_MODEL)), tab_spec, tab_spec,
                  _const_spec((D_MODEL, q_cols)), _const_spec((D_MODEL, kv_cols)), _const_spec((D_MODEL, kv_cols))],
        out_specs=[pl.BlockSpec((rows, ATTN_HEADS, tm, HEAD_DIM), lambda bb, tt: (bb, 0, tt, 0)),
                   pl.BlockSpec((rows, ATTN_KV_HEADS, tm, HEAD_DIM), lambda bb, tt: (bb, 0, tt, 0)),
                   pl.BlockSpec((rows, ATTN_KV_HEADS, tm, HEAD_DIM), lambda bb, tt: (bb, 0, tt, 0))],
        out_shape=[jax.ShapeDtypeStruct((b, ATTN_HEADS, t, HEAD_DIM), bf16),
                   jax.ShapeDtypeStruct((b, ATTN_KV_HEADS, t, HEAD_DIM), bf16),
                   jax.ShapeDtypeStruct((b, ATTN_KV_HEADS, t, HEAD_DIM), bf16)],
        compiler_params=_params(2),
        name="attn_proj",
    )(*x_args, mods, gains, cos_t, sin_t, wb[:, :q_cols], wb[:, q_cols:q_cols + kv_cols], wb[:, q_cols + kv_cols:])

    blk = ATTN_BLOCK
    n_blocks = t // blk

    def kv_spec(off):
        return pl.BlockSpec((1, ATTN_KV_HEADS, blk, HEAD_DIM),
                            lambda bb, qb: (bb, 0, jnp.clip(qb + off, 0, n_blocks - 1), 0))

    ctx_spec = pl.BlockSpec((1, ATTN_KV_HEADS, geo.n_ctx, HEAD_DIM), lambda bb, qb: (bb, 0, 0, 0))
    o = pl.pallas_call(
        functools.partial(_attn_core_kernel, geo),
        grid=(b, n_blocks),
        in_specs=[pl.BlockSpec(memory_space=pltpu.SMEM),
                  pl.BlockSpec((1, ATTN_HEADS, blk, HEAD_DIM), lambda bb, qb: (bb, 0, qb, 0)),
                  kv_spec(-1), kv_spec(0), kv_spec(1), ctx_spec,
                  kv_spec(-1), kv_spec(0), kv_spec(1), ctx_spec],
        out_specs=pl.BlockSpec((1, blk, q_cols), lambda bb, qb: (bb, qb, 0)),
        out_shape=jax.ShapeDtypeStruct((b, t, q_cols), bf16),
        compiler_params=_params(2),
        name="attn_core",
    )(sink, q, k, k, k, k, v, v, v, v)

    t0 = geo.ctx_tiles if skip_ctx else 0
    rows = _post_rows(geo, 1)
    return _post_mlp(geo, x, mods, gains, _pre_attn, [o], [_tile_spec(geo, D_MODEL, 0, t0, rows)],
                     w_o.astype(bf16), w_in, w_out, skip_ctx, "attn_post", rows)


def _gated_prep(entries, heads, dk, dv):
    sums = [_cumsum_rows(g, reverse) for _, _, _, g, _, _, reverse in entries]
    units = []
    for (q, k, v, g, st_ref, base, reverse), b in zip(entries, sums):
        n = q.shape[0]
        btot = b[0:1] if reverse else b[n - 1:n]
        q_dec = (q * jnp.exp(b)).astype(bf16)
        k_inv32 = k * jnp.exp(-b)
        k_inv = k_inv32.astype(bf16)
        dec = jnp.exp(btot)
        k_end = (k_inv32 * dec).astype(bf16)
        vb = v.astype(bf16)
        ri = lax.broadcasted_iota(jnp.int32, (n, n), 0)
        ci = lax.broadcasted_iota(jnp.int32, (n, n), 1)
        tri = (ci >= ri) if reverse else (ci <= ri)
        for h in range(heads):
            ks = slice(h * dk, (h + 1) * dk)
            vs = slice(h * dv, (h + 1) * dv)
            units.append(dict(q=q_dec[:, ks], ki=k_inv[:, ks], ke=k_end[:, ks], v=vb[:, vs], dec=dec[:, ks],
                              tri=tri, st=st_ref[base + h], ref=st_ref, h=base + h))
    return units


def _gated_matmuls(units, heads):
    for u in units:
        a = lax.dot_general(u["q"], u["ki"], NT, preferred_element_type=f32)
        u["a"] = jnp.where(u["tri"], a, 0.0).astype(bf16)
        u["qs"] = lax.dot_general(u["q"], u["st"].astype(bf16), NT, preferred_element_type=f32)
    for u in units:
        u["o"] = _dot(u["a"], u["v"]) + u["qs"]
        u["new"] = u["st"] * u["dec"] + lax.dot_general(u["v"], u["ke"], TN, preferred_element_type=f32)
    for u in units:
        u["ref"][u["h"]] = u["new"]
    return [jnp.concatenate([u["o"] for u in units[i:i + heads]], axis=1) for i in range(0, len(units), heads)]


def _gated_chunk(groups, heads, dk, dv, staggered):
    def prep(group):
        return _gated_prep([thunk() for thunk in group], heads, dk, dv)

    if not staggered:
        units = prep([thunk for group in groups for thunk in group])
        per_row = len(groups[0])
        outs = _gated_matmuls(units, heads)
        return [outs[i:i + per_row] for i in range(0, len(outs), per_row)]
    outs = []
    ready = prep(groups[0])
    for i in range(len(groups)):
        nxt = prep(groups[i + 1]) if i + 1 < len(groups) else None
        outs.append(_gated_matmuls(ready, heads))
        ready = nxt
    return outs


def _gla_gate(zd, wu_ref, bias_ref, d):
    zg = _dot(zd.astype(bf16), wu_ref[d]) + bias_ref[d:d + 1]
    return (jnp.minimum(zg, 0.0) - jnp.log(1.0 + jnp.exp(-jnp.abs(zg)))) * (1.0 / GLA_TAU)


def _gla_scan_kernel(qkf_ref, vf_ref, zdf_ref, qkb_ref, vb_ref, zdb_ref, wu_ref, bias_ref,
                     of_ref, ob_ref, sf_ref, sb_ref):
    @pl.when(pl.program_id(1) == 0)
    def _():
        sf_ref[...] = jnp.zeros_like(sf_ref)
        sb_ref[...] = jnp.zeros_like(sb_ref)

    def entry(row, d, qk_ref, v_ref, zd_ref, s_ref):
        qk = qk_ref[row]
        return (qk[:, :GLA_KEY_DIM] * (GLA_DK ** -0.5), qk[:, GLA_KEY_DIM:], v_ref[row],
                _gla_gate(zd_ref[row], wu_ref, bias_ref, d), s_ref, row * GLA_HEADS, d == 1)

    groups = [[functools.partial(entry, row, 0, qkf_ref, vf_ref, zdf_ref, sf_ref),
               functools.partial(entry, row, 1, qkb_ref, vb_ref, zdb_ref, sb_ref)]
              for row in range(qkf_ref.shape[0])]
    for row, (o_f, o_b) in enumerate(_gated_chunk(groups, GLA_HEADS, GLA_DK, GLA_DV, staggered=False)):
        of_ref[row] = o_f
        ob_ref[row] = o_b


def _hgrn_scan_kernel(qf_ref, if_ref, zff_ref, qb_ref, ib_ref, zfb_ref, lb_ref, of_ref, ob_ref, sf_ref, sb_ref):
    @pl.when(pl.program_id(1) == 0)
    def _():
        sf_ref[...] = jnp.zeros_like(sf_ref)
        sb_ref[...] = jnp.zeros_like(sb_ref)

    lb = lb_ref[...]
    def entry(row, d, q_ref, i_ref, zf_ref, s_ref):
        f = lb + (1.0 - lb) * _sigmoid(zf_ref[row])
        return (_silu(q_ref[row]), 1.0 - f, i_ref[row], jnp.log(f), s_ref, row * HGRN_HEADS, d == 1)

    groups = [[functools.partial(entry, row, 0, qf_ref, if_ref, zff_ref, sf_ref),
               functools.partial(entry, row, 1, qb_ref, ib_ref, zfb_ref, sb_ref)]
              for row in range(qf_ref.shape[0])]
    for row, (o_f, o_b) in enumerate(_gated_chunk(groups, HGRN_HEADS, HGRN_EXPAND, HGRN_EXPAND, staggered=True)):
        of_ref[row] = o_f
        ob_ref[row] = o_b


SCAN_ROWS_PER_STEP = 8


def _scan_rows(geo):
    return SCAN_ROWS_PER_STEP if geo.batch % SCAN_ROWS_PER_STEP == 0 else 1


def _chunk_spec(geo, width, col, reverse):
    rows = _scan_rows(geo)
    if reverse:
        return pl.BlockSpec((rows, SCAN_CHUNK, width), lambda b, i: (b, geo.bwd_chunk(i), col))
    return pl.BlockSpec((rows, SCAN_CHUNK, width), lambda b, i: (b, i, col))


def _scan_call(geo, body, args, specs, heads, dk, dv, name):
    out_sds = jax.ShapeDtypeStruct((geo.batch, geo.t, heads * dv), f32)
    rows = _scan_rows(geo)
    state = pltpu.VMEM((rows * heads, dv, dk), f32)
    return pl.pallas_call(
        body,
        grid=(geo.batch // rows, geo.nc),
        in_specs=specs,
        out_specs=[_chunk_spec(geo, heads * dv, 0, False), _chunk_spec(geo, heads * dv, 0, True)],
        out_shape=[out_sds, out_sds],
        scratch_shapes=[state, state],
        compiler_params=_params(2),
        name=name,
    )(*args)


def _gla_layer(geo, x, mods, gains, w_in_p, w_gd, w_gu, g_bias, g_norm, w_o, w_in, w_out, skip_ctx):
    r = GLA_GATE_RANK
    n_z = 2 * GLA_KEY_DIM + 2 * D_MODEL
    w_all = jnp.concatenate([w_in_p, w_gd[0], w_gd[1], jnp.zeros((D_MODEL, LANES - 2 * r), f32)], axis=1).astype(bf16)
    z = _mod_linear(geo, x, mods, gains, w_all, "gla_proj")
    wu = jnp.zeros((2, LANES, GLA_KEY_DIM), f32).at[0, :r].set(w_gu[0]).at[1, r:2 * r].set(w_gu[1]).astype(bf16)
    zd_col = n_z // LANES
    specs = []
    for rev in (False, True):
        specs += [_chunk_spec(geo, 2 * GLA_KEY_DIM, 0, rev), _chunk_spec(geo, D_MODEL, 1, rev),
                  _chunk_spec(geo, LANES, zd_col, rev)]
    specs += [_const_spec((2, LANES, GLA_KEY_DIM)), _const_spec((2, GLA_KEY_DIM))]
    o_f, o_b = _scan_call(geo, _gla_scan_kernel, [z, z, z, z, z, z, wu, g_bias], specs,
                          GLA_HEADS, GLA_DK, GLA_DV, "gla_scan")
    t0 = geo.ctx_tiles if skip_ctx else 0
    rows = _post_rows(geo, 3)
    mix_specs = [_tile_spec(geo, D_MODEL, 0, t0, rows), _tile_spec(geo, D_MODEL, 0, t0, rows),
                 _tile_spec(geo, D_MODEL, 2, t0, rows), _const_spec((1, GLA_DV))]
    return _post_mlp(geo, x, mods, gains, functools.partial(_pre_scan, GLA_DV),
                     [o_f, o_b, z, g_norm.reshape(1, GLA_DV)], mix_specs,
                     w_o.astype(bf16), w_in, w_out, skip_ctx, "gla_post", rows)


def _hgrn_layer(geo, x, mods, gains, w_in_p, w_f, lower_bound, g_norm, w_o, w_in, w_out, skip_ctx):
    w_all = jnp.concatenate([w_in_p, w_f[0], w_f[1]], axis=1).astype(bf16)
    z = _mod_linear(geo, x, mods, gains, w_all, "hgrn_proj")
    specs = []
    for rev in (False, True):
        specs += [_chunk_spec(geo, D_MODEL, 0, rev), _chunk_spec(geo, D_MODEL, 1, rev),
                  _chunk_spec(geo, D_MODEL, 4 if rev else 3, rev)]
    specs += [_const_spec((1, D_MODEL))]
    o_f, o_b = _scan_call(geo, _hgrn_scan_kernel, [z, z, z, z, z, z, lower_bound.reshape(1, D_MODEL)], specs,
                          HGRN_HEADS, HGRN_EXPAND, HGRN_EXPAND, "hgrn_scan")
    t0 = geo.ctx_tiles if skip_ctx else 0
    rows = _post_rows(geo, 3)
    mix_specs = [_tile_spec(geo, D_MODEL, 0, t0, rows), _tile_spec(geo, D_MODEL, 0, t0, rows),
                 _tile_spec(geo, D_MODEL, 2, t0, rows), _const_spec((1, HGRN_EXPAND))]
    return _post_mlp(geo, x, mods, gains, functools.partial(_pre_scan, HGRN_EXPAND),
                     [o_f, o_b, z, g_norm.reshape(1, HGRN_EXPAND)], mix_specs,
                     w_o.astype(bf16), w_in, w_out, skip_ctx, "hgrn_post", rows)


def _rwkv_proj_kernel(geo, x_ref, xp_ref, xn_ref, m_ref, g_ref, mix_ref, vec_ref, wrkv_ref, wdn_ref, wup_ref,
                      aup_ref, gup_ref, r_ref, v_ref, gate_ref, kk_ref, lw0_ref, lw1_ref, a0_ref, a1_ref,
                      k0_ref, k1_ref, bg_ref):
    t = pl.program_id(1)
    tm = geo.tm
    gain, shift, scale = g_ref[0:1], m_ref[0, 0, 0:1], m_ref[0, 0, 1:2]
    h = _modulate(x_ref[0], gain, shift, scale)
    seg_first = (t == 0) | (t == geo.ctx_tiles)
    seg_last = (t == geo.ctx_tiles - 1) | (t == geo.nt - 1)
    h_prev = jnp.where(seg_first, 0.0, _modulate(xp_ref[0], gain, shift, scale)[7:8])
    h_next = jnp.where(seg_last, 0.0, _modulate(xn_ref[0], gain, shift, scale)[0:1])
    row = lax.broadcasted_iota(jnp.int32, h.shape, 0)
    up = jnp.where(row == 0, h_prev, pltpu.roll(h, 1, axis=0))
    dn = jnp.where(row == tm - 1, h_next, pltpu.roll(h, tm - 1, axis=0))
    dx = 0.5 * (up + dn) - h

    def mixed(n):
        return (h + dx * mix_ref[n:n + 1]).astype(bf16)

    r = _dot(mixed(0), wrkv_ref[0])
    k = _dot(mixed(1), wrkv_ref[1])
    v = _dot(mixed(2), wrkv_ref[2])
    dw = jnp.tanh(_dot(mixed(3), wdn_ref[:, 0:LANES])).astype(bf16)
    da = _dot(mixed(4), wdn_ref[:, LANES:2 * LANES]).astype(bf16)
    dg = _sigmoid(_dot(mixed(5), wdn_ref[:, 2 * LANES:3 * LANES])).astype(bf16)
    gate = _dot(dg, gup_ref[...])
    r_ref[0] = r
    v_ref[0] = v
    gate_ref[0] = gate
    kk = k * vec_ref[4:5]
    kk_ref[0] = kk * lax.rsqrt(jnp.maximum(_seg_sum(kk * kk, RWKV_HEAD_SIZE), L2_EPS * L2_EPS))
    kds = []
    for d, (lw_ref, a_ref, kd_ref) in enumerate(((lw0_ref, a0_ref, k0_ref), (lw1_ref, a1_ref, k1_ref))):
        lw_ref[0] = -RWKV_DECAY_SCALE * _sigmoid(vec_ref[d:d + 1] + _dot(dw, wup_ref[d]))
        a = _sigmoid(vec_ref[2 + d:3 + d] + _dot(da, aup_ref[d]))
        a_ref[0] = a
        kds.append(k * (1.0 + (a - 1.0) * vec_ref[5:6]))
        kd_ref[0] = kds[d]
    bg_ref[0] = _seg_sum(r * (0.5 * (kds[0] + kds[1])) * vec_ref[6:7], RWKV_HEAD_SIZE) * v * gate


def _rwkv_chunk_units(units):
    n = SCAN_CHUNK
    hs = RWKV_HEAD_SIZE
    lane = lax.broadcasted_iota(jnp.int32, (n, RWKV_PAIR), 1)
    t_i = lax.broadcasted_iota(jnp.int32, (n, RWKV_PAIR), 0)
    s_i = lane % hs
    head0 = lane < hs
    ri = lax.broadcasted_iota(jnp.int32, (RWKV_PAIR, RWKV_PAIR), 0)
    ci = lax.broadcasted_iota(jnp.int32, (RWKV_PAIR, RWKV_PAIR), 1)
    same_head = ri // hs == ci // hs
    top_rows = ri < hs

    def block_diag(x):
        return jnp.concatenate([jnp.where(head0, x, 0.0), jnp.where(head0, 0.0, x)], axis=0)

    def inverse_4x4_blocks(nm, reverse):
        toward = RWKV_PAIR - 1 if not reverse else 1
        dist = (s_i - t_i) if reverse else (t_i - s_i)
        near = pltpu.roll(nm, toward, axis=1)
        far = pltpu.roll(nm, (2 * toward) % RWKV_PAIR, axis=1)
        diag1 = jnp.sum(jnp.where(same4 & (dist == 1), nm, 0.0), axis=0, keepdims=True)
        diag2 = jnp.sum(jnp.where(same4 & (dist == 2), nm, 0.0), axis=0, keepdims=True)
        p2 = diag2 + pltpu.roll(diag1, toward, axis=1) * diag1
        two = nm + near * diag1
        three = two + far * p2
        return jnp.where(s_i == t_i, 1.0,
                         jnp.where(same4 & (dist == 1), nm,
                                   jnp.where(same4 & (dist == 2), two, jnp.where(same4 & (dist == 3), three, 0.0))))

    def cross_blocks(b):
        return (t_i // (2 * b) == s_i // (2 * b)) & (t_i // b != s_i // b)

    st = []
    for r, kd, v, kk, a, lw, ht, reverse in units:
        c = _cumsum_rows(lw, reverse)
        tot = c[0:1] if reverse else c[n - 1:n]
        e_neg = jnp.exp(-c)
        dec = jnp.exp(tot)
        kb = jnp.concatenate([kd * e_neg, kk * a * e_neg], axis=0)
        kb_t = kb.T
        kb_sw = pltpu.roll(kb_t, hs, axis=1)
        rhs = jnp.concatenate([jnp.where(same_head, jnp.where(top_rows, kb_t, kb_sw), 0.0),
                               jnp.where(same_head, jnp.where(top_rows, kb_sw, kb_t), 0.0)], axis=1)
        st.append(dict(
            strict=(s_i > t_i) if reverse else (s_i < t_i), incl=(s_i >= t_i) if reverse else (s_i <= t_i),
            a_bar=(-kk * jnp.exp(c - lw)).astype(bf16), r_bar=(r * jnp.exp(c)).astype(bf16), rhs=rhs.astype(bf16),
            v_bd=block_diag(v).astype(bf16), h_t=ht.T.astype(bf16),
            ends=(kb * dec).astype(bf16),
            v=v, ht=ht, dec=dec, reverse=reverse))
    for s in st:
        g = _dot(jnp.concatenate([s["a_bar"], s["r_bar"]], axis=0), s["rhs"])
        s["a_ak"] = jnp.where(s["strict"], g[0:n, 0:RWKV_PAIR], 0.0).astype(bf16)
        s["nmat"] = jnp.where(s["strict"], g[0:n, RWKV_PAIR:], 0.0)
        s["q_k"] = jnp.where(s["incl"], g[n:, 0:RWKV_PAIR], 0.0).astype(bf16)
        s["q_b"] = jnp.where(s["incl"], g[n:, RWKV_PAIR:], 0.0).astype(bf16)
    same4 = t_i // 4 == s_i // 4
    for s in st:
        s["inv"] = inverse_4x4_blocks(s["nmat"], s["reverse"])
    b = 4
    while 2 * b < n:
        for s in st:
            s["pc"] = _dot(s["inv"].astype(bf16), block_diag(jnp.where(cross_blocks(b), s["nmat"], 0.0)).astype(bf16))
        for s in st:
            s["inv"] = s["inv"] + _dot(s["pc"].astype(bf16), block_diag(s["inv"]).astype(bf16))
        b *= 2
    for s in st:
        x = _dot(jnp.concatenate([s["a_bar"], s["a_ak"]], axis=1), jnp.concatenate([s["h_t"], s["v_bd"]], axis=0))
        s["inv"] = s["inv"].astype(bf16)
        s["w"] = _dot(s["inv"], block_diag(x).astype(bf16))
    for s in st:
        s["cw"] = _dot(jnp.where(cross_blocks(n // 2), s["nmat"], 0.0).astype(bf16), block_diag(s["w"]).astype(bf16))
    for s in st:
        s["u"] = s["w"] + _dot(s["inv"], block_diag(s["cw"]).astype(bf16))
    for s in st:
        s["vu_t"] = jnp.concatenate([s["v"], s["u"]], axis=0).T.astype(bf16)
    for s in st:
        s["y"] = _dot(jnp.concatenate([s["r_bar"], s["q_k"], s["q_b"]], axis=1),
                      jnp.concatenate([s["h_t"], s["v_bd"], block_diag(s["u"]).astype(bf16)], axis=0))
    for s in st:
        s["upd"] = _dot(s["vu_t"], s["ends"])
    return [(s["y"], s["ht"] * s["dec"] + jnp.where(same_head, s["upd"], 0.0)) for s in st]


def _rwkv_scan_kernel(rows, pairs, *refs):
    fwd, bwd = refs[0:6], refs[6:12]
    yf_ref, yb_ref, hf_ref, hb_ref = refs[12:]

    @pl.when(pl.program_id(2) == 0)
    def _():
        hf_ref[...] = jnp.zeros_like(hf_ref)
        hb_ref[...] = jnp.zeros_like(hb_ref)

    units, dests = [], []
    for ins, y_ref, h_ref, reverse in ((fwd, yf_ref, hf_ref, False), (bwd, yb_ref, hb_ref, True)):
        r_ref, v_ref, kk_ref, lw_ref, a_ref, kd_ref = ins
        for row in range(rows):
            for p in range(pairs):
                sl = slice(p * RWKV_PAIR, (p + 1) * RWKV_PAIR)
                slot = row * pairs + p
                units.append((r_ref[row, :, sl], kd_ref[row, :, sl], v_ref[row, :, sl], kk_ref[row, :, sl],
                              a_ref[row, :, sl], lw_ref[row, :, sl], h_ref[slot], reverse))
                dests.append((y_ref, h_ref, row, slot, sl))
    for (y, h_new), (y_ref, h_ref, row, slot, sl) in zip(_rwkv_chunk_units(units), dests):
        y_ref[row, :, sl] = y
        h_ref[slot] = h_new


RWKV_PAIRS_PER_STEP = 8
RWKV_ROWS_PER_STEP = 2


def _rwkv_layer(geo, x, mods, gains, mix, w_rkv, w0, w_down, w_up, a0, a_down, a_up, g_down, g_up, k_k, k_a,
                r_k, ln_w, ln_b, w_o, w_in, w_out, skip_ctx):
    b, t, tm = geo.batch, geo.t, geo.tm
    d = D_MODEL
    rank = w_down.shape[-1]
    assert 2 * rank == LANES and a_down.shape[-1] == rank and g_down.shape[-1] == LANES
    mix8 = jnp.pad(mix, ((0, 2), (0, 0)))
    vec = jnp.stack([w0[0], w0[1], a0[0], a0[1], k_k, k_a, r_k.reshape(d), jnp.zeros_like(k_k)])
    w_dn = jnp.concatenate([w_down[0], w_down[1], a_down[0], a_down[1], g_down], axis=1).astype(bf16)

    def padded_up(w):
        return jnp.zeros((2, LANES, d), f32).at[0, :rank].set(w[0]).at[1, rank:].set(w[1]).astype(bf16)

    halo = 8
    n_halo = t // halo
    per = tm // halo
    x_prev = pl.BlockSpec((1, halo, d), lambda bb, tt: (bb, jnp.maximum(tt * per - 1, 0), 0))
    x_next = pl.BlockSpec((1, halo, d), lambda bb, tt: (bb, jnp.minimum((tt + 1) * per, n_halo - 1), 0))
    out_spec = pl.BlockSpec((1, tm, d), lambda bb, tt: (bb, tt, 0))
    sds = jax.ShapeDtypeStruct((b, t, d), f32)
    r, v, gate, kk, lw0, lw1, a_0, a_1, k0, k1, bonus_g = pl.pallas_call(
        functools.partial(_rwkv_proj_kernel, geo),
        grid=(b, geo.nt),
        in_specs=[geo.x_spec(), x_prev, x_next, geo.mod_spec(), _const_spec((8, d)), _const_spec((8, d)),
                  _const_spec((8, d)), _const_spec((3, d, d)), _const_spec((d, 3 * LANES)),
                  _const_spec((2, LANES, d)), _const_spec((2, LANES, d)), _const_spec((LANES, d))],
        out_specs=[out_spec] * 11,
        out_shape=[sds] * 11,
        compiler_params=_params(2),
        name="rwkv_proj",
    )(x, x, x, mods, gains, mix8, vec, w_rkv.astype(bf16), w_dn, padded_up(w_up), padded_up(a_up),
      g_up.astype(bf16))

    pairs = RWKV_PAIRS_PER_STEP
    rows = RWKV_ROWS_PER_STEP if b % RWKV_ROWS_PER_STEP == 0 else 1
    width = pairs * RWKV_PAIR
    groups = d // width

    def cspec(reverse):
        if reverse:
            return pl.BlockSpec((rows, SCAN_CHUNK, width), lambda bb, gg, i: (bb, geo.bwd_chunk(i), gg))
        return pl.BlockSpec((rows, SCAN_CHUNK, width), lambda bb, gg, i: (bb, i, gg))

    state = pltpu.VMEM((rows * pairs, RWKV_PAIR, RWKV_PAIR), f32)
    y_f, y_b = pl.pallas_call(
        functools.partial(_rwkv_scan_kernel, rows, pairs),
        grid=(b // rows, groups, geo.nc),
        in_specs=[cspec(False)] * 6 + [cspec(True)] * 6,
        out_specs=[cspec(False), cspec(True)],
        out_shape=[sds, sds],
        scratch_shapes=[state, state],
        compiler_params=_params(3),
        name="rwkv_scan",
    )(r, v, kk, lw0, a_0, k0, r, v, kk, lw1, a_1, k1)

    t0 = geo.ctx_tiles if skip_ctx else 0
    post_vec = jnp.stack([ln_w, ln_b] + [jnp.zeros_like(ln_w)] * 6)
    rows = _post_rows(geo, 4)
    mix_specs = [_tile_spec(geo, d, 0, t0, rows)] * 4 + [_const_spec((8, d))]
    return _post_mlp(geo, x, mods, gains, _pre_rwkv, [y_f, y_b, gate, bonus_g, post_vec], mix_specs,
                     w_o.astype(bf16), w_in, w_out, skip_ctx, "rwkv_post", rows)


def _hgrn_lower_bound(lb_param, layer):
    p = jax.nn.softmax(lb_param.astype(f32), axis=0)
    return (jnp.cumsum(p, axis=0) - p[0])[layer]


def kernel(x, c, ctx, c_ctx, w_mod, b_mod, g_pre_mix, g_post_mix, g_pre_mlp, g_post_mlp, w_mlp_in, w_mlp_out, attn_w_qkv, attn_w_o, attn_sink, gla_w_in, gla_w_gate_down, gla_w_gate_up, gla_gate_bias, gla_g_norm, gla_w_o, rwkv_mix, rwkv_w_rkv, rwkv_w0, rwkv_w_down, rwkv_w_up, rwkv_a0, rwkv_a_down, rwkv_a_up, rwkv_g_down, rwkv_g_up, rwkv_k_k, rwkv_k_a, rwkv_r_k, rwkv_ln_w, rwkv_ln_b, rwkv_w_o, hgrn_w_in, hgrn_w_f, hgrn_lb, hgrn_g_norm, hgrn_w_o):
    depth = w_mod.shape[0]
    geo = _Geom(x.shape[0], ctx.shape[1], x.shape[1])
    mods_all = _mod_vectors(c, c_ctx, w_mod, b_mod)
    xs = (ctx, x)
    for i in range(depth):
        kind, j = i % 4, i // 4
        skip_ctx = i == depth - 1
        mods = mods_all[i]
        gains = jnp.stack([g_pre_mix[i], g_post_mix[i], g_pre_mlp[i], g_post_mlp[i]] + [jnp.zeros_like(g_pre_mix[i])] * 4)
        w_in, w_out = w_mlp_in[i].astype(bf16), w_mlp_out[i].astype(bf16)
        if kind == 0:
            xs = _attn_layer(geo, xs, mods, gains, attn_w_qkv[j], attn_w_o[j], attn_sink[j], w_in, w_out, skip_ctx)
        elif kind == 1:
            xs = _gla_layer(geo, xs, mods, gains, gla_w_in[j], gla_w_gate_down[j], gla_w_gate_up[j],
                            gla_gate_bias[j], gla_g_norm[j], gla_w_o[j], w_in, w_out, skip_ctx)
        elif kind == 2:
            xs = _rwkv_layer(geo, xs, mods, gains, rwkv_mix[j], rwkv_w_rkv[j], rwkv_w0[j], rwkv_w_down[j],
                             rwkv_w_up[j], rwkv_a0[j], rwkv_a_down[j], rwkv_a_up[j], rwkv_g_down[j], rwkv_g_up[j],
                             rwkv_k_k[j], rwkv_k_a[j], rwkv_r_k[j], rwkv_ln_w[j], rwkv_ln_b[j], rwkv_w_o[j],
                             w_in, w_out, skip_ctx)
        else:
            xs = _hgrn_layer(geo, xs, mods, gains, hgrn_w_in[j], hgrn_w_f[j], _hgrn_lower_bound(hgrn_lb, i),
                             hgrn_g_norm[j], hgrn_w_o[j], w_in, w_out, skip_ctx)
        if skip_ctx:
            return xs
    return xs[:, geo.n_ctx:]
```

```python
import functools

import jax
import jax.numpy as jnp
from jax import lax
from jax.experimental import pallas as pl
from jax.experimental.pallas import tpu as pltpu

f32 = jnp.float32
bf16 = jnp.bfloat16

D_MODEL = 1024
N_MOD = 6
MLP_HIDDEN = 4 * D_MODEL
NORM_EPS = 1e-6
NEG_INF = -1e30
GRID_W = 64

ATTN_HEADS = 16
ATTN_KV_HEADS = 4
ATTN_GROUP = ATTN_HEADS // ATTN_KV_HEADS
HEAD_DIM = 64
WINDOW = 128
ATTN_BLOCK = 128
ROPE_BASE = 10000.0
ROPE_AXIS_DIM = HEAD_DIM // 2
ROPE_FREQS = ROPE_AXIS_DIM // 2

GLA_HEADS = 4
GLA_KEY_DIM = D_MODEL // 2
GLA_DK = GLA_KEY_DIM // GLA_HEADS
GLA_DV = D_MODEL // GLA_HEADS
GLA_GATE_RANK = 16
GLA_TAU = 16.0
SCAN_CHUNK = 64

RWKV_HEAD_SIZE = 64
RWKV_LN_EPS = 64e-5
L2_EPS = 1e-12
RWKV_PAIR = 2 * RWKV_HEAD_SIZE
RWKV_DECAY_SCALE = 0.6065306597126334

HGRN_EXPAND = 128
HGRN_HEADS = D_MODEL // HGRN_EXPAND

LANES = 128
MOD_ROWS = 8
VMEM_LIMIT = 56 * 1024 * 1024

NT = (((1,), (1,)), ((), ()))
TN = (((0,), (0,)), ((), ()))


def _params(n_grid):
    return pltpu.CompilerParams(dimension_semantics=("arbitrary",) * n_grid, vmem_limit_bytes=VMEM_LIMIT)


def _const_spec(shape):
    nd = len(shape)
    return pl.BlockSpec(shape, lambda *_: (0,) * nd, pipeline_mode=pl.Buffered(1))


def _dot(a, b):
    return jnp.dot(a, b, preferred_element_type=f32)


def _sigmoid(x):
    return 0.5 * jnp.tanh(0.5 * x) + 0.5


def _silu(x):
    return x * _sigmoid(x)


def _rms(x, gain):
    return x * lax.rsqrt(jnp.mean(x * x, axis=-1, keepdims=True) + NORM_EPS) * gain


def _modulate(x, gain, shift, scale):
    return _rms(x, gain) * (1.0 + scale) + shift


def _seg_sum(x, seg):
    r = lax.broadcasted_iota(jnp.int32, (LANES, LANES), 0) // seg
    c = lax.broadcasted_iota(jnp.int32, (LANES, LANES), 1) // seg
    ones_bd = (r == c).astype(bf16)
    hi = x.astype(bf16)
    lo = (x - hi.astype(f32)).astype(bf16)
    outs = []
    for j in range(x.shape[1] // LANES):
        sl = slice(j * LANES, (j + 1) * LANES)
        outs.append(_dot(hi[:, sl], ones_bd) + _dot(lo[:, sl], ones_bd))
    return jnp.concatenate(outs, axis=1)


def _cumsum_rows(g, reverse):
    n = g.shape[0]
    ri = lax.broadcasted_iota(jnp.int32, (n, n), 0)
    ci = lax.broadcasted_iota(jnp.int32, (n, n), 1)
    tri = ((ci >= ri) if reverse else (ci <= ri)).astype(bf16)
    hi = g.astype(bf16)
    lo = (g - hi.astype(f32)).astype(bf16)
    return _dot(tri, hi) + _dot(tri, lo)


def _mod_kernel(c_ref, w_ref, b_ref, o_ref):
    o_ref[0] = _dot(_silu(c_ref[...]).astype(bf16), w_ref[0]) + b_ref[0]


def _mod_vectors(c, c_ctx, w_mod, b_mod):
    depth, d, _ = w_mod.shape
    batch = c.shape[0]
    rows = -(-(batch + 1) // 8) * 8
    cc = jnp.zeros((rows, d), f32).at[:batch].set(c).at[batch].set(c_ctx)
    out = pl.pallas_call(
        _mod_kernel,
        grid=(depth, N_MOD),
        in_specs=[
            pl.BlockSpec((rows, d), lambda i, j: (0, 0)),
            pl.BlockSpec((1, d, d), lambda i, j: (i, 0, j)),
            pl.BlockSpec((1, 1, d), lambda i, j: (i, 0, j)),
        ],
        out_specs=pl.BlockSpec((1, rows, d), lambda i, j: (i, 0, j)),
        out_shape=jax.ShapeDtypeStruct((depth, rows, N_MOD * d), f32),
        compiler_params=_params(2),
        name="mod_vectors",
    )(cc, w_mod.astype(bf16), b_mod.reshape(depth, 1, N_MOD * d))
    lat = out[:, :batch].reshape(depth, batch, 1, N_MOD, d)
    con = jnp.broadcast_to(out[:, batch].reshape(depth, 1, 1, N_MOD, d), lat.shape)
    mods = jnp.concatenate([con, lat], axis=2)
    return jnp.pad(mods, ((0, 0), (0, 0), (0, 0), (0, MOD_ROWS - N_MOD), (0, 0)))


class _Geom:
    def __init__(self, batch, n_ctx, n_lat):
        self.batch, self.n_ctx, self.n_lat = batch, n_ctx, n_lat
        self.t = n_ctx + n_lat
        self.tm = min(256, n_ctx)
        assert n_ctx % self.tm == 0 and n_lat % self.tm == 0
        assert n_ctx % ATTN_BLOCK == 0 and n_lat % ATTN_BLOCK == 0 and n_lat % GRID_W == 0
        self.nt = self.t // self.tm
        self.ctx_tiles = n_ctx // self.tm
        self.nc = self.t // SCAN_CHUNK
        self.ctx_chunks = n_ctx // SCAN_CHUNK

    def x_spec(self, t0=0, rows=1):
        return pl.BlockSpec((rows, self.tm, D_MODEL), lambda b, t: (b, t + t0, 0))

    def x_in(self, x, t0=0, rows=1):
        if not isinstance(x, tuple):
            return [x], [self.x_spec(t0, rows)]
        ct = self.ctx_tiles
        block = (rows, self.tm, D_MODEL)
        return list(x), [pl.BlockSpec(block, lambda b, t: (b, jnp.minimum(t + t0, ct - 1), 0)),
                         pl.BlockSpec(block, lambda b, t: (b, jnp.maximum(t + t0 - ct, 0), 0))]

    def mod_spec(self, t0=0, rows=1):
        ct = self.ctx_tiles
        return pl.BlockSpec((rows, 1, MOD_ROWS, D_MODEL), lambda b, t: (b, ((t + t0) >= ct).astype(jnp.int32), 0, 0))

    def bwd_chunk(self, i):
        cc = self.ctx_chunks
        return jnp.where(i < cc, cc - 1 - i, self.nc - 1 + cc - i)


LIN_ROWS_PER_STEP = 2


def _lin_kernel(x_ref, m_ref, g_ref, w_ref, o_ref):
    rows, tm = x_ref.shape[0], x_ref.shape[1]
    h = [_modulate(x_ref[r], g_ref[0:1], m_ref[r, 0, 0:1], m_ref[r, 0, 1:2]).astype(bf16) for r in range(rows)]
    z = _dot(jnp.concatenate(h, axis=0), w_ref[...])
    for r in range(rows):
        o_ref[r] = z[r * tm:(r + 1) * tm]


def _mod_linear(geo, x, mods, gains, w, name):
    n = w.shape[1]
    rows = LIN_ROWS_PER_STEP if geo.batch % LIN_ROWS_PER_STEP == 0 else 1
    return pl.pallas_call(
        _lin_kernel,
        grid=(geo.batch // rows, geo.nt),
        in_specs=[geo.x_spec(0, rows), geo.mod_spec(0, rows), _const_spec((8, D_MODEL)), _const_spec((D_MODEL, n))],
        out_specs=pl.BlockSpec((rows, geo.tm, n), lambda b, t: (b, t, 0)),
        out_shape=jax.ShapeDtypeStruct((geo.batch, geo.t, n), f32),
        compiler_params=_params(2),
        name=name,
    )(x, mods, gains, w)


def _read_tile(x_refs, row, is_ctx):
    if len(x_refs) == 1:
        return x_refs[0][row]
    return jnp.where(is_ctx, x_refs[0][row], x_refs[1][row])


def _head_rms(o, gain, width):
    outs = []
    for h in range(o.shape[1] // width):
        oh = o[:, h * width:(h + 1) * width]
        outs.append(oh * lax.rsqrt(jnp.mean(oh * oh, axis=-1, keepdims=True) + NORM_EPS) * gain)
    return jnp.concatenate(outs, axis=1)


def _pre_attn(refs, row):
    (o_ref,) = refs
    return o_ref[row]


def _pre_scan(width, refs, row):
    of_ref, ob_ref, gate_ref, gn_ref = refs
    o = _head_rms(of_ref[row] + ob_ref[row], gn_ref[...], width)
    return (o * _silu(gate_ref[row])).astype(bf16)


def _pre_rwkv(refs, row):
    yf_ref, yb_ref, g_ref, bg_ref, vec_ref = refs
    y = yf_ref[row] + yb_ref[row]
    inv_n = 1.0 / RWKV_HEAD_SIZE
    mu = _seg_sum(y, RWKV_HEAD_SIZE) * inv_n
    dlt = y - mu
    var = _seg_sum(dlt * dlt, RWKV_HEAD_SIZE) * inv_n
    yn = dlt * lax.rsqrt(var + RWKV_LN_EPS) * vec_ref[0:1] + vec_ref[1:2]
    return (yn * g_ref[row] + bg_ref[row]).astype(bf16)


def _post_kernel(pre, n_x, n_mix, first_ctx_tiles, *refs):
    x_refs, (m_ref, g_ref), mix_refs = refs[:n_x], refs[n_x:n_x + 2], refs[n_x + 2:n_x + 2 + n_mix]
    wo_ref, win_ref, wout_ref, out_ref = refs[n_x + 2 + n_mix:]
    rows = range(out_ref.shape[0])
    is_ctx = pl.program_id(1) < first_ctx_tiles
    o = [pre(mix_refs, r) for r in rows]
    y = [_dot(o[r], wo_ref[...]) for r in rows]
    x1 = [_read_tile(x_refs, r, is_ctx) + m_ref[r, 0, 2:3] * _rms(y[r], g_ref[1:2]) for r in rows]
    h2 = [_modulate(x1[r], g_ref[2:3], m_ref[r, 0, 3:4], m_ref[r, 0, 4:5]).astype(bf16) for r in rows]
    us = [[] for _ in rows]
    for c in range(MLP_HIDDEN // D_MODEL):
        cols = slice(c * D_MODEL, (c + 1) * D_MODEL)
        for r in rows:
            us[r].append(jnp.square(jnp.maximum(_dot(h2[r], win_ref[:, cols]), 0.0)).astype(bf16))
    acc = [_dot(jnp.concatenate(us[r], axis=1), wout_ref[...]) for r in rows]
    for r in rows:
        out_ref[r] = x1[r] + m_ref[r, 0, 5:6] * _rms(acc[r], g_ref[3:4])


POST_ROWS_PER_STEP = 2


def _post_rows(geo, n_tile_inputs):
    tile = geo.tm * D_MODEL * 4
    weights = 2 * (D_MODEL * D_MODEL + 2 * D_MODEL * MLP_HIDDEN)
    for rows in (POST_ROWS_PER_STEP, 1):
        streamed = 2 * (n_tile_inputs + 2) * rows * tile
        live = 6 * rows * tile
        if geo.batch % rows == 0 and weights + streamed + live <= VMEM_LIMIT:
            return rows
    return 1


def _post_mlp(geo, x, mods, gains, pre, mix_args, mix_specs, w_o, w_in, w_out, skip_ctx, name, rows):
    t0 = geo.ctx_tiles if skip_ctx else 0
    x_args, x_specs = geo.x_in(x, t0, rows)
    return pl.pallas_call(
        functools.partial(_post_kernel, pre, len(x_args), len(mix_args), geo.ctx_tiles - t0),
        grid=(geo.batch // rows, geo.nt - t0),
        in_specs=x_specs + [geo.mod_spec(t0, rows), _const_spec((8, D_MODEL))] + mix_specs + [
            _const_spec((D_MODEL, D_MODEL)), _const_spec((D_MODEL, MLP_HIDDEN)), _const_spec((MLP_HIDDEN, D_MODEL))],
        out_specs=pl.BlockSpec((rows, geo.tm, D_MODEL), lambda b, t: (b, t, 0)),
        out_shape=jax.ShapeDtypeStruct((geo.batch, geo.t - t0 * geo.tm, D_MODEL), f32),
        compiler_params=_params(2),
        name=name,
    )(*x_args, mods, gains, *mix_args, w_o, w_in, w_out)


def _tile_spec(geo, width, col, t0, rows):
    return pl.BlockSpec((rows, geo.tm, width), lambda b, t: (b, t + t0, col))


def _rope(x, cos, sin):
    w = x.shape[1]
    reps = w // LANES
    cw = jnp.concatenate([cos] * reps, axis=1)
    sw = jnp.concatenate([sin] * reps, axis=1)
    lane = lax.broadcasted_iota(jnp.int32, x.shape, 1)
    first = (lane % ROPE_AXIS_DIM) < ROPE_FREQS
    partner = jnp.where(first, pltpu.roll(x, w - ROPE_FREQS, axis=1), pltpu.roll(x, ROPE_FREQS, axis=1))
    return x * cw + partner * sw


def _attn_proj_kernel(ctx_tiles, *refs):
    x_refs, (m_ref, g_ref, cos_ref, sin_ref, wq_ref, wk_ref, wv_ref, q_ref, k_ref, v_ref) = refs[:-10], refs[-10:]
    rows = range(q_ref.shape[0])
    is_ctx = pl.program_id(1) < ctx_tiles
    cos, sin = cos_ref[...], sin_ref[...]
    h = [_modulate(_read_tile(x_refs, r, is_ctx), g_ref[0:1], m_ref[r, 0, 0:1], m_ref[r, 0, 1:2]).astype(bf16)
         for r in rows]
    q = [_dot(h[r], wq_ref[...]) * (HEAD_DIM ** -0.5) for r in rows]
    k = [_dot(h[r], wk_ref[...]) for r in rows]
    v = [_dot(h[r], wv_ref[...]) for r in rows]
    q = [_rope(q[r], cos, sin) for r in rows]
    k = [_rope(k[r], cos, sin) for r in rows]
    for r in rows:
        for hd in range(ATTN_HEADS):
            q_ref[r, hd] = q[r][:, hd * HEAD_DIM:(hd + 1) * HEAD_DIM].astype(bf16)
        for hd in range(ATTN_KV_HEADS):
            k_ref[r, hd] = k[r][:, hd * HEAD_DIM:(hd + 1) * HEAD_DIM].astype(bf16)
            v_ref[r, hd] = v[r][:, hd * HEAD_DIM:(hd + 1) * HEAD_DIM].astype(bf16)


ATTN_ROWS_PER_STEP = 4


def _attn_core_kernel(geo, sink_ref, q_ref, kp_ref, kc_ref, kn_ref, kx_ref, vp_ref, vc_ref, vn_ref, vx_ref, o_ref):
    qb = pl.program_id(1)
    blk = ATTN_BLOCK
    n_loc = 3 * blk
    n_keys = n_loc + geo.n_ctx
    rows = ATTN_GROUP * blk
    first_lat = geo.n_ctx // blk
    n_blocks = geo.t // blk
    row = lax.broadcasted_iota(jnp.int32, (blk, n_keys), 0)
    col = lax.broadcasted_iota(jnp.int32, (blk, n_keys), 1)
    kblk = qb - 1 + col // blk
    ok_local = ((jnp.abs(col - blk - row) <= WINDOW) & (qb >= first_lat) & (kblk >= first_lat) & (kblk < n_blocks))
    ok = (col >= n_loc) | ok_local
    groups = range(ATTN_GROUP)
    units = [(r, j) for r in range(q_ref.shape[0]) for j in range(ATTN_KV_HEADS)]

    def scores(r, j):
        q4 = q_ref[r, ATTN_GROUP * j:ATTN_GROUP * (j + 1)].reshape(rows, HEAD_DIM)
        keys = jnp.concatenate([kp_ref[r, j], kc_ref[r, j], kn_ref[r, j], kx_ref[r, j]], axis=0)
        return lax.dot_general(q4, keys, NT, preferred_element_type=f32)

    def softmax(j, s):
        sg = [jnp.where(ok, s[g * blk:(g + 1) * blk], NEG_INF) for g in groups]
        sk = [sink_ref[ATTN_GROUP * j + g] for g in groups]
        m = [jnp.maximum(jnp.max(sg[g], axis=-1, keepdims=True), sk[g]) for g in groups]
        e = [jnp.exp(sg[g] - m[g]) for g in groups]
        inv = [1.0 / (jnp.sum(e[g], axis=-1, keepdims=True) + jnp.exp(sk[g] - m[g])) for g in groups]
        return jnp.concatenate([e[g].astype(bf16) for g in groups], axis=0), inv

    def values(r, j, p, inv):
        vals = jnp.concatenate([vp_ref[r, j], vc_ref[r, j], vn_ref[r, j], vx_ref[r, j]], axis=0)
        o4 = _dot(p, vals)
        return [o4[g * blk:(g + 1) * blk] * inv[g] for g in groups]

    outs = [None] * len(units)
    s_next = scores(*units[0])
    soft_prev = None
    for i, (r, j) in enumerate(units):
        s_cur = s_next
        if i + 1 < len(units):
            s_next = scores(*units[i + 1])
        soft_cur = softmax(j, s_cur)
        if soft_prev is not None:
            outs[i - 1] = values(*units[i - 1], *soft_prev)
        soft_prev = soft_cur
    outs[-1] = values(*units[-1], *soft_prev)
    for r in range(q_ref.shape[0]):
        heads = outs[r * ATTN_KV_HEADS:(r + 1) * ATTN_KV_HEADS]
        o_ref[r] = jnp.concatenate([o for head_outs in heads for o in head_outs], axis=1).astype(bf16)


def _rope_tables(geo):
    inv_freq = ROPE_BASE ** (-jnp.arange(ROPE_FREQS, dtype=f32) * 2.0 / ROPE_AXIS_DIM)
    pos = jnp.arange(geo.n_lat)
    row = (pos // GRID_W).astype(f32)
    col = (pos % GRID_W).astype(f32)
    ang = jnp.stack([row[:, None] * inv_freq, col[:, None] * inv_freq], axis=1)
    cos = jnp.cos(ang)
    sin = jnp.sin(ang)
    cos_h = jnp.concatenate([cos, cos], axis=2).reshape(geo.n_lat, HEAD_DIM)
    sin_h = jnp.concatenate([-sin, sin], axis=2).reshape(geo.n_lat, HEAD_DIM)
    cos_t = jnp.concatenate([jnp.ones((geo.n_ctx, HEAD_DIM), f32), cos_h], axis=0)
    sin_t = jnp.concatenate([jnp.zeros((geo.n_ctx, HEAD_DIM), f32), sin_h], axis=0)
    return jnp.tile(cos_t, (1, 2)), jnp.tile(sin_t, (1, 2))


def _attn_layer(geo, x, mods, gains, w_qkv, w_o, sink, w_in, w_out, skip_ctx):
    b, t, tm = geo.batch, geo.t, geo.tm
    q_cols = ATTN_HEADS * HEAD_DIM
    kv_cols = ATTN_KV_HEADS * HEAD_DIM
    wb = w_qkv.astype(bf16)
    cos_t, sin_t = _rope_tables(geo)
    tab_spec = pl.BlockSpec((tm, LANES), lambda bb, tt: (tt, 0))
    rows = LIN_ROWS_PER_STEP if b % LIN_ROWS_PER_STEP == 0 else 1
    x_args, x_specs = geo.x_in(x, 0, rows)
    q, k, v = pl.pallas_call(
        functools.partial(_attn_proj_kernel, geo.ctx_tiles),
        grid=(b // rows, geo.nt),
        in_specs=x_specs + [geo.mod_spec(0, rows), _const_spec((8, D_MODEL)), tab_spec, tab_spec,
                  _const_spec((D_MODEL, q_cols)), _const_spec((D_MODEL, kv_cols)), _const_spec((D_MODEL, kv_cols))],
        out_specs=[pl.BlockSpec((rows, ATTN_HEADS, tm, HEAD_DIM), lambda bb, tt: (bb, 0, tt, 0)),
                   pl.BlockSpec((rows, ATTN_KV_HEADS, tm, HEAD_DIM), lambda bb, tt: (bb, 0, tt, 0)),
                   pl.BlockSpec((rows, ATTN_KV_HEADS, tm, HEAD_DIM), lambda bb, tt: (bb, 0, tt, 0))],
        out_shape=[jax.ShapeDtypeStruct((b, ATTN_HEADS, t, HEAD_DIM), bf16),
                   jax.ShapeDtypeStruct((b, ATTN_KV_HEADS, t, HEAD_DIM), bf16),
                   jax.ShapeDtypeStruct((b, ATTN_KV_HEADS, t, HEAD_DIM), bf16)],
        compiler_params=_params(2),
        name="attn_proj",
    )(*x_args, mods, gains, cos_t, sin_t, wb[:, :q_cols], wb[:, q_cols:q_cols + kv_cols], wb[:, q_cols + kv_cols:])

    blk = ATTN_BLOCK
    n_blocks = t // blk

    arows = ATTN_ROWS_PER_STEP if b % ATTN_ROWS_PER_STEP == 0 else 1

    def kv_spec(off):
        return pl.BlockSpec((arows, ATTN_KV_HEADS, blk, HEAD_DIM),
                            lambda bb, qb: (bb, 0, jnp.clip(qb + off, 0, n_blocks - 1), 0))

    ctx_spec = pl.BlockSpec((arows, ATTN_KV_HEADS, geo.n_ctx, HEAD_DIM), lambda bb, qb: (bb, 0, 0, 0))
    o = pl.pallas_call(
        functools.partial(_attn_core_kernel, geo),
        grid=(b // arows, n_blocks),
        in_specs=[pl.BlockSpec(memory_space=pltpu.SMEM),
                  pl.BlockSpec((arows, ATTN_HEADS, blk, HEAD_DIM), lambda bb, qb: (bb, 0, qb, 0)),
                  kv_spec(-1), kv_spec(0), kv_spec(1), ctx_spec,
                  kv_spec(-1), kv_spec(0), kv_spec(1), ctx_spec],
        out_specs=pl.BlockSpec((arows, blk, q_cols), lambda bb, qb: (bb, qb, 0)),
        out_shape=jax.ShapeDtypeStruct((b, t, q_cols), bf16),
        compiler_params=_params(2),
        name="attn_core",
    )(sink, q, k, k, k, k, v, v, v, v)

    t0 = geo.ctx_tiles if skip_ctx else 0
    rows = _post_rows(geo, 1)
    return _post_mlp(geo, x, mods, gains, _pre_attn, [o], [_tile_spec(geo, D_MODEL, 0, t0, rows)],
                     w_o.astype(bf16), w_in, w_out, skip_ctx, "attn_post", rows)


def _gated_prep(entries, heads, dk, dv):
    sums = [_cumsum_rows(g, reverse) for _, _, _, g, _, _, reverse in entries]
    units = []
    for (q, k, v, g, st_ref, base, reverse), b in zip(entries, sums):
        n = q.shape[0]
        btot = b[0:1] if reverse else b[n - 1:n]
        q_dec = (q * jnp.exp(b)).astype(bf16)
        k_inv32 = k * jnp.exp(-b)
        k_inv = k_inv32.astype(bf16)
        dec = jnp.exp(btot)
        k_end = (k_inv32 * dec).astype(bf16)
        vb = v.astype(bf16)
        ri = lax.broadcasted_iota(jnp.int32, (n, n), 0)
        ci = lax.broadcasted_iota(jnp.int32, (n, n), 1)
        tri = (ci >= ri) if reverse else (ci <= ri)
        for h in range(heads):
            ks = slice(h * dk, (h + 1) * dk)
            vs = slice(h * dv, (h + 1) * dv)
            units.append(dict(q=q_dec[:, ks], ki=k_inv[:, ks], ke=k_end[:, ks], v=vb[:, vs], dec=dec[:, ks],
                              tri=tri, st=st_ref[base + h], ref=st_ref, h=base + h))
    return units


def _gated_matmuls(units, heads):
    for u in units:
        a = lax.dot_general(u["q"], u["ki"], NT, preferred_element_type=f32)
        u["a"] = jnp.where(u["tri"], a, 0.0).astype(bf16)
        u["qs"] = lax.dot_general(u["q"], u["st"].astype(bf16), NT, preferred_element_type=f32)
    for u in units:
        u["o"] = _dot(u["a"], u["v"]) + u["qs"]
        u["new"] = u["st"] * u["dec"] + lax.dot_general(u["v"], u["ke"], TN, preferred_element_type=f32)
    for u in units:
        u["ref"][u["h"]] = u["new"]
    return [jnp.concatenate([u["o"] for u in units[i:i + heads]], axis=1) for i in range(0, len(units), heads)]


def _gated_chunk(groups, heads, dk, dv, staggered):
    def prep(group):
        return _gated_prep([thunk() for thunk in group], heads, dk, dv)

    if not staggered:
        units = prep([thunk for group in groups for thunk in group])
        per_row = len(groups[0])
        outs = _gated_matmuls(units, heads)
        return [outs[i:i + per_row] for i in range(0, len(outs), per_row)]
    outs = []
    ready = prep(groups[0])
    for i in range(len(groups)):
        nxt = prep(groups[i + 1]) if i + 1 < len(groups) else None
        outs.append(_gated_matmuls(ready, heads))
        ready = nxt
    return outs


def _gla_gate(zd, wu_ref, bias_ref, d):
    zg = _dot(zd.astype(bf16), wu_ref[d]) + bias_ref[d:d + 1]
    return (jnp.minimum(zg, 0.0) - jnp.log(1.0 + jnp.exp(-jnp.abs(zg)))) * (1.0 / GLA_TAU)


def _gla_scan_kernel(qkf_ref, vf_ref, zdf_ref, qkb_ref, vb_ref, zdb_ref, wu_ref, bias_ref,
                     of_ref, ob_ref, sf_ref, sb_ref):
    @pl.when(pl.program_id(1) == 0)
    def _():
        sf_ref[...] = jnp.zeros_like(sf_ref)
        sb_ref[...] = jnp.zeros_like(sb_ref)

    def entry(row, d, qk_ref, v_ref, zd_ref, s_ref):
        qk = qk_ref[row]
        return (qk[:, :GLA_KEY_DIM] * (GLA_DK ** -0.5), qk[:, GLA_KEY_DIM:], v_ref[row],
                _gla_gate(zd_ref[row], wu_ref, bias_ref, d), s_ref, row * GLA_HEADS, d == 1)

    groups = [[functools.partial(entry, row, 0, qkf_ref, vf_ref, zdf_ref, sf_ref),
               functools.partial(entry, row, 1, qkb_ref, vb_ref, zdb_ref, sb_ref)]
              for row in range(qkf_ref.shape[0])]
    for row, (o_f, o_b) in enumerate(_gated_chunk(groups, GLA_HEADS, GLA_DK, GLA_DV, staggered=False)):
        of_ref[row] = o_f
        ob_ref[row] = o_b


def _hgrn_scan_kernel(qf_ref, if_ref, zff_ref, qb_ref, ib_ref, zfb_ref, lb_ref, of_ref, ob_ref, sf_ref, sb_ref):
    @pl.when(pl.program_id(1) == 0)
    def _():
        sf_ref[...] = jnp.zeros_like(sf_ref)
        sb_ref[...] = jnp.zeros_like(sb_ref)

    lb = lb_ref[...]
    def entry(row, d, q_ref, i_ref, zf_ref, s_ref):
        f = lb + (1.0 - lb) * _sigmoid(zf_ref[row])
        return (_silu(q_ref[row]), 1.0 - f, i_ref[row], jnp.log(f), s_ref, row * HGRN_HEADS, d == 1)

    groups = [[functools.partial(entry, row, 0, qf_ref, if_ref, zff_ref, sf_ref),
               functools.partial(entry, row, 1, qb_ref, ib_ref, zfb_ref, sb_ref)]
              for row in range(qf_ref.shape[0])]
    for row, (o_f, o_b) in enumerate(_gated_chunk(groups, HGRN_HEADS, HGRN_EXPAND, HGRN_EXPAND, staggered=True)):
        of_ref[row] = o_f
        ob_ref[row] = o_b


SCAN_ROWS_PER_STEP = 8


def _scan_rows(geo):
    return SCAN_ROWS_PER_STEP if geo.batch % SCAN_ROWS_PER_STEP == 0 else 1


def _chunk_spec(geo, width, col, reverse):
    rows = _scan_rows(geo)
    if reverse:
        return pl.BlockSpec((rows, SCAN_CHUNK, width), lambda b, i: (b, geo.bwd_chunk(i), col))
    return pl.BlockSpec((rows, SCAN_CHUNK, width), lambda b, i: (b, i, col))


def _scan_call(geo, body, args, specs, heads, dk, dv, name):
    out_sds = jax.ShapeDtypeStruct((geo.batch, geo.t, heads * dv), f32)
    rows = _scan_rows(geo)
    state = pltpu.VMEM((rows * heads, dv, dk), f32)
    return pl.pallas_call(
        body,
        grid=(geo.batch // rows, geo.nc),
        in_specs=specs,
        out_specs=[_chunk_spec(geo, heads * dv, 0, False), _chunk_spec(geo, heads * dv, 0, True)],
        out_shape=[out_sds, out_sds],
        scratch_shapes=[state, state],
        compiler_params=_params(2),
        name=name,
    )(*args)


def _gla_layer(geo, x, mods, gains, w_in_p, w_gd, w_gu, g_bias, g_norm, w_o, w_in, w_out, skip_ctx):
    r = GLA_GATE_RANK
    n_z = 2 * GLA_KEY_DIM + 2 * D_MODEL
    w_all = jnp.concatenate([w_in_p, w_gd[0], w_gd[1], jnp.zeros((D_MODEL, LANES - 2 * r), f32)], axis=1).astype(bf16)
    z = _mod_linear(geo, x, mods, gains, w_all, "gla_proj")
    wu = jnp.zeros((2, LANES, GLA_KEY_DIM), f32).at[0, :r].set(w_gu[0]).at[1, r:2 * r].set(w_gu[1]).astype(bf16)
    zd_col = n_z // LANES
    specs = []
    for rev in (False, True):
        specs += [_chunk_spec(geo, 2 * GLA_KEY_DIM, 0, rev), _chunk_spec(geo, D_MODEL, 1, rev),
                  _chunk_spec(geo, LANES, zd_col, rev)]
    specs += [_const_spec((2, LANES, GLA_KEY_DIM)), _const_spec((2, GLA_KEY_DIM))]
    o_f, o_b = _scan_call(geo, _gla_scan_kernel, [z, z, z, z, z, z, wu, g_bias], specs,
                          GLA_HEADS, GLA_DK, GLA_DV, "gla_scan")
    t0 = geo.ctx_tiles if skip_ctx else 0
    rows = _post_rows(geo, 3)
    mix_specs = [_tile_spec(geo, D_MODEL, 0, t0, rows), _tile_spec(geo, D_MODEL, 0, t0, rows),
                 _tile_spec(geo, D_MODEL, 2, t0, rows), _const_spec((1, GLA_DV))]
    return _post_mlp(geo, x, mods, gains, functools.partial(_pre_scan, GLA_DV),
                     [o_f, o_b, z, g_norm.reshape(1, GLA_DV)], mix_specs,
                     w_o.astype(bf16), w_in, w_out, skip_ctx, "gla_post", rows)


def _hgrn_layer(geo, x, mods, gains, w_in_p, w_f, lower_bound, g_norm, w_o, w_in, w_out, skip_ctx):
    w_all = jnp.concatenate([w_in_p, w_f[0], w_f[1]], axis=1).astype(bf16)
    z = _mod_linear(geo, x, mods, gains, w_all, "hgrn_proj")
    specs = []
    for rev in (False, True):
        specs += [_chunk_spec(geo, D_MODEL, 0, rev), _chunk_spec(geo, D_MODEL, 1, rev),
                  _chunk_spec(geo, D_MODEL, 4 if rev else 3, rev)]
    specs += [_const_spec((1, D_MODEL))]
    o_f, o_b = _scan_call(geo, _hgrn_scan_kernel, [z, z, z, z, z, z, lower_bound.reshape(1, D_MODEL)], specs,
                          HGRN_HEADS, HGRN_EXPAND, HGRN_EXPAND, "hgrn_scan")
    t0 = geo.ctx_tiles if skip_ctx else 0
    rows = _post_rows(geo, 3)
    mix_specs = [_tile_spec(geo, D_MODEL, 0, t0, rows), _tile_spec(geo, D_MODEL, 0, t0, rows),
                 _tile_spec(geo, D_MODEL, 2, t0, rows), _const_spec((1, HGRN_EXPAND))]
    return _post_mlp(geo, x, mods, gains, functools.partial(_pre_scan, HGRN_EXPAND),
                     [o_f, o_b, z, g_norm.reshape(1, HGRN_EXPAND)], mix_specs,
                     w_o.astype(bf16), w_in, w_out, skip_ctx, "hgrn_post", rows)


def _rwkv_proj_kernel(geo, x_ref, xp_ref, xn_ref, m_ref, g_ref, mix_ref, vec_ref, wrkv_ref, wdn_ref, wup_ref,
                      aup_ref, gup_ref, r_ref, v_ref, gate_ref, kk_ref, lw0_ref, lw1_ref, a0_ref, a1_ref,
                      k0_ref, k1_ref, bg_ref):
    t = pl.program_id(1)
    tm = geo.tm
    gain, shift, scale = g_ref[0:1], m_ref[0, 0, 0:1], m_ref[0, 0, 1:2]
    h = _modulate(x_ref[0], gain, shift, scale)
    seg_first = (t == 0) | (t == geo.ctx_tiles)
    seg_last = (t == geo.ctx_tiles - 1) | (t == geo.nt - 1)
    h_prev = jnp.where(seg_first, 0.0, _modulate(xp_ref[0], gain, shift, scale)[7:8])
    h_next = jnp.where(seg_last, 0.0, _modulate(xn_ref[0], gain, shift, scale)[0:1])
    row = lax.broadcasted_iota(jnp.int32, h.shape, 0)
    up = jnp.where(row == 0, h_prev, pltpu.roll(h, 1, axis=0))
    dn = jnp.where(row == tm - 1, h_next, pltpu.roll(h, tm - 1, axis=0))
    dx = 0.5 * (up + dn) - h

    def mixed(n):
        return (h + dx * mix_ref[n:n + 1]).astype(bf16)

    r = _dot(mixed(0), wrkv_ref[0])
    k = _dot(mixed(1), wrkv_ref[1])
    v = _dot(mixed(2), wrkv_ref[2])
    dw = jnp.tanh(_dot(mixed(3), wdn_ref[:, 0:LANES])).astype(bf16)
    da = _dot(mixed(4), wdn_ref[:, LANES:2 * LANES]).astype(bf16)
    dg = _sigmoid(_dot(mixed(5), wdn_ref[:, 2 * LANES:3 * LANES])).astype(bf16)
    gate = _dot(dg, gup_ref[...])
    r_ref[0] = r
    v_ref[0] = v
    gate_ref[0] = gate
    kk = k * vec_ref[4:5]
    kk_ref[0] = kk * lax.rsqrt(jnp.maximum(_seg_sum(kk * kk, RWKV_HEAD_SIZE), L2_EPS * L2_EPS))
    kds = []
    for d, (lw_ref, a_ref, kd_ref) in enumerate(((lw0_ref, a0_ref, k0_ref), (lw1_ref, a1_ref, k1_ref))):
        lw_ref[0] = -RWKV_DECAY_SCALE * _sigmoid(vec_ref[d:d + 1] + _dot(dw, wup_ref[d]))
        a = _sigmoid(vec_ref[2 + d:3 + d] + _dot(da, aup_ref[d]))
        a_ref[0] = a
        kds.append(k * (1.0 + (a - 1.0) * vec_ref[5:6]))
        kd_ref[0] = kds[d]
    bg_ref[0] = _seg_sum(r * (0.5 * (kds[0] + kds[1])) * vec_ref[6:7], RWKV_HEAD_SIZE) * v * gate


def _rwkv_chunk_units(units):
    n = SCAN_CHUNK
    hs = RWKV_HEAD_SIZE
    lane = lax.broadcasted_iota(jnp.int32, (n, RWKV_PAIR), 1)
    t_i = lax.broadcasted_iota(jnp.int32, (n, RWKV_PAIR), 0)
    s_i = lane % hs
    head0 = lane < hs
    ri = lax.broadcasted_iota(jnp.int32, (RWKV_PAIR, RWKV_PAIR), 0)
    ci = lax.broadcasted_iota(jnp.int32, (RWKV_PAIR, RWKV_PAIR), 1)
    same_head = ri // hs == ci // hs
    top_rows = ri < hs

    def block_diag(x):
        return jnp.concatenate([jnp.where(head0, x, 0.0), jnp.where(head0, 0.0, x)], axis=0)

    def inverse_4x4_blocks(nm, reverse):
        toward = RWKV_PAIR - 1 if not reverse else 1
        dist = (s_i - t_i) if reverse else (t_i - s_i)
        near = pltpu.roll(nm, toward, axis=1)
        far = pltpu.roll(nm, (2 * toward) % RWKV_PAIR, axis=1)
        diag1 = jnp.sum(jnp.where(same4 & (dist == 1), nm, 0.0), axis=0, keepdims=True)
        diag2 = jnp.sum(jnp.where(same4 & (dist == 2), nm, 0.0), axis=0, keepdims=True)
        p2 = diag2 + pltpu.roll(diag1, toward, axis=1) * diag1
        two = nm + near * diag1
        three = two + far * p2
        return jnp.where(s_i == t_i, 1.0,
                         jnp.where(same4 & (dist == 1), nm,
                                   jnp.where(same4 & (dist == 2), two, jnp.where(same4 & (dist == 3), three, 0.0))))

    def cross_blocks(b):
        return (t_i // (2 * b) == s_i // (2 * b)) & (t_i // b != s_i // b)

    st = []
    for r, kd, v, kk, a, lw, ht, reverse in units:
        c = _cumsum_rows(lw, reverse)
        tot = c[0:1] if reverse else c[n - 1:n]
        e_neg = jnp.exp(-c)
        dec = jnp.exp(tot)
        kb = jnp.concatenate([kd * e_neg, kk * a * e_neg], axis=0)
        kb_t = kb.T
        kb_sw = pltpu.roll(kb_t, hs, axis=1)
        rhs = jnp.concatenate([jnp.where(same_head, jnp.where(top_rows, kb_t, kb_sw), 0.0),
                               jnp.where(same_head, jnp.where(top_rows, kb_sw, kb_t), 0.0)], axis=1)
        st.append(dict(
            strict=(s_i > t_i) if reverse else (s_i < t_i), incl=(s_i >= t_i) if reverse else (s_i <= t_i),
            a_bar=(-kk * jnp.exp(c - lw)).astype(bf16), r_bar=(r * jnp.exp(c)).astype(bf16), rhs=rhs.astype(bf16),
            v_bd=block_diag(v).astype(bf16), h_t=ht.T.astype(bf16),
            ends=(kb * dec).astype(bf16),
            v=v, ht=ht, dec=dec, reverse=reverse))
    for s in st:
        g = _dot(jnp.concatenate([s["a_bar"], s["r_bar"]], axis=0), s["rhs"])
        s["a_ak"] = jnp.where(s["strict"], g[0:n, 0:RWKV_PAIR], 0.0).astype(bf16)
        s["nmat"] = jnp.where(s["strict"], g[0:n, RWKV_PAIR:], 0.0)
        s["q_k"] = jnp.where(s["incl"], g[n:, 0:RWKV_PAIR], 0.0).astype(bf16)
        s["q_b"] = jnp.where(s["incl"], g[n:, RWKV_PAIR:], 0.0).astype(bf16)
    same4 = t_i // 4 == s_i // 4
    for s in st:
        s["inv"] = inverse_4x4_blocks(s["nmat"], s["reverse"])
    b = 4
    while 2 * b < n:
        for s in st:
            s["pc"] = _dot(s["inv"].astype(bf16), block_diag(jnp.where(cross_blocks(b), s["nmat"], 0.0)).astype(bf16))
        for s in st:
            s["inv"] = s["inv"] + _dot(s["pc"].astype(bf16), block_diag(s["inv"]).astype(bf16))
        b *= 2
    for s in st:
        x = _dot(jnp.concatenate([s["a_bar"], s["a_ak"]], axis=1), jnp.concatenate([s["h_t"], s["v_bd"]], axis=0))
        s["inv"] = s["inv"].astype(bf16)
        s["w"] = _dot(s["inv"], block_diag(x).astype(bf16))
    for s in st:
        s["cw"] = _dot(jnp.where(cross_blocks(n // 2), s["nmat"], 0.0).astype(bf16), block_diag(s["w"]).astype(bf16))
    for s in st:
        s["u"] = s["w"] + _dot(s["inv"], block_diag(s["cw"]).astype(bf16))
    for s in st:
        s["vu_t"] = jnp.concatenate([s["v"], s["u"]], axis=0).T.astype(bf16)
    for s in st:
        s["y"] = _dot(jnp.concatenate([s["r_bar"], s["q_k"], s["q_b"]], axis=1),
                      jnp.concatenate([s["h_t"], s["v_bd"], block_diag(s["u"]).astype(bf16)], axis=0))
    for s in st:
        s["upd"] = _dot(s["vu_t"], s["ends"])
    return [(s["y"], s["ht"] * s["dec"] + jnp.where(same_head, s["upd"], 0.0)) for s in st]


def _rwkv_scan_kernel(rows, pairs, *refs):
    fwd, bwd = refs[0:6], refs[6:12]
    yf_ref, yb_ref, hf_ref, hb_ref = refs[12:]

    @pl.when(pl.program_id(2) == 0)
    def _():
        hf_ref[...] = jnp.zeros_like(hf_ref)
        hb_ref[...] = jnp.zeros_like(hb_ref)

    units, dests = [], []
    for ins, y_ref, h_ref, reverse in ((fwd, yf_ref, hf_ref, False), (bwd, yb_ref, hb_ref, True)):
        r_ref, v_ref, kk_ref, lw_ref, a_ref, kd_ref = ins
        for row in range(rows):
            for p in range(pairs):
                sl = slice(p * RWKV_PAIR, (p + 1) * RWKV_PAIR)
                slot = row * pairs + p
                units.append((r_ref[row, :, sl], kd_ref[row, :, sl], v_ref[row, :, sl], kk_ref[row, :, sl],
                              a_ref[row, :, sl], lw_ref[row, :, sl], h_ref[slot], reverse))
                dests.append((y_ref, h_ref, row, slot, sl))
    for (y, h_new), (y_ref, h_ref, row, slot, sl) in zip(_rwkv_chunk_units(units), dests):
        y_ref[row, :, sl] = y
        h_ref[slot] = h_new


RWKV_PAIRS_PER_STEP = 8
RWKV_ROWS_PER_STEP = 2


def _rwkv_layer(geo, x, mods, gains, mix, w_rkv, w0, w_down, w_up, a0, a_down, a_up, g_down, g_up, k_k, k_a,
                r_k, ln_w, ln_b, w_o, w_in, w_out, skip_ctx):
    b, t, tm = geo.batch, geo.t, geo.tm
    d = D_MODEL
    rank = w_down.shape[-1]
    assert 2 * rank == LANES and a_down.shape[-1] == rank and g_down.shape[-1] == LANES
    mix8 = jnp.pad(mix, ((0, 2), (0, 0)))
    vec = jnp.stack([w0[0], w0[1], a0[0], a0[1], k_k, k_a, r_k.reshape(d), jnp.zeros_like(k_k)])
    w_dn = jnp.concatenate([w_down[0], w_down[1], a_down[0], a_down[1], g_down], axis=1).astype(bf16)

    def padded_up(w):
        return jnp.zeros((2, LANES, d), f32).at[0, :rank].set(w[0]).at[1, rank:].set(w[1]).astype(bf16)

    halo = 8
    n_halo = t // halo
    per = tm // halo
    x_prev = pl.BlockSpec((1, halo, d), lambda bb, tt: (bb, jnp.maximum(tt * per - 1, 0), 0))
    x_next = pl.BlockSpec((1, halo, d), lambda bb, tt: (bb, jnp.minimum((tt + 1) * per, n_halo - 1), 0))
    out_spec = pl.BlockSpec((1, tm, d), lambda bb, tt: (bb, tt, 0))
    sds = jax.ShapeDtypeStruct((b, t, d), f32)
    r, v, gate, kk, lw0, lw1, a_0, a_1, k0, k1, bonus_g = pl.pallas_call(
        functools.partial(_rwkv_proj_kernel, geo),
        grid=(b, geo.nt),
        in_specs=[geo.x_spec(), x_prev, x_next, geo.mod_spec(), _const_spec((8, d)), _const_spec((8, d)),
                  _const_spec((8, d)), _const_spec((3, d, d)), _const_spec((d, 3 * LANES)),
                  _const_spec((2, LANES, d)), _const_spec((2, LANES, d)), _const_spec((LANES, d))],
        out_specs=[out_spec] * 11,
        out_shape=[sds] * 11,
        compiler_params=_params(2),
        name="rwkv_proj",
    )(x, x, x, mods, gains, mix8, vec, w_rkv.astype(bf16), w_dn, padded_up(w_up), padded_up(a_up),
      g_up.astype(bf16))

    pairs = RWKV_PAIRS_PER_STEP
    rows = RWKV_ROWS_PER_STEP if b % RWKV_ROWS_PER_STEP == 0 else 1
    width = pairs * RWKV_PAIR
    groups = d // width

    def cspec(reverse):
        if reverse:
            return pl.BlockSpec((rows, SCAN_CHUNK, width), lambda bb, gg, i: (bb, geo.bwd_chunk(i), gg))
        return pl.BlockSpec((rows, SCAN_CHUNK, width), lambda bb, gg, i: (bb, i, gg))

    state = pltpu.VMEM((rows * pairs, RWKV_PAIR, RWKV_PAIR), f32)
    y_f, y_b = pl.pallas_call(
        functools.partial(_rwkv_scan_kernel, rows, pairs),
        grid=(b // rows, groups, geo.nc),
        in_specs=[cspec(False)] * 6 + [cspec(True)] * 6,
        out_specs=[cspec(False), cspec(True)],
        out_shape=[sds, sds],
        scratch_shapes=[state, state],
        compiler_params=_params(3),
        name="rwkv_scan",
    )(r, v, kk, lw0, a_0, k0, r, v, kk, lw1, a_1, k1)

    t0 = geo.ctx_tiles if skip_ctx else 0
    post_vec = jnp.stack([ln_w, ln_b] + [jnp.zeros_like(ln_w)] * 6)
    rows = _post_rows(geo, 4)
    mix_specs = [_tile_spec(geo, d, 0, t0, rows)] * 4 + [_const_spec((8, d))]
    return _post_mlp(geo, x, mods, gains, _pre_rwkv, [y_f, y_b, gate, bonus_g, post_vec], mix_specs,
                     w_o.astype(bf16), w_in, w_out, skip_ctx, "rwkv_post", rows)


def _hgrn_lower_bound(lb_param, layer):
    p = jax.nn.softmax(lb_param.astype(f32), axis=0)
    return (jnp.cumsum(p, axis=0) - p[0])[layer]


def kernel(x, c, ctx, c_ctx, w_mod, b_mod, g_pre_mix, g_post_mix, g_pre_mlp, g_post_mlp, w_mlp_in, w_mlp_out, attn_w_qkv, attn_w_o, attn_sink, gla_w_in, gla_w_gate_down, gla_w_gate_up, gla_gate_bias, gla_g_norm, gla_w_o, rwkv_mix, rwkv_w_rkv, rwkv_w0, rwkv_w_down, rwkv_w_up, rwkv_a0, rwkv_a_down, rwkv_a_up, rwkv_g_down, rwkv_g_up, rwkv_k_k, rwkv_k_a, rwkv_r_k, rwkv_ln_w, rwkv_ln_b, rwkv_w_o, hgrn_w_in, hgrn_w_f, hgrn_lb, hgrn_g_norm, hgrn_w_o):
    depth = w_mod.shape[0]
    geo = _Geom(x.shape[0], ctx.shape[1], x.shape[1])
    mods_all = _mod_vectors(c, c_ctx, w_mod, b_mod)
    xs = (ctx, x)
    for i in range(depth):
        kind, j = i % 4, i // 4
        skip_ctx = i == depth - 1
        mods = mods_all[i]
        gains = jnp.stack([g_pre_mix[i], g_post_mix[i], g_pre_mlp[i], g_post_mlp[i]] + [jnp.zeros_like(g_pre_mix[i])] * 4)
        w_in, w_out = w_mlp_in[i].astype(bf16), w_mlp_out[i].astype(bf16)
        if kind == 0:
            xs = _attn_layer(geo, xs, mods, gains, attn_w_qkv[j], attn_w_o[j], attn_sink[j], w_in, w_out, skip_ctx)
        elif kind == 1:
            xs = _gla_layer(geo, xs, mods, gains, gla_w_in[j], gla_w_gate_down[j], gla_w_gate_up[j],
                            gla_gate_bias[j], gla_g_norm[j], gla_w_o[j], w_in, w_out, skip_ctx)
        elif kind == 2:
            xs = _rwkv_layer(geo, xs, mods, gains, rwkv_mix[j], rwkv_w_rkv[j], rwkv_w0[j], rwkv_w_down[j],
                             rwkv_w_up[j], rwkv_a0[j], rwkv_a_down[j], rwkv_a_up[j], rwkv_g_down[j], rwkv_g_up[j],
                             rwkv_k_k[j], rwkv_k_a[j], rwkv_r_k[j], rwkv_ln_w[j], rwkv_ln_b[j], rwkv_w_o[j],
                             w_in, w_out, skip_ctx)
        else:
            xs = _hgrn_layer(geo, xs, mods, gains, hgrn_w_in[j], hgrn_w_f[j], _hgrn_lower_bound(hgrn_lb, i),
                             hgrn_g_norm[j], hgrn_w_o[j], w_in, w_out, skip_ctx)
        if skip_ctx:
            return xs
    return xs[:, geo.n_ctx:]
```

```python
import functools

import jax
import jax.numpy as jnp
from jax import lax
from jax.experimental import pallas as pl
from jax.experimental.pallas import tpu as pltpu

f32 = jnp.float32
bf16 = jnp.bfloat16

D_MODEL = 1024
N_MOD = 6
MLP_HIDDEN = 4 * D_MODEL
NORM_EPS = 1e-6
NEG_INF = -1e30
GRID_W = 64

ATTN_HEADS = 16
ATTN_KV_HEADS = 4
ATTN_GROUP = ATTN_HEADS // ATTN_KV_HEADS
HEAD_DIM = 64
WINDOW = 128
ATTN_BLOCK = 128
ROPE_BASE = 10000.0
ROPE_AXIS_DIM = HEAD_DIM // 2
ROPE_FREQS = ROPE_AXIS_DIM // 2

GLA_HEADS = 4
GLA_KEY_DIM = D_MODEL // 2
GLA_DK = GLA_KEY_DIM // GLA_HEADS
GLA_DV = D_MODEL // GLA_HEADS
GLA_GATE_RANK = 16
GLA_TAU = 16.0
SCAN_CHUNK = 64

RWKV_HEAD_SIZE = 64
RWKV_LN_EPS = 64e-5
L2_EPS = 1e-12
RWKV_PAIR = 2 * RWKV_HEAD_SIZE
RWKV_DECAY_SCALE = 0.6065306597126334

HGRN_EXPAND = 128
HGRN_HEADS = D_MODEL // HGRN_EXPAND

LANES = 128
MOD_ROWS = 8
VMEM_LIMIT = 56 * 1024 * 1024

NT = (((1,), (1,)), ((), ()))
TN = (((0,), (0,)), ((), ()))


def _params(n_grid):
    return pltpu.CompilerParams(dimension_semantics=("arbitrary",) * n_grid, vmem_limit_bytes=VMEM_LIMIT)


def _const_spec(shape):
    nd = len(shape)
    return pl.BlockSpec(shape, lambda *_: (0,) * nd, pipeline_mode=pl.Buffered(1))


def _dot(a, b):
    return jnp.dot(a, b, preferred_element_type=f32)


def _sigmoid(x):
    return 0.5 * jnp.tanh(0.5 * x) + 0.5


def _silu(x):
    return x * _sigmoid(x)


def _rms(x, gain):
    return x * lax.rsqrt(jnp.mean(x * x, axis=-1, keepdims=True) + NORM_EPS) * gain


def _modulate(x, gain, shift, scale):
    return _rms(x, gain) * (1.0 + scale) + shift


def _seg_sum(x, seg):
    r = lax.broadcasted_iota(jnp.int32, (LANES, LANES), 0) // seg
    c = lax.broadcasted_iota(jnp.int32, (LANES, LANES), 1) // seg
    ones_bd = (r == c).astype(bf16)
    hi = x.astype(bf16)
    lo = (x - hi.astype(f32)).astype(bf16)
    outs = []
    for j in range(x.shape[1] // LANES):
        sl = slice(j * LANES, (j + 1) * LANES)
        outs.append(_dot(hi[:, sl], ones_bd) + _dot(lo[:, sl], ones_bd))
    return jnp.concatenate(outs, axis=1)


def _cumsum_rows(g, reverse):
    n = g.shape[0]
    ri = lax.broadcasted_iota(jnp.int32, (n, n), 0)
    ci = lax.broadcasted_iota(jnp.int32, (n, n), 1)
    tri = ((ci >= ri) if reverse else (ci <= ri)).astype(bf16)
    hi = g.astype(bf16)
    lo = (g - hi.astype(f32)).astype(bf16)
    return _dot(tri, hi) + _dot(tri, lo)


def _mod_kernel(c_ref, w_ref, b_ref, o_ref):
    o_ref[0] = _dot(_silu(c_ref[...]).astype(bf16), w_ref[0]) + b_ref[0]


def _mod_vectors(c, c_ctx, w_mod, b_mod):
    depth, d, _ = w_mod.shape
    batch = c.shape[0]
    rows = -(-(batch + 1) // 8) * 8
    cc = jnp.zeros((rows, d), f32).at[:batch].set(c).at[batch].set(c_ctx)
    out = pl.pallas_call(
        _mod_kernel,
        grid=(depth, N_MOD),
        in_specs=[
            pl.BlockSpec((rows, d), lambda i, j: (0, 0)),
            pl.BlockSpec((1, d, d), lambda i, j: (i, 0, j)),
            pl.BlockSpec((1, 1, d), lambda i, j: (i, 0, j)),
        ],
        out_specs=pl.BlockSpec((1, rows, d), lambda i, j: (i, 0, j)),
        out_shape=jax.ShapeDtypeStruct((depth, rows, N_MOD * d), f32),
        compiler_params=_params(2),
        name="mod_vectors",
    )(cc, w_mod.astype(bf16), b_mod.reshape(depth, 1, N_MOD * d))
    lat = out[:, :batch].reshape(depth, batch, 1, N_MOD, d)
    con = jnp.broadcast_to(out[:, batch].reshape(depth, 1, 1, N_MOD, d), lat.shape)
    mods = jnp.concatenate([con, lat], axis=2)
    return jnp.pad(mods, ((0, 0), (0, 0), (0, 0), (0, MOD_ROWS - N_MOD), (0, 0)))


class _Geom:
    def __init__(self, batch, n_ctx, n_lat):
        self.batch, self.n_ctx, self.n_lat = batch, n_ctx, n_lat
        self.t = n_ctx + n_lat
        self.tm = min(256, n_ctx)
        assert n_ctx % self.tm == 0 and n_lat % self.tm == 0
        assert n_ctx % ATTN_BLOCK == 0 and n_lat % ATTN_BLOCK == 0 and n_lat % GRID_W == 0
        self.nt = self.t // self.tm
        self.ctx_tiles = n_ctx // self.tm
        self.nc = self.t // SCAN_CHUNK
        self.ctx_chunks = n_ctx // SCAN_CHUNK

    def x_spec(self, t0=0, rows=1):
        return pl.BlockSpec((rows, self.tm, D_MODEL), lambda b, t: (b, t + t0, 0))

    def x_in(self, x, t0=0, rows=1):
        if not isinstance(x, tuple):
            return [x], [self.x_spec(t0, rows)]
        ct = self.ctx_tiles
        block = (rows, self.tm, D_MODEL)
        return list(x), [pl.BlockSpec(block, lambda b, t: (b, jnp.minimum(t + t0, ct - 1), 0)),
                         pl.BlockSpec(block, lambda b, t: (b, jnp.maximum(t + t0 - ct, 0), 0))]

    def mod_spec(self, t0=0, rows=1):
        ct = self.ctx_tiles
        return pl.BlockSpec((rows, 1, MOD_ROWS, D_MODEL), lambda b, t: (b, ((t + t0) >= ct).astype(jnp.int32), 0, 0))

    def bwd_chunk(self, i):
        cc = self.ctx_chunks
        return jnp.where(i < cc, cc - 1 - i, self.nc - 1 + cc - i)


LIN_ROWS_PER_STEP = 2


def _lin_kernel(x_ref, m_ref, g_ref, w_ref, o_ref):
    rows, tm = x_ref.shape[0], x_ref.shape[1]
    h = [_modulate(x_ref[r], g_ref[0:1], m_ref[r, 0, 0:1], m_ref[r, 0, 1:2]).astype(bf16) for r in range(rows)]
    z = _dot(jnp.concatenate(h, axis=0), w_ref[...])
    for r in range(rows):
        o_ref[r] = z[r * tm:(r + 1) * tm]


def _mod_linear(geo, x, mods, gains, w, name):
    n = w.shape[1]
    rows = LIN_ROWS_PER_STEP if geo.batch % LIN_ROWS_PER_STEP == 0 else 1
    return pl.pallas_call(
        _lin_kernel,
        grid=(geo.batch // rows, geo.nt),
        in_specs=[geo.x_spec(0, rows), geo.mod_spec(0, rows), _const_spec((8, D_MODEL)), _const_spec((D_MODEL, n))],
        out_specs=pl.BlockSpec((rows, geo.tm, n), lambda b, t: (b, t, 0)),
        out_shape=jax.ShapeDtypeStruct((geo.batch, geo.t, n), f32),
        compiler_params=_params(2),
        name=name,
    )(x, mods, gains, w)


def _read_tile(x_refs, row, is_ctx):
    if len(x_refs) == 1:
        return x_refs[0][row]
    return jnp.where(is_ctx, x_refs[0][row], x_refs[1][row])


def _head_rms(o, gain, width):
    outs = []
    for h in range(o.shape[1] // width):
        oh = o[:, h * width:(h + 1) * width]
        outs.append(oh * lax.rsqrt(jnp.mean(oh * oh, axis=-1, keepdims=True) + NORM_EPS) * gain)
    return jnp.concatenate(outs, axis=1)


def _pre_attn(refs, row):
    (o_ref,) = refs
    return o_ref[row]


def _pre_scan(width, refs, row):
    of_ref, ob_ref, gate_ref, gn_ref = refs
    o = _head_rms(of_ref[row] + ob_ref[row], gn_ref[...], width)
    return (o * _silu(gate_ref[row])).astype(bf16)


def _pre_rwkv(refs, row):
    yf_ref, yb_ref, g_ref, bg_ref, vec_ref = refs
    y = yf_ref[row] + yb_ref[row]
    inv_n = 1.0 / RWKV_HEAD_SIZE
    mu = _seg_sum(y, RWKV_HEAD_SIZE) * inv_n
    dlt = y - mu
    var = _seg_sum(dlt * dlt, RWKV_HEAD_SIZE) * inv_n
    yn = dlt * lax.rsqrt(var + RWKV_LN_EPS) * vec_ref[0:1] + vec_ref[1:2]
    return (yn * g_ref[row] + bg_ref[row]).astype(bf16)


def _post_kernel(pre, n_x, n_mix, first_ctx_tiles, *refs):
    x_refs, (m_ref, g_ref), mix_refs = refs[:n_x], refs[n_x:n_x + 2], refs[n_x + 2:n_x + 2 + n_mix]
    wo_ref, win_ref, wout_ref, out_ref = refs[n_x + 2 + n_mix:]
    rows = range(out_ref.shape[0])
    is_ctx = pl.program_id(1) < first_ctx_tiles
    o = [pre(mix_refs, r) for r in rows]
    y = [_dot(o[r], wo_ref[...]) for r in rows]
    x1 = [_read_tile(x_refs, r, is_ctx) + m_ref[r, 0, 2:3] * _rms(y[r], g_ref[1:2]) for r in rows]
    h2 = [_modulate(x1[r], g_ref[2:3], m_ref[r, 0, 3:4], m_ref[r, 0, 4:5]).astype(bf16) for r in rows]
    us = [[] for _ in rows]
    for c in range(MLP_HIDDEN // D_MODEL):
        cols = slice(c * D_MODEL, (c + 1) * D_MODEL)
        for r in rows:
            us[r].append(jnp.square(jnp.maximum(_dot(h2[r], win_ref[:, cols]), 0.0)).astype(bf16))
    acc = [_dot(jnp.concatenate(us[r], axis=1), wout_ref[...]) for r in rows]
    for r in rows:
        out_ref[r] = x1[r] + m_ref[r, 0, 5:6] * _rms(acc[r], g_ref[3:4])


POST_ROWS_PER_STEP = 2


def _post_rows(geo, n_tile_inputs):
    tile = geo.tm * D_MODEL * 4
    weights = 2 * (D_MODEL * D_MODEL + 2 * D_MODEL * MLP_HIDDEN)
    for rows in (POST_ROWS_PER_STEP, 1):
        streamed = 2 * (n_tile_inputs + 2) * rows * tile
        live = 6 * rows * tile
        if geo.batch % rows == 0 and weights + streamed + live <= VMEM_LIMIT:
            return rows
    return 1


def _post_mlp(geo, x, mods, gains, pre, mix_args, mix_specs, w_o, w_in, w_out, skip_ctx, name, rows):
    t0 = geo.ctx_tiles if skip_ctx else 0
    x_args, x_specs = geo.x_in(x, t0, rows)
    return pl.pallas_call(
        functools.partial(_post_kernel, pre, len(x_args), len(mix_args), geo.ctx_tiles - t0),
        grid=(geo.batch // rows, geo.nt - t0),
        in_specs=x_specs + [geo.mod_spec(t0, rows), _const_spec((8, D_MODEL))] + mix_specs + [
            _const_spec((D_MODEL, D_MODEL)), _const_spec((D_MODEL, MLP_HIDDEN)), _const_spec((MLP_HIDDEN, D_MODEL))],
        out_specs=pl.BlockSpec((rows, geo.tm, D_MODEL), lambda b, t: (b, t, 0)),
        out_shape=jax.ShapeDtypeStruct((geo.batch, geo.t - t0 * geo.tm, D_MODEL), f32),
        compiler_params=_params(2),
        name=name,
    )(*x_args, mods, gains, *mix_args, w_o, w_in, w_out)


def _tile_spec(geo, width, col, t0, rows):
    return pl.BlockSpec((rows, geo.tm, width), lambda b, t: (b, t + t0, col))


def _rope(x, cos, sin):
    w = x.shape[1]
    reps = w // LANES
    cw = jnp.concatenate([cos] * reps, axis=1)
    sw = jnp.concatenate([sin] * reps, axis=1)
    lane = lax.broadcasted_iota(jnp.int32, x.shape, 1)
    first = (lane % ROPE_AXIS_DIM) < ROPE_FREQS
    partner = jnp.where(first, pltpu.roll(x, w - ROPE_FREQS, axis=1), pltpu.roll(x, ROPE_FREQS, axis=1))
    return x * cw + partner * sw


def _attn_proj_kernel(ctx_tiles, *refs):
    x_refs, (m_ref, g_ref, cos_ref, sin_ref, wq_ref, wk_ref, wv_ref, q_ref, k_ref, v_ref) = refs[:-10], refs[-10:]
    rows = range(q_ref.shape[0])
    is_ctx = pl.program_id(1) < ctx_tiles
    cos, sin = cos_ref[...], sin_ref[...]
    h = [_modulate(_read_tile(x_refs, r, is_ctx), g_ref[0:1], m_ref[r, 0, 0:1], m_ref[r, 0, 1:2]).astype(bf16)
         for r in rows]
    q = [_dot(h[r], wq_ref[...]) * (HEAD_DIM ** -0.5) for r in rows]
    k = [_dot(h[r], wk_ref[...]) for r in rows]
    v = [_dot(h[r], wv_ref[...]) for r in rows]
    q = [_rope(q[r], cos, sin) for r in rows]
    k = [_rope(k[r], cos, sin) for r in rows]
    for r in rows:
        for hd in range(ATTN_HEADS):
            q_ref[r, hd] = q[r][:, hd * HEAD_DIM:(hd + 1) * HEAD_DIM].astype(bf16)
        for hd in range(ATTN_KV_HEADS):
            k_ref[r, hd] = k[r][:, hd * HEAD_DIM:(hd + 1) * HEAD_DIM].astype(bf16)
            v_ref[r, hd] = v[r][:, hd * HEAD_DIM:(hd + 1) * HEAD_DIM].astype(bf16)


ATTN_ROWS_PER_STEP = 4


def _attn_core_kernel(geo, sink_ref, q_ref, kp_ref, kc_ref, kn_ref, kx_ref, vp_ref, vc_ref, vn_ref, vx_ref, o_ref):
    qb = pl.program_id(1)
    blk = ATTN_BLOCK
    n_loc = 3 * blk
    n_keys = n_loc + geo.n_ctx
    rows = ATTN_GROUP * blk
    first_lat = geo.n_ctx // blk
    n_blocks = geo.t // blk
    row = lax.broadcasted_iota(jnp.int32, (blk, n_keys), 0)
    col = lax.broadcasted_iota(jnp.int32, (blk, n_keys), 1)
    kblk = qb - 1 + col // blk
    ok_local = ((jnp.abs(col - blk - row) <= WINDOW) & (qb >= first_lat) & (kblk >= first_lat) & (kblk < n_blocks))
    ok = (col >= n_loc) | ok_local
    groups = range(ATTN_GROUP)
    units = [(r, j) for r in range(q_ref.shape[0]) for j in range(ATTN_KV_HEADS)]

    def scores(r, j):
        q4 = q_ref[r, ATTN_GROUP * j:ATTN_GROUP * (j + 1)].reshape(rows, HEAD_DIM)
        keys = jnp.concatenate([kp_ref[r, j], kc_ref[r, j], kn_ref[r, j], kx_ref[r, j]], axis=0)
        return lax.dot_general(q4, keys, NT, preferred_element_type=f32)

    def softmax(j, s):
        sg = [jnp.where(ok, s[g * blk:(g + 1) * blk], NEG_INF) for g in groups]
        sk = [sink_ref[ATTN_GROUP * j + g] for g in groups]
        m = [jnp.maximum(jnp.max(sg[g], axis=-1, keepdims=True), sk[g]) for g in groups]
        e = [jnp.exp(sg[g] - m[g]) for g in groups]
        inv = [1.0 / (jnp.sum(e[g], axis=-1, keepdims=True) + jnp.exp(sk[g] - m[g])) for g in groups]
        return jnp.concatenate([e[g].astype(bf16) for g in groups], axis=0), inv

    def values(r, j, p, inv):
        vals = jnp.concatenate([vp_ref[r, j], vc_ref[r, j], vn_ref[r, j], vx_ref[r, j]], axis=0)
        o4 = _dot(p, vals)
        return [o4[g * blk:(g + 1) * blk] * inv[g] for g in groups]

    outs = [None] * len(units)
    s_next = scores(*units[0])
    soft_prev = None
    for i, (r, j) in enumerate(units):
        s_cur = s_next
        if i + 1 < len(units):
            s_next = scores(*units[i + 1])
        soft_cur = softmax(j, s_cur)
        if soft_prev is not None:
            outs[i - 1] = values(*units[i - 1], *soft_prev)
        soft_prev = soft_cur
    outs[-1] = values(*units[-1], *soft_prev)
    for r in range(q_ref.shape[0]):
        heads = outs[r * ATTN_KV_HEADS:(r + 1) * ATTN_KV_HEADS]
        o_ref[r] = jnp.concatenate([o for head_outs in heads for o in head_outs], axis=1).astype(bf16)


def _rope_tables(geo):
    inv_freq = ROPE_BASE ** (-jnp.arange(ROPE_FREQS, dtype=f32) * 2.0 / ROPE_AXIS_DIM)
    pos = jnp.arange(geo.n_lat)
    row = (pos // GRID_W).astype(f32)
    col = (pos % GRID_W).astype(f32)
    ang = jnp.stack([row[:, None] * inv_freq, col[:, None] * inv_freq], axis=1)
    cos = jnp.cos(ang)
    sin = jnp.sin(ang)
    cos_h = jnp.concatenate([cos, cos], axis=2).reshape(geo.n_lat, HEAD_DIM)
    sin_h = jnp.concatenate([-sin, sin], axis=2).reshape(geo.n_lat, HEAD_DIM)
    cos_t = jnp.concatenate([jnp.ones((geo.n_ctx, HEAD_DIM), f32), cos_h], axis=0)
    sin_t = jnp.concatenate([jnp.zeros((geo.n_ctx, HEAD_DIM), f32), sin_h], axis=0)
    return jnp.tile(cos_t, (1, 2)), jnp.tile(sin_t, (1, 2))


def _attn_layer(geo, x, mods, gains, w_qkv, w_o, sink, w_in, w_out, skip_ctx):
    b, t, tm = geo.batch, geo.t, geo.tm
    q_cols = ATTN_HEADS * HEAD_DIM
    kv_cols = ATTN_KV_HEADS * HEAD_DIM
    wb = w_qkv.astype(bf16)
    cos_t, sin_t = _rope_tables(geo)
    tab_spec = pl.BlockSpec((tm, LANES), lambda bb, tt: (tt, 0))
    rows = LIN_ROWS_PER_STEP if b % LIN_ROWS_PER_STEP == 0 else 1
    x_args, x_specs = geo.x_in(x, 0, rows)
    q, k, v = pl.pallas_call(
        functools.partial(_attn_proj_kernel, geo.ctx_tiles),
        grid=(b // rows, geo.nt),
        in_specs=x_specs + [geo.mod_spec(0, rows), _const_spec((8, D_MODEL)), tab_spec, tab_spec,
                  _const_spec((D_MODEL, q_cols)), _const_spec((D_MODEL, kv_cols)), _const_spec((D_MODEL, kv_cols))],
        out_specs=[pl.BlockSpec((rows, ATTN_HEADS, tm, HEAD_DIM), lambda bb, tt: (bb, 0, tt, 0)),
                   pl.BlockSpec((rows, ATTN_KV_HEADS, tm, HEAD_DIM), lambda bb, tt: (bb, 0, tt, 0)),
                   pl.BlockSpec((rows, ATTN_KV_HEADS, tm, HEAD_DIM), lambda bb, tt: (bb, 0, tt, 0))],
        out_shape=[jax.ShapeDtypeStruct((b, ATTN_HEADS, t, HEAD_DIM), bf16),
                   jax.ShapeDtypeStruct((b, ATTN_KV_HEADS, t, HEAD_DIM), bf16),
                   jax.ShapeDtypeStruct((b, ATTN_KV_HEADS, t, HEAD_DIM), bf16)],
        compiler_params=_params(2),
        name="attn_proj",
    )(*x_args, mods, gains, cos_t, sin_t, wb[:, :q_cols], wb[:, q_cols:q_cols + kv_cols], wb[:, q_cols + kv_cols:])

    blk = ATTN_BLOCK
    n_blocks = t // blk

    arows = ATTN_ROWS_PER_STEP if b % ATTN_ROWS_PER_STEP == 0 else 1

    def kv_spec(off):
        return pl.BlockSpec((arows, ATTN_KV_HEADS, blk, HEAD_DIM),
                            lambda bb, qb: (bb, 0, jnp.clip(qb + off, 0, n_blocks - 1), 0))

    ctx_spec = pl.BlockSpec((arows, ATTN_KV_HEADS, geo.n_ctx, HEAD_DIM), lambda bb, qb: (bb, 0, 0, 0))
    o = pl.pallas_call(
        functools.partial(_attn_core_kernel, geo),
        grid=(b // arows, n_blocks),
        in_specs=[pl.BlockSpec(memory_space=pltpu.SMEM),
                  pl.BlockSpec((arows, ATTN_HEADS, blk, HEAD_DIM), lambda bb, qb: (bb, 0, qb, 0)),
                  kv_spec(-1), kv_spec(0), kv_spec(1), ctx_spec,
                  kv_spec(-1), kv_spec(0), kv_spec(1), ctx_spec],
        out_specs=pl.BlockSpec((arows, blk, q_cols), lambda bb, qb: (bb, qb, 0)),
        out_shape=jax.ShapeDtypeStruct((b, t, q_cols), bf16),
        compiler_params=_params(2),
        name="attn_core",
    )(sink, q, k, k, k, k, v, v, v, v)

    t0 = geo.ctx_tiles if skip_ctx else 0
    rows = _post_rows(geo, 1)
    return _post_mlp(geo, x, mods, gains, _pre_attn, [o], [_tile_spec(geo, D_MODEL, 0, t0, rows)],
                     w_o.astype(bf16), w_in, w_out, skip_ctx, "attn_post", rows)


def _gated_prep(entries, heads, dk, dv):
    sums = [_cumsum_rows(g, reverse) for _, _, _, g, _, _, reverse in entries]
    units = []
    for (q, k, v, g, st_ref, base, reverse), b in zip(entries, sums):
        n = q.shape[0]
        btot = b[0:1] if reverse else b[n - 1:n]
        q_dec = (q * jnp.exp(b)).astype(bf16)
        k_inv32 = k * jnp.exp(-b)
        k_inv = k_inv32.astype(bf16)
        dec = jnp.exp(btot)
        k_end = (k_inv32 * dec).astype(bf16)
        vb = v.astype(bf16)
        ri = lax.broadcasted_iota(jnp.int32, (n, n), 0)
        ci = lax.broadcasted_iota(jnp.int32, (n, n), 1)
        tri = (ci >= ri) if reverse else (ci <= ri)
        for h in range(heads):
            ks = slice(h * dk, (h + 1) * dk)
            vs = slice(h * dv, (h + 1) * dv)
            units.append(dict(q=q_dec[:, ks], ki=k_inv[:, ks], ke=k_end[:, ks], v=vb[:, vs], dec=dec[:, ks],
                              tri=tri, st=st_ref[base + h], ref=st_ref, h=base + h))
    return units


def _gated_matmuls(units, heads):
    for u in units:
        a = lax.dot_general(u["q"], u["ki"], NT, preferred_element_type=f32)
        u["a"] = jnp.where(u["tri"], a, 0.0).astype(bf16)
        u["qs"] = lax.dot_general(u["q"], u["st"].astype(bf16), NT, preferred_element_type=f32)
    for u in units:
        u["o"] = _dot(u["a"], u["v"]) + u["qs"]
        u["new"] = u["st"] * u["dec"] + lax.dot_general(u["v"], u["ke"], TN, preferred_element_type=f32)
    for u in units:
        u["ref"][u["h"]] = u["new"]
    return [jnp.concatenate([u["o"] for u in units[i:i + heads]], axis=1) for i in range(0, len(units), heads)]


def _gated_chunk(groups, heads, dk, dv, staggered):
    def prep(group):
        return _gated_prep([thunk() for thunk in group], heads, dk, dv)

    if not staggered:
        units = prep([thunk for group in groups for thunk in group])
        per_row = len(groups[0])
        outs = _gated_matmuls(units, heads)
        return [outs[i:i + per_row] for i in range(0, len(outs), per_row)]
    outs = []
    ready = prep(groups[0])
    for i in range(len(groups)):
        nxt = prep(groups[i + 1]) if i + 1 < len(groups) else None
        outs.append(_gated_matmuls(ready, heads))
        ready = nxt
    return outs


def _gla_gate(zd, wu_ref, bias_ref, d):
    zg = _dot(zd.astype(bf16), wu_ref[d]) + bias_ref[d:d + 1]
    return (jnp.minimum(zg, 0.0) - jnp.log(1.0 + jnp.exp(-jnp.abs(zg)))) * (1.0 / GLA_TAU)


def _gla_scan_kernel(qkf_ref, vf_ref, zdf_ref, qkb_ref, vb_ref, zdb_ref, wu_ref, bias_ref,
                     of_ref, ob_ref, sf_ref, sb_ref):
    @pl.when(pl.program_id(1) == 0)
    def _():
        sf_ref[...] = jnp.zeros_like(sf_ref)
        sb_ref[...] = jnp.zeros_like(sb_ref)

    def entry(row, d, qk_ref, v_ref, zd_ref, s_ref):
        qk = qk_ref[row]
        return (qk[:, :GLA_KEY_DIM] * (GLA_DK ** -0.5), qk[:, GLA_KEY_DIM:], v_ref[row],
                _gla_gate(zd_ref[row], wu_ref, bias_ref, d), s_ref, row * GLA_HEADS, d == 1)

    groups = [[functools.partial(entry, row, 0, qkf_ref, vf_ref, zdf_ref, sf_ref),
               functools.partial(entry, row, 1, qkb_ref, vb_ref, zdb_ref, sb_ref)]
              for row in range(qkf_ref.shape[0])]
    for row, (o_f, o_b) in enumerate(_gated_chunk(groups, GLA_HEADS, GLA_DK, GLA_DV, staggered=False)):
        of_ref[row] = o_f
        ob_ref[row] = o_b


def _hgrn_scan_kernel(qf_ref, if_ref, zff_ref, qb_ref, ib_ref, zfb_ref, lb_ref, of_ref, ob_ref, sf_ref, sb_ref):
    @pl.when(pl.program_id(1) == 0)
    def _():
        sf_ref[...] = jnp.zeros_like(sf_ref)
        sb_ref[...] = jnp.zeros_like(sb_ref)

    lb = lb_ref[...]
    def entry(row, d, q_ref, i_ref, zf_ref, s_ref):
        f = lb + (1.0 - lb) * _sigmoid(zf_ref[row])
        return (_silu(q_ref[row]), 1.0 - f, i_ref[row], jnp.log(f), s_ref, row * HGRN_HEADS, d == 1)

    groups = [[functools.partial(entry, row, 0, qf_ref, if_ref, zff_ref, sf_ref),
               functools.partial(entry, row, 1, qb_ref, ib_ref, zfb_ref, sb_ref)]
              for row in range(qf_ref.shape[0])]
    for row, (o_f, o_b) in enumerate(_gated_chunk(groups, HGRN_HEADS, HGRN_EXPAND, HGRN_EXPAND, staggered=True)):
        of_ref[row] = o_f
        ob_ref[row] = o_b


SCAN_ROWS_PER_STEP = 8


def _scan_rows(geo):
    return SCAN_ROWS_PER_STEP if geo.batch % SCAN_ROWS_PER_STEP == 0 else 1


def _chunk_spec(geo, width, col, reverse):
    rows = _scan_rows(geo)
    if reverse:
        return pl.BlockSpec((rows, SCAN_CHUNK, width), lambda b, i: (b, geo.bwd_chunk(i), col))
    return pl.BlockSpec((rows, SCAN_CHUNK, width), lambda b, i: (b, i, col))


def _scan_call(geo, body, args, specs, heads, dk, dv, name):
    out_sds = jax.ShapeDtypeStruct((geo.batch, geo.t, heads * dv), f32)
    rows = _scan_rows(geo)
    state = pltpu.VMEM((rows * heads, dv, dk), f32)
    return pl.pallas_call(
        body,
        grid=(geo.batch // rows, geo.nc),
        in_specs=specs,
        out_specs=[_chunk_spec(geo, heads * dv, 0, False), _chunk_spec(geo, heads * dv, 0, True)],
        out_shape=[out_sds, out_sds],
        scratch_shapes=[state, state],
        compiler_params=_params(2),
        name=name,
    )(*args)


def _gla_layer(geo, x, mods, gains, w_in_p, w_gd, w_gu, g_bias, g_norm, w_o, w_in, w_out, skip_ctx):
    r = GLA_GATE_RANK
    n_z = 2 * GLA_KEY_DIM + 2 * D_MODEL
    w_all = jnp.concatenate([w_in_p, w_gd[0], w_gd[1], jnp.zeros((D_MODEL, LANES - 2 * r), f32)], axis=1).astype(bf16)
    z = _mod_linear(geo, x, mods, gains, w_all, "gla_proj")
    wu = jnp.zeros((2, LANES, GLA_KEY_DIM), f32).at[0, :r].set(w_gu[0]).at[1, r:2 * r].set(w_gu[1]).astype(bf16)
    zd_col = n_z // LANES
    specs = []
    for rev in (False, True):
        specs += [_chunk_spec(geo, 2 * GLA_KEY_DIM, 0, rev), _chunk_spec(geo, D_MODEL, 1, rev),
                  _chunk_spec(geo, LANES, zd_col, rev)]
    specs += [_const_spec((2, LANES, GLA_KEY_DIM)), _const_spec((2, GLA_KEY_DIM))]
    o_f, o_b = _scan_call(geo, _gla_scan_kernel, [z, z, z, z, z, z, wu, g_bias], specs,
                          GLA_HEADS, GLA_DK, GLA_DV, "gla_scan")
    t0 = geo.ctx_tiles if skip_ctx else 0
    rows = _post_rows(geo, 3)
    mix_specs = [_tile_spec(geo, D_MODEL, 0, t0, rows), _tile_spec(geo, D_MODEL, 0, t0, rows),
                 _tile_spec(geo, D_MODEL, 2, t0, rows), _const_spec((1, GLA_DV))]
    return _post_mlp(geo, x, mods, gains, functools.partial(_pre_scan, GLA_DV),
                     [o_f, o_b, z, g_norm.reshape(1, GLA_DV)], mix_specs,
                     w_o.astype(bf16), w_in, w_out, skip_ctx, "gla_post", rows)


def _hgrn_layer(geo, x, mods, gains, w_in_p, w_f, lower_bound, g_norm, w_o, w_in, w_out, skip_ctx):
    w_all = jnp.concatenate([w_in_p, w_f[0], w_f[1]], axis=1).astype(bf16)
    z = _mod_linear(geo, x, mods, gains, w_all, "hgrn_proj")
    specs = []
    for rev in (False, True):
        specs += [_chunk_spec(geo, D_MODEL, 0, rev), _chunk_spec(geo, D_MODEL, 1, rev),
                  _chunk_spec(geo, D_MODEL, 4 if rev else 3, rev)]
    specs += [_const_spec((1, D_MODEL))]
    o_f, o_b = _scan_call(geo, _hgrn_scan_kernel, [z, z, z, z, z, z, lower_bound.reshape(1, D_MODEL)], specs,
                          HGRN_HEADS, HGRN_EXPAND, HGRN_EXPAND, "hgrn_scan")
    t0 = geo.ctx_tiles if skip_ctx else 0
    rows = _post_rows(geo, 3)
    mix_specs = [_tile_spec(geo, D_MODEL, 0, t0, rows), _tile_spec(geo, D_MODEL, 0, t0, rows),
                 _tile_spec(geo, D_MODEL, 2, t0, rows), _const_spec((1, HGRN_EXPAND))]
    return _post_mlp(geo, x, mods, gains, functools.partial(_pre_scan, HGRN_EXPAND),
                     [o_f, o_b, z, g_norm.reshape(1, HGRN_EXPAND)], mix_specs,
                     w_o.astype(bf16), w_in, w_out, skip_ctx, "hgrn_post", rows)


def _rwkv_proj_kernel(geo, x_ref, xp_ref, xn_ref, m_ref, g_ref, mix_ref, vec_ref, wrkv_ref, wdn_ref, wup_ref,
                      aup_ref, gup_ref, r_ref, v_ref, gate_ref, kk_ref, lw0_ref, lw1_ref, a0_ref, a1_ref,
                      k0_ref, k1_ref, bg_ref):
    t = pl.program_id(1)
    tm = geo.tm
    gain, shift, scale = g_ref[0:1], m_ref[0, 0, 0:1], m_ref[0, 0, 1:2]
    h = _modulate(x_ref[0], gain, shift, scale)
    seg_first = (t == 0) | (t == geo.ctx_tiles)
    seg_last = (t == geo.ctx_tiles - 1) | (t == geo.nt - 1)
    h_prev = jnp.where(seg_first, 0.0, _modulate(xp_ref[0], gain, shift, scale)[7:8])
    h_next = jnp.where(seg_last, 0.0, _modulate(xn_ref[0], gain, shift, scale)[0:1])
    row = lax.broadcasted_iota(jnp.int32, h.shape, 0)
    up = jnp.where(row == 0, h_prev, pltpu.roll(h, 1, axis=0))
    dn = jnp.where(row == tm - 1, h_next, pltpu.roll(h, tm - 1, axis=0))
    dx = 0.5 * (up + dn) - h

    def mixed(n):
        return (h + dx * mix_ref[n:n + 1]).astype(bf16)

    dw = jnp.tanh(_dot(mixed(3), wdn_ref[:, 0:LANES])).astype(bf16)
    da = _dot(mixed(4), wdn_ref[:, LANES:2 * LANES]).astype(bf16)
    dg = _sigmoid(_dot(mixed(5), wdn_ref[:, 2 * LANES:3 * LANES])).astype(bf16)
    gate = _dot(dg, gup_ref[...])
    gate_ref[0] = gate
    a_dirs = []
    for d, (lw_ref, a_ref) in enumerate(((lw0_ref, a0_ref), (lw1_ref, a1_ref))):
        lw_ref[0] = -RWKV_DECAY_SCALE * _sigmoid(vec_ref[d:d + 1] + _dot(dw, wup_ref[d]))
        a_dirs.append(_sigmoid(vec_ref[2 + d:3 + d] + _dot(da, aup_ref[d])))
        a_ref[0] = a_dirs[d]
    k = _dot(mixed(1), wrkv_ref[1])
    r = _dot(mixed(0), wrkv_ref[0])
    v = _dot(mixed(2), wrkv_ref[2])
    kk = k * vec_ref[4:5]
    kk_ref[0] = kk * lax.rsqrt(jnp.maximum(_seg_sum(kk * kk, RWKV_HEAD_SIZE), L2_EPS * L2_EPS))
    kds = []
    for d, kd_ref in enumerate((k0_ref, k1_ref)):
        kds.append(k * (1.0 + (a_dirs[d] - 1.0) * vec_ref[5:6]))
        kd_ref[0] = kds[d]
    r_ref[0] = r
    v_ref[0] = v
    bg_ref[0] = _seg_sum(r * (0.5 * (kds[0] + kds[1])) * vec_ref[6:7], RWKV_HEAD_SIZE) * v * gate


def _rwkv_chunk_units(units):
    n = SCAN_CHUNK
    hs = RWKV_HEAD_SIZE
    lane = lax.broadcasted_iota(jnp.int32, (n, RWKV_PAIR), 1)
    t_i = lax.broadcasted_iota(jnp.int32, (n, RWKV_PAIR), 0)
    s_i = lane % hs
    head0 = lane < hs
    ri = lax.broadcasted_iota(jnp.int32, (RWKV_PAIR, RWKV_PAIR), 0)
    ci = lax.broadcasted_iota(jnp.int32, (RWKV_PAIR, RWKV_PAIR), 1)
    same_head = ri // hs == ci // hs
    top_rows = ri < hs

    def block_diag(x):
        return jnp.concatenate([jnp.where(head0, x, 0.0), jnp.where(head0, 0.0, x)], axis=0)

    def inverse_4x4_blocks(nm, reverse):
        toward = RWKV_PAIR - 1 if not reverse else 1
        dist = (s_i - t_i) if reverse else (t_i - s_i)
        near = pltpu.roll(nm, toward, axis=1)
        far = pltpu.roll(nm, (2 * toward) % RWKV_PAIR, axis=1)
        diag1 = jnp.sum(jnp.where(same4 & (dist == 1), nm, 0.0), axis=0, keepdims=True)
        diag2 = jnp.sum(jnp.where(same4 & (dist == 2), nm, 0.0), axis=0, keepdims=True)
        p2 = diag2 + pltpu.roll(diag1, toward, axis=1) * diag1
        two = nm + near * diag1
        three = two + far * p2
        return jnp.where(s_i == t_i, 1.0,
                         jnp.where(same4 & (dist == 1), nm,
                                   jnp.where(same4 & (dist == 2), two, jnp.where(same4 & (dist == 3), three, 0.0))))

    def cross_blocks(b):
        return (t_i // (2 * b) == s_i // (2 * b)) & (t_i // b != s_i // b)

    st = []
    for r, kd, v, kk, a, lw, ht, reverse in units:
        c = _cumsum_rows(lw, reverse)
        tot = c[0:1] if reverse else c[n - 1:n]
        e_neg = jnp.exp(-c)
        dec = jnp.exp(tot)
        kb = jnp.concatenate([kd * e_neg, kk * a * e_neg], axis=0)
        kb_t = kb.T
        kb_sw = pltpu.roll(kb_t, hs, axis=1)
        rhs = jnp.concatenate([jnp.where(same_head, jnp.where(top_rows, kb_t, kb_sw), 0.0),
                               jnp.where(same_head, jnp.where(top_rows, kb_sw, kb_t), 0.0)], axis=1)
        st.append(dict(
            strict=(s_i > t_i) if reverse else (s_i < t_i), incl=(s_i >= t_i) if reverse else (s_i <= t_i),
            a_bar=(-kk * jnp.exp(c - lw)).astype(bf16), r_bar=(r * jnp.exp(c)).astype(bf16), rhs=rhs.astype(bf16),
            v_bd=block_diag(v).astype(bf16), h_t=ht.T.astype(bf16),
            ends=(kb * dec).astype(bf16),
            v=v, ht=ht, dec=dec, reverse=reverse))
    for s in st:
        g = _dot(jnp.concatenate([s["a_bar"], s["r_bar"]], axis=0), s["rhs"])
        s["a_ak"] = jnp.where(s["strict"], g[0:n, 0:RWKV_PAIR], 0.0).astype(bf16)
        s["nmat"] = jnp.where(s["strict"], g[0:n, RWKV_PAIR:], 0.0)
        s["q_k"] = jnp.where(s["incl"], g[n:, 0:RWKV_PAIR], 0.0).astype(bf16)
        s["q_b"] = jnp.where(s["incl"], g[n:, RWKV_PAIR:], 0.0).astype(bf16)
    same4 = t_i // 4 == s_i // 4
    for s in st:
        s["inv"] = inverse_4x4_blocks(s["nmat"], s["reverse"])
    b = 4
    while 2 * b < n:
        for s in st:
            s["pc"] = _dot(s["inv"].astype(bf16), block_diag(jnp.where(cross_blocks(b), s["nmat"], 0.0)).astype(bf16))
        for s in st:
            s["inv"] = s["inv"] + _dot(s["pc"].astype(bf16), block_diag(s["inv"]).astype(bf16))
        b *= 2
    for s in st:
        x = _dot(jnp.concatenate([s["a_bar"], s["a_ak"]], axis=1), jnp.concatenate([s["h_t"], s["v_bd"]], axis=0))
        s["inv"] = s["inv"].astype(bf16)
        s["w"] = _dot(s["inv"], block_diag(x).astype(bf16))
    for s in st:
        s["cw"] = _dot(jnp.where(cross_blocks(n // 2), s["nmat"], 0.0).astype(bf16), block_diag(s["w"]).astype(bf16))
    for s in st:
        s["u"] = s["w"] + _dot(s["inv"], block_diag(s["cw"]).astype(bf16))
    for s in st:
        s["vu_t"] = jnp.concatenate([s["v"], s["u"]], axis=0).T.astype(bf16)
    for s in st:
        s["y"] = _dot(jnp.concatenate([s["r_bar"], s["q_k"], s["q_b"]], axis=1),
                      jnp.concatenate([s["h_t"], s["v_bd"], block_diag(s["u"]).astype(bf16)], axis=0))
    for s in st:
        s["upd"] = _dot(s["vu_t"], s["ends"])
    return [(s["y"], s["ht"] * s["dec"] + jnp.where(same_head, s["upd"], 0.0)) for s in st]


def _rwkv_scan_kernel(rows, pairs, *refs):
    fwd, bwd = refs[0:6], refs[6:12]
    yf_ref, yb_ref, hf_ref, hb_ref = refs[12:]

    @pl.when(pl.program_id(2) == 0)
    def _():
        hf_ref[...] = jnp.zeros_like(hf_ref)
        hb_ref[...] = jnp.zeros_like(hb_ref)

    units, dests = [], []
    for ins, y_ref, h_ref, reverse in ((fwd, yf_ref, hf_ref, False), (bwd, yb_ref, hb_ref, True)):
        r_ref, v_ref, kk_ref, lw_ref, a_ref, kd_ref = ins
        for row in range(rows):
            for p in range(pairs):
                sl = slice(p * RWKV_PAIR, (p + 1) * RWKV_PAIR)
                slot = row * pairs + p
                units.append((r_ref[row, :, sl], kd_ref[row, :, sl], v_ref[row, :, sl], kk_ref[row, :, sl],
                              a_ref[row, :, sl], lw_ref[row, :, sl], h_ref[slot], reverse))
                dests.append((y_ref, h_ref, row, slot, sl))
    for (y, h_new), (y_ref, h_ref, row, slot, sl) in zip(_rwkv_chunk_units(units), dests):
        y_ref[row, :, sl] = y
        h_ref[slot] = h_new


RWKV_PAIRS_PER_STEP = 8
RWKV_ROWS_PER_STEP = 4


def _rwkv_layer(geo, x, mods, gains, mix, w_rkv, w0, w_down, w_up, a0, a_down, a_up, g_down, g_up, k_k, k_a,
                r_k, ln_w, ln_b, w_o, w_in, w_out, skip_ctx):
    b, t, tm = geo.batch, geo.t, geo.tm
    d = D_MODEL
    rank = w_down.shape[-1]
    assert 2 * rank == LANES and a_down.shape[-1] == rank and g_down.shape[-1] == LANES
    mix8 = jnp.pad(mix, ((0, 2), (0, 0)))
    vec = jnp.stack([w0[0], w0[1], a0[0], a0[1], k_k, k_a, r_k.reshape(d), jnp.zeros_like(k_k)])
    w_dn = jnp.concatenate([w_down[0], w_down[1], a_down[0], a_down[1], g_down], axis=1).astype(bf16)

    def padded_up(w):
        return jnp.zeros((2, LANES, d), f32).at[0, :rank].set(w[0]).at[1, rank:].set(w[1]).astype(bf16)

    halo = 8
    n_halo = t // halo
    per = tm // halo
    x_prev = pl.BlockSpec((1, halo, d), lambda bb, tt: (bb, jnp.maximum(tt * per - 1, 0), 0))
    x_next = pl.BlockSpec((1, halo, d), lambda bb, tt: (bb, jnp.minimum((tt + 1) * per, n_halo - 1), 0))
    out_spec = pl.BlockSpec((1, tm, d), lambda bb, tt: (bb, tt, 0))
    sds = jax.ShapeDtypeStruct((b, t, d), f32)
    r, v, gate, kk, lw0, lw1, a_0, a_1, k0, k1, bonus_g = pl.pallas_call(
        functools.partial(_rwkv_proj_kernel, geo),
        grid=(b, geo.nt),
        in_specs=[geo.x_spec(), x_prev, x_next, geo.mod_spec(), _const_spec((8, d)), _const_spec((8, d)),
                  _const_spec((8, d)), _const_spec((3, d, d)), _const_spec((d, 3 * LANES)),
                  _const_spec((2, LANES, d)), _const_spec((2, LANES, d)), _const_spec((LANES, d))],
        out_specs=[out_spec] * 11,
        out_shape=[sds] * 11,
        compiler_params=_params(2),
        name="rwkv_proj",
    )(x, x, x, mods, gains, mix8, vec, w_rkv.astype(bf16), w_dn, padded_up(w_up), padded_up(a_up),
      g_up.astype(bf16))

    pairs = RWKV_PAIRS_PER_STEP
    rows = RWKV_ROWS_PER_STEP if b % RWKV_ROWS_PER_STEP == 0 else 1
    width = pairs * RWKV_PAIR
    groups = d // width

    def cspec(reverse):
        if reverse:
            return pl.BlockSpec((rows, SCAN_CHUNK, width), lambda bb, gg, i: (bb, geo.bwd_chunk(i), gg))
        return pl.BlockSpec((rows, SCAN_CHUNK, width), lambda bb, gg, i: (bb, i, gg))

    state = pltpu.VMEM((rows * pairs, RWKV_PAIR, RWKV_PAIR), f32)
    y_f, y_b = pl.pallas_call(
        functools.partial(_rwkv_scan_kernel, rows, pairs),
        grid=(b // rows, groups, geo.nc),
        in_specs=[cspec(False)] * 6 + [cspec(True)] * 6,
        out_specs=[cspec(False), cspec(True)],
        out_shape=[sds, sds],
        scratch_shapes=[state, state],
        compiler_params=_params(3),
        name="rwkv_scan",
    )(r, v, kk, lw0, a_0, k0, r, v, kk, lw1, a_1, k1)

    t0 = geo.ctx_tiles if skip_ctx else 0
    post_vec = jnp.stack([ln_w, ln_b] + [jnp.zeros_like(ln_w)] * 6)
    rows = _post_rows(geo, 4)
    mix_specs = [_tile_spec(geo, d, 0, t0, rows)] * 4 + [_const_spec((8, d))]
    return _post_mlp(geo, x, mods, gains, _pre_rwkv, [y_f, y_b, gate, bonus_g, post_vec], mix_specs,
                     w_o.astype(bf16), w_in, w_out, skip_ctx, "rwkv_post", rows)


def _hgrn_lower_bound(lb_param, layer):
    p = jax.nn.softmax(lb_param.astype(f32), axis=0)
    return (jnp.cumsum(p, axis=0) - p[0])[layer]


def kernel(x, c, ctx, c_ctx, w_mod, b_mod, g_pre_mix, g_post_mix, g_pre_mlp, g_post_mlp, w_mlp_in, w_mlp_out, attn_w_qkv, attn_w_o, attn_sink, gla_w_in, gla_w_gate_down, gla_w_gate_up, gla_gate_bias, gla_g_norm, gla_w_o, rwkv_mix, rwkv_w_rkv, rwkv_w0, rwkv_w_down, rwkv_w_up, rwkv_a0, rwkv_a_down, rwkv_a_up, rwkv_g_down, rwkv_g_up, rwkv_k_k, rwkv_k_a, rwkv_r_k, rwkv_ln_w, rwkv_ln_b, rwkv_w_o, hgrn_w_in, hgrn_w_f, hgrn_lb, hgrn_g_norm, hgrn_w_o):
    depth = w_mod.shape[0]
    geo = _Geom(x.shape[0], ctx.shape[1], x.shape[1])
    mods_all = _mod_vectors(c, c_ctx, w_mod, b_mod)
    xs = (ctx, x)
    for i in range(depth):
        kind, j = i % 4, i // 4
        skip_ctx = i == depth - 1
        mods = mods_all[i]
        gains = jnp.stack([g_pre_mix[i], g_post_mix[i], g_pre_mlp[i], g_post_mlp[i]] + [jnp.zeros_like(g_pre_mix[i])] * 4)
        w_in, w_out = w_mlp_in[i].astype(bf16), w_mlp_out[i].astype(bf16)
        if kind == 0:
            xs = _attn_layer(geo, xs, mods, gains, attn_w_qkv[j], attn_w_o[j], attn_sink[j], w_in, w_out, skip_ctx)
        elif kind == 1:
            xs = _gla_layer(geo, xs, mods, gains, gla_w_in[j], gla_w_gate_down[j], gla_w_gate_up[j],
                            gla_gate_bias[j], gla_g_norm[j], gla_w_o[j], w_in, w_out, skip_ctx)
        elif kind == 2:
            xs = _rwkv_layer(geo, xs, mods, gains, rwkv_mix[j], rwkv_w_rkv[j], rwkv_w0[j], rwkv_w_down[j],
                             rwkv_w_up[j], rwkv_a0[j], rwkv_a_down[j], rwkv_a_up[j], rwkv_g_down[j], rwkv_g_up[j],
                             rwkv_k_k[j], rwkv_k_a[j], rwkv_r_k[j], rwkv_ln_w[j], rwkv_ln_b[j], rwkv_w_o[j],
                             w_in, w_out, skip_ctx)
        else:
            xs = _hgrn_layer(geo, xs, mods, gains, hgrn_w_in[j], hgrn_w_f[j], _hgrn_lower_bound(hgrn_lb, i),
                             hgrn_g_norm[j], hgrn_w_o[j], w_in, w_out, skip_ctx)
        if skip_ctx:
            return xs
    return xs[:, geo.n_ctx:]
```

```python
import functools

import jax
import jax.numpy as jnp
from jax import lax
from jax.experimental import pallas as pl
from jax.experimental.pallas import tpu as pltpu

f32 = jnp.float32
bf16 = jnp.bfloat16

D_MODEL = 1024
N_MOD = 6
MLP_HIDDEN = 4 * D_MODEL
NORM_EPS = 1e-6
NEG_INF = -1e30
GRID_W = 64

ATTN_HEADS = 16
ATTN_KV_HEADS = 4
ATTN_GROUP = ATTN_HEADS // ATTN_KV_HEADS
HEAD_DIM = 64
WINDOW = 128
ATTN_BLOCK = 128
ROPE_BASE = 10000.0
ROPE_AXIS_DIM = HEAD_DIM // 2
ROPE_FREQS = ROPE_AXIS_DIM // 2

GLA_HEADS = 4
GLA_KEY_DIM = D_MODEL // 2
GLA_DK = GLA_KEY_DIM // GLA_HEADS
GLA_DV = D_MODEL // GLA_HEADS
GLA_GATE_RANK = 16
GLA_TAU = 16.0
SCAN_CHUNK = 64

RWKV_HEAD_SIZE = 64
RWKV_LN_EPS = 64e-5
L2_EPS = 1e-12
RWKV_PAIR = 2 * RWKV_HEAD_SIZE
RWKV_DECAY_SCALE = 0.6065306597126334

HGRN_EXPAND = 128
HGRN_HEADS = D_MODEL // HGRN_EXPAND

LANES = 128
MOD_ROWS = 8
VMEM_LIMIT = 56 * 1024 * 1024

NT = (((1,), (1,)), ((), ()))
TN = (((0,), (0,)), ((), ()))


def _params(n_grid):
    return pltpu.CompilerParams(dimension_semantics=("arbitrary",) * n_grid, vmem_limit_bytes=VMEM_LIMIT)


def _const_spec(shape):
    nd = len(shape)
    return pl.BlockSpec(shape, lambda *_: (0,) * nd, pipeline_mode=pl.Buffered(1))


def _dot(a, b):
    return jnp.dot(a, b, preferred_element_type=f32)


def _sigmoid(x):
    return 0.5 * jnp.tanh(0.5 * x) + 0.5


def _silu(x):
    return x * _sigmoid(x)


def _rms(x, gain):
    return x * lax.rsqrt(jnp.mean(x * x, axis=-1, keepdims=True) + NORM_EPS) * gain


def _modulate(x, gain, shift, scale):
    return _rms(x, gain) * (1.0 + scale) + shift


def _seg_sum(x, seg, exact_terms=True):
    r = lax.broadcasted_iota(jnp.int32, (LANES, LANES), 0) // seg
    c = lax.broadcasted_iota(jnp.int32, (LANES, LANES), 1) // seg
    ones_bd = (r == c).astype(bf16)
    hi = x.astype(bf16)
    lo = (x - hi.astype(f32)).astype(bf16) if exact_terms else None
    outs = []
    for j in range(x.shape[1] // LANES):
        sl = slice(j * LANES, (j + 1) * LANES)
        part = _dot(hi[:, sl], ones_bd)
        outs.append(part + _dot(lo[:, sl], ones_bd) if exact_terms else part)
    return jnp.concatenate(outs, axis=1)


def _cumsum_rows(g, reverse):
    n = g.shape[0]
    ri = lax.broadcasted_iota(jnp.int32, (n, n), 0)
    ci = lax.broadcasted_iota(jnp.int32, (n, n), 1)
    tri = ((ci >= ri) if reverse else (ci <= ri)).astype(bf16)
    hi = g.astype(bf16)
    lo = (g - hi.astype(f32)).astype(bf16)
    return _dot(tri, hi) + _dot(tri, lo)


def _mod_kernel(c_ref, w_ref, b_ref, o_ref):
    o_ref[0] = _dot(_silu(c_ref[...]).astype(bf16), w_ref[0]) + b_ref[0]


def _mod_vectors(c, c_ctx, w_mod, b_mod):
    depth, d, _ = w_mod.shape
    batch = c.shape[0]
    rows = -(-(batch + 1) // 8) * 8
    cc = jnp.zeros((rows, d), f32).at[:batch].set(c).at[batch].set(c_ctx)
    out = pl.pallas_call(
        _mod_kernel,
        grid=(depth, N_MOD),
        in_specs=[
            pl.BlockSpec((rows, d), lambda i, j: (0, 0)),
            pl.BlockSpec((1, d, d), lambda i, j: (i, 0, j)),
            pl.BlockSpec((1, 1, d), lambda i, j: (i, 0, j)),
        ],
        out_specs=pl.BlockSpec((1, rows, d), lambda i, j: (i, 0, j)),
        out_shape=jax.ShapeDtypeStruct((depth, rows, N_MOD * d), f32),
        compiler_params=_params(2),
        name="mod_vectors",
    )(cc, w_mod.astype(bf16), b_mod.reshape(depth, 1, N_MOD * d))
    lat = out[:, :batch].reshape(depth, batch, 1, N_MOD, d)
    con = jnp.broadcast_to(out[:, batch].reshape(depth, 1, 1, N_MOD, d), lat.shape)
    mods = jnp.concatenate([con, lat], axis=2)
    return jnp.pad(mods, ((0, 0), (0, 0), (0, 0), (0, MOD_ROWS - N_MOD), (0, 0)))


class _Geom:
    def __init__(self, batch, n_ctx, n_lat):
        self.batch, self.n_ctx, self.n_lat = batch, n_ctx, n_lat
        self.t = n_ctx + n_lat
        self.tm = min(256, n_ctx)
        assert n_ctx % self.tm == 0 and n_lat % self.tm == 0
        assert n_ctx % ATTN_BLOCK == 0 and n_lat % ATTN_BLOCK == 0 and n_lat % GRID_W == 0
        self.nt = self.t // self.tm
        self.ctx_tiles = n_ctx // self.tm
        self.nc = self.t // SCAN_CHUNK
        self.ctx_chunks = n_ctx // SCAN_CHUNK

    def x_spec(self, t0=0, rows=1):
        return pl.BlockSpec((rows, self.tm, D_MODEL), lambda b, t: (b, t + t0, 0))

    def x_in(self, x, t0=0, rows=1):
        if not isinstance(x, tuple):
            return [x], [self.x_spec(t0, rows)]
        ct = self.ctx_tiles
        block = (rows, self.tm, D_MODEL)
        return list(x), [pl.BlockSpec(block, lambda b, t: (b, jnp.minimum(t + t0, ct - 1), 0)),
                         pl.BlockSpec(block, lambda b, t: (b, jnp.maximum(t + t0 - ct, 0), 0))]

    def mod_spec(self, t0=0, rows=1):
        ct = self.ctx_tiles
        return pl.BlockSpec((rows, 1, MOD_ROWS, D_MODEL), lambda b, t: (b, ((t + t0) >= ct).astype(jnp.int32), 0, 0))

    def bwd_chunk(self, i):
        cc = self.ctx_chunks
        return jnp.where(i < cc, cc - 1 - i, self.nc - 1 + cc - i)


LIN_ROWS_PER_STEP = 2


def _lin_kernel(x_ref, m_ref, g_ref, w_ref, o_ref):
    rows, tm = x_ref.shape[0], x_ref.shape[1]
    h = [_modulate(x_ref[r], g_ref[0:1], m_ref[r, 0, 0:1], m_ref[r, 0, 1:2]).astype(bf16) for r in range(rows)]
    z = _dot(jnp.concatenate(h, axis=0), w_ref[...])
    for r in range(rows):
        o_ref[r] = z[r * tm:(r + 1) * tm]


def _mod_linear(geo, x, mods, gains, w, name):
    n = w.shape[1]
    rows = LIN_ROWS_PER_STEP if geo.batch % LIN_ROWS_PER_STEP == 0 else 1
    return pl.pallas_call(
        _lin_kernel,
        grid=(geo.batch // rows, geo.nt),
        in_specs=[geo.x_spec(0, rows), geo.mod_spec(0, rows), _const_spec((8, D_MODEL)), _const_spec((D_MODEL, n))],
        out_specs=pl.BlockSpec((rows, geo.tm, n), lambda b, t: (b, t, 0)),
        out_shape=jax.ShapeDtypeStruct((geo.batch, geo.t, n), f32),
        compiler_params=_params(2),
        name=name,
    )(x, mods, gains, w)


def _read_tile(x_refs, row, is_ctx):
    if len(x_refs) == 1:
        return x_refs[0][row]
    return jnp.where(is_ctx, x_refs[0][row], x_refs[1][row])


def _head_rms(o, gain, width):
    outs = []
    for h in range(o.shape[1] // width):
        oh = o[:, h * width:(h + 1) * width]
        outs.append(oh * lax.rsqrt(jnp.mean(oh * oh, axis=-1, keepdims=True) + NORM_EPS) * gain)
    return jnp.concatenate(outs, axis=1)


def _pre_attn(refs, row):
    (o_ref,) = refs
    return o_ref[row]


def _pre_scan(width, refs, row):
    of_ref, ob_ref, gate_ref, gn_ref = refs
    o = _head_rms(of_ref[row] + ob_ref[row], gn_ref[...], width)
    return (o * _silu(gate_ref[row])).astype(bf16)


def _pre_rwkv(refs, row):
    yf_ref, yb_ref, g_ref, bg_ref, vec_ref = refs
    y = yf_ref[row] + yb_ref[row]
    inv_n = 1.0 / RWKV_HEAD_SIZE
    mu = _seg_sum(y, RWKV_HEAD_SIZE) * inv_n
    dlt = y - mu
    var = _seg_sum(dlt * dlt, RWKV_HEAD_SIZE, exact_terms=False) * inv_n
    yn = dlt * lax.rsqrt(var + RWKV_LN_EPS) * vec_ref[0:1] + vec_ref[1:2]
    return (yn * g_ref[row] + bg_ref[row]).astype(bf16)


def _post_kernel(pre, n_x, n_mix, first_ctx_tiles, *refs):
    x_refs, (m_ref, g_ref), mix_refs = refs[:n_x], refs[n_x:n_x + 2], refs[n_x + 2:n_x + 2 + n_mix]
    wo_ref, win_ref, wout_ref, out_ref = refs[n_x + 2 + n_mix:]
    rows = range(out_ref.shape[0])
    is_ctx = pl.program_id(1) < first_ctx_tiles
    o = [pre(mix_refs, r) for r in rows]
    y = [_dot(o[r], wo_ref[...]) for r in rows]
    x1 = [_read_tile(x_refs, r, is_ctx) + m_ref[r, 0, 2:3] * _rms(y[r], g_ref[1:2]) for r in rows]
    h2 = [_modulate(x1[r], g_ref[2:3], m_ref[r, 0, 3:4], m_ref[r, 0, 4:5]).astype(bf16) for r in rows]
    us = [[] for _ in rows]
    for c in range(MLP_HIDDEN // D_MODEL):
        cols = slice(c * D_MODEL, (c + 1) * D_MODEL)
        for r in rows:
            us[r].append(jnp.square(jnp.maximum(_dot(h2[r], win_ref[:, cols]), 0.0)).astype(bf16))
    acc = [_dot(jnp.concatenate(us[r], axis=1), wout_ref[...]) for r in rows]
    for r in rows:
        out_ref[r] = x1[r] + m_ref[r, 0, 5:6] * _rms(acc[r], g_ref[3:4])


POST_ROWS_PER_STEP = 2


def _post_rows(geo, n_tile_inputs):
    tile = geo.tm * D_MODEL * 4
    weights = 2 * (D_MODEL * D_MODEL + 2 * D_MODEL * MLP_HIDDEN)
    for rows in (POST_ROWS_PER_STEP, 1):
        streamed = 2 * (n_tile_inputs + 2) * rows * tile
        live = 6 * rows * tile
        if geo.batch % rows == 0 and weights + streamed + live <= VMEM_LIMIT:
            return rows
    return 1


def _post_mlp(geo, x, mods, gains, pre, mix_args, mix_specs, w_o, w_in, w_out, skip_ctx, name, rows):
    t0 = geo.ctx_tiles if skip_ctx else 0
    x_args, x_specs = geo.x_in(x, t0, rows)
    return pl.pallas_call(
        functools.partial(_post_kernel, pre, len(x_args), len(mix_args), geo.ctx_tiles - t0),
        grid=(geo.batch // rows, geo.nt - t0),
        in_specs=x_specs + [geo.mod_spec(t0, rows), _const_spec((8, D_MODEL))] + mix_specs + [
            _const_spec((D_MODEL, D_MODEL)), _const_spec((D_MODEL, MLP_HIDDEN)), _const_spec((MLP_HIDDEN, D_MODEL))],
        out_specs=pl.BlockSpec((rows, geo.tm, D_MODEL), lambda b, t: (b, t, 0)),
        out_shape=jax.ShapeDtypeStruct((geo.batch, geo.t - t0 * geo.tm, D_MODEL), f32),
        compiler_params=_params(2),
        name=name,
    )(*x_args, mods, gains, *mix_args, w_o, w_in, w_out)


def _tile_spec(geo, width, col, t0, rows):
    return pl.BlockSpec((rows, geo.tm, width), lambda b, t: (b, t + t0, col))


def _rope(x, cos, sin):
    w = x.shape[1]
    reps = w // LANES
    cw = jnp.concatenate([cos] * reps, axis=1)
    sw = jnp.concatenate([sin] * reps, axis=1)
    lane = lax.broadcasted_iota(jnp.int32, x.shape, 1)
    first = (lane % ROPE_AXIS_DIM) < ROPE_FREQS
    partner = jnp.where(first, pltpu.roll(x, w - ROPE_FREQS, axis=1), pltpu.roll(x, ROPE_FREQS, axis=1))
    return x * cw + partner * sw


def _attn_proj_kernel(ctx_tiles, *refs):
    x_refs, (m_ref, g_ref, cos_ref, sin_ref, wq_ref, wk_ref, wv_ref, q_ref, k_ref, v_ref) = refs[:-10], refs[-10:]
    rows = range(q_ref.shape[0])
    is_ctx = pl.program_id(1) < ctx_tiles
    cos, sin = cos_ref[...], sin_ref[...]
    h = [_modulate(_read_tile(x_refs, r, is_ctx), g_ref[0:1], m_ref[r, 0, 0:1], m_ref[r, 0, 1:2]).astype(bf16)
         for r in rows]
    q = [_dot(h[r], wq_ref[...]) * (HEAD_DIM ** -0.5) for r in rows]
    k = [_dot(h[r], wk_ref[...]) for r in rows]
    v = [_dot(h[r], wv_ref[...]) for r in rows]
    q = [_rope(q[r], cos, sin) for r in rows]
    k = [_rope(k[r], cos, sin) for r in rows]
    for r in rows:
        for hd in range(ATTN_HEADS):
            q_ref[r, hd] = q[r][:, hd * HEAD_DIM:(hd + 1) * HEAD_DIM].astype(bf16)
        for hd in range(ATTN_KV_HEADS):
            k_ref[r, hd] = k[r][:, hd * HEAD_DIM:(hd + 1) * HEAD_DIM].astype(bf16)
            v_ref[r, hd] = v[r][:, hd * HEAD_DIM:(hd + 1) * HEAD_DIM].astype(bf16)


ATTN_ROWS_PER_STEP = 4


def _attn_core_kernel(geo, sink_ref, q_ref, kp_ref, kc_ref, kn_ref, kx_ref, vp_ref, vc_ref, vn_ref, vx_ref, o_ref):
    qb = pl.program_id(1)
    blk = ATTN_BLOCK
    n_loc = 3 * blk
    n_keys = n_loc + geo.n_ctx
    rows = ATTN_GROUP * blk
    first_lat = geo.n_ctx // blk
    n_blocks = geo.t // blk
    row = lax.broadcasted_iota(jnp.int32, (blk, n_keys), 0)
    col = lax.broadcasted_iota(jnp.int32, (blk, n_keys), 1)
    kblk = qb - 1 + col // blk
    ok_local = ((jnp.abs(col - blk - row) <= WINDOW) & (qb >= first_lat) & (kblk >= first_lat) & (kblk < n_blocks))
    ok = (col >= n_loc) | ok_local
    groups = range(ATTN_GROUP)
    units = [(r, j) for r in range(q_ref.shape[0]) for j in range(ATTN_KV_HEADS)]

    def scores(r, j):
        q4 = q_ref[r, ATTN_GROUP * j:ATTN_GROUP * (j + 1)].reshape(rows, HEAD_DIM)
        keys = jnp.concatenate([kp_ref[r, j], kc_ref[r, j], kn_ref[r, j], kx_ref[r, j]], axis=0)
        return lax.dot_general(q4, keys, NT, preferred_element_type=f32)

    def softmax(j, s):
        sg = [jnp.where(ok, s[g * blk:(g + 1) * blk], NEG_INF) for g in groups]
        sk = [sink_ref[ATTN_GROUP * j + g] for g in groups]
        m = [jnp.maximum(jnp.max(sg[g], axis=-1, keepdims=True), sk[g]) for g in groups]
        e = [jnp.exp(sg[g] - m[g]) for g in groups]
        inv = [1.0 / (jnp.sum(e[g], axis=-1, keepdims=True) + jnp.exp(sk[g] - m[g])) for g in groups]
        return jnp.concatenate([e[g].astype(bf16) for g in groups], axis=0), inv

    def values(r, j, p, inv):
        vals = jnp.concatenate([vp_ref[r, j], vc_ref[r, j], vn_ref[r, j], vx_ref[r, j]], axis=0)
        o4 = _dot(p, vals)
        return [o4[g * blk:(g + 1) * blk] * inv[g] for g in groups]

    outs = [None] * len(units)
    s_next = scores(*units[0])
    soft_prev = None
    for i, (r, j) in enumerate(units):
        s_cur = s_next
        if i + 1 < len(units):
            s_next = scores(*units[i + 1])
        soft_cur = softmax(j, s_cur)
        if soft_prev is not None:
            outs[i - 1] = values(*units[i - 1], *soft_prev)
        soft_prev = soft_cur
    outs[-1] = values(*units[-1], *soft_prev)
    for r in range(q_ref.shape[0]):
        heads = outs[r * ATTN_KV_HEADS:(r + 1) * ATTN_KV_HEADS]
        o_ref[r] = jnp.concatenate([o for head_outs in heads for o in head_outs], axis=1).astype(bf16)


def _rope_tables(geo):
    inv_freq = ROPE_BASE ** (-jnp.arange(ROPE_FREQS, dtype=f32) * 2.0 / ROPE_AXIS_DIM)
    pos = jnp.arange(geo.n_lat)
    row = (pos // GRID_W).astype(f32)
    col = (pos % GRID_W).astype(f32)
    ang = jnp.stack([row[:, None] * inv_freq, col[:, None] * inv_freq], axis=1)
    cos = jnp.cos(ang)
    sin = jnp.sin(ang)
    cos_h = jnp.concatenate([cos, cos], axis=2).reshape(geo.n_lat, HEAD_DIM)
    sin_h = jnp.concatenate([-sin, sin], axis=2).reshape(geo.n_lat, HEAD_DIM)
    cos_t = jnp.concatenate([jnp.ones((geo.n_ctx, HEAD_DIM), f32), cos_h], axis=0)
    sin_t = jnp.concatenate([jnp.zeros((geo.n_ctx, HEAD_DIM), f32), sin_h], axis=0)
    return jnp.tile(cos_t, (1, 2)), jnp.tile(sin_t, (1, 2))


def _attn_layer(geo, x, mods, gains, w_qkv, w_o, sink, w_in, w_out, skip_ctx):
    b, t, tm = geo.batch, geo.t, geo.tm
    q_cols = ATTN_HEADS * HEAD_DIM
    kv_cols = ATTN_KV_HEADS * HEAD_DIM
    wb = w_qkv.astype(bf16)
    cos_t, sin_t = _rope_tables(geo)
    tab_spec = pl.BlockSpec((tm, LANES), lambda bb, tt: (tt, 0))
    rows = LIN_ROWS_PER_STEP if b % LIN_ROWS_PER_STEP == 0 else 1
    x_args, x_specs = geo.x_in(x, 0, rows)
    q, k, v = pl.pallas_call(
        functools.partial(_attn_proj_kernel, geo.ctx_tiles),
        grid=(b // rows, geo.nt),
        in_specs=x_specs + [geo.mod_spec(0, rows), _const_spec((8, D_MODEL)), tab_spec, tab_spec,
                  _const_spec((D_MODEL, q_cols)), _const_spec((D_MODEL, kv_cols)), _const_spec((D_MODEL, kv_cols))],
        out_specs=[pl.BlockSpec((rows, ATTN_HEADS, tm, HEAD_DIM), lambda bb, tt: (bb, 0, tt, 0)),
                   pl.BlockSpec((rows, ATTN_KV_HEADS, tm, HEAD_DIM), lambda bb, tt: (bb, 0, tt, 0)),
                   pl.BlockSpec((rows, ATTN_KV_HEADS, tm, HEAD_DIM), lambda bb, tt: (bb, 0, tt, 0))],
        out_shape=[jax.ShapeDtypeStruct((b, ATTN_HEADS, t, HEAD_DIM), bf16),
                   jax.ShapeDtypeStruct((b, ATTN_KV_HEADS, t, HEAD_DIM), bf16),
                   jax.ShapeDtypeStruct((b, ATTN_KV_HEADS, t, HEAD_DIM), bf16)],
        compiler_params=_params(2),
        name="attn_proj",
    )(*x_args, mods, gains, cos_t, sin_t, wb[:, :q_cols], wb[:, q_cols:q_cols + kv_cols], wb[:, q_cols + kv_cols:])

    blk = ATTN_BLOCK
    n_blocks = t // blk

    arows = ATTN_ROWS_PER_STEP if b % ATTN_ROWS_PER_STEP == 0 else 1

    def kv_spec(off):
        return pl.BlockSpec((arows, ATTN_KV_HEADS, blk, HEAD_DIM),
                            lambda bb, qb: (bb, 0, jnp.clip(qb + off, 0, n_blocks - 1), 0))

    ctx_spec = pl.BlockSpec((arows, ATTN_KV_HEADS, geo.n_ctx, HEAD_DIM), lambda bb, qb: (bb, 0, 0, 0))
    o = pl.pallas_call(
        functools.partial(_attn_core_kernel, geo),
        grid=(b // arows, n_blocks),
        in_specs=[pl.BlockSpec(memory_space=pltpu.SMEM),
                  pl.BlockSpec((arows, ATTN_HEADS, blk, HEAD_DIM), lambda bb, qb: (bb, 0, qb, 0)),
                  kv_spec(-1), kv_spec(0), kv_spec(1), ctx_spec,
                  kv_spec(-1), kv_spec(0), kv_spec(1), ctx_spec],
        out_specs=pl.BlockSpec((arows, blk, q_cols), lambda bb, qb: (bb, qb, 0)),
        out_shape=jax.ShapeDtypeStruct((b, t, q_cols), bf16),
        compiler_params=_params(2),
        name="attn_core",
    )(sink, q, k, k, k, k, v, v, v, v)

    t0 = geo.ctx_tiles if skip_ctx else 0
    rows = _post_rows(geo, 1)
    return _post_mlp(geo, x, mods, gains, _pre_attn, [o], [_tile_spec(geo, D_MODEL, 0, t0, rows)],
                     w_o.astype(bf16), w_in, w_out, skip_ctx, "attn_post", rows)


def _gated_prep(entries, heads, dk, dv):
    sums = [_cumsum_rows(g, reverse) for _, _, _, g, _, _, reverse in entries]
    units = []
    for (q, k, v, g, st_ref, base, reverse), b in zip(entries, sums):
        n = q.shape[0]
        btot = b[0:1] if reverse else b[n - 1:n]
        q_dec = (q * jnp.exp(b)).astype(bf16)
        k_inv32 = k * jnp.exp(-b)
        k_inv = k_inv32.astype(bf16)
        dec = jnp.exp(btot)
        k_end = (k_inv32 * dec).astype(bf16)
        vb = v.astype(bf16)
        ri = lax.broadcasted_iota(jnp.int32, (n, n), 0)
        ci = lax.broadcasted_iota(jnp.int32, (n, n), 1)
        tri = (ci >= ri) if reverse else (ci <= ri)
        for h in range(heads):
            ks = slice(h * dk, (h + 1) * dk)
            vs = slice(h * dv, (h + 1) * dv)
            units.append(dict(q=q_dec[:, ks], ki=k_inv[:, ks], ke=k_end[:, ks], v=vb[:, vs], dec=dec[:, ks],
                              tri=tri, st=st_ref[base + h], ref=st_ref, h=base + h))
    return units


def _gated_matmuls(units, heads):
    for u in units:
        a = lax.dot_general(u["q"], u["ki"], NT, preferred_element_type=f32)
        u["a"] = jnp.where(u["tri"], a, 0.0).astype(bf16)
        u["qs"] = lax.dot_general(u["q"], u["st"].astype(bf16), NT, preferred_element_type=f32)
    for u in units:
        u["o"] = _dot(u["a"], u["v"]) + u["qs"]
        u["new"] = u["st"] * u["dec"] + lax.dot_general(u["v"], u["ke"], TN, preferred_element_type=f32)
    for u in units:
        u["ref"][u["h"]] = u["new"]
    return [jnp.concatenate([u["o"] for u in units[i:i + heads]], axis=1) for i in range(0, len(units), heads)]


def _gated_chunk(groups, heads, dk, dv, staggered):
    def prep(group):
        return _gated_prep([thunk() for thunk in group], heads, dk, dv)

    if not staggered:
        units = prep([thunk for group in groups for thunk in group])
        per_row = len(groups[0])
        outs = _gated_matmuls(units, heads)
        return [outs[i:i + per_row] for i in range(0, len(outs), per_row)]
    outs = []
    ready = prep(groups[0])
    for i in range(len(groups)):
        nxt = prep(groups[i + 1]) if i + 1 < len(groups) else None
        outs.append(_gated_matmuls(ready, heads))
        ready = nxt
    return outs


def _gla_gate(zd, wu_ref, bias_ref, d):
    zg = _dot(zd.astype(bf16), wu_ref[d]) + bias_ref[d:d + 1]
    return (jnp.minimum(zg, 0.0) - jnp.log(1.0 + jnp.exp(-jnp.abs(zg)))) * (1.0 / GLA_TAU)


def _gla_scan_kernel(qkf_ref, vf_ref, zdf_ref, qkb_ref, vb_ref, zdb_ref, wu_ref, bias_ref,
                     of_ref, ob_ref, sf_ref, sb_ref):
    @pl.when(pl.program_id(1) == 0)
    def _():
        sf_ref[...] = jnp.zeros_like(sf_ref)
        sb_ref[...] = jnp.zeros_like(sb_ref)

    def entry(row, d, qk_ref, v_ref, zd_ref, s_ref):
        qk = qk_ref[row]
        return (qk[:, :GLA_KEY_DIM] * (GLA_DK ** -0.5), qk[:, GLA_KEY_DIM:], v_ref[row],
                _gla_gate(zd_ref[row], wu_ref, bias_ref, d), s_ref, row * GLA_HEADS, d == 1)

    groups = [[functools.partial(entry, row, 0, qkf_ref, vf_ref, zdf_ref, sf_ref),
               functools.partial(entry, row, 1, qkb_ref, vb_ref, zdb_ref, sb_ref)]
              for row in range(qkf_ref.shape[0])]
    for row, (o_f, o_b) in enumerate(_gated_chunk(groups, GLA_HEADS, GLA_DK, GLA_DV, staggered=False)):
        of_ref[row] = o_f
        ob_ref[row] = o_b


def _hgrn_scan_kernel(qf_ref, if_ref, zff_ref, qb_ref, ib_ref, zfb_ref, lb_ref, of_ref, ob_ref, sf_ref, sb_ref):
    @pl.when(pl.program_id(1) == 0)
    def _():
        sf_ref[...] = jnp.zeros_like(sf_ref)
        sb_ref[...] = jnp.zeros_like(sb_ref)

    lb = lb_ref[...]
    def entry(row, d, q_ref, i_ref, zf_ref, s_ref):
        f = lb + (1.0 - lb) * _sigmoid(zf_ref[row])
        return (_silu(q_ref[row]), 1.0 - f, i_ref[row], jnp.log(f), s_ref, row * HGRN_HEADS, d == 1)

    groups = [[functools.partial(entry, row, 0, qf_ref, if_ref, zff_ref, sf_ref),
               functools.partial(entry, row, 1, qb_ref, ib_ref, zfb_ref, sb_ref)]
              for row in range(qf_ref.shape[0])]
    for row, (o_f, o_b) in enumerate(_gated_chunk(groups, HGRN_HEADS, HGRN_EXPAND, HGRN_EXPAND, staggered=True)):
        of_ref[row] = o_f
        ob_ref[row] = o_b


SCAN_ROWS_PER_STEP = 8


def _scan_rows(geo):
    return SCAN_ROWS_PER_STEP if geo.batch % SCAN_ROWS_PER_STEP == 0 else 1


def _chunk_spec(geo, width, col, reverse):
    rows = _scan_rows(geo)
    if reverse:
        return pl.BlockSpec((rows, SCAN_CHUNK, width), lambda b, i: (b, geo.bwd_chunk(i), col))
    return pl.BlockSpec((rows, SCAN_CHUNK, width), lambda b, i: (b, i, col))


def _scan_call(geo, body, args, specs, heads, dk, dv, name):
    out_sds = jax.ShapeDtypeStruct((geo.batch, geo.t, heads * dv), f32)
    rows = _scan_rows(geo)
    state = pltpu.VMEM((rows * heads, dv, dk), f32)
    return pl.pallas_call(
        body,
        grid=(geo.batch // rows, geo.nc),
        in_specs=specs,
        out_specs=[_chunk_spec(geo, heads * dv, 0, False), _chunk_spec(geo, heads * dv, 0, True)],
        out_shape=[out_sds, out_sds],
        scratch_shapes=[state, state],
        compiler_params=_params(2),
        name=name,
    )(*args)


def _gla_layer(geo, x, mods, gains, w_in_p, w_gd, w_gu, g_bias, g_norm, w_o, w_in, w_out, skip_ctx):
    r = GLA_GATE_RANK
    n_z = 2 * GLA_KEY_DIM + 2 * D_MODEL
    w_all = jnp.concatenate([w_in_p, w_gd[0], w_gd[1], jnp.zeros((D_MODEL, LANES - 2 * r), f32)], axis=1).astype(bf16)
    z = _mod_linear(geo, x, mods, gains, w_all, "gla_proj")
    wu = jnp.zeros((2, LANES, GLA_KEY_DIM), f32).at[0, :r].set(w_gu[0]).at[1, r:2 * r].set(w_gu[1]).astype(bf16)
    zd_col = n_z // LANES
    specs = []
    for rev in (False, True):
        specs += [_chunk_spec(geo, 2 * GLA_KEY_DIM, 0, rev), _chunk_spec(geo, D_MODEL, 1, rev),
                  _chunk_spec(geo, LANES, zd_col, rev)]
    specs += [_const_spec((2, LANES, GLA_KEY_DIM)), _const_spec((2, GLA_KEY_DIM))]
    o_f, o_b = _scan_call(geo, _gla_scan_kernel, [z, z, z, z, z, z, wu, g_bias], specs,
                          GLA_HEADS, GLA_DK, GLA_DV, "gla_scan")
    t0 = geo.ctx_tiles if skip_ctx else 0
    rows = _post_rows(geo, 3)
    mix_specs = [_tile_spec(geo, D_MODEL, 0, t0, rows), _tile_spec(geo, D_MODEL, 0, t0, rows),
                 _tile_spec(geo, D_MODEL, 2, t0, rows), _const_spec((1, GLA_DV))]
    return _post_mlp(geo, x, mods, gains, functools.partial(_pre_scan, GLA_DV),
                     [o_f, o_b, z, g_norm.reshape(1, GLA_DV)], mix_specs,
                     w_o.astype(bf16), w_in, w_out, skip_ctx, "gla_post", rows)


def _hgrn_layer(geo, x, mods, gains, w_in_p, w_f, lower_bound, g_norm, w_o, w_in, w_out, skip_ctx):
    w_all = jnp.concatenate([w_in_p, w_f[0], w_f[1]], axis=1).astype(bf16)
    z = _mod_linear(geo, x, mods, gains, w_all, "hgrn_proj")
    specs = []
    for rev in (False, True):
        specs += [_chunk_spec(geo, D_MODEL, 0, rev), _chunk_spec(geo, D_MODEL, 1, rev),
                  _chunk_spec(geo, D_MODEL, 4 if rev else 3, rev)]
    specs += [_const_spec((1, D_MODEL))]
    o_f, o_b = _scan_call(geo, _hgrn_scan_kernel, [z, z, z, z, z, z, lower_bound.reshape(1, D_MODEL)], specs,
                          HGRN_HEADS, HGRN_EXPAND, HGRN_EXPAND, "hgrn_scan")
    t0 = geo.ctx_tiles if skip_ctx else 0
    rows = _post_rows(geo, 3)
    mix_specs = [_tile_spec(geo, D_MODEL, 0, t0, rows), _tile_spec(geo, D_MODEL, 0, t0, rows),
                 _tile_spec(geo, D_MODEL, 2, t0, rows), _const_spec((1, HGRN_EXPAND))]
    return _post_mlp(geo, x, mods, gains, functools.partial(_pre_scan, HGRN_EXPAND),
                     [o_f, o_b, z, g_norm.reshape(1, HGRN_EXPAND)], mix_specs,
                     w_o.astype(bf16), w_in, w_out, skip_ctx, "hgrn_post", rows)


def _rwkv_proj_kernel(geo, x_ref, xp_ref, xn_ref, m_ref, g_ref, mix_ref, vec_ref, wrkv_ref, wdn_ref, wup_ref,
                      aup_ref, gup_ref, r_ref, v_ref, gate_ref, kk_ref, lw0_ref, lw1_ref, a0_ref, a1_ref,
                      k0_ref, k1_ref, bg_ref):
    t = pl.program_id(1)
    tm = geo.tm
    gain, shift, scale = g_ref[0:1], m_ref[0, 0, 0:1], m_ref[0, 0, 1:2]
    h = _modulate(x_ref[0], gain, shift, scale)
    seg_first = (t == 0) | (t == geo.ctx_tiles)
    seg_last = (t == geo.ctx_tiles - 1) | (t == geo.nt - 1)
    h_prev = jnp.where(seg_first, 0.0, _modulate(xp_ref[0], gain, shift, scale)[7:8])
    h_next = jnp.where(seg_last, 0.0, _modulate(xn_ref[0], gain, shift, scale)[0:1])
    row = lax.broadcasted_iota(jnp.int32, h.shape, 0)
    up = jnp.where(row == 0, h_prev, pltpu.roll(h, 1, axis=0))
    dn = jnp.where(row == tm - 1, h_next, pltpu.roll(h, tm - 1, axis=0))
    dx = 0.5 * (up + dn) - h

    def mixed(n):
        return (h + dx * mix_ref[n:n + 1]).astype(bf16)

    dw = jnp.tanh(_dot(mixed(3), wdn_ref[:, 0:LANES])).astype(bf16)
    da = _dot(mixed(4), wdn_ref[:, LANES:2 * LANES]).astype(bf16)
    dg = _sigmoid(_dot(mixed(5), wdn_ref[:, 2 * LANES:3 * LANES])).astype(bf16)
    gate = _dot(dg, gup_ref[...])
    gate_ref[0] = gate
    a_dirs = []
    for d, (lw_ref, a_ref) in enumerate(((lw0_ref, a0_ref), (lw1_ref, a1_ref))):
        lw_ref[0] = -RWKV_DECAY_SCALE * _sigmoid(vec_ref[d:d + 1] + _dot(dw, wup_ref[d]))
        a_dirs.append(_sigmoid(vec_ref[2 + d:3 + d] + _dot(da, aup_ref[d])))
        a_ref[0] = a_dirs[d]
    k = _dot(mixed(1), wrkv_ref[1])
    r = _dot(mixed(0), wrkv_ref[0])
    v = _dot(mixed(2), wrkv_ref[2])
    kk = k * vec_ref[4:5]
    kk_ref[0] = kk * lax.rsqrt(jnp.maximum(_seg_sum(kk * kk, RWKV_HEAD_SIZE, exact_terms=False), L2_EPS * L2_EPS))
    kds = []
    for d, kd_ref in enumerate((k0_ref, k1_ref)):
        kds.append(k * (1.0 + (a_dirs[d] - 1.0) * vec_ref[5:6]))
        kd_ref[0] = kds[d]
    r_ref[0] = r
    v_ref[0] = v
    bg_ref[0] = _seg_sum(r * (0.5 * (kds[0] + kds[1])) * vec_ref[6:7], RWKV_HEAD_SIZE,
                         exact_terms=False) * v * gate


def _rwkv_chunk_units(units):
    n = SCAN_CHUNK
    hs = RWKV_HEAD_SIZE
    lane = lax.broadcasted_iota(jnp.int32, (n, RWKV_PAIR), 1)
    t_i = lax.broadcasted_iota(jnp.int32, (n, RWKV_PAIR), 0)
    s_i = lane % hs
    head0 = lane < hs
    ri = lax.broadcasted_iota(jnp.int32, (RWKV_PAIR, RWKV_PAIR), 0)
    ci = lax.broadcasted_iota(jnp.int32, (RWKV_PAIR, RWKV_PAIR), 1)
    same_head = ri // hs == ci // hs
    top_rows = ri < hs

    def block_diag(x):
        return jnp.concatenate([jnp.where(head0, x, 0.0), jnp.where(head0, 0.0, x)], axis=0)

    def inverse_4x4_blocks(nm, reverse):
        toward = RWKV_PAIR - 1 if not reverse else 1
        dist = (s_i - t_i) if reverse else (t_i - s_i)
        near = pltpu.roll(nm, toward, axis=1)
        far = pltpu.roll(nm, (2 * toward) % RWKV_PAIR, axis=1)
        diag1 = jnp.sum(jnp.where(same4 & (dist == 1), nm, 0.0), axis=0, keepdims=True)
        diag2 = jnp.sum(jnp.where(same4 & (dist == 2), nm, 0.0), axis=0, keepdims=True)
        p2 = diag2 + pltpu.roll(diag1, toward, axis=1) * diag1
        two = nm + near * diag1
        three = two + far * p2
        return jnp.where(s_i == t_i, 1.0,
                         jnp.where(same4 & (dist == 1), nm,
                                   jnp.where(same4 & (dist == 2), two, jnp.where(same4 & (dist == 3), three, 0.0))))

    def cross_blocks(b):
        return (t_i // (2 * b) == s_i // (2 * b)) & (t_i // b != s_i // b)

    st = []
    for r, kd, v, kk, a, lw, ht, reverse in units:
        c = _cumsum_rows(lw, reverse)
        tot = c[0:1] if reverse else c[n - 1:n]
        e_neg = jnp.exp(-c)
        dec = jnp.exp(tot)
        kb = jnp.concatenate([kd * e_neg, kk * a * e_neg], axis=0)
        kb_t = kb.T
        kb_sw = pltpu.roll(kb_t, hs, axis=1)
        rhs = jnp.concatenate([jnp.where(same_head, jnp.where(top_rows, kb_t, kb_sw), 0.0),
                               jnp.where(same_head, jnp.where(top_rows, kb_sw, kb_t), 0.0)], axis=1)
        st.append(dict(
            strict=(s_i > t_i) if reverse else (s_i < t_i), incl=(s_i >= t_i) if reverse else (s_i <= t_i),
            a_bar=(-kk * jnp.exp(c - lw)).astype(bf16), r_bar=(r * jnp.exp(c)).astype(bf16), rhs=rhs.astype(bf16),
            v_bd=block_diag(v).astype(bf16), h_t=ht.T.astype(bf16),
            ends=(kb * dec).astype(bf16),
            v=v, ht=ht, dec=dec, reverse=reverse))
    for s in st:
        g = _dot(jnp.concatenate([s["a_bar"], s["r_bar"]], axis=0), s["rhs"])
        s["a_ak"] = jnp.where(s["strict"], g[0:n, 0:RWKV_PAIR], 0.0).astype(bf16)
        s["nmat"] = jnp.where(s["strict"], g[0:n, RWKV_PAIR:], 0.0)
        s["q_k"] = jnp.where(s["incl"], g[n:, 0:RWKV_PAIR], 0.0).astype(bf16)
        s["q_b"] = jnp.where(s["incl"], g[n:, RWKV_PAIR:], 0.0).astype(bf16)
    same4 = t_i // 4 == s_i // 4
    for s in st:
        s["inv"] = inverse_4x4_blocks(s["nmat"], s["reverse"])
    b = 4
    while 2 * b < n:
        for s in st:
            s["pc"] = _dot(s["inv"].astype(bf16), block_diag(jnp.where(cross_blocks(b), s["nmat"], 0.0)).astype(bf16))
        for s in st:
            s["inv"] = s["inv"] + _dot(s["pc"].astype(bf16), block_diag(s["inv"]).astype(bf16))
        b *= 2
    for s in st:
        x = _dot(jnp.concatenate([s["a_bar"], s["a_ak"]], axis=1), jnp.concatenate([s["h_t"], s["v_bd"]], axis=0))
        s["inv"] = s["inv"].astype(bf16)
        s["w"] = _dot(s["inv"], block_diag(x).astype(bf16))
    for s in st:
        s["cw"] = _dot(jnp.where(cross_blocks(n // 2), s["nmat"], 0.0).astype(bf16), block_diag(s["w"]).astype(bf16))
    for s in st:
        s["u"] = s["w"] + _dot(s["inv"], block_diag(s["cw"]).astype(bf16))
    for s in st:
        s["vu_t"] = jnp.concatenate([s["v"], s["u"]], axis=0).T.astype(bf16)
    for s in st:
        s["y"] = _dot(jnp.concatenate([s["r_bar"], s["q_k"], s["q_b"]], axis=1),
                      jnp.concatenate([s["h_t"], s["v_bd"], block_diag(s["u"]).astype(bf16)], axis=0))
    for s in st:
        s["upd"] = _dot(s["vu_t"], s["ends"])
    return [(s["y"], s["ht"] * s["dec"] + jnp.where(same_head, s["upd"], 0.0)) for s in st]


def _rwkv_scan_kernel(rows, pairs, *refs):
    fwd, bwd = refs[0:6], refs[6:12]
    yf_ref, yb_ref, hf_ref, hb_ref = refs[12:]

    @pl.when(pl.program_id(2) == 0)
    def _():
        hf_ref[...] = jnp.zeros_like(hf_ref)
        hb_ref[...] = jnp.zeros_like(hb_ref)

    units, dests = [], []
    for ins, y_ref, h_ref, reverse in ((fwd, yf_ref, hf_ref, False), (bwd, yb_ref, hb_ref, True)):
        r_ref, v_ref, kk_ref, lw_ref, a_ref, kd_ref = ins
        for row in range(rows):
            for p in range(pairs):
                sl = slice(p * RWKV_PAIR, (p + 1) * RWKV_PAIR)
                slot = row * pairs + p
                units.append((r_ref[row, :, sl], kd_ref[row, :, sl], v_ref[row, :, sl], kk_ref[row, :, sl],
                              a_ref[row, :, sl], lw_ref[row, :, sl], h_ref[slot], reverse))
                dests.append((y_ref, h_ref, row, slot, sl))
    for (y, h_new), (y_ref, h_ref, row, slot, sl) in zip(_rwkv_chunk_units(units), dests):
        y_ref[row, :, sl] = y
        h_ref[slot] = h_new


RWKV_PAIRS_PER_STEP = 8
RWKV_ROWS_PER_STEP = 4


def _rwkv_layer(geo, x, mods, gains, mix, w_rkv, w0, w_down, w_up, a0, a_down, a_up, g_down, g_up, k_k, k_a,
                r_k, ln_w, ln_b, w_o, w_in, w_out, skip_ctx):
    b, t, tm = geo.batch, geo.t, geo.tm
    d = D_MODEL
    rank = w_down.shape[-1]
    assert 2 * rank == LANES and a_down.shape[-1] == rank and g_down.shape[-1] == LANES
    mix8 = jnp.pad(mix, ((0, 2), (0, 0)))
    vec = jnp.stack([w0[0], w0[1], a0[0], a0[1], k_k, k_a, r_k.reshape(d), jnp.zeros_like(k_k)])
    w_dn = jnp.concatenate([w_down[0], w_down[1], a_down[0], a_down[1], g_down], axis=1).astype(bf16)

    def padded_up(w):
        return jnp.zeros((2, LANES, d), f32).at[0, :rank].set(w[0]).at[1, rank:].set(w[1]).astype(bf16)

    halo = 8
    n_halo = t // halo
    per = tm // halo
    x_prev = pl.BlockSpec((1, halo, d), lambda bb, tt: (bb, jnp.maximum(tt * per - 1, 0), 0))
    x_next = pl.BlockSpec((1, halo, d), lambda bb, tt: (bb, jnp.minimum((tt + 1) * per, n_halo - 1), 0))
    out_spec = pl.BlockSpec((1, tm, d), lambda bb, tt: (bb, tt, 0))
    sds = jax.ShapeDtypeStruct((b, t, d), f32)
    r, v, gate, kk, lw0, lw1, a_0, a_1, k0, k1, bonus_g = pl.pallas_call(
        functools.partial(_rwkv_proj_kernel, geo),
        grid=(b, geo.nt),
        in_specs=[geo.x_spec(), x_prev, x_next, geo.mod_spec(), _const_spec((8, d)), _const_spec((8, d)),
                  _const_spec((8, d)), _const_spec((3, d, d)), _const_spec((d, 3 * LANES)),
                  _const_spec((2, LANES, d)), _const_spec((2, LANES, d)), _const_spec((LANES, d))],
        out_specs=[out_spec] * 11,
        out_shape=[sds] * 11,
        compiler_params=_params(2),
        name="rwkv_proj",
    )(x, x, x, mods, gains, mix8, vec, w_rkv.astype(bf16), w_dn, padded_up(w_up), padded_up(a_up),
      g_up.astype(bf16))

    pairs = RWKV_PAIRS_PER_STEP
    rows = RWKV_ROWS_PER_STEP if b % RWKV_ROWS_PER_STEP == 0 else 1
    width = pairs * RWKV_PAIR
    groups = d // width

    def cspec(reverse):
        if reverse:
            return pl.BlockSpec((rows, SCAN_CHUNK, width), lambda bb, gg, i: (bb, geo.bwd_chunk(i), gg))
        return pl.BlockSpec((rows, SCAN_CHUNK, width), lambda bb, gg, i: (bb, i, gg))

    state = pltpu.VMEM((rows * pairs, RWKV_PAIR, RWKV_PAIR), f32)
    y_f, y_b = pl.pallas_call(
        functools.partial(_rwkv_scan_kernel, rows, pairs),
        grid=(b // rows, groups, geo.nc),
        in_specs=[cspec(False)] * 6 + [cspec(True)] * 6,
        out_specs=[cspec(False), cspec(True)],
        out_shape=[sds, sds],
        scratch_shapes=[state, state],
        compiler_params=_params(3),
        name="rwkv_scan",
    )(r, v, kk, lw0, a_0, k0, r, v, kk, lw1, a_1, k1)

    t0 = geo.ctx_tiles if skip_ctx else 0
    post_vec = jnp.stack([ln_w, ln_b] + [jnp.zeros_like(ln_w)] * 6)
    rows = _post_rows(geo, 4)
    mix_specs = [_tile_spec(geo, d, 0, t0, rows)] * 4 + [_const_spec((8, d))]
    return _post_mlp(geo, x, mods, gains, _pre_rwkv, [y_f, y_b, gate, bonus_g, post_vec], mix_specs,
                     w_o.astype(bf16), w_in, w_out, skip_ctx, "rwkv_post", rows)


def _hgrn_lower_bound(lb_param, layer):
    p = jax.nn.softmax(lb_param.astype(f32), axis=0)
    return (jnp.cumsum(p, axis=0) - p[0])[layer]


def kernel(x, c, ctx, c_ctx, w_mod, b_mod, g_pre_mix, g_post_mix, g_pre_mlp, g_post_mlp, w_mlp_in, w_mlp_out, attn_w_qkv, attn_w_o, attn_sink, gla_w_in, gla_w_gate_down, gla_w_gate_up, gla_gate_bias, gla_g_norm, gla_w_o, rwkv_mix, rwkv_w_rkv, rwkv_w0, rwkv_w_down, rwkv_w_up, rwkv_a0, rwkv_a_down, rwkv_a_up, rwkv_g_down, rwkv_g_up, rwkv_k_k, rwkv_k_a, rwkv_r_k, rwkv_ln_w, rwkv_ln_b, rwkv_w_o, hgrn_w_in, hgrn_w_f, hgrn_lb, hgrn_g_norm, hgrn_w_o):
    depth = w_mod.shape[0]
    geo = _Geom(x.shape[0], ctx.shape[1], x.shape[1])
    mods_all = _mod_vectors(c, c_ctx, w_mod, b_mod)
    xs = (ctx, x)
    for i in range(depth):
        kind, j = i % 4, i // 4
        skip_ctx = i == depth - 1
        mods = mods_all[i]
        gains = jnp.stack([g_pre_mix[i], g_post_mix[i], g_pre_mlp[i], g_post_mlp[i]] + [jnp.zeros_like(g_pre_mix[i])] * 4)
        w_in, w_out = w_mlp_in[i].astype(bf16), w_mlp_out[i].astype(bf16)
        if kind == 0:
            xs = _attn_layer(geo, xs, mods, gains, attn_w_qkv[j], attn_w_o[j], attn_sink[j], w_in, w_out, skip_ctx)
        elif kind == 1:
            xs = _gla_layer(geo, xs, mods, gains, gla_w_in[j], gla_w_gate_down[j], gla_w_gate_up[j],
                            gla_gate_bias[j], gla_g_norm[j], gla_w_o[j], w_in, w_out, skip_ctx)
        elif kind == 2:
            xs = _rwkv_layer(geo, xs, mods, gains, rwkv_mix[j], rwkv_w_rkv[j], rwkv_w0[j], rwkv_w_down[j],
                             rwkv_w_up[j], rwkv_a0[j], rwkv_a_down[j], rwkv_a_up[j], rwkv_g_down[j], rwkv_g_up[j],
                             rwkv_k_k[j], rwkv_k_a[j], rwkv_r_k[j], rwkv_ln_w[j], rwkv_ln_b[j], rwkv_w_o[j],
                             w_in, w_out, skip_ctx)
        else:
            xs = _hgrn_layer(geo, xs, mods, gains, hgrn_w_in[j], hgrn_w_f[j], _hgrn_lower_bound(hgrn_lb, i),
                             hgrn_g_norm[j], hgrn_w_o[j], w_in, w_out, skip_ctx)
        if skip_ctx:
            return xs
    return xs[:, geo.n_ctx:]
```

```python
import functools

import jax
import jax.numpy as jnp
from jax import lax
from jax.experimental import pallas as pl
from jax.experimental.pallas import tpu as pltpu

f32 = jnp.float32
bf16 = jnp.bfloat16

D_MODEL = 1024
N_MOD = 6
MLP_HIDDEN = 4 * D_MODEL
NORM_EPS = 1e-6
NEG_INF = -1e30
GRID_W = 64

ATTN_HEADS = 16
ATTN_KV_HEADS = 4
ATTN_GROUP = ATTN_HEADS // ATTN_KV_HEADS
HEAD_DIM = 64
WINDOW = 128
ATTN_BLOCK = 128
ROPE_BASE = 10000.0
ROPE_AXIS_DIM = HEAD_DIM // 2
ROPE_FREQS = ROPE_AXIS_DIM // 2

GLA_HEADS = 4
GLA_KEY_DIM = D_MODEL // 2
GLA_DK = GLA_KEY_DIM // GLA_HEADS
GLA_DV = D_MODEL // GLA_HEADS
GLA_GATE_RANK = 16
GLA_TAU = 16.0
SCAN_CHUNK = 64

RWKV_HEAD_SIZE = 64
RWKV_LN_EPS = 64e-5
L2_EPS = 1e-12
RWKV_PAIR = 2 * RWKV_HEAD_SIZE
RWKV_DECAY_SCALE = 0.6065306597126334

HGRN_EXPAND = 128
HGRN_HEADS = D_MODEL // HGRN_EXPAND

LANES = 128
MOD_ROWS = 8
VMEM_LIMIT = 56 * 1024 * 1024

NT = (((1,), (1,)), ((), ()))
TN = (((0,), (0,)), ((), ()))


def _params(n_grid):
    return pltpu.CompilerParams(dimension_semantics=("arbitrary",) * n_grid, vmem_limit_bytes=VMEM_LIMIT)


def _const_spec(shape):
    nd = len(shape)
    return pl.BlockSpec(shape, lambda *_: (0,) * nd, pipeline_mode=pl.Buffered(1))


def _dot(a, b):
    return jnp.dot(a, b, preferred_element_type=f32)


def _sigmoid(x):
    return 0.5 * jnp.tanh(0.5 * x) + 0.5


def _silu(x):
    return x * _sigmoid(x)


def _rms(x, gain):
    return x * lax.rsqrt(jnp.mean(x * x, axis=-1, keepdims=True) + NORM_EPS) * gain


def _modulate(x, gain, shift, scale):
    return _rms(x, gain) * (1.0 + scale) + shift


def _seg_sum(x, seg, exact_terms=True):
    r = lax.broadcasted_iota(jnp.int32, (LANES, LANES), 0) // seg
    c = lax.broadcasted_iota(jnp.int32, (LANES, LANES), 1) // seg
    ones_bd = (r == c).astype(bf16)
    hi = x.astype(bf16)
    lo = (x - hi.astype(f32)).astype(bf16) if exact_terms else None
    outs = []
    for j in range(x.shape[1] // LANES):
        sl = slice(j * LANES, (j + 1) * LANES)
        part = _dot(hi[:, sl], ones_bd)
        outs.append(part + _dot(lo[:, sl], ones_bd) if exact_terms else part)
    return jnp.concatenate(outs, axis=1)


def _cumsum_rows(g, reverse):
    n = g.shape[0]
    ri = lax.broadcasted_iota(jnp.int32, (n, n), 0)
    ci = lax.broadcasted_iota(jnp.int32, (n, n), 1)
    tri = ((ci >= ri) if reverse else (ci <= ri)).astype(bf16)
    hi = g.astype(bf16)
    lo = (g - hi.astype(f32)).astype(bf16)
    return _dot(tri, hi) + _dot(tri, lo)


def _mod_kernel(c_ref, w_ref, b_ref, o_ref):
    o_ref[0] = _dot(_silu(c_ref[...]).astype(bf16), w_ref[0]) + b_ref[0]


def _mod_vectors(c, c_ctx, w_mod, b_mod):
    depth, d, _ = w_mod.shape
    batch = c.shape[0]
    rows = -(-(batch + 1) // 8) * 8
    cc = jnp.zeros((rows, d), f32).at[:batch].set(c).at[batch].set(c_ctx)
    out = pl.pallas_call(
        _mod_kernel,
        grid=(depth, N_MOD),
        in_specs=[
            pl.BlockSpec((rows, d), lambda i, j: (0, 0)),
            pl.BlockSpec((1, d, d), lambda i, j: (i, 0, j)),
            pl.BlockSpec((1, 1, d), lambda i, j: (i, 0, j)),
        ],
        out_specs=pl.BlockSpec((1, rows, d), lambda i, j: (i, 0, j)),
        out_shape=jax.ShapeDtypeStruct((depth, rows, N_MOD * d), f32),
        compiler_params=_params(2),
        name="mod_vectors",
    )(cc, w_mod.astype(bf16), b_mod.reshape(depth, 1, N_MOD * d))
    lat = out[:, :batch].reshape(depth, batch, 1, N_MOD, d)
    con = jnp.broadcast_to(out[:, batch].reshape(depth, 1, 1, N_MOD, d), lat.shape)
    mods = jnp.concatenate([con, lat], axis=2)
    return jnp.pad(mods, ((0, 0), (0, 0), (0, 0), (0, MOD_ROWS - N_MOD), (0, 0)))


class _Geom:
    def __init__(self, batch, n_ctx, n_lat):
        self.batch, self.n_ctx, self.n_lat = batch, n_ctx, n_lat
        self.t = n_ctx + n_lat
        self.tm = min(256, n_ctx)
        assert n_ctx % self.tm == 0 and n_lat % self.tm == 0
        assert n_ctx % ATTN_BLOCK == 0 and n_lat % ATTN_BLOCK == 0 and n_lat % GRID_W == 0
        self.nt = self.t // self.tm
        self.ctx_tiles = n_ctx // self.tm
        self.nc = self.t // SCAN_CHUNK
        self.ctx_chunks = n_ctx // SCAN_CHUNK

    def x_spec(self, t0=0, rows=1):
        return pl.BlockSpec((rows, self.tm, D_MODEL), lambda b, t: (b, t + t0, 0))

    def x_in(self, x, t0=0, rows=1):
        if not isinstance(x, tuple):
            return [x], [self.x_spec(t0, rows)]
        ct = self.ctx_tiles
        block = (rows, self.tm, D_MODEL)
        return list(x), [pl.BlockSpec(block, lambda b, t: (b, jnp.minimum(t + t0, ct - 1), 0)),
                         pl.BlockSpec(block, lambda b, t: (b, jnp.maximum(t + t0 - ct, 0), 0))]

    def mod_spec(self, t0=0, rows=1):
        ct = self.ctx_tiles
        return pl.BlockSpec((rows, 1, MOD_ROWS, D_MODEL), lambda b, t: (b, ((t + t0) >= ct).astype(jnp.int32), 0, 0))

    def bwd_chunk(self, i):
        cc = self.ctx_chunks
        return jnp.where(i < cc, cc - 1 - i, self.nc - 1 + cc - i)


LIN_ROWS_PER_STEP = 2


def _lin_kernel(x_ref, m_ref, g_ref, w_ref, o_ref):
    rows, tm = x_ref.shape[0], x_ref.shape[1]
    h = [_modulate(x_ref[r], g_ref[0:1], m_ref[r, 0, 0:1], m_ref[r, 0, 1:2]).astype(bf16) for r in range(rows)]
    z = _dot(jnp.concatenate(h, axis=0), w_ref[...])
    for r in range(rows):
        o_ref[r] = z[r * tm:(r + 1) * tm]


def _mod_linear(geo, x, mods, gains, w, name):
    n = w.shape[1]
    rows = LIN_ROWS_PER_STEP if geo.batch % LIN_ROWS_PER_STEP == 0 else 1
    return pl.pallas_call(
        _lin_kernel,
        grid=(geo.batch // rows, geo.nt),
        in_specs=[geo.x_spec(0, rows), geo.mod_spec(0, rows), _const_spec((8, D_MODEL)), _const_spec((D_MODEL, n))],
        out_specs=pl.BlockSpec((rows, geo.tm, n), lambda b, t: (b, t, 0)),
        out_shape=jax.ShapeDtypeStruct((geo.batch, geo.t, n), f32),
        compiler_params=_params(2),
        name=name,
    )(x, mods, gains, w)


def _read_tile(x_refs, row, is_ctx):
    if len(x_refs) == 1:
        return x_refs[0][row]
    return jnp.where(is_ctx, x_refs[0][row], x_refs[1][row])


def _head_rms(o, gain, width):
    outs = []
    for h in range(o.shape[1] // width):
        oh = o[:, h * width:(h + 1) * width]
        outs.append(oh * lax.rsqrt(jnp.mean(oh * oh, axis=-1, keepdims=True) + NORM_EPS) * gain)
    return jnp.concatenate(outs, axis=1)


def _pre_attn(refs, row):
    (o_ref,) = refs
    return o_ref[row]


def _pre_scan(width, refs, row):
    of_ref, ob_ref, gate_ref, gn_ref = refs
    o = _head_rms(of_ref[row] + ob_ref[row], gn_ref[...], width)
    return (o * _silu(gate_ref[row])).astype(bf16)


def _pre_rwkv(refs, row):
    yf_ref, yb_ref, g_ref, bg_ref, vec_ref = refs
    y = yf_ref[row] + yb_ref[row]
    inv_n = 1.0 / RWKV_HEAD_SIZE
    mu = _seg_sum(y, RWKV_HEAD_SIZE) * inv_n
    dlt = y - mu
    var = _seg_sum(dlt * dlt, RWKV_HEAD_SIZE, exact_terms=False) * inv_n
    yn = dlt * lax.rsqrt(var + RWKV_LN_EPS) * vec_ref[0:1] + vec_ref[1:2]
    return (yn * g_ref[row] + bg_ref[row]).astype(bf16)


def _post_kernel(pre, n_x, n_mix, first_ctx_tiles, *refs):
    x_refs, (m_ref, g_ref), mix_refs = refs[:n_x], refs[n_x:n_x + 2], refs[n_x + 2:n_x + 2 + n_mix]
    wo_ref, win_ref, wout_ref, out_ref = refs[n_x + 2 + n_mix:]
    rows = range(out_ref.shape[0])
    is_ctx = pl.program_id(1) < first_ctx_tiles
    o = [pre(mix_refs, r) for r in rows]
    y = [_dot(o[r], wo_ref[...]) for r in rows]
    x1 = [_read_tile(x_refs, r, is_ctx) + m_ref[r, 0, 2:3] * _rms(y[r], g_ref[1:2]) for r in rows]
    h2 = [_modulate(x1[r], g_ref[2:3], m_ref[r, 0, 3:4], m_ref[r, 0, 4:5]).astype(bf16) for r in rows]
    us = [[] for _ in rows]
    for c in range(MLP_HIDDEN // D_MODEL):
        cols = slice(c * D_MODEL, (c + 1) * D_MODEL)
        for r in rows:
            us[r].append(jnp.square(jnp.maximum(_dot(h2[r], win_ref[:, cols]), 0.0)).astype(bf16))
    acc = [_dot(jnp.concatenate(us[r], axis=1), wout_ref[...]) for r in rows]
    for r in rows:
        out_ref[r] = x1[r] + m_ref[r, 0, 5:6] * _rms(acc[r], g_ref[3:4])


POST_ROWS_PER_STEP = 2


def _post_rows(geo, n_tile_inputs):
    tile = geo.tm * D_MODEL * 4
    weights = 2 * (D_MODEL * D_MODEL + 2 * D_MODEL * MLP_HIDDEN)
    for rows in (POST_ROWS_PER_STEP, 1):
        streamed = 2 * (n_tile_inputs + 2) * rows * tile
        live = 6 * rows * tile
        if geo.batch % rows == 0 and weights + streamed + live <= VMEM_LIMIT:
            return rows
    return 1


def _post_mlp(geo, x, mods, gains, pre, mix_args, mix_specs, w_o, w_in, w_out, skip_ctx, name, rows):
    t0 = geo.ctx_tiles if skip_ctx else 0
    x_args, x_specs = geo.x_in(x, t0, rows)
    return pl.pallas_call(
        functools.partial(_post_kernel, pre, len(x_args), len(mix_args), geo.ctx_tiles - t0),
        grid=(geo.batch // rows, geo.nt - t0),
        in_specs=x_specs + [geo.mod_spec(t0, rows), _const_spec((8, D_MODEL))] + mix_specs + [
            _const_spec((D_MODEL, D_MODEL)), _const_spec((D_MODEL, MLP_HIDDEN)), _const_spec((MLP_HIDDEN, D_MODEL))],
        out_specs=pl.BlockSpec((rows, geo.tm, D_MODEL), lambda b, t: (b, t, 0)),
        out_shape=jax.ShapeDtypeStruct((geo.batch, geo.t - t0 * geo.tm, D_MODEL), f32),
        compiler_params=_params(2),
        name=name,
    )(*x_args, mods, gains, *mix_args, w_o, w_in, w_out)


def _tile_spec(geo, width, col, t0, rows):
    return pl.BlockSpec((rows, geo.tm, width), lambda b, t: (b, t + t0, col))


def _rope(x, cos, sin):
    w = x.shape[1]
    reps = w // LANES
    cw = jnp.concatenate([cos] * reps, axis=1)
    sw = jnp.concatenate([sin] * reps, axis=1)
    lane = lax.broadcasted_iota(jnp.int32, x.shape, 1)
    first = (lane % ROPE_AXIS_DIM) < ROPE_FREQS
    partner = jnp.where(first, pltpu.roll(x, w - ROPE_FREQS, axis=1), pltpu.roll(x, ROPE_FREQS, axis=1))
    return x * cw + partner * sw


def _attn_proj_kernel(ctx_tiles, *refs):
    x_refs, (m_ref, g_ref, cos_ref, sin_ref, wq_ref, wk_ref, wv_ref, q_ref, k_ref, v_ref) = refs[:-10], refs[-10:]
    rows = range(q_ref.shape[0])
    is_ctx = pl.program_id(1) < ctx_tiles
    cos, sin = cos_ref[...], sin_ref[...]
    h = [_modulate(_read_tile(x_refs, r, is_ctx), g_ref[0:1], m_ref[r, 0, 0:1], m_ref[r, 0, 1:2]).astype(bf16)
         for r in rows]
    q = [_dot(h[r], wq_ref[...]) * (HEAD_DIM ** -0.5) for r in rows]
    k = [_dot(h[r], wk_ref[...]) for r in rows]
    v = [_dot(h[r], wv_ref[...]) for r in rows]
    q = [_rope(q[r], cos, sin) for r in rows]
    k = [_rope(k[r], cos, sin) for r in rows]
    for r in rows:
        for hd in range(ATTN_HEADS):
            q_ref[r, hd] = q[r][:, hd * HEAD_DIM:(hd + 1) * HEAD_DIM].astype(bf16)
        for hd in range(ATTN_KV_HEADS):
            k_ref[r, hd] = k[r][:, hd * HEAD_DIM:(hd + 1) * HEAD_DIM].astype(bf16)
            v_ref[r, hd] = v[r][:, hd * HEAD_DIM:(hd + 1) * HEAD_DIM].astype(bf16)


ATTN_ROWS_PER_STEP = 4


def _attn_core_kernel(geo, sink_ref, q_ref, kp_ref, kc_ref, kn_ref, kx_ref, vp_ref, vc_ref, vn_ref, vx_ref, o_ref):
    qb = pl.program_id(1)
    blk = ATTN_BLOCK
    n_loc = 3 * blk
    n_keys = n_loc + geo.n_ctx
    rows = ATTN_GROUP * blk
    first_lat = geo.n_ctx // blk
    n_blocks = geo.t // blk
    row = lax.broadcasted_iota(jnp.int32, (blk, n_keys), 0)
    col = lax.broadcasted_iota(jnp.int32, (blk, n_keys), 1)
    kblk = qb - 1 + col // blk
    ok_local = (jnp.abs(col - blk - row) <= WINDOW) & (kblk >= first_lat) & (kblk < n_blocks)
    ok = (col >= n_loc) | ok_local
    groups = range(ATTN_GROUP)
    units = [(r, j) for r in range(q_ref.shape[0]) for j in range(ATTN_KV_HEADS)]

    def run(local):
        def scores(r, j):
            q4 = q_ref[r, ATTN_GROUP * j:ATTN_GROUP * (j + 1)].reshape(rows, HEAD_DIM)
            keys = (jnp.concatenate([kp_ref[r, j], kc_ref[r, j], kn_ref[r, j], kx_ref[r, j]], axis=0)
                    if local else kx_ref[r, j])
            return lax.dot_general(q4, keys, NT, preferred_element_type=f32)

        def softmax(j, s):
            sg = [s[g * blk:(g + 1) * blk] for g in groups]
            if local:
                sg = [jnp.where(ok, x, NEG_INF) for x in sg]
            sk = [sink_ref[ATTN_GROUP * j + g] for g in groups]
            m = [jnp.maximum(jnp.max(sg[g], axis=-1, keepdims=True), sk[g]) for g in groups]
            e = [jnp.exp(sg[g] - m[g]) for g in groups]
            inv = [1.0 / (jnp.sum(e[g], axis=-1, keepdims=True) + jnp.exp(sk[g] - m[g])) for g in groups]
            return jnp.concatenate([e[g].astype(bf16) for g in groups], axis=0), inv

        def values(r, j, p, inv):
            vals = (jnp.concatenate([vp_ref[r, j], vc_ref[r, j], vn_ref[r, j], vx_ref[r, j]], axis=0)
                    if local else vx_ref[r, j])
            o4 = _dot(p, vals)
            return [o4[g * blk:(g + 1) * blk] * inv[g] for g in groups]

        outs = [None] * len(units)
        s_next = scores(*units[0])
        soft_prev = None
        for i, (r, j) in enumerate(units):
            s_cur = s_next
            if i + 1 < len(units):
                s_next = scores(*units[i + 1])
            soft_cur = softmax(j, s_cur)
            if soft_prev is not None:
                outs[i - 1] = values(*units[i - 1], *soft_prev)
            soft_prev = soft_cur
        outs[-1] = values(*units[-1], *soft_prev)
        for r in range(q_ref.shape[0]):
            heads = outs[r * ATTN_KV_HEADS:(r + 1) * ATTN_KV_HEADS]
            o_ref[r] = jnp.concatenate([o for head_outs in heads for o in head_outs], axis=1).astype(bf16)

    @pl.when(qb >= first_lat)
    def _():
        run(True)

    @pl.when(qb < first_lat)
    def _():
        run(False)


def _rope_tables(geo):
    inv_freq = ROPE_BASE ** (-jnp.arange(ROPE_FREQS, dtype=f32) * 2.0 / ROPE_AXIS_DIM)
    pos = jnp.arange(geo.n_lat)
    row = (pos // GRID_W).astype(f32)
    col = (pos % GRID_W).astype(f32)
    ang = jnp.stack([row[:, None] * inv_freq, col[:, None] * inv_freq], axis=1)
    cos = jnp.cos(ang)
    sin = jnp.sin(ang)
    cos_h = jnp.concatenate([cos, cos], axis=2).reshape(geo.n_lat, HEAD_DIM)
    sin_h = jnp.concatenate([-sin, sin], axis=2).reshape(geo.n_lat, HEAD_DIM)
    cos_t = jnp.concatenate([jnp.ones((geo.n_ctx, HEAD_DIM), f32), cos_h], axis=0)
    sin_t = jnp.concatenate([jnp.zeros((geo.n_ctx, HEAD_DIM), f32), sin_h], axis=0)
    return jnp.tile(cos_t, (1, 2)), jnp.tile(sin_t, (1, 2))


def _attn_layer(geo, x, mods, gains, w_qkv, w_o, sink, w_in, w_out, skip_ctx):
    b, t, tm = geo.batch, geo.t, geo.tm
    q_cols = ATTN_HEADS * HEAD_DIM
    kv_cols = ATTN_KV_HEADS * HEAD_DIM
    wb = w_qkv.astype(bf16)
    cos_t, sin_t = _rope_tables(geo)
    tab_spec = pl.BlockSpec((tm, LANES), lambda bb, tt: (tt, 0))
    rows = LIN_ROWS_PER_STEP if b % LIN_ROWS_PER_STEP == 0 else 1
    x_args, x_specs = geo.x_in(x, 0, rows)
    q, k, v = pl.pallas_call(
        functools.partial(_attn_proj_kernel, geo.ctx_tiles),
        grid=(b // rows, geo.nt),
        in_specs=x_specs + [geo.mod_spec(0, rows), _const_spec((8, D_MODEL)), tab_spec, tab_spec,
                  _const_spec((D_MODEL, q_cols)), _const_spec((D_MODEL, kv_cols)), _const_spec((D_MODEL, kv_cols))],
        out_specs=[pl.BlockSpec((rows, ATTN_HEADS, tm, HEAD_DIM), lambda bb, tt: (bb, 0, tt, 0)),
                   pl.BlockSpec((rows, ATTN_KV_HEADS, tm, HEAD_DIM), lambda bb, tt: (bb, 0, tt, 0)),
                   pl.BlockSpec((rows, ATTN_KV_HEADS, tm, HEAD_DIM), lambda bb, tt: (bb, 0, tt, 0))],
        out_shape=[jax.ShapeDtypeStruct((b, ATTN_HEADS, t, HEAD_DIM), bf16),
                   jax.ShapeDtypeStruct((b, ATTN_KV_HEADS, t, HEAD_DIM), bf16),
                   jax.ShapeDtypeStruct((b, ATTN_KV_HEADS, t, HEAD_DIM), bf16)],
        compiler_params=_params(2),
        name="attn_proj",
    )(*x_args, mods, gains, cos_t, sin_t, wb[:, :q_cols], wb[:, q_cols:q_cols + kv_cols], wb[:, q_cols + kv_cols:])

    blk = ATTN_BLOCK
    n_blocks = t // blk

    arows = ATTN_ROWS_PER_STEP if b % ATTN_ROWS_PER_STEP == 0 else 1

    def kv_spec(off):
        return pl.BlockSpec((arows, ATTN_KV_HEADS, blk, HEAD_DIM),
                            lambda bb, qb: (bb, 0, jnp.clip(qb + off, 0, n_blocks - 1), 0))

    ctx_spec = pl.BlockSpec((arows, ATTN_KV_HEADS, geo.n_ctx, HEAD_DIM), lambda bb, qb: (bb, 0, 0, 0))
    o = pl.pallas_call(
        functools.partial(_attn_core_kernel, geo),
        grid=(b // arows, n_blocks),
        in_specs=[pl.BlockSpec(memory_space=pltpu.SMEM),
                  pl.BlockSpec((arows, ATTN_HEADS, blk, HEAD_DIM), lambda bb, qb: (bb, 0, qb, 0)),
                  kv_spec(-1), kv_spec(0), kv_spec(1), ctx_spec,
                  kv_spec(-1), kv_spec(0), kv_spec(1), ctx_spec],
        out_specs=pl.BlockSpec((arows, blk, q_cols), lambda bb, qb: (bb, qb, 0)),
        out_shape=jax.ShapeDtypeStruct((b, t, q_cols), bf16),
        compiler_params=_params(2),
        name="attn_core",
    )(sink, q, k, k, k, k, v, v, v, v)

    t0 = geo.ctx_tiles if skip_ctx else 0
    rows = _post_rows(geo, 1)
    return _post_mlp(geo, x, mods, gains, _pre_attn, [o], [_tile_spec(geo, D_MODEL, 0, t0, rows)],
                     w_o.astype(bf16), w_in, w_out, skip_ctx, "attn_post", rows)


def _gated_prep(entries, heads, dk, dv):
    sums = [_cumsum_rows(g, reverse) for _, _, _, g, _, _, reverse in entries]
    units = []
    for (q, k, v, g, st_ref, base, reverse), b in zip(entries, sums):
        n = q.shape[0]
        btot = b[0:1] if reverse else b[n - 1:n]
        q_dec = (q * jnp.exp(b)).astype(bf16)
        k_inv32 = k * jnp.exp(-b)
        k_inv = k_inv32.astype(bf16)
        dec = jnp.exp(btot)
        k_end = (k_inv32 * dec).astype(bf16)
        vb = v.astype(bf16)
        ri = lax.broadcasted_iota(jnp.int32, (n, n), 0)
        ci = lax.broadcasted_iota(jnp.int32, (n, n), 1)
        tri = (ci >= ri) if reverse else (ci <= ri)
        for h in range(heads):
            ks = slice(h * dk, (h + 1) * dk)
            vs = slice(h * dv, (h + 1) * dv)
            units.append(dict(q=q_dec[:, ks], ki=k_inv[:, ks], ke=k_end[:, ks], v=vb[:, vs], dec=dec[:, ks],
                              tri=tri, st=st_ref[base + h], ref=st_ref, h=base + h))
    return units


def _gated_matmuls(units, heads):
    for u in units:
        a = lax.dot_general(u["q"], u["ki"], NT, preferred_element_type=f32)
        u["a"] = jnp.where(u["tri"], a, 0.0).astype(bf16)
        u["qs"] = lax.dot_general(u["q"], u["st"].astype(bf16), NT, preferred_element_type=f32)
    for u in units:
        u["o"] = _dot(u["a"], u["v"]) + u["qs"]
        u["new"] = u["st"] * u["dec"] + lax.dot_general(u["v"], u["ke"], TN, preferred_element_type=f32)
    for u in units:
        u["ref"][u["h"]] = u["new"]
    return [jnp.concatenate([u["o"] for u in units[i:i + heads]], axis=1) for i in range(0, len(units), heads)]


def _gated_chunk(groups, heads, dk, dv, staggered):
    def prep(group):
        return _gated_prep([thunk() for thunk in group], heads, dk, dv)

    if not staggered:
        units = prep([thunk for group in groups for thunk in group])
        per_row = len(groups[0])
        outs = _gated_matmuls(units, heads)
        return [outs[i:i + per_row] for i in range(0, len(outs), per_row)]
    outs = []
    ready = prep(groups[0])
    for i in range(len(groups)):
        nxt = prep(groups[i + 1]) if i + 1 < len(groups) else None
        outs.append(_gated_matmuls(ready, heads))
        ready = nxt
    return outs


def _gla_gate(zd, wu_ref, bias_ref, d):
    zg = _dot(zd.astype(bf16), wu_ref[d]) + bias_ref[d:d + 1]
    return (jnp.minimum(zg, 0.0) - jnp.log(1.0 + jnp.exp(-jnp.abs(zg)))) * (1.0 / GLA_TAU)


def _gla_scan_kernel(qkf_ref, vf_ref, zdf_ref, qkb_ref, vb_ref, zdb_ref, wu_ref, bias_ref,
                     of_ref, ob_ref, sf_ref, sb_ref):
    @pl.when(pl.program_id(1) == 0)
    def _():
        sf_ref[...] = jnp.zeros_like(sf_ref)
        sb_ref[...] = jnp.zeros_like(sb_ref)

    def entry(row, d, qk_ref, v_ref, zd_ref, s_ref):
        qk = qk_ref[row]
        return (qk[:, :GLA_KEY_DIM] * (GLA_DK ** -0.5), qk[:, GLA_KEY_DIM:], v_ref[row],
                _gla_gate(zd_ref[row], wu_ref, bias_ref, d), s_ref, row * GLA_HEADS, d == 1)

    groups = [[functools.partial(entry, row, 0, qkf_ref, vf_ref, zdf_ref, sf_ref),
               functools.partial(entry, row, 1, qkb_ref, vb_ref, zdb_ref, sb_ref)]
              for row in range(qkf_ref.shape[0])]
    for row, (o_f, o_b) in enumerate(_gated_chunk(groups, GLA_HEADS, GLA_DK, GLA_DV, staggered=False)):
        of_ref[row] = o_f
        ob_ref[row] = o_b


def _hgrn_scan_kernel(qf_ref, if_ref, zff_ref, qb_ref, ib_ref, zfb_ref, lb_ref, of_ref, ob_ref, sf_ref, sb_ref):
    @pl.when(pl.program_id(1) == 0)
    def _():
        sf_ref[...] = jnp.zeros_like(sf_ref)
        sb_ref[...] = jnp.zeros_like(sb_ref)

    lb = lb_ref[...]
    def entry(row, d, q_ref, i_ref, zf_ref, s_ref):
        f = lb + (1.0 - lb) * _sigmoid(zf_ref[row])
        return (_silu(q_ref[row]), 1.0 - f, i_ref[row], jnp.log(f), s_ref, row * HGRN_HEADS, d == 1)

    groups = [[functools.partial(entry, row, 0, qf_ref, if_ref, zff_ref, sf_ref),
               functools.partial(entry, row, 1, qb_ref, ib_ref, zfb_ref, sb_ref)]
              for row in range(qf_ref.shape[0])]
    for row, (o_f, o_b) in enumerate(_gated_chunk(groups, HGRN_HEADS, HGRN_EXPAND, HGRN_EXPAND, staggered=True)):
        of_ref[row] = o_f
        ob_ref[row] = o_b


SCAN_ROWS_PER_STEP = 8


def _scan_rows(geo):
    return SCAN_ROWS_PER_STEP if geo.batch % SCAN_ROWS_PER_STEP == 0 else 1


def _chunk_spec(geo, width, col, reverse):
    rows = _scan_rows(geo)
    if reverse:
        return pl.BlockSpec((rows, SCAN_CHUNK, width), lambda b, i: (b, geo.bwd_chunk(i), col))
    return pl.BlockSpec((rows, SCAN_CHUNK, width), lambda b, i: (b, i, col))


def _scan_call(geo, body, args, specs, heads, dk, dv, name):
    out_sds = jax.ShapeDtypeStruct((geo.batch, geo.t, heads * dv), f32)
    rows = _scan_rows(geo)
    state = pltpu.VMEM((rows * heads, dv, dk), f32)
    return pl.pallas_call(
        body,
        grid=(geo.batch // rows, geo.nc),
        in_specs=specs,
        out_specs=[_chunk_spec(geo, heads * dv, 0, False), _chunk_spec(geo, heads * dv, 0, True)],
        out_shape=[out_sds, out_sds],
        scratch_shapes=[state, state],
        compiler_params=_params(2),
        name=name,
    )(*args)


def _gla_layer(geo, x, mods, gains, w_in_p, w_gd, w_gu, g_bias, g_norm, w_o, w_in, w_out, skip_ctx):
    r = GLA_GATE_RANK
    n_z = 2 * GLA_KEY_DIM + 2 * D_MODEL
    w_all = jnp.concatenate([w_in_p, w_gd[0], w_gd[1], jnp.zeros((D_MODEL, LANES - 2 * r), f32)], axis=1).astype(bf16)
    z = _mod_linear(geo, x, mods, gains, w_all, "gla_proj")
    wu = jnp.zeros((2, LANES, GLA_KEY_DIM), f32).at[0, :r].set(w_gu[0]).at[1, r:2 * r].set(w_gu[1]).astype(bf16)
    zd_col = n_z // LANES
    specs = []
    for rev in (False, True):
        specs += [_chunk_spec(geo, 2 * GLA_KEY_DIM, 0, rev), _chunk_spec(geo, D_MODEL, 1, rev),
                  _chunk_spec(geo, LANES, zd_col, rev)]
    specs += [_const_spec((2, LANES, GLA_KEY_DIM)), _const_spec((2, GLA_KEY_DIM))]
    o_f, o_b = _scan_call(geo, _gla_scan_kernel, [z, z, z, z, z, z, wu, g_bias], specs,
                          GLA_HEADS, GLA_DK, GLA_DV, "gla_scan")
    t0 = geo.ctx_tiles if skip_ctx else 0
    rows = _post_rows(geo, 3)
    mix_specs = [_tile_spec(geo, D_MODEL, 0, t0, rows), _tile_spec(geo, D_MODEL, 0, t0, rows),
                 _tile_spec(geo, D_MODEL, 2, t0, rows), _const_spec((1, GLA_DV))]
    return _post_mlp(geo, x, mods, gains, functools.partial(_pre_scan, GLA_DV),
                     [o_f, o_b, z, g_norm.reshape(1, GLA_DV)], mix_specs,
                     w_o.astype(bf16), w_in, w_out, skip_ctx, "gla_post", rows)


def _hgrn_layer(geo, x, mods, gains, w_in_p, w_f, lower_bound, g_norm, w_o, w_in, w_out, skip_ctx):
    w_all = jnp.concatenate([w_in_p, w_f[0], w_f[1]], axis=1).astype(bf16)
    z = _mod_linear(geo, x, mods, gains, w_all, "hgrn_proj")
    specs = []
    for rev in (False, True):
        specs += [_chunk_spec(geo, D_MODEL, 0, rev), _chunk_spec(geo, D_MODEL, 1, rev),
                  _chunk_spec(geo, D_MODEL, 4 if rev else 3, rev)]
    specs += [_const_spec((1, D_MODEL))]
    o_f, o_b = _scan_call(geo, _hgrn_scan_kernel, [z, z, z, z, z, z, lower_bound.reshape(1, D_MODEL)], specs,
                          HGRN_HEADS, HGRN_EXPAND, HGRN_EXPAND, "hgrn_scan")
    t0 = geo.ctx_tiles if skip_ctx else 0
    rows = _post_rows(geo, 3)
    mix_specs = [_tile_spec(geo, D_MODEL, 0, t0, rows), _tile_spec(geo, D_MODEL, 0, t0, rows),
                 _tile_spec(geo, D_MODEL, 2, t0, rows), _const_spec((1, HGRN_EXPAND))]
    return _post_mlp(geo, x, mods, gains, functools.partial(_pre_scan, HGRN_EXPAND),
                     [o_f, o_b, z, g_norm.reshape(1, HGRN_EXPAND)], mix_specs,
                     w_o.astype(bf16), w_in, w_out, skip_ctx, "hgrn_post", rows)


def _rwkv_proj_kernel(geo, x_ref, xp_ref, xn_ref, m_ref, g_ref, mix_ref, vec_ref, wrkv_ref, wdn_ref, wup_ref,
                      aup_ref, gup_ref, r_ref, v_ref, gate_ref, kk_ref, lw0_ref, lw1_ref, a0_ref, a1_ref,
                      k0_ref, k1_ref, bg_ref):
    t = pl.program_id(1)
    tm = geo.tm
    gain, shift, scale = g_ref[0:1], m_ref[0, 0, 0:1], m_ref[0, 0, 1:2]
    h = _modulate(x_ref[0], gain, shift, scale)
    seg_first = (t == 0) | (t == geo.ctx_tiles)
    seg_last = (t == geo.ctx_tiles - 1) | (t == geo.nt - 1)
    h_prev = jnp.where(seg_first, 0.0, _modulate(xp_ref[0], gain, shift, scale)[7:8])
    h_next = jnp.where(seg_last, 0.0, _modulate(xn_ref[0], gain, shift, scale)[0:1])
    row = lax.broadcasted_iota(jnp.int32, h.shape, 0)
    up = jnp.where(row == 0, h_prev, pltpu.roll(h, 1, axis=0))
    dn = jnp.where(row == tm - 1, h_next, pltpu.roll(h, tm - 1, axis=0))
    dx = 0.5 * (up + dn) - h

    def mixed(n):
        return (h + dx * mix_ref[n:n + 1]).astype(bf16)

    dw = jnp.tanh(_dot(mixed(3), wdn_ref[:, 0:LANES])).astype(bf16)
    da = _dot(mixed(4), wdn_ref[:, LANES:2 * LANES]).astype(bf16)
    dg = _sigmoid(_dot(mixed(5), wdn_ref[:, 2 * LANES:3 * LANES])).astype(bf16)
    gate = _dot(dg, gup_ref[...])
    gate_ref[0] = gate
    a_dirs = []
    for d, (lw_ref, a_ref) in enumerate(((lw0_ref, a0_ref), (lw1_ref, a1_ref))):
        lw_ref[0] = -RWKV_DECAY_SCALE * _sigmoid(vec_ref[d:d + 1] + _dot(dw, wup_ref[d]))
        a_dirs.append(_sigmoid(vec_ref[2 + d:3 + d] + _dot(da, aup_ref[d])))
        a_ref[0] = a_dirs[d]
    k = _dot(mixed(1), wrkv_ref[1])
    r = _dot(mixed(0), wrkv_ref[0])
    v = _dot(mixed(2), wrkv_ref[2])
    kk = k * vec_ref[4:5]
    kk_ref[0] = kk * lax.rsqrt(jnp.maximum(_seg_sum(kk * kk, RWKV_HEAD_SIZE, exact_terms=False), L2_EPS * L2_EPS))
    kds = []
    for d, kd_ref in enumerate((k0_ref, k1_ref)):
        kds.append(k * (1.0 + (a_dirs[d] - 1.0) * vec_ref[5:6]))
        kd_ref[0] = kds[d]
    r_ref[0] = r
    v_ref[0] = v
    bg_ref[0] = _seg_sum(r * (0.5 * (kds[0] + kds[1])) * vec_ref[6:7], RWKV_HEAD_SIZE,
                         exact_terms=False) * v * gate


def _rwkv_chunk_units(units):
    n = SCAN_CHUNK
    hs = RWKV_HEAD_SIZE
    lane = lax.broadcasted_iota(jnp.int32, (n, RWKV_PAIR), 1)
    t_i = lax.broadcasted_iota(jnp.int32, (n, RWKV_PAIR), 0)
    s_i = lane % hs
    head0 = lane < hs
    ri = lax.broadcasted_iota(jnp.int32, (RWKV_PAIR, RWKV_PAIR), 0)
    ci = lax.broadcasted_iota(jnp.int32, (RWKV_PAIR, RWKV_PAIR), 1)
    same_head = ri // hs == ci // hs
    top_rows = ri < hs

    def block_diag(x):
        return jnp.concatenate([jnp.where(head0, x, 0.0), jnp.where(head0, 0.0, x)], axis=0)

    def inverse_4x4_blocks(nm, reverse):
        toward = RWKV_PAIR - 1 if not reverse else 1
        dist = (s_i - t_i) if reverse else (t_i - s_i)
        near = pltpu.roll(nm, toward, axis=1)
        far = pltpu.roll(nm, (2 * toward) % RWKV_PAIR, axis=1)
        diag1 = jnp.sum(jnp.where(same4 & (dist == 1), nm, 0.0), axis=0, keepdims=True)
        diag2 = jnp.sum(jnp.where(same4 & (dist == 2), nm, 0.0), axis=0, keepdims=True)
        p2 = diag2 + pltpu.roll(diag1, toward, axis=1) * diag1
        two = nm + near * diag1
        three = two + far * p2
        return jnp.where(s_i == t_i, 1.0,
                         jnp.where(same4 & (dist == 1), nm,
                                   jnp.where(same4 & (dist == 2), two, jnp.where(same4 & (dist == 3), three, 0.0))))

    def cross_blocks(b):
        return (t_i // (2 * b) == s_i // (2 * b)) & (t_i // b != s_i // b)

    st = []
    for r, kd, v, kk, a, lw, ht, reverse in units:
        c = _cumsum_rows(lw, reverse)
        tot = c[0:1] if reverse else c[n - 1:n]
        e_neg = jnp.exp(-c)
        dec = jnp.exp(tot)
        kb = jnp.concatenate([kd * e_neg, kk * a * e_neg], axis=0)
        kb_t = kb.T
        kb_sw = pltpu.roll(kb_t, hs, axis=1)
        rhs = jnp.concatenate([jnp.where(same_head, jnp.where(top_rows, kb_t, kb_sw), 0.0),
                               jnp.where(same_head, jnp.where(top_rows, kb_sw, kb_t), 0.0)], axis=1)
        st.append(dict(
            strict=(s_i > t_i) if reverse else (s_i < t_i), incl=(s_i >= t_i) if reverse else (s_i <= t_i),
            a_bar=(-kk * jnp.exp(c - lw)).astype(bf16), r_bar=(r * jnp.exp(c)).astype(bf16), rhs=rhs.astype(bf16),
            v_bd=block_diag(v).astype(bf16), h_t=ht.T.astype(bf16),
            ends=(kb * dec).astype(bf16),
            v=v, ht=ht, dec=dec, reverse=reverse))
    for s in st:
        g = _dot(jnp.concatenate([s["a_bar"], s["r_bar"]], axis=0), s["rhs"])
        s["a_ak"] = jnp.where(s["strict"], g[0:n, 0:RWKV_PAIR], 0.0).astype(bf16)
        s["nmat"] = jnp.where(s["strict"], g[0:n, RWKV_PAIR:], 0.0)
        s["q_k"] = jnp.where(s["incl"], g[n:, 0:RWKV_PAIR], 0.0).astype(bf16)
        s["q_b"] = jnp.where(s["incl"], g[n:, RWKV_PAIR:], 0.0).astype(bf16)
    same4 = t_i // 4 == s_i // 4
    for s in st:
        s["inv"] = inverse_4x4_blocks(s["nmat"], s["reverse"])
    b = 4
    while 2 * b < n:
        for s in st:
            s["pc"] = _dot(s["inv"].astype(bf16), block_diag(jnp.where(cross_blocks(b), s["nmat"], 0.0)).astype(bf16))
        for s in st:
            s["inv"] = s["inv"] + _dot(s["pc"].astype(bf16), block_diag(s["inv"]).astype(bf16))
        b *= 2
    for s in st:
        x = _dot(jnp.concatenate([s["a_bar"], s["a_ak"]], axis=1), jnp.concatenate([s["h_t"], s["v_bd"]], axis=0))
        s["inv"] = s["inv"].astype(bf16)
        s["w"] = _dot(s["inv"], block_diag(x).astype(bf16))
    for s in st:
        s["cw"] = _dot(jnp.where(cross_blocks(n // 2), s["nmat"], 0.0).astype(bf16), block_diag(s["w"]).astype(bf16))
    for s in st:
        s["u"] = s["w"] + _dot(s["inv"], block_diag(s["cw"]).astype(bf16))
    for s in st:
        s["vu_t"] = jnp.concatenate([s["v"], s["u"]], axis=0).T.astype(bf16)
    for s in st:
        s["y"] = _dot(jnp.concatenate([s["r_bar"], s["q_k"], s["q_b"]], axis=1),
                      jnp.concatenate([s["h_t"], s["v_bd"], block_diag(s["u"]).astype(bf16)], axis=0))
    for s in st:
        s["upd"] = _dot(s["vu_t"], s["ends"])
    return [(s["y"], s["ht"] * s["dec"] + jnp.where(same_head, s["upd"], 0.0)) for s in st]


def _rwkv_scan_kernel(rows, pairs, *refs):
    fwd, bwd = refs[0:6], refs[6:12]
    yf_ref, yb_ref, hf_ref, hb_ref = refs[12:]

    @pl.when(pl.program_id(2) == 0)
    def _():
        hf_ref[...] = jnp.zeros_like(hf_ref)
        hb_ref[...] = jnp.zeros_like(hb_ref)

    units, dests = [], []
    for ins, y_ref, h_ref, reverse in ((fwd, yf_ref, hf_ref, False), (bwd, yb_ref, hb_ref, True)):
        r_ref, v_ref, kk_ref, lw_ref, a_ref, kd_ref = ins
        for row in range(rows):
            for p in range(pairs):
                sl = slice(p * RWKV_PAIR, (p + 1) * RWKV_PAIR)
                slot = row * pairs + p
                units.append((r_ref[row, :, sl], kd_ref[row, :, sl], v_ref[row, :, sl], kk_ref[row, :, sl],
                              a_ref[row, :, sl], lw_ref[row, :, sl], h_ref[slot], reverse))
                dests.append((y_ref, h_ref, row, slot, sl))
    for (y, h_new), (y_ref, h_ref, row, slot, sl) in zip(_rwkv_chunk_units(units), dests):
        y_ref[row, :, sl] = y
        h_ref[slot] = h_new


RWKV_PAIRS_PER_STEP = 8
RWKV_ROWS_PER_STEP = 4


def _rwkv_layer(geo, x, mods, gains, mix, w_rkv, w0, w_down, w_up, a0, a_down, a_up, g_down, g_up, k_k, k_a,
                r_k, ln_w, ln_b, w_o, w_in, w_out, skip_ctx):
    b, t, tm = geo.batch, geo.t, geo.tm
    d = D_MODEL
    rank = w_down.shape[-1]
    assert 2 * rank == LANES and a_down.shape[-1] == rank and g_down.shape[-1] == LANES
    mix8 = jnp.pad(mix, ((0, 2), (0, 0)))
    vec = jnp.stack([w0[0], w0[1], a0[0], a0[1], k_k, k_a, r_k.reshape(d), jnp.zeros_like(k_k)])
    w_dn = jnp.concatenate([w_down[0], w_down[1], a_down[0], a_down[1], g_down], axis=1).astype(bf16)

    def padded_up(w):
        return jnp.zeros((2, LANES, d), f32).at[0, :rank].set(w[0]).at[1, rank:].set(w[1]).astype(bf16)

    halo = 8
    n_halo = t // halo
    per = tm // halo
    x_prev = pl.BlockSpec((1, halo, d), lambda bb, tt: (bb, jnp.maximum(tt * per - 1, 0), 0))
    x_next = pl.BlockSpec((1, halo, d), lambda bb, tt: (bb, jnp.minimum((tt + 1) * per, n_halo - 1), 0))
    out_spec = pl.BlockSpec((1, tm, d), lambda bb, tt: (bb, tt, 0))
    sds = jax.ShapeDtypeStruct((b, t, d), f32)
    r, v, gate, kk, lw0, lw1, a_0, a_1, k0, k1, bonus_g = pl.pallas_call(
        functools.partial(_rwkv_proj_kernel, geo),
        grid=(b, geo.nt),
        in_specs=[geo.x_spec(), x_prev, x_next, geo.mod_spec(), _const_spec((8, d)), _const_spec((8, d)),
                  _const_spec((8, d)), _const_spec((3, d, d)), _const_spec((d, 3 * LANES)),
                  _const_spec((2, LANES, d)), _const_spec((2, LANES, d)), _const_spec((LANES, d))],
        out_specs=[out_spec] * 11,
        out_shape=[sds] * 11,
        compiler_params=_params(2),
        name="rwkv_proj",
    )(x, x, x, mods, gains, mix8, vec, w_rkv.astype(bf16), w_dn, padded_up(w_up), padded_up(a_up),
      g_up.astype(bf16))

    pairs = RWKV_PAIRS_PER_STEP
    rows = RWKV_ROWS_PER_STEP if b % RWKV_ROWS_PER_STEP == 0 else 1
    width = pairs * RWKV_PAIR
    groups = d // width

    def cspec(reverse):
        if reverse:
            return pl.BlockSpec((rows, SCAN_CHUNK, width), lambda bb, gg, i: (bb, geo.bwd_chunk(i), gg))
        return pl.BlockSpec((rows, SCAN_CHUNK, width), lambda bb, gg, i: (bb, i, gg))

    state = pltpu.VMEM((rows * pairs, RWKV_PAIR, RWKV_PAIR), f32)
    y_f, y_b = pl.pallas_call(
        functools.partial(_rwkv_scan_kernel, rows, pairs),
        grid=(b // rows, groups, geo.nc),
        in_specs=[cspec(False)] * 6 + [cspec(True)] * 6,
        out_specs=[cspec(False), cspec(True)],
        out_shape=[sds, sds],
        scratch_shapes=[state, state],
        compiler_params=_params(3),
        name="rwkv_scan",
    )(r, v, kk, lw0, a_0, k0, r, v, kk, lw1, a_1, k1)

    t0 = geo.ctx_tiles if skip_ctx else 0
    post_vec = jnp.stack([ln_w, ln_b] + [jnp.zeros_like(ln_w)] * 6)
    rows = _post_rows(geo, 4)
    mix_specs = [_tile_spec(geo, d, 0, t0, rows)] * 4 + [_const_spec((8, d))]
    return _post_mlp(geo, x, mods, gains, _pre_rwkv, [y_f, y_b, gate, bonus_g, post_vec], mix_specs,
                     w_o.astype(bf16), w_in, w_out, skip_ctx, "rwkv_post", rows)


def _hgrn_lower_bound(lb_param, layer):
    p = jax.nn.softmax(lb_param.astype(f32), axis=0)
    return (jnp.cumsum(p, axis=0) - p[0])[layer]


def kernel(x, c, ctx, c_ctx, w_mod, b_mod, g_pre_mix, g_post_mix, g_pre_mlp, g_post_mlp, w_mlp_in, w_mlp_out, attn_w_qkv, attn_w_o, attn_sink, gla_w_in, gla_w_gate_down, gla_w_gate_up, gla_gate_bias, gla_g_norm, gla_w_o, rwkv_mix, rwkv_w_rkv, rwkv_w0, rwkv_w_down, rwkv_w_up, rwkv_a0, rwkv_a_down, rwkv_a_up, rwkv_g_down, rwkv_g_up, rwkv_k_k, rwkv_k_a, rwkv_r_k, rwkv_ln_w, rwkv_ln_b, rwkv_w_o, hgrn_w_in, hgrn_w_f, hgrn_lb, hgrn_g_norm, hgrn_w_o):
    depth = w_mod.shape[0]
    geo = _Geom(x.shape[0], ctx.shape[1], x.shape[1])
    mods_all = _mod_vectors(c, c_ctx, w_mod, b_mod)
    xs = (ctx, x)
    for i in range(depth):
        kind, j = i % 4, i // 4
        skip_ctx = i == depth - 1
        mods = mods_all[i]
        gains = jnp.stack([g_pre_mix[i], g_post_mix[i], g_pre_mlp[i], g_post_mlp[i]] + [jnp.zeros_like(g_pre_mix[i])] * 4)
        w_in, w_out = w_mlp_in[i].astype(bf16), w_mlp_out[i].astype(bf16)
        if kind == 0:
            xs = _attn_layer(geo, xs, mods, gains, attn_w_qkv[j], attn_w_o[j], attn_sink[j], w_in, w_out, skip_ctx)
        elif kind == 1:
            xs = _gla_layer(geo, xs, mods, gains, gla_w_in[j], gla_w_gate_down[j], gla_w_gate_up[j],
                            gla_gate_bias[j], gla_g_norm[j], gla_w_o[j], w_in, w_out, skip_ctx)
        elif kind == 2:
            xs = _rwkv_layer(geo, xs, mods, gains, rwkv_mix[j], rwkv_w_rkv[j], rwkv_w0[j], rwkv_w_down[j],
                             rwkv_w_up[j], rwkv_a0[j], rwkv_a_down[j], rwkv_a_up[j], rwkv_g_down[j], rwkv_g_up[j],
                             rwkv_k_k[j], rwkv_k_a[j], rwkv_r_k[j], rwkv_ln_w[j], rwkv_ln_b[j], rwkv_w_o[j],
                             w_in, w_out, skip_ctx)
        else:
            xs = _hgrn_layer(geo, xs, mods, gains, hgrn_w_in[j], hgrn_w_f[j], _hgrn_lower_bound(hgrn_lb, i),
                             hgrn_g_norm[j], hgrn_w_o[j], w_in, w_out, skip_ctx)
        if skip_ctx:
            return xs
    return xs[:, geo.n_ctx:]
```
